```python
import jax, jax.numpy as jnp
from jax import lax
import numpy as np

D_MODEL = 2048
BATCH = 8
SEQ = 8192
DEPTH = 4

CHUNK = 64
N_MEM = 256
N_A_LAYERS = DEPTH // 2
N_B_LAYERS = DEPTH - N_A_LAYERS
D_RNN = 3 * D_MODEL // 4
RNN_BLOCK = 128
N_RNN_BLOCKS = D_RNN // RNN_BLOCK
CONV_WIDTH = 4
LRU_C = 8.0
SB_HEAD_DIM = 128
D_SB = 3 * D_MODEL // 4
N_SB_HEADS = D_SB // SB_HEAD_DIM
N_MEM_HEADS = 4
MEM_HEAD_DIM = 128
D_MEM = N_MEM_HEADS * MEM_HEAD_DIM
D_MIX = D_RNN + D_MEM
Q_BLOCK = 128
EPS = 1e-6

kernel_name = "yoco_rglru_stickbreaking_memory_trunk"


def rms_norm(x, g):
    xf = x.astype(jnp.float32)
    y = xf * lax.rsqrt(jnp.mean(xf * xf, axis=-1, keepdims=True) + EPS)
    return (y * g.astype(jnp.float32)).astype(x.dtype)


def memory_attention(q, mem_k, mem_v):
    b, t, _ = q.shape
    qh = q.reshape(b, t, N_MEM_HEADS, MEM_HEAD_DIM)
    kh = mem_k.reshape(b, N_MEM, N_MEM_HEADS, MEM_HEAD_DIM)
    vh = mem_v.reshape(b, N_MEM, N_MEM_HEADS, MEM_HEAD_DIM)
    s = jnp.einsum('bthd,bmhd->bhtm', qh, kh).astype(jnp.float32) * (MEM_HEAD_DIM ** -0.5)
    p = jax.nn.softmax(s, axis=-1).astype(vh.dtype)
    o = jnp.einsum('bhtm,bmhd->bthd', p, vh)
    return o.reshape(b, t, D_MEM)


def causal_depthwise_conv(x, w, bias):
    y = lax.conv_general_dilated(
        x, w[:, None, :], window_strides=(1,), padding=[(CONV_WIDTH - 1, 0)],
        dimension_numbers=('NWC', 'WIO', 'NWC'), feature_group_count=x.shape[-1])
    return y + bias


def rg_lru(x, w_r, b_r, w_i, b_i, lam):
    b, t, _ = x.shape
    xb = x.reshape(b, t, N_RNN_BLOCKS, RNN_BLOCK)
    r = jax.nn.sigmoid(jnp.einsum('btnc,ncd->btnd', xb, w_r).reshape(b, t, D_RNN) + b_r)
    i = jax.nn.sigmoid(jnp.einsum('btnc,ncd->btnd', xb, w_i).reshape(b, t, D_RNN) + b_i)
    log_a = (-LRU_C * jax.nn.softplus(-lam.astype(jnp.float32))) * r.astype(jnp.float32)
    a = jnp.exp(log_a)
    u = jnp.sqrt(-jnp.expm1(2.0 * log_a)) * (i * x).astype(jnp.float32)

    def combine(left, right):
        a1, h1 = left
        a2, h2 = right
        return a1 * a2, a2 * h1 + h2

    _, h = lax.associative_scan(combine, (a, u), axis=1)
    return h.astype(x.dtype)


def stick_breaking_attention(q, k, v):
    b, t, _ = q.shape
    n_blk = t // Q_BLOCK
    qh = q.reshape(b, n_blk, Q_BLOCK, N_SB_HEADS, SB_HEAD_DIM).transpose(1, 0, 3, 2, 4)
    kh = k.reshape(b, t, N_SB_HEADS, SB_HEAD_DIM).transpose(0, 2, 1, 3)
    vh = v.reshape(b, t, N_SB_HEADS, SB_HEAD_DIM).transpose(0, 2, 1, 3)
    key_pos = jnp.arange(t)
    scale = SB_HEAD_DIM ** -0.5

    def one_block(args):
        q_blk, blk = args
        q_pos = blk * Q_BLOCK + jnp.arange(Q_BLOCK)
        mask = key_pos[None, :] < q_pos[:, None]
        z = jnp.einsum('bhqd,bhkd->bhqk', q_blk, kh).astype(jnp.float32) * scale
        log_beta = jax.nn.log_sigmoid(z)
        log_keep = jnp.where(mask, jax.nn.log_sigmoid(-z), 0.0)
        later = lax.cumsum(log_keep, axis=3, reverse=True) - log_keep
        w = jnp.where(mask, jnp.exp(log_beta + later), 0.0)
        return jnp.einsum('bhqk,bhkd->bhqd', w.astype(vh.dtype), vh)

    o = lax.map(one_block, (qh, jnp.arange(n_blk)))
    return o.transpose(1, 0, 3, 2, 4).reshape(b, t, D_SB)


def layer_a(h, g_norm, w_in, conv_w, conv_b, w_r, b_r, w_i, b_i, lam, w_out, mem_k, mem_v):
    u = rms_norm(h, g_norm)
    proj = u @ w_in
    x_rnn, g_rnn, q_mem, g_mem = jnp.split(proj, [D_RNN, 2 * D_RNN, 2 * D_RNN + D_MEM], axis=-1)
    x_rnn = causal_depthwise_conv(x_rnn, conv_w, conv_b)
    y_rnn = rg_lru(x_rnn, w_r, b_r, w_i, b_i, lam) * jax.nn.silu(g_rnn)
    y_mem = memory_attention(q_mem, mem_k, mem_v) * jax.nn.silu(g_mem)
    return h + jnp.concatenate([y_rnn, y_mem], axis=-1) @ w_out


def layer_b(h, g_norm, w_in, w_out, k_sb, v_sb, mem_k, mem_v):
    u = rms_norm(h, g_norm)
    proj = u @ w_in
    q_sb, g_sb, q_mem, g_mem = jnp.split(proj, [D_SB, 2 * D_SB, 2 * D_SB + D_MEM], axis=-1)
    y_sb = stick_breaking_attention(q_sb, k_sb, v_sb) * jax.nn.silu(g_sb)
    y_mem = memory_attention(q_mem, mem_k, mem_v) * jax.nn.silu(g_mem)
    return h + jnp.concatenate([y_sb, y_mem], axis=-1) @ w_out


def _fwd_setup_inputs(seed: int = 0) -> dict:
    key = jax.random.key(seed)
    ks = jax.random.split(key, 24)
    f32 = jnp.float32

    def normal(k, shape, scale):
        return jax.random.normal(k, shape, f32) * scale

    def gain(k, shape):
        return 1.0 + 0.02 * jax.random.normal(k, shape, f32)

    a_pow = jax.random.uniform(ks[13], (N_A_LAYERS, D_RNN), f32, 0.9, 0.999)
    s = a_pow ** (1.0 / LRU_C)
    lru_lambda = jnp.log(s) - jnp.log1p(-s)
    return {
        "x": normal(ks[0], (BATCH, SEQ, D_MODEL), 1.0),
        "mem": normal(ks[1], (BATCH, N_MEM, D_MODEL), 1.0),
        "mem_norm": gain(ks[2], (D_MODEL,)),
        "w_mem_kv": normal(ks[3], (DEPTH, D_MODEL, 2 * D_MEM), D_MODEL ** -0.5),
        "norm_a": gain(ks[4], (N_A_LAYERS, D_MODEL)),
        "w_in_a": normal(ks[5], (N_A_LAYERS, D_MODEL, 2 * D_RNN + 2 * D_MEM), D_MODEL ** -0.5),
        "conv_w": normal(ks[6], (N_A_LAYERS, CONV_WIDTH, D_RNN), CONV_WIDTH ** -0.5),
        "conv_b": normal(ks[7], (N_A_LAYERS, D_RNN), 0.01),
        "w_rec_gate": normal(ks[8], (N_A_LAYERS, N_RNN_BLOCKS, RNN_BLOCK, RNN_BLOCK), RNN_BLOCK ** -0.5),
        "b_rec_gate": normal(ks[9], (N_A_LAYERS, D_RNN), 0.01),
        "w_in_gate": normal(ks[10], (N_A_LAYERS, N_RNN_BLOCKS, RNN_BLOCK, RNN_BLOCK), RNN_BLOCK ** -0.5),
        "b_in_gate": normal(ks[11], (N_A_LAYERS, D_RNN), 0.01),
        "lru_lambda": lru_lambda,
        "w_out_a": normal(ks[12], (N_A_LAYERS, D_MIX, D_MODEL), D_MIX ** -0.5),
        "kv_norm": gain(ks[14], (D_MODEL,)),
        "w_kv": normal(ks[15], (D_MODEL, 2 * D_SB), D_MODEL ** -0.5),
        "norm_b": gain(ks[16], (N_B_LAYERS, D_MODEL)),
        "w_in_b": normal(ks[17], (N_B_LAYERS, D_MODEL, 2 * D_SB + 2 * D_MEM), D_MODEL ** -0.5),
        "w_out_b": normal(ks[18], (N_B_LAYERS, D_SB + D_MEM, D_MODEL), (D_SB + D_MEM) ** -0.5),
        "final_norm": gain(ks[19], (D_MODEL,)),
    }


def _fwd_reference(x, mem, mem_norm, w_mem_kv, norm_a, w_in_a, conv_w, conv_b, w_rec_gate, b_rec_gate,
              w_in_gate, b_in_gate, lru_lambda, w_out_a, kv_norm, w_kv, norm_b, w_in_b, w_out_b,
              final_norm):
    mem_n = rms_norm(mem, mem_norm)
    h = x
    k_sb = None
    v_sb = None
    for layer in range(DEPTH):
        mem_k, mem_v = jnp.split(mem_n @ w_mem_kv[layer], 2, axis=-1)
        if layer < N_A_LAYERS:
            h = layer_a(h, norm_a[layer], w_in_a[layer], conv_w[layer], conv_b[layer],
                        w_rec_gate[layer], b_rec_gate[layer], w_in_gate[layer], b_in_gate[layer],
                        lru_lambda[layer], w_out_a[layer], mem_k, mem_v)
            if layer == N_A_LAYERS - 1:
                k_sb, v_sb = jnp.split(rms_norm(h, kv_norm) @ w_kv, 2, axis=-1)
        else:
            j = layer - N_A_LAYERS
            h = layer_b(h, norm_b[j], w_in_b[j], w_out_b[j], k_sb, v_sb, mem_k, mem_v)
    return rms_norm(h, final_norm)


import jax as _jax
import jax.numpy as _jnp

TWIN_FORMAT = 'train_step'
FWD_PARAMS = ['x', 'mem', 'mem_norm', 'w_mem_kv', 'norm_a', 'w_in_a', 'conv_w', 'conv_b', 'w_rec_gate', 'b_rec_gate', 'w_in_gate', 'b_in_gate', 'lru_lambda', 'w_out_a', 'kv_norm', 'w_kv', 'norm_b', 'w_in_b', 'w_out_b', 'final_norm']
TWIN_WEIGHTS = ['mem_norm', 'w_mem_kv', 'norm_a', 'w_in_a', 'conv_w', 'conv_b', 'w_rec_gate', 'b_rec_gate', 'w_in_gate', 'b_in_gate', 'lru_lambda', 'w_out_a', 'kv_norm', 'w_kv', 'norm_b', 'w_in_b', 'w_out_b', 'final_norm']
TWIN_DIFF_INPUT = 'x'
TWIN_INPUTS = ['x', 'mem', 'mem_norm', 'w_mem_kv', 'norm_a', 'w_in_a', 'conv_w', 'conv_b', 'w_rec_gate', 'b_rec_gate', 'w_in_gate', 'b_in_gate', 'lru_lambda', 'w_out_a', 'kv_norm', 'w_kv', 'norm_b', 'w_in_b', 'w_out_b', 'final_norm', 'loss_target', 'm_mem_norm', 'm_w_mem_kv', 'm_norm_a', 'm_w_in_a', 'm_conv_w', 'm_conv_b', 'm_w_rec_gate', 'm_b_rec_gate', 'm_w_in_gate', 'm_b_in_gate', 'm_lru_lambda', 'm_w_out_a', 'm_kv_norm', 'm_w_kv', 'm_norm_b', 'm_w_in_b', 'm_w_out_b', 'm_final_norm', 'v_mem_norm', 'v_w_mem_kv', 'v_norm_a', 'v_w_in_a', 'v_conv_w', 'v_conv_b', 'v_w_rec_gate', 'v_b_rec_gate', 'v_w_in_gate', 'v_b_in_gate', 'v_lru_lambda', 'v_w_out_a', 'v_kv_norm', 'v_w_kv', 'v_norm_b', 'v_w_in_b', 'v_w_out_b', 'v_final_norm']
TWIN_OUTPUTS = ['loss', 'grad_x', 'grad_mem_norm', 'grad_w_mem_kv', 'grad_norm_a', 'grad_w_in_a', 'grad_conv_w', 'grad_conv_b', 'grad_w_rec_gate', 'grad_b_rec_gate', 'grad_w_in_gate', 'grad_b_in_gate', 'grad_lru_lambda', 'grad_w_out_a', 'grad_kv_norm', 'grad_w_kv', 'grad_norm_b', 'grad_w_in_b', 'grad_w_out_b', 'grad_final_norm', 'delta_mem_norm', 'delta_w_mem_kv', 'delta_norm_a', 'delta_w_in_a', 'delta_conv_w', 'delta_conv_b', 'delta_w_rec_gate', 'delta_b_rec_gate', 'delta_w_in_gate', 'delta_b_in_gate', 'delta_lru_lambda', 'delta_w_out_a', 'delta_kv_norm', 'delta_w_kv', 'delta_norm_b', 'delta_w_in_b', 'delta_w_out_b', 'delta_final_norm', 'new_m_mem_norm', 'new_m_w_mem_kv', 'new_m_norm_a', 'new_m_w_in_a', 'new_m_conv_w', 'new_m_conv_b', 'new_m_w_rec_gate', 'new_m_b_rec_gate', 'new_m_w_in_gate', 'new_m_b_in_gate', 'new_m_lru_lambda', 'new_m_w_out_a', 'new_m_kv_norm', 'new_m_w_kv', 'new_m_norm_b', 'new_m_w_in_b', 'new_m_w_out_b', 'new_m_final_norm', 'new_v_mem_norm', 'new_v_w_mem_kv', 'new_v_norm_a', 'new_v_w_in_a', 'new_v_conv_w', 'new_v_conv_b', 'new_v_w_rec_gate', 'new_v_b_rec_gate', 'new_v_w_in_gate', 'new_v_b_in_gate', 'new_v_lru_lambda', 'new_v_w_out_a', 'new_v_kv_norm', 'new_v_w_kv', 'new_v_norm_b', 'new_v_w_in_b', 'new_v_w_out_b', 'new_v_final_norm']
TWIN_LEAF_KINDS = {'loss': 'loss', 'grad_x': 'grad_x', 'grad_mem_norm': 'grad_w', 'grad_w_mem_kv': 'grad_w', 'grad_norm_a': 'grad_w', 'grad_w_in_a': 'grad_w', 'grad_conv_w': 'grad_w', 'grad_conv_b': 'grad_w', 'grad_w_rec_gate': 'grad_w', 'grad_b_rec_gate': 'grad_w', 'grad_w_in_gate': 'grad_w', 'grad_b_in_gate': 'grad_w', 'grad_lru_lambda': 'grad_w', 'grad_w_out_a': 'grad_w', 'grad_kv_norm': 'grad_w', 'grad_w_kv': 'grad_w', 'grad_norm_b': 'grad_w', 'grad_w_in_b': 'grad_w', 'grad_w_out_b': 'grad_w', 'grad_final_norm': 'grad_w', 'delta_mem_norm': 'delta_w', 'delta_w_mem_kv': 'delta_w', 'delta_norm_a': 'delta_w', 'delta_w_in_a': 'delta_w', 'delta_conv_w': 'delta_w', 'delta_conv_b': 'delta_w', 'delta_w_rec_gate': 'delta_w', 'delta_b_rec_gate': 'delta_w', 'delta_w_in_gate': 'delta_w', 'delta_b_in_gate': 'delta_w', 'delta_lru_lambda': 'delta_w', 'delta_w_out_a': 'delta_w', 'delta_kv_norm': 'delta_w', 'delta_w_kv': 'delta_w', 'delta_norm_b': 'delta_w', 'delta_w_in_b': 'delta_w', 'delta_w_out_b': 'delta_w', 'delta_final_norm': 'delta_w', 'new_m_mem_norm': 'new_m', 'new_m_w_mem_kv': 'new_m', 'new_m_norm_a': 'new_m', 'new_m_w_in_a': 'new_m', 'new_m_conv_w': 'new_m', 'new_m_conv_b': 'new_m', 'new_m_w_rec_gate': 'new_m', 'new_m_b_rec_gate': 'new_m', 'new_m_w_in_gate': 'new_m', 'new_m_b_in_gate': 'new_m', 'new_m_lru_lambda': 'new_m', 'new_m_w_out_a': 'new_m', 'new_m_kv_norm': 'new_m', 'new_m_w_kv': 'new_m', 'new_m_norm_b': 'new_m', 'new_m_w_in_b': 'new_m', 'new_m_w_out_b': 'new_m', 'new_m_final_norm': 'new_m', 'new_v_mem_norm': 'new_v', 'new_v_w_mem_kv': 'new_v', 'new_v_norm_a': 'new_v', 'new_v_w_in_a': 'new_v', 'new_v_conv_w': 'new_v', 'new_v_conv_b': 'new_v', 'new_v_w_rec_gate': 'new_v', 'new_v_b_rec_gate': 'new_v', 'new_v_w_in_gate': 'new_v', 'new_v_b_in_gate': 'new_v', 'new_v_lru_lambda': 'new_v', 'new_v_w_out_a': 'new_v', 'new_v_kv_norm': 'new_v', 'new_v_w_kv': 'new_v', 'new_v_norm_b': 'new_v', 'new_v_w_in_b': 'new_v', 'new_v_w_out_b': 'new_v', 'new_v_final_norm': 'new_v'}


def _forward(args):
    return _fwd_reference(*[args[k] for k in FWD_PARAMS])


def _output_shape():
    def fwd():
        inp = _fwd_setup_inputs(0)
        return _fwd_reference(*[inp[k] for k in FWD_PARAMS])
    out = _jax.eval_shape(fwd)
    return out.shape, out.dtype

N_MICROBATCH = 1
ADAM_LR = 0.001
ADAM_B1 = 0.9
ADAM_B2 = 0.999
ADAM_EPS = 1e-08
ADAM_WD = 0.01
ADAM_STEP = 10
PER_EXAMPLE_BATCH_AXIS = {'x': 0, 'mem': 0, 'loss_target': 0}
SHARED_INPUTS = []
_WEIGHT_DTYPES = {'mem_norm': _jnp.float32, 'w_mem_kv': _jnp.float32, 'norm_a': _jnp.float32, 'w_in_a': _jnp.float32, 'conv_w': _jnp.float32, 'conv_b': _jnp.float32, 'w_rec_gate': _jnp.float32, 'b_rec_gate': _jnp.float32, 'w_in_gate': _jnp.float32, 'b_in_gate': _jnp.float32, 'lru_lambda': _jnp.float32, 'w_out_a': _jnp.float32, 'kv_norm': _jnp.float32, 'w_kv': _jnp.float32, 'norm_b': _jnp.float32, 'w_in_b': _jnp.float32, 'w_out_b': _jnp.float32, 'final_norm': _jnp.float32}
MOMENT_SCALE = {'mem_norm': 1.120574e-02, 'w_mem_kv': 7.659311e-03, 'norm_a': 7.588553e-02, 'w_in_a': 5.393814e-02, 'conv_w': 6.968737e-02, 'conv_b': 8.118001e-01, 'w_rec_gate': 2.058921e-02, 'b_rec_gate': 1.755644e-02, 'w_in_gate': 3.723002e-02, 'b_in_gate': 2.528440e-02, 'lru_lambda': 3.630239e-02, 'w_out_a': 6.005213e-02, 'kv_norm': 5.629728e-02, 'w_kv': 4.623058e-02, 'norm_b': 4.174247e-02, 'w_in_b': 2.925086e-02, 'w_out_b': 3.662218e-02, 'final_norm': 3.197054e+01}


def _to_microbatches(a, axis):
    t = _jnp.moveaxis(a, axis, 0)
    t = t.reshape((N_MICROBATCH, t.shape[0] // N_MICROBATCH) + t.shape[1:])
    return _jnp.moveaxis(t, 1, axis + 1)


def setup_inputs(seed: int = 0) -> dict:
    inp = _fwd_setup_inputs(seed)
    key = _jax.random.fold_in(_jax.random.key(seed), 7919)
    shape, _ = _output_shape()
    out = dict(inp)
    out["loss_target"] = _jax.random.normal(_jax.random.fold_in(key, 0), shape, _jnp.float32)
    for i, name in enumerate(TWIN_WEIGHTS):
        w = inp[name].astype(_jnp.float32)
        if MOMENT_SCALE is None:
            s = _jnp.sqrt(_jnp.mean(_jnp.square(w)) + 1e-30)
        else:
            s = MOMENT_SCALE[name]
        km, kv = _jax.random.split(_jax.random.fold_in(key, i + 1))
        out[name] = w
        out["m_" + name] = s * _jax.random.normal(km, w.shape, _jnp.float32)
        out["v_" + name] = (s * s) * _jax.random.uniform(kv, w.shape, _jnp.float32, 0.5, 1.5)
    if N_MICROBATCH > 1:
        for name, axis in PER_EXAMPLE_BATCH_AXIS.items():
            out[name] = _to_microbatches(out[name], axis)
    return {'x': out['x'], 'mem': out['mem'], 'mem_norm': out['mem_norm'], 'w_mem_kv': out['w_mem_kv'], 'norm_a': out['norm_a'], 'w_in_a': out['w_in_a'], 'conv_w': out['conv_w'], 'conv_b': out['conv_b'], 'w_rec_gate': out['w_rec_gate'], 'b_rec_gate': out['b_rec_gate'], 'w_in_gate': out['w_in_gate'], 'b_in_gate': out['b_in_gate'], 'lru_lambda': out['lru_lambda'], 'w_out_a': out['w_out_a'], 'kv_norm': out['kv_norm'], 'w_kv': out['w_kv'], 'norm_b': out['norm_b'], 'w_in_b': out['w_in_b'], 'w_out_b': out['w_out_b'], 'final_norm': out['final_norm'], 'loss_target': out['loss_target'], 'm_mem_norm': out['m_mem_norm'], 'm_w_mem_kv': out['m_w_mem_kv'], 'm_norm_a': out['m_norm_a'], 'm_w_in_a': out['m_w_in_a'], 'm_conv_w': out['m_conv_w'], 'm_conv_b': out['m_conv_b'], 'm_w_rec_gate': out['m_w_rec_gate'], 'm_b_rec_gate': out['m_b_rec_gate'], 'm_w_in_gate': out['m_w_in_gate'], 'm_b_in_gate': out['m_b_in_gate'], 'm_lru_lambda': out['m_lru_lambda'], 'm_w_out_a': out['m_w_out_a'], 'm_kv_norm': out['m_kv_norm'], 'm_w_kv': out['m_w_kv'], 'm_norm_b': out['m_norm_b'], 'm_w_in_b': out['m_w_in_b'], 'm_w_out_b': out['m_w_out_b'], 'm_final_norm': out['m_final_norm'], 'v_mem_norm': out['v_mem_norm'], 'v_w_mem_kv': out['v_w_mem_kv'], 'v_norm_a': out['v_norm_a'], 'v_w_in_a': out['v_w_in_a'], 'v_conv_w': out['v_conv_w'], 'v_conv_b': out['v_conv_b'], 'v_w_rec_gate': out['v_w_rec_gate'], 'v_b_rec_gate': out['v_b_rec_gate'], 'v_w_in_gate': out['v_w_in_gate'], 'v_b_in_gate': out['v_b_in_gate'], 'v_lru_lambda': out['v_lru_lambda'], 'v_w_out_a': out['v_w_out_a'], 'v_kv_norm': out['v_kv_norm'], 'v_w_kv': out['v_w_kv'], 'v_norm_b': out['v_norm_b'], 'v_w_in_b': out['v_w_in_b'], 'v_w_out_b': out['v_w_out_b'], 'v_final_norm': out['v_final_norm']}


def _loss(weights, diff, rest, loss_target):
    with _jax.named_scope("forward"):
        args = {**rest, TWIN_DIFF_INPUT: diff, **{k: w.astype(_WEIGHT_DTYPES[k]) for k, w in weights.items()}}
        y = _forward(args)
    with _jax.named_scope("loss_head"):
        err = _jnp.square(y.astype(_jnp.float32) - loss_target)
        return 0.5 * _jnp.sum(_jnp.mean(err, axis=-1)) if err.ndim else 0.5 * err


def _adamw(w, g, m, v):
    m = ADAM_B1 * m + (1.0 - ADAM_B1) * g
    v = ADAM_B2 * v + (1.0 - ADAM_B2) * _jnp.square(g)
    m_hat = m / (1.0 - ADAM_B1 ** ADAM_STEP)
    v_hat = v / (1.0 - ADAM_B2 ** ADAM_STEP)
    delta = -ADAM_LR * (m_hat / (_jnp.sqrt(v_hat) + ADAM_EPS) + ADAM_WD * w)
    return delta, m, v


def reference(x, mem, mem_norm, w_mem_kv, norm_a, w_in_a, conv_w, conv_b, w_rec_gate, b_rec_gate, w_in_gate, b_in_gate, lru_lambda, w_out_a, kv_norm, w_kv, norm_b, w_in_b, w_out_b, final_norm, loss_target, m_mem_norm, m_w_mem_kv, m_norm_a, m_w_in_a, m_conv_w, m_conv_b, m_w_rec_gate, m_b_rec_gate, m_w_in_gate, m_b_in_gate, m_lru_lambda, m_w_out_a, m_kv_norm, m_w_kv, m_norm_b, m_w_in_b, m_w_out_b, m_final_norm, v_mem_norm, v_w_mem_kv, v_norm_a, v_w_in_a, v_conv_w, v_conv_b, v_w_rec_gate, v_b_rec_gate, v_w_in_gate, v_b_in_gate, v_lru_lambda, v_w_out_a, v_kv_norm, v_w_kv, v_norm_b, v_w_in_b, v_w_out_b, v_final_norm):
    given = dict(x=x, mem=mem, mem_norm=mem_norm, w_mem_kv=w_mem_kv, norm_a=norm_a, w_in_a=w_in_a, conv_w=conv_w, conv_b=conv_b, w_rec_gate=w_rec_gate, b_rec_gate=b_rec_gate, w_in_gate=w_in_gate, b_in_gate=b_in_gate, lru_lambda=lru_lambda, w_out_a=w_out_a, kv_norm=kv_norm, w_kv=w_kv, norm_b=norm_b, w_in_b=w_in_b, w_out_b=w_out_b, final_norm=final_norm, loss_target=loss_target, m_mem_norm=m_mem_norm, m_w_mem_kv=m_w_mem_kv, m_norm_a=m_norm_a, m_w_in_a=m_w_in_a, m_conv_w=m_conv_w, m_conv_b=m_conv_b, m_w_rec_gate=m_w_rec_gate, m_b_rec_gate=m_b_rec_gate, m_w_in_gate=m_w_in_gate, m_b_in_gate=m_b_in_gate, m_lru_lambda=m_lru_lambda, m_w_out_a=m_w_out_a, m_kv_norm=m_kv_norm, m_w_kv=m_w_kv, m_norm_b=m_norm_b, m_w_in_b=m_w_in_b, m_w_out_b=m_w_out_b, m_final_norm=m_final_norm, v_mem_norm=v_mem_norm, v_w_mem_kv=v_w_mem_kv, v_norm_a=v_norm_a, v_w_in_a=v_w_in_a, v_conv_w=v_conv_w, v_conv_b=v_conv_b, v_w_rec_gate=v_w_rec_gate, v_b_rec_gate=v_b_rec_gate, v_w_in_gate=v_w_in_gate, v_b_in_gate=v_b_in_gate, v_lru_lambda=v_lru_lambda, v_w_out_a=v_w_out_a, v_kv_norm=v_kv_norm, v_w_kv=v_w_kv, v_norm_b=v_norm_b, v_w_in_b=v_w_in_b, v_w_out_b=v_w_out_b, v_final_norm=v_final_norm)
    weights = {n: given[n] for n in TWIN_WEIGHTS}
    shared = {n: given[n] for n in SHARED_INPUTS}
    per_example = {n: given[n] for n in ['x', 'mem']}
    grad_fn = _jax.value_and_grad(_loss, argnums=(0, 1))

    def one_microbatch(ex, loss_target):
        ex = dict(ex)
        diff = ex.pop(TWIN_DIFF_INPUT)
        return grad_fn(weights, diff, {**shared, **ex}, loss_target)

    if N_MICROBATCH == 1:
        loss, (grad_w, grad_x) = one_microbatch(per_example, given["loss_target"])
    else:
        def body(carry, xs):
            loss_sum, grad_sum = carry
            l_k, (gw_k, gx_k) = one_microbatch(xs[0], xs[1])
            with _jax.named_scope("update"):
                return (loss_sum + l_k, _jax.tree.map(_jnp.add, grad_sum, gw_k)), gx_k

        init = (_jnp.zeros((), _jnp.float32), _jax.tree.map(_jnp.zeros_like, weights))
        (loss, grad_w), grad_x = _jax.lax.scan(body, init, (per_example, given["loss_target"]))
    with _jax.named_scope("update"):
        delta_w, new_m, new_v = {}, {}, {}
        for n in TWIN_WEIGHTS:
            delta_w[n], new_m[n], new_v[n] = _adamw(weights[n], grad_w[n], given["m_" + n], given["v_" + n])
    return (loss, grad_x, *[grad_w[n] for n in TWIN_WEIGHTS], *[delta_w[n] for n in TWIN_WEIGHTS],
            *[new_m[n] for n in TWIN_WEIGHTS], *[new_v[n] for n in TWIN_WEIGHTS])
```

```python
import functools

import jax
import jax.numpy as jnp
from jax import lax
from jax.experimental import pallas as pl
from jax.experimental.pallas import tpu as pltpu

F32 = jnp.float32
BF16 = jnp.bfloat16

N_DEV = 8
EPS = 1e-6
LRU_C = 8.0
HEAD = 128
CONV_WIDTH = 4
LANES = 128
VMEM_LIMIT = 56 * 1024 * 1024

ADAM_LR = 0.001
ADAM_B1 = 0.9
ADAM_B2 = 0.999
ADAM_EPS = 1e-08
ADAM_WD = 0.01
ADAM_STEP = 10

HBM_SPEC = pl.BlockSpec(memory_space=pltpu.HBM)


def _params(*semantics):
    return pltpu.CompilerParams(dimension_semantics=semantics, vmem_limit_bytes=VMEM_LIMIT)


def _tile(n, cap):
    if n <= cap:
        return n
    t = cap - cap % LANES
    while n % t:
        t -= LANES
    return t


def _row_tile(n, cap):
    if n <= cap:
        return n
    t = cap - cap % 8
    while t >= 8:
        if n % t == 0:
            return t
        t -= 8
    return n


def _rows_for(n, bytes_per_row, budget=2 * 1024 * 1024):
    return _row_tile(n, max(8, budget // bytes_per_row))


def _dot(a, b):
    return lax.dot_general(a, b, (((1,), (0,)), ((), ())), preferred_element_type=F32)


def _dot_nt(a, b):
    return lax.dot_general(a, b, (((1,), (1,)), ((), ())), preferred_element_type=F32)


def _dot_tn(a, b):
    return lax.dot_general(a, b, (((0,), (0,)), ((), ())), preferred_element_type=F32)


def _split_bf16(x):
    hi = x.astype(BF16)
    lo = (x - hi.astype(F32)).astype(BF16)
    return hi, lo


def _dot_hilo(x, b):
    hi, lo = _split_bf16(x)
    return _dot(hi, b) + _dot(lo, b)


def _softplus(x):
    return jnp.maximum(x, 0.0) + jnp.log1p(jnp.exp(-jnp.abs(x)))


def _silu_and_grad(g):
    sg = jax.nn.sigmoid(g)
    return g * sg, sg * (1.0 + g * (1.0 - sg))


def _neg_expm1(x):
    series = -x * (1.0 + x * (0.5 + x * (1.0 / 6.0 + x * (1.0 / 24.0 + x * (1.0 / 120.0)))))
    return jnp.where(x > -0.1, series, 1.0 - jnp.exp(x))


def _rms_fwd(x, g, name):
    T, D = x.shape
    tt = _row_tile(T, 512)

    def body(x_ref, g_ref, o_ref):
        xv = x_ref[...]
        r = lax.rsqrt(jnp.mean(xv * xv, axis=-1, keepdims=True) + EPS)
        o_ref[...] = ((xv * r) * g_ref[...]).astype(o_ref.dtype)

    return pl.pallas_call(
        body, name=name, grid=(T // tt,),
        in_specs=[pl.BlockSpec((tt, D), lambda i: (i, 0)), pl.BlockSpec((1, D), lambda i: (0, 0))],
        out_specs=pl.BlockSpec((tt, D), lambda i: (i, 0)),
        out_shape=jax.ShapeDtypeStruct((T, D), BF16),
        compiler_params=_params("parallel"),
    )(x, g.reshape(1, D))


def _rms_bwd(du, h, g, res, name):
    T, D = h.shape
    tt = _row_tile(T, 256)

    def body(du_ref, h_ref, g_ref, res_ref, dh_ref, dg_ref):
        @pl.when(pl.program_id(0) == 0)
        def _():
            dg_ref[...] = jnp.zeros_like(dg_ref)

        hv = h_ref[...]
        duv = du_ref[...]
        r = lax.rsqrt(jnp.mean(hv * hv, axis=-1, keepdims=True) + EPS)
        xhat = hv * r
        dxhat = duv * g_ref[...]
        dh_ref[...] = res_ref[...] + r * (dxhat - xhat * jnp.mean(dxhat * xhat, axis=-1, keepdims=True))
        dg_ref[...] += jnp.sum(duv * xhat, axis=0, keepdims=True)

    row = pl.BlockSpec((tt, D), lambda i: (i, 0))
    vec = pl.BlockSpec((1, D), lambda i: (0, 0))
    return pl.pallas_call(
        body, name=name, grid=(T // tt,),
        in_specs=[row, row, vec, row],
        out_specs=[row, vec],
        out_shape=[jax.ShapeDtypeStruct((T, D), F32), jax.ShapeDtypeStruct((1, D), F32)],
        compiler_params=_params("arbitrary"),
    )(du, h, g.reshape(1, D), res)


def _mm_nn(a, b3, res=None, out_dtype=F32, name=None):
    M, K = a.shape
    S, _, ns = b3.shape
    tm = _row_tile(M, 1024 if a.dtype == BF16 else 512)
    tn = _tile(ns, 512)
    nj = ns // tn

    def body(*refs):
        if res is None:
            a_ref, b_ref, o_ref = refs
        else:
            a_ref, b_ref, r_ref, o_ref = refs
        acc = _dot(a_ref[...].astype(BF16), b_ref[...])
        if res is not None:
            acc = acc + r_ref[...]
        o_ref[...] = acc.astype(o_ref.dtype)

    in_specs = [pl.BlockSpec((tm, K), lambda i, j: (i, 0)),
                pl.BlockSpec((None, K, tn), lambda i, j: (j // nj, 0, j % nj))]
    args = [a, b3]
    if res is not None:
        in_specs.append(pl.BlockSpec((tm, tn), lambda i, j: (i, j)))
        args.append(res)
    return pl.pallas_call(
        body, name=name, grid=(M // tm, S * nj),
        in_specs=in_specs,
        out_specs=pl.BlockSpec((tm, tn), lambda i, j: (i, j)),
        out_shape=jax.ShapeDtypeStruct((M, S * ns), out_dtype),
        compiler_params=_params("parallel", "parallel"),
    )(*args)


def _mm_nt(a, b3, out_dtype=F32, name=None):
    M = a.shape[0]
    S, N, ns = b3.shape
    tm = _row_tile(M, 1024 if ns <= 1024 else 512)
    tn = _tile(N, 512)

    def body(a_ref, b_ref, o_ref, acc_ref):
        s = pl.program_id(2)
        p = _dot_nt(a_ref[...].astype(BF16), b_ref[...])

        @pl.when(s == 0)
        def _():
            acc_ref[...] = p

        @pl.when(s > 0)
        def _():
            acc_ref[...] += p

        @pl.when(s == S - 1)
        def _():
            o_ref[...] = acc_ref[...].astype(o_ref.dtype)

    return pl.pallas_call(
        body, name=name, grid=(M // tm, N // tn, S),
        in_specs=[pl.BlockSpec((tm, ns), lambda i, j, s: (i, s)),
                  pl.BlockSpec((None, tn, ns), lambda i, j, s: (s, j, 0))],
        out_specs=pl.BlockSpec((tm, tn), lambda i, j, s: (i, j)),
        out_shape=jax.ShapeDtypeStruct((M, N), out_dtype),
        scratch_shapes=[pltpu.VMEM((tm, tn), F32)],
        compiler_params=_params("parallel", "parallel", "arbitrary"),
    )(a, b3)


def _mm_tn(a, b, S, name=None):
    T, K = a.shape
    ns = b.shape[1] // S
    tk = _tile(K, 1024)
    tn = _tile(ns, 512)
    nj = ns // tn
    tt = _row_tile(T, 512)
    nt = T // tt

    def body(a_ref, b_ref, o_ref, acc_ref):
        t = pl.program_id(2)
        p = _dot_tn(a_ref[...].astype(BF16), b_ref[...].astype(BF16))

        @pl.when(t == 0)
        def _():
            acc_ref[...] = p

        @pl.when(t > 0)
        def _():
            acc_ref[...] += p

        @pl.when(t == nt - 1)
        def _():
            o_ref[...] = acc_ref[...]

    return pl.pallas_call(
        body, name=name, grid=(K // tk, S * nj, nt),
        in_specs=[pl.BlockSpec((tt, tk), lambda i, j, t: (t, i)),
                  pl.BlockSpec((tt, tn), lambda i, j, t: (t, j))],
        out_specs=pl.BlockSpec((None, tk, tn), lambda i, j, t: (j // nj, i, j % nj)),
        out_shape=jax.ShapeDtypeStruct((S, K, ns), F32),
        scratch_shapes=[pltpu.VMEM((tk, tn), F32)],
        compiler_params=_params("parallel", "parallel", "arbitrary"),
    )(a, b)


def _lru_coeffs(lam_row, r):
    cl = -LRU_C * _softplus(-lam_row)
    log_a = cl * r
    a = jnp.exp(log_a)
    em = _neg_expm1(2.0 * log_a)
    return cl, a, em, jnp.sqrt(em)


def _rglru_fwd(proj, conv_w, conv_b, w_r, b_r, w_i, b_i, lam, name):
    T = proj.shape[0]
    DR = conv_w.shape[1]
    NB = DR // HEAD
    tt = _row_tile(T, 256)
    PAD = 8

    def body(x_ref, g_ref, cw_ref, cb_ref, wr_ref, br_ref, wi_ref, bi_ref, lam_ref,
             y_ref, xc_ref, r_ref, i_ref, h_ref, xbuf, hcar):
        @pl.when(pl.program_id(0) == 0)
        def _():
            xbuf[0:PAD, :] = jnp.zeros((PAD, DR), F32)
            hcar[...] = jnp.zeros_like(hcar)

        xbuf[PAD:PAD + tt, :] = x_ref[...]
        xc = cb_ref[...] + cw_ref[0:1, :] * xbuf[pl.ds(PAD - 3, tt), :]
        for k in range(1, CONV_WIDTH):
            xc = xc + cw_ref[k:k + 1, :] * xbuf[pl.ds(PAD - 3 + k, tt), :]
        xbuf[0:PAD, :] = xbuf[tt:tt + PAD, :]
        xc_ref[...] = xc
        xcb = xc.astype(BF16)
        for n in range(NB):
            sl = slice(n * HEAD, (n + 1) * HEAD)
            r_ref[:, sl] = jax.nn.sigmoid(_dot(xcb[:, sl], wr_ref[n]) + br_ref[:, sl])
            i_ref[:, sl] = jax.nn.sigmoid(_dot(xcb[:, sl], wi_ref[n]) + bi_ref[:, sl])
        _, a, _, mult = _lru_coeffs(lam_ref[...], r_ref[...])
        hs = mult * (i_ref[...] * xc)
        row = lax.broadcasted_iota(jnp.int32, (tt, DR), 0)
        d = 1
        while d < tt:
            keep = row >= d
            a_sh = jnp.where(keep, pltpu.roll(a, d, 0), 1.0)
            h_sh = jnp.where(keep, pltpu.roll(hs, d, 0), 0.0)
            hs = a * h_sh + hs
            a = a * a_sh
            d *= 2
        h = hs + a * hcar[...]
        h_ref[...] = h
        hcar[...] = h_ref[tt - 1:tt, :]
        g = g_ref[...]
        y_ref[...] = (h * (g * jax.nn.sigmoid(g))).astype(y_ref.dtype)

    col = lambda j: pl.BlockSpec((tt, DR), lambda c: (c, j))
    vec = pl.BlockSpec((1, DR), lambda c: (0, 0))
    gate = pl.BlockSpec((NB, HEAD, HEAD), lambda c: (0, 0, 0))
    f32_out = jax.ShapeDtypeStruct((T, DR), F32)
    return pl.pallas_call(
        body, name=name, grid=(T // tt,),
        in_specs=[col(0), col(1), pl.BlockSpec((CONV_WIDTH, DR), lambda c: (0, 0)), vec, gate, vec, gate, vec, vec],
        out_specs=[col(0)] * 5,
        out_shape=[jax.ShapeDtypeStruct((T, DR), BF16), f32_out, f32_out, f32_out, f32_out],
        scratch_shapes=[pltpu.VMEM((tt + PAD, DR), F32), pltpu.VMEM((1, DR), F32)],
        compiler_params=_params("arbitrary"),
    )(proj, proj, conv_w, conv_b.reshape(1, DR), w_r, b_r.reshape(1, DR), w_i, b_i.reshape(1, DR),
      lam.reshape(1, DR))


def _rglru_bwd(dy, proj, xc, r, i, h, conv_w, w_r, w_i, lam, name):
    T = proj.shape[0]
    DR = conv_w.shape[1]
    NB = DR // HEAD
    tt = _row_tile(T, 128)
    nc = T // tt
    PAD = 8
    per = tt // PAD

    def body(dy_ref, x_ref, g_ref, xc_ref, r_ref, i_ref, h_ref, xprev_ref, hprev_ref,
             cw_ref, wr_ref, wi_ref, lam_ref,
             dxg_ref, dcw_ref, dcb_ref, dwr_ref, dbr_ref, dwi_ref, dbi_ref, dlam_ref,
             xbuf, dxcbuf, gcar, acar):
        step = pl.program_id(0)
        chunk = nc - 1 - step

        @pl.when(step == 0)
        def _():
            for ref in (dcw_ref, dcb_ref, dwr_ref, dbr_ref, dwi_ref, dbi_ref, dlam_ref, gcar, acar):
                ref[...] = jnp.zeros_like(ref)
            dxcbuf[tt:tt + PAD, :] = jnp.zeros((PAD, DR), F32)

        not_first = (chunk > 0).astype(F32)
        row = lax.broadcasted_iota(jnp.int32, (tt, DR), 0)
        silu, dsilu = _silu_and_grad(g_ref[...])
        dyv = dy_ref[...]
        hv = h_ref[...]
        dxg_ref[:, DR:2 * DR] = (dyv * hv * dsilu).astype(dxg_ref.dtype)
        dh = dyv * silu
        rv = r_ref[...]
        iv = i_ref[...]
        xcv = xc_ref[...]
        lam_row = lam_ref[...]
        cl, a, em, mult = _lru_coeffs(lam_row, rv)
        b = jnp.where(row == tt - 1, acar[...], pltpu.roll(a, tt - 1, 0))
        gs = dh
        d = 1
        while d < tt:
            keep = row < tt - d
            b_sh = jnp.where(keep, pltpu.roll(b, tt - d, 0), 1.0)
            g_sh = jnp.where(keep, pltpu.roll(gs, tt - d, 0), 0.0)
            gs = gs + b * g_sh
            b = b * b_sh
            d *= 2
        gt = gs + b * gcar[...]
        xbuf[0:tt, :] = gt
        gcar[...] = xbuf[0:1, :]
        acar[...] = _lru_coeffs(lam_row, r_ref[0:1, :])[1]
        h_before = hprev_ref[PAD - 1:PAD, :] * not_first
        hprev = jnp.where(row == 0, h_before, pltpu.roll(hv, 1, 0))
        da = gt * hprev
        dmult = gt * (iv * xcv)
        di = gt * mult * xcv
        dxc = gt * mult * iv
        dlog_a = da * a - dmult * (1.0 - em) / mult
        dr = dlog_a * cl
        dlam_ref[...] += jnp.sum(dlog_a * rv, axis=0, keepdims=True) * (LRU_C * jax.nn.sigmoid(-lam_row))
        drp = dr * rv * (1.0 - rv)
        dip = di * iv * (1.0 - iv)
        dbr_ref[...] += jnp.sum(drp, axis=0, keepdims=True)
        dbi_ref[...] += jnp.sum(dip, axis=0, keepdims=True)
        drpb = drp.astype(BF16)
        dipb = dip.astype(BF16)
        xcb = xcv.astype(BF16)
        for n in range(NB):
            sl = slice(n * HEAD, (n + 1) * HEAD)
            dxcbuf[0:tt, sl] = dxc[:, sl] + _dot_nt(drpb[:, sl], wr_ref[n]) + _dot_nt(dipb[:, sl], wi_ref[n])
            dwr_ref[n] += _dot_tn(xcb[:, sl], drpb[:, sl])
            dwi_ref[n] += _dot_tn(xcb[:, sl], dipb[:, sl])
        dxc_all = dxcbuf[0:tt, :]
        dcb_ref[...] += jnp.sum(dxc_all, axis=0, keepdims=True)
        xbuf[0:PAD, :] = xprev_ref[...] * not_first
        xbuf[PAD:PAD + tt, :] = x_ref[...]
        dx = cw_ref[0:1, :] * dxcbuf[pl.ds(3, tt), :]
        for k in range(1, CONV_WIDTH):
            dx = dx + cw_ref[k:k + 1, :] * dxcbuf[pl.ds(3 - k, tt), :]
        dxg_ref[:, 0:DR] = dx.astype(dxg_ref.dtype)
        for k in range(CONV_WIDTH):
            dcw_ref[k:k + 1, :] += jnp.sum(xbuf[pl.ds(PAD - 3 + k, tt), :] * dxc_all, axis=0, keepdims=True)
        dxcbuf[tt:tt + PAD, :] = dxcbuf[0:PAD, :]

    rev = lambda j: pl.BlockSpec((tt, DR), lambda s: (nc - 1 - s, j))
    prev = pl.BlockSpec((PAD, DR), lambda s: (jnp.maximum((nc - 1 - s) * per - 1, 0), 0))
    vec = pl.BlockSpec((1, DR), lambda s: (0, 0))
    gate = pl.BlockSpec((NB, HEAD, HEAD), lambda s: (0, 0, 0))
    taps = pl.BlockSpec((CONV_WIDTH, DR), lambda s: (0, 0))
    vec_out = jax.ShapeDtypeStruct((1, DR), F32)
    gate_out = jax.ShapeDtypeStruct((NB, HEAD, HEAD), F32)
    return pl.pallas_call(
        body, name=name, grid=(nc,),
        in_specs=[rev(0), rev(0), rev(1), rev(0), rev(0), rev(0), rev(0), prev, prev, taps, gate, gate, vec],
        out_specs=[pl.BlockSpec((tt, 2 * DR), lambda s: (nc - 1 - s, 0)), taps, vec, gate, vec, gate, vec, vec],
        out_shape=[jax.ShapeDtypeStruct((T, 2 * DR), BF16), jax.ShapeDtypeStruct((CONV_WIDTH, DR), F32),
                   vec_out, gate_out, vec_out, gate_out, vec_out, vec_out],
        scratch_shapes=[pltpu.VMEM((tt + PAD, DR), F32), pltpu.VMEM((tt + PAD, DR), F32),
                        pltpu.VMEM((1, DR), F32), pltpu.VMEM((1, DR), F32)],
        compiler_params=_params("arbitrary"),
    )(dy, proj, proj, xc, r, i, h, proj, h, conv_w, w_r, w_i, lam.reshape(1, DR))


def _mem_probs(q, k, scale):
    s = _dot_nt(q, k) * scale
    p = jnp.exp(s - jnp.max(s, axis=-1, keepdims=True))
    return p * (1.0 / jnp.sum(p, axis=-1, keepdims=True))


def _memattn_fwd(proj, mkv, layer, DR, DM, name):
    T = proj.shape[0]
    M = mkv.shape[0]
    NH = DM // HEAD
    tt = _row_tile(T, 512)
    qcol = 2 * DR // DM
    scale = HEAD ** -0.5

    def body(q_ref, g_ref, k_ref, v_ref, y_ref):
        for n in range(NH):
            sl = slice(n * HEAD, (n + 1) * HEAD)
            p = _mem_probs(q_ref[:, sl].astype(BF16), k_ref[:, sl], scale)
            o = _dot(p.astype(BF16), v_ref[:, sl])
            g = g_ref[:, sl]
            y_ref[:, sl] = (o * (g * jax.nn.sigmoid(g))).astype(y_ref.dtype)

    return pl.pallas_call(
        body, name=name, grid=(T // tt,),
        in_specs=[pl.BlockSpec((tt, DM), lambda t: (t, qcol)), pl.BlockSpec((tt, DM), lambda t: (t, qcol + 1)),
                  pl.BlockSpec((M, DM), lambda t: (0, 2 * layer)), pl.BlockSpec((M, DM), lambda t: (0, 2 * layer + 1))],
        out_specs=pl.BlockSpec((tt, DM), lambda t: (t, 0)),
        out_shape=jax.ShapeDtypeStruct((T, DM), BF16),
        compiler_params=_params("parallel"),
    )(proj, proj, mkv, mkv)


def _memattn_bwd(dy, proj, mkv, layer, DR, DM, name):
    T = proj.shape[0]
    M = mkv.shape[0]
    NH = DM // HEAD
    tt = _row_tile(T, 512)
    qcol = 2 * DR // DM
    scale = HEAD ** -0.5

    def body(dy_ref, q_ref, g_ref, k_ref, v_ref, dqg_ref, dkv_ref):
        @pl.when(pl.program_id(0) == 0)
        def _():
            dkv_ref[...] = jnp.zeros_like(dkv_ref)

        for n in range(NH):
            sl = slice(n * HEAD, (n + 1) * HEAD)
            qb = q_ref[:, sl].astype(BF16)
            kb = k_ref[:, sl]
            vb = v_ref[:, sl]
            p = _mem_probs(qb, kb, scale)
            pb = p.astype(BF16)
            o = _dot(pb, vb)
            silu, dsilu = _silu_and_grad(g_ref[:, sl])
            dyv = dy_ref[:, sl]
            dqg_ref[:, DM + n * HEAD:DM + (n + 1) * HEAD] = (dyv * o * dsilu).astype(dqg_ref.dtype)
            dob = (dyv * silu).astype(BF16)
            dp = _dot_nt(dob, vb)
            ds = (p * (dp - jnp.sum(dp * p, axis=-1, keepdims=True)) * scale).astype(BF16)
            dqg_ref[:, sl] = _dot(ds, kb).astype(dqg_ref.dtype)
            dkv_ref[:, sl] += _dot_tn(ds, qb)
            dkv_ref[:, DM + n * HEAD:DM + (n + 1) * HEAD] += _dot_tn(pb, dob)

    return pl.pallas_call(
        body, name=name, grid=(T // tt,),
        in_specs=[pl.BlockSpec((tt, DM), lambda t: (t, DR // DM)),
                  pl.BlockSpec((tt, DM), lambda t: (t, qcol)), pl.BlockSpec((tt, DM), lambda t: (t, qcol + 1)),
                  pl.BlockSpec((M, DM), lambda t: (0, 2 * layer)), pl.BlockSpec((M, DM), lambda t: (0, 2 * layer + 1))],
        out_specs=[pl.BlockSpec((tt, 2 * DM), lambda t: (t, 0)), pl.BlockSpec((M, 2 * DM), lambda t: (0, 0))],
        out_shape=[jax.ShapeDtypeStruct((T, 2 * DM), BF16), jax.ShapeDtypeStruct((M, 2 * DM), F32)],
        compiler_params=_params("arbitrary"),
    )(dy, proj, proj, mkv, mkv)


def _sb_masks(tq):
    row = lax.broadcasted_iota(jnp.int32, (tq, tq), 0)
    col = lax.broadcasted_iota(jnp.int32, (tq, tq), 1)
    return col < row, (row > col).astype(BF16)


def _sb_weights(q, kb, later_c, causal, upper, scale, diag):
    z = _dot_nt(q, kb) * scale
    sp = _softplus(z)
    lk = jnp.where(causal, -sp, 0.0) if diag else -sp
    hi, lo = _split_bf16(lk)
    later = _dot(hi, upper) + _dot(lo, upper) + later_c
    w = jnp.exp(z - sp + later)
    if diag:
        w = jnp.where(causal, w, 0.0)
    return z, sp, lk, w


def _sb_fwd(proj, kv, DR, name):
    T = proj.shape[0]
    NH = DR // HEAD
    tq = _row_tile(T, 256)
    scale = HEAD ** -0.5

    def body(q_ref, g_ref, k_ref, v_ref, y_ref, o_ref):
        i = pl.program_id(1)
        q = q_ref[...].astype(BF16)
        causal, upper = _sb_masks(tq)

        def block(jb, later_c, acc, diag):
            start = pl.multiple_of(jb * tq, tq)
            kb = k_ref[pl.ds(start, tq), :]
            vb = v_ref[pl.ds(start, tq), :]
            _, _, lk, w = _sb_weights(q, kb, later_c, causal, upper, scale, diag)
            return later_c + jnp.sum(lk, axis=-1, keepdims=True), acc + _dot_hilo(w, vb)

        carry = block(i, jnp.zeros((tq, 1), F32), jnp.zeros((tq, HEAD), F32), True)
        _, acc = lax.fori_loop(0, i, lambda jj, c: block(i - 1 - jj, c[0], c[1], False), carry)
        o_ref[...] = acc
        g = g_ref[...]
        y_ref[...] = (acc * (g * jax.nn.sigmoid(g))).astype(y_ref.dtype)

    blk = lambda off: pl.BlockSpec((tq, HEAD), lambda h, i: (i, off + h))
    whole = lambda off: pl.BlockSpec((T, HEAD), lambda h, i: (0, off + h))
    return pl.pallas_call(
        body, name=name, grid=(NH, T // tq),
        in_specs=[blk(0), blk(NH), whole(0), whole(NH)],
        out_specs=[blk(0), blk(0)],
        out_shape=[jax.ShapeDtypeStruct((T, DR), BF16), jax.ShapeDtypeStruct((T, DR), F32)],
        compiler_params=_params("parallel", "arbitrary"),
    )(proj, proj, kv, kv)


def _sb_bwd(dy, proj, kv, o, DR, name):
    T = proj.shape[0]
    NH = DR // HEAD
    tq = _row_tile(T, 256)
    scale = HEAD ** -0.5

    def body(dy_ref, q_ref, g_ref, k_ref, v_ref, o_ref, dq_ref, dg_ref, dk_ref, dv_ref):
        i = pl.program_id(1)

        @pl.when(i == 0)
        def _():
            dk_ref[...] = jnp.zeros_like(dk_ref)
            dv_ref[...] = jnp.zeros_like(dv_ref)

        q = q_ref[...].astype(BF16)
        causal, upper = _sb_masks(tq)
        silu, dsilu = _silu_and_grad(g_ref[...])
        dyv = dy_ref[...]
        ov = o_ref[...]
        dg_ref[...] = (dyv * ov * dsilu).astype(dg_ref.dtype)
        dob = (dyv * silu).astype(BF16)
        total = jnp.sum(dob.astype(F32) * ov, axis=-1, keepdims=True)

        def block(jb, later_c, e_after, dq, diag):
            start = pl.multiple_of(jb * tq, tq)
            kb = k_ref[pl.ds(start, tq), :]
            vb = v_ref[pl.ds(start, tq), :]
            z, sp, lk, w = _sb_weights(q, kb, later_c, causal, upper, scale, diag)
            e = _dot_nt(dob, vb) * w
            hi, lo = _split_bf16(e)
            e_later = _dot(hi, upper) + _dot(lo, upper) + e_after
            before = total - e - e_later
            sig = jnp.exp(z - sp)
            dz = (e * (1.0 - sig) - before * sig) * scale
            if diag:
                dz = jnp.where(causal, dz, 0.0)
            dzb = dz.astype(BF16)
            dk_ref[pl.ds(start, tq), :] += _dot_tn(dzb, q)
            dv_ref[pl.ds(start, tq), :] += _dot_tn(w.astype(BF16), dob)
            return (later_c + jnp.sum(lk, axis=-1, keepdims=True), e_after + jnp.sum(e, axis=-1, keepdims=True),
                    dq + _dot(dzb, kb))

        zero = jnp.zeros((tq, 1), F32)
        carry = block(i, zero, zero, jnp.zeros((tq, HEAD), F32), True)
        _, _, dq = lax.fori_loop(0, i, lambda jj, c: block(i - 1 - jj, c[0], c[1], c[2], False), carry)
        dq_ref[...] = dq.astype(dq_ref.dtype)

    blk = lambda off: pl.BlockSpec((tq, HEAD), lambda h, i: (i, off + h))
    whole = lambda off: pl.BlockSpec((T, HEAD), lambda h, i: (0, off + h))
    return pl.pallas_call(
        body, name=name, grid=(NH, T // tq),
        in_specs=[blk(0), blk(0), blk(NH), whole(0), whole(NH), blk(0)],
        out_specs=[blk(0), blk(0), whole(0), whole(0)],
        out_shape=[jax.ShapeDtypeStruct((T, DR), BF16), jax.ShapeDtypeStruct((T, DR), BF16),
                   jax.ShapeDtypeStruct((T, DR), F32), jax.ShapeDtypeStruct((T, DR), F32)],
        compiler_params=_params("parallel", "arbitrary"),
    )(dy, proj, proj, kv, kv, o)


def _merge_dkv(parts, name):
    T, DR = parts[0][0].shape
    tt = _row_tile(T, 256)
    n = len(parts)

    def body(*refs):
        o_ref = refs[-1]
        dk = refs[0][...]
        dv = refs[1][...]
        for p in range(1, n):
            dk = dk + refs[2 * p][...]
            dv = dv + refs[2 * p + 1][...]
        o_ref[:, 0:DR] = dk.astype(o_ref.dtype)
        o_ref[:, DR:2 * DR] = dv.astype(o_ref.dtype)

    row = pl.BlockSpec((tt, DR), lambda t: (t, 0))
    return pl.pallas_call(
        body, name=name, grid=(T // tt,),
        in_specs=[row] * (2 * n),
        out_specs=pl.BlockSpec((tt, 2 * DR), lambda t: (t, 0)),
        out_shape=jax.ShapeDtypeStruct((T, 2 * DR), BF16),
        compiler_params=_params("parallel"),
    )(*[a for pair in parts for a in pair])


def _final_loss(h, g, target, name):
    T, D = h.shape
    tt = _row_tile(T, 256)

    def body(h_ref, g_ref, t_ref, dh_ref, dg_ref, sq_ref):
        @pl.when(pl.program_id(0) == 0)
        def _():
            dg_ref[...] = jnp.zeros_like(dg_ref)
            sq_ref[...] = jnp.zeros_like(sq_ref)

        hv = h_ref[...]
        gv = g_ref[...]
        r = lax.rsqrt(jnp.mean(hv * hv, axis=-1, keepdims=True) + EPS)
        xhat = hv * r
        err = xhat * gv - t_ref[...]
        sq_ref[...] += jnp.sum(err * err, axis=0, keepdims=True)
        dy = err * (1.0 / D)
        dxhat = dy * gv
        dh_ref[...] = r * (dxhat - xhat * jnp.mean(dxhat * xhat, axis=-1, keepdims=True))
        dg_ref[...] += jnp.sum(dy * xhat, axis=0, keepdims=True)

    row = pl.BlockSpec((tt, D), lambda i: (i, 0))
    vec = pl.BlockSpec((1, D), lambda i: (0, 0))
    return pl.pallas_call(
        body, name=name, grid=(T // tt,),
        in_specs=[row, vec, row],
        out_specs=[row, vec, vec],
        out_shape=[jax.ShapeDtypeStruct((T, D), F32), jax.ShapeDtypeStruct((1, D), F32),
                   jax.ShapeDtypeStruct((1, D), F32)],
        compiler_params=_params("arbitrary"),
    )(h, g.reshape(1, D), target)


def _sum_parts(parts_ref):
    g = parts_ref[0]
    for s in range(1, parts_ref.shape[0]):
        g = g + parts_ref[s].astype(F32)
    return g


def _adamw(parts, w, m, v, name):
    P, R, C = parts.shape
    tr = _rows_for(R, P * C * 4)

    def body(p_ref, w_ref, m_ref, v_ref, g_ref, d_ref, nm_ref, nv_ref):
        g = _sum_parts(p_ref)
        nm = ADAM_B1 * m_ref[...] + (1.0 - ADAM_B1) * g
        nv = ADAM_B2 * v_ref[...] + (1.0 - ADAM_B2) * jnp.square(g)
        m_hat = nm / (1.0 - ADAM_B1 ** ADAM_STEP)
        v_hat = nv / (1.0 - ADAM_B2 ** ADAM_STEP)
        g_ref[...] = g
        d_ref[...] = -ADAM_LR * (m_hat / (jnp.sqrt(v_hat) + ADAM_EPS) + ADAM_WD * w_ref[...])
        nm_ref[...] = nm
        nv_ref[...] = nv

    row = pl.BlockSpec((tr, C), lambda i: (i, 0))
    out = jax.ShapeDtypeStruct((R, C), F32)
    return pl.pallas_call(
        body, name=name, grid=(R // tr,),
        in_specs=[pl.BlockSpec((P, tr, C), lambda i: (0, i, 0)), row, row, row],
        out_specs=[row] * 4,
        out_shape=[out] * 4,
        compiler_params=_params("parallel"),
    )(parts, w, m, v)


def _sum_devices(parts, name):
    P, R, C = parts.shape
    tr = _rows_for(R, P * C * 4)

    def body(p_ref, o_ref):
        o_ref[...] = _sum_parts(p_ref)

    return pl.pallas_call(
        body, name=name, grid=(R // tr,),
        in_specs=[pl.BlockSpec((P, tr, C), lambda i: (0, i, 0))],
        out_specs=pl.BlockSpec((tr, C), lambda i: (i, 0)),
        out_shape=jax.ShapeDtypeStruct((R, C), F32),
        compiler_params=_params("parallel"),
    )(parts)


def _mesh_position():
    return lax.axis_index("x"), lax.axis_index("y"), lax.axis_index("c")


def _device_index(p):
    return 4 * p[0] + 2 * p[1] + p[2]


def _all_gather(arrs, name):
    n = len(arrs)

    def body(*refs):
        ins, outs = refs[:n], refs[n:2 * n]
        send_sems, recv_sems, local_sems = refs[2 * n:]
        x, y, c = _mesh_position()
        me, sibling = (x, y, c), (x, y, 1 - c)
        chips = [(1 - x, y), (x, 1 - y), (1 - x, 1 - y)]

        def slot(a, p):
            return outs[a].at[_device_index(p)]

        def copy(a, k, block, to, src=None):
            return pltpu.make_async_remote_copy(
                src_ref=slot(a, block) if src is None else src, dst_ref=slot(a, block),
                send_sem=send_sems.at[a, k], recv_sem=recv_sems.at[a, k],
                device_id=to, device_id_type=pl.DeviceIdType.MESH)

        mine = [pltpu.make_async_copy(ins[a], slot(a, me), local_sems.at[a]) for a in range(n)]
        for cp in mine:
            cp.start()
        first = []
        for a in range(n):
            first.append(copy(a, 0, me, sibling, src=ins[a]))
            first += [copy(a, 1 + j, me, (*chip, c), src=ins[a]) for j, chip in enumerate(chips)]
        for cp in first:
            cp.start()
        passed = []
        for a in range(n):
            for j, chip in enumerate(chips):
                copy(a, 1 + j, (*chip, c), me).wait_recv()
                fwd = copy(a, 4 + j, (*chip, c), sibling)
                fwd.start()
                passed.append(fwd)
        for a in range(n):
            copy(a, 0, sibling, me).wait_recv()
            for j, chip in enumerate(chips):
                copy(a, 4 + j, (*chip, 1 - c), me).wait_recv()
        for cp in first + passed:
            cp.wait_send()
        for cp in mine:
            cp.wait()

    return pl.pallas_call(
        body, name=name,
        in_specs=[HBM_SPEC] * n, out_specs=[HBM_SPEC] * n,
        out_shape=[jax.ShapeDtypeStruct((N_DEV,) + a.shape, a.dtype) for a in arrs],
        scratch_shapes=[pltpu.SemaphoreType.DMA((n, 7)), pltpu.SemaphoreType.DMA((n, 7)),
                        pltpu.SemaphoreType.DMA((n,))],
    )(*arrs)


def _all_to_all(arrs, name):
    n = len(arrs)

    def body(*refs):
        ins, outs = refs[:n], refs[n:2 * n]
        send_sems, recv_sems, local_sems = refs[2 * n:]
        x, y, c = _mesh_position()
        me = _device_index((x, y, c))
        peers = [(1 - x if k & 4 else x, 1 - y if k & 2 else y, 1 - c if k & 1 else c) for k in range(1, N_DEV)]
        local = [pltpu.make_async_copy(ins[a].at[me], outs[a].at[me], local_sems.at[a]) for a in range(n)]
        for cp in local:
            cp.start()
        sends, recvs = [], []
        for a in range(n):
            for k, peer in enumerate(peers):
                there = _device_index(peer)
                sems = dict(send_sem=send_sems.at[a, k], recv_sem=recv_sems.at[a, k],
                            device_id=peer, device_id_type=pl.DeviceIdType.MESH)
                sends.append(pltpu.make_async_remote_copy(src_ref=ins[a].at[there], dst_ref=outs[a].at[me], **sems))
                recvs.append(pltpu.make_async_remote_copy(src_ref=ins[a].at[there], dst_ref=outs[a].at[there], **sems))
        for cp in sends:
            cp.start()
        for cp in recvs:
            cp.wait_recv()
        for cp in sends:
            cp.wait_send()
        for cp in local:
            cp.wait()

    return pl.pallas_call(
        body, name=name,
        in_specs=[HBM_SPEC] * n, out_specs=[HBM_SPEC] * n,
        out_shape=[jax.ShapeDtypeStruct(a.shape, a.dtype) for a in arrs],
        scratch_shapes=[pltpu.SemaphoreType.DMA((n, 7)), pltpu.SemaphoreType.DMA((n, 7)),
                        pltpu.SemaphoreType.DMA((n,))],
    )(*arrs)


def _pack(arrs, row_multiple):
    parts = []
    rows = 0
    for a in arrs:
        flat = a.reshape(-1).astype(F32)
        r = -(-flat.shape[0] // (8 * LANES)) * 8
        parts.append(jnp.pad(flat, (0, r * LANES - flat.shape[0])).reshape(r, LANES))
        rows += r
    pad = -rows % row_multiple
    if pad:
        parts.append(jnp.zeros((pad, LANES), F32))
    return jnp.concatenate(parts, axis=0)


def _unpack(buf, shapes, lead=()):
    out = []
    r0 = 0
    for shape in shapes:
        size = 1
        for s in shape:
            size *= s
        r = -(-size // (8 * LANES)) * 8
        part = buf[..., r0:r0 + r, :].reshape(lead + (r * LANES,))[..., :size]
        out.append(part.reshape(lead + tuple(shape)))
        r0 += r
    return out


def _gathered_cols(g):
    g = jnp.moveaxis(g, 0, -2)
    return g.reshape(g.shape[:-2] + (g.shape[-2] * g.shape[-1],))


def kernel(x, mem, mem_norm, w_mem_kv, norm_a, w_in_a, conv_w, conv_b, w_rec_gate, b_rec_gate, w_in_gate, b_in_gate, lru_lambda, w_out_a, kv_norm, w_kv, norm_b, w_in_b, w_out_b, final_norm, loss_target, m_mem_norm, m_w_mem_kv, m_norm_a, m_w_in_a, m_conv_w, m_conv_b, m_w_rec_gate, m_b_rec_gate, m_w_in_gate, m_b_in_gate, m_lru_lambda, m_w_out_a, m_kv_norm, m_w_kv, m_norm_b, m_w_in_b, m_w_out_b, m_final_norm, v_mem_norm, v_w_mem_kv, v_norm_a, v_w_in_a, v_conv_w, v_conv_b, v_w_rec_gate, v_b_rec_gate, v_w_in_gate, v_b_in_gate, v_lru_lambda, v_w_out_a, v_kv_norm, v_w_kv, v_norm_b, v_w_in_b, v_w_out_b, v_final_norm):
    xs = x[0]
    T, D = xs.shape
    L = w_mem_kv.shape[0]
    NA = w_in_a.shape[0]
    NB = w_in_b.shape[0]
    DM2 = w_mem_kv.shape[2]
    DM = DM2 // 2
    DR = w_rec_gate.shape[1] * w_rec_gate.shape[2]
    me = _device_index(_mesh_position())

    small_sharded = [norm_a, conv_w, conv_b, b_rec_gate, b_in_gate, lru_lambda]
    big = [w_mem_kv.reshape(-1, DM2)]
    big += [w_in_a[l] for l in range(NA)] + [w_out_a[l] for l in range(NA)] + [w_kv]
    big += [w_in_b[l] for l in range(NB)] + [w_out_b[l] for l in range(NB)]
    gathered = _all_gather([w.astype(BF16) for w in big] + [_pack(small_sharded, 8)], "gather_params")
    it = iter(gathered[:-1])
    g_mem_kv = next(it)
    g_in_a = [next(it) for _ in range(NA)]
    g_out_a = [next(it).reshape(1, -1, D) for _ in range(NA)]
    g_kv = next(it)
    g_in_b = [next(it) for _ in range(NB)]
    g_out_b = [next(it).reshape(1, -1, D) for _ in range(NB)]
    g_mem_kv = jnp.swapaxes(g_mem_kv.reshape(N_DEV, L, D // N_DEV, DM2), 0, 1).reshape(L, D, DM2)
    norm_a_f, conv_w_f, conv_b_f, b_r_f, b_i_f, lam_f = [
        _gathered_cols(s) for s in _unpack(gathered[-1], [s.shape for s in small_sharded], lead=(N_DEV,))]
    w_r_bf = w_rec_gate.astype(BF16)
    w_i_bf = w_in_gate.astype(BF16)

    zeros_mem = jnp.zeros_like(mem[0])
    mem_n = _rms_fwd(mem[0], mem_norm, "rms_mem")
    mkv = _mm_nn(mem_n, g_mem_kv, out_dtype=BF16, name="mm_mem_kv")

    h = xs
    saved_a = []
    for l in range(NA):
        u = _rms_fwd(h, norm_a_f[l], f"rms_a{l}")
        proj = _mm_nn(u, g_in_a[l], name=f"mm_in_a{l}")
        y_rnn, xc, r, i, hr = _rglru_fwd(proj, conv_w_f[l], conv_b_f[l], w_r_bf[l], b_r_f[l], w_i_bf[l], b_i_f[l],
                                         lam_f[l], f"rglru_fwd{l}")
        y_mem = _memattn_fwd(proj, mkv, l, DR, DM, f"memattn_fwd_a{l}")
        ycat = jnp.concatenate([y_rnn, y_mem], axis=-1)
        h_next = _mm_nn(ycat, g_out_a[l], res=h, name=f"mm_out_a{l}")
        saved_a.append((h, u, proj, xc, r, i, hr, ycat))
        h = h_next
    h_kv = h
    u_kv = _rms_fwd(h_kv, kv_norm, "rms_kv")
    kv = _mm_nn(u_kv, g_kv, out_dtype=BF16, name="mm_kv")
    saved_b = []
    for j in range(NB):
        u = _rms_fwd(h, norm_b[j], f"rms_b{j}")
        proj = _mm_nn(u, g_in_b[j], name=f"mm_in_b{j}")
        y_sb, o_sb = _sb_fwd(proj, kv, DR, f"sb_fwd{j}")
        y_mem = _memattn_fwd(proj, mkv, NA + j, DR, DM, f"memattn_fwd_b{j}")
        ycat = jnp.concatenate([y_sb, y_mem], axis=-1)
        h_next = _mm_nn(ycat, g_out_b[j], res=h, name=f"mm_out_b{j}")
        saved_b.append((h, u, proj, o_sb, ycat))
        h = h_next

    dh, d_final_norm, sq = _final_loss(h, final_norm, loss_target[0], "final_loss")
    loss = lax.psum(0.5 * jnp.sum(sq) / D, ("x", "y", "c"))

    big_grads = {}
    dmkv = [None] * L
    d_norm_b = [None] * NB
    dkv_parts = []
    for j in reversed(range(NB)):
        h_in, u, proj, o_sb, ycat = saved_b[j]
        dy = _mm_nt(dh, g_out_b[j], name=f"mm_dy_b{j}")
        big_grads[f"out_b{j}"] = _mm_tn(ycat, dh, 1, name=f"mm_dw_out_b{j}").reshape(N_DEV, -1, D)
        dq, dg, dk, dv = _sb_bwd(dy, proj, kv, o_sb, DR, f"sb_bwd{j}")
        dkv_parts.append((dk, dv))
        dqg_mem, dmkv[NA + j] = _memattn_bwd(dy, proj, mkv, NA + j, DR, DM, f"memattn_bwd_b{j}")
        dproj = jnp.concatenate([dq, dg, dqg_mem], axis=-1)
        du = _mm_nt(dproj, g_in_b[j], name=f"mm_du_b{j}")
        big_grads[f"in_b{j}"] = _mm_tn(u, dproj, N_DEV, name=f"mm_dw_in_b{j}")
        dh, d_norm_b[j] = _rms_bwd(du, h_in, norm_b[j], dh, f"rms_bwd_b{j}")

    dkv = _merge_dkv(dkv_parts, "merge_dkv")
    du_kv = _mm_nt(dkv, g_kv, name="mm_du_kv")
    big_grads["kv"] = _mm_tn(u_kv, dkv, N_DEV, name="mm_dw_kv")
    dh, d_kv_norm = _rms_bwd(du_kv, h_kv, kv_norm, dh, "rms_bwd_kv")

    d_norm_a, d_conv_w, d_conv_b, d_w_r, d_b_r, d_w_i, d_b_i, d_lam = ([None] * NA for _ in range(8))
    for l in reversed(range(NA)):
        h_in, u, proj, xc, r, i, hr, ycat = saved_a[l]
        dy = _mm_nt(dh, g_out_a[l], name=f"mm_dy_a{l}")
        big_grads[f"out_a{l}"] = _mm_tn(ycat, dh, 1, name=f"mm_dw_out_a{l}").reshape(N_DEV, -1, D)
        (dxg, d_conv_w[l], d_conv_b[l], d_w_r[l], d_b_r[l], d_w_i[l], d_b_i[l], d_lam[l]) = _rglru_bwd(
            dy, proj, xc, r, i, hr, conv_w_f[l], w_r_bf[l], w_i_bf[l], lam_f[l], f"rglru_bwd{l}")
        dqg_mem, dmkv[l] = _memattn_bwd(dy, proj, mkv, l, DR, DM, f"memattn_bwd_a{l}")
        dproj = jnp.concatenate([dxg, dqg_mem], axis=-1)
        du = _mm_nt(dproj, g_in_a[l], name=f"mm_du_a{l}")
        big_grads[f"in_a{l}"] = _mm_tn(u, dproj, N_DEV, name=f"mm_dw_in_a{l}")
        dh, d_norm_a[l] = _rms_bwd(du, h_in, norm_a_f[l], dh, f"rms_bwd_a{l}")
    grad_x = dh.reshape(x.shape)

    dmkv_all = jnp.concatenate(dmkv, axis=-1)
    dmem_n = _mm_nt(dmkv_all, g_mem_kv, name="mm_dmem")
    d_w_mem = _mm_tn(mem_n, dmkv_all, L, name="mm_dw_mem_kv")
    big_grads["mem_kv"] = jnp.swapaxes(d_w_mem.reshape(L, N_DEV, D // N_DEV, DM2), 0, 1).reshape(N_DEV, -1, DM2)
    _, d_mem_norm = _rms_bwd(dmem_n, mem[0], mem_norm, zeros_mem, "rms_bwd_mem")

    names = (["mem_kv"] + [f"in_a{l}" for l in range(NA)] + [f"out_a{l}" for l in range(NA)] + ["kv"]
             + [f"in_b{j}" for j in range(NB)] + [f"out_b{j}" for j in range(NB)])
    received = dict(zip(names, _all_to_all([big_grads[k] for k in names], "scatter_grads")))

    def update(key, w, m, v):
        shape = w.shape
        two_d = (-1, shape[-1])
        return [o.reshape(shape) for o in _adamw(received[key], w.reshape(two_d), m.reshape(two_d),
                                                 v.reshape(two_d), f"adamw_{key}")]

    def update_layers(prefix, w, m, v):
        per_layer = [update(f"{prefix}{l}", w[l], m[l], v[l]) for l in range(w.shape[0])]
        return [jnp.stack([per_layer[l][k] for l in range(w.shape[0])]) for k in range(4)]

    upd = {
        "w_mem_kv": update("mem_kv", w_mem_kv, m_w_mem_kv, v_w_mem_kv),
        "w_in_a": update_layers("in_a", w_in_a, m_w_in_a, v_w_in_a),
        "w_out_a": update_layers("out_a", w_out_a, m_w_out_a, v_w_out_a),
        "w_kv": update("kv", w_kv, m_w_kv, v_w_kv),
        "w_in_b": update_layers("in_b", w_in_b, m_w_in_b, v_w_in_b),
        "w_out_b": update_layers("out_b", w_out_b, m_w_out_b, v_w_out_b),
    }

    small_full = {
        "mem_norm": d_mem_norm.reshape(-1),
        "norm_a": jnp.concatenate(d_norm_a, axis=0),
        "conv_w": jnp.stack(d_conv_w),
        "conv_b": jnp.concatenate(d_conv_b, axis=0),
        "w_rec_gate": jnp.stack(d_w_r),
        "b_rec_gate": jnp.concatenate(d_b_r, axis=0),
        "w_in_gate": jnp.stack(d_w_i),
        "b_in_gate": jnp.concatenate(d_b_i, axis=0),
        "lru_lambda": jnp.concatenate(d_lam, axis=0),
        "kv_norm": d_kv_norm.reshape(-1),
        "norm_b": jnp.concatenate(d_norm_b, axis=0),
        "final_norm": d_final_norm.reshape(-1),
    }
    small_names = list(small_full)
    (small_gathered,) = _all_gather([_pack([small_full[k] for k in small_names], 256)], "gather_small_grads")
    small_sum = _sum_devices(small_gathered, "sum_small_grads")
    small_grad = dict(zip(small_names, _unpack(small_sum, [small_full[k].shape for k in small_names])))
    small_w = {"mem_norm": (mem_norm, m_mem_norm, v_mem_norm), "norm_a": (norm_a, m_norm_a, v_norm_a),
               "conv_w": (conv_w, m_conv_w, v_conv_w), "conv_b": (conv_b, m_conv_b, v_conv_b),
               "w_rec_gate": (w_rec_gate, m_w_rec_gate, v_w_rec_gate),
               "b_rec_gate": (b_rec_gate, m_b_rec_gate, v_b_rec_gate),
               "w_in_gate": (w_in_gate, m_w_in_gate, v_w_in_gate), "b_in_gate": (b_in_gate, m_b_in_gate, v_b_in_gate),
               "lru_lambda": (lru_lambda, m_lru_lambda, v_lru_lambda), "kv_norm": (kv_norm, m_kv_norm, v_kv_norm),
               "norm_b": (norm_b, m_norm_b, v_norm_b), "final_norm": (final_norm, m_final_norm, v_final_norm)}
    for k in small_names:
        w = small_w[k][0]
        if small_grad[k].shape != w.shape:
            n = w.shape[-1]
            small_grad[k] = lax.dynamic_slice_in_dim(small_grad[k], me * n, n, axis=-1)
    small_shapes = [small_w[k][0].shape for k in small_names]
    packed = [_pack([small_grad[k] for k in small_names], 256)[None]]
    packed += [_pack([small_w[k][t] for k in small_names], 256) for t in range(3)]
    small_out = [_unpack(o, small_shapes) for o in _adamw(*packed, "adamw_small")]
    for idx, k in enumerate(small_names):
        upd[k] = [small_out[t][idx] for t in range(4)]

    order = ["mem_norm", "w_mem_kv", "norm_a", "w_in_a", "conv_w", "conv_b", "w_rec_gate", "b_rec_gate", "w_in_gate",
             "b_in_gate", "lru_lambda", "w_out_a", "kv_norm", "w_kv", "norm_b", "w_in_b", "w_out_b", "final_norm"]
    return (loss, grad_x, *[upd[k][0] for k in order], *[upd[k][1] for k in order],
            *[upd[k][2] for k in order], *[upd[k][3] for k in order])
```

```python
import functools

import jax
import jax.numpy as jnp
from jax import lax
from jax.experimental import pallas as pl
from jax.experimental.pallas import tpu as pltpu

F32 = jnp.float32
BF16 = jnp.bfloat16

N_DEV = 8
EPS = 1e-6
LRU_C = 8.0
HEAD = 128
CONV_WIDTH = 4
LANES = 128
VMEM_LIMIT = 56 * 1024 * 1024

ADAM_LR = 0.001
ADAM_B1 = 0.9
ADAM_B2 = 0.999
ADAM_EPS = 1e-08
ADAM_WD = 0.01
ADAM_STEP = 10

HBM_SPEC = pl.BlockSpec(memory_space=pltpu.HBM)


def _params(*semantics):
    return pltpu.CompilerParams(dimension_semantics=semantics, vmem_limit_bytes=VMEM_LIMIT)


def _tile(n, cap):
    if n <= cap:
        return n
    t = cap - cap % LANES
    while n % t:
        t -= LANES
    return t


def _row_tile(n, cap):
    if n <= cap:
        return n
    t = cap - cap % 8
    while t >= 8:
        if n % t == 0:
            return t
        t -= 8
    return n


def _rows_for(n, bytes_per_row, budget=2 * 1024 * 1024):
    return _row_tile(n, max(8, budget // bytes_per_row))


def _dot(a, b):
    return lax.dot_general(a, b, (((1,), (0,)), ((), ())), preferred_element_type=F32)


def _dot_nt(a, b):
    return lax.dot_general(a, b, (((1,), (1,)), ((), ())), preferred_element_type=F32)


def _dot_tn(a, b):
    return lax.dot_general(a, b, (((0,), (0,)), ((), ())), preferred_element_type=F32)


def _split_bf16(x):
    hi = x.astype(BF16)
    lo = (x - hi.astype(F32)).astype(BF16)
    return hi, lo


def _dot_hilo(x, b):
    hi, lo = _split_bf16(x)
    return _dot(hi, b) + _dot(lo, b)


def _softplus(x):
    return jnp.maximum(x, 0.0) + jnp.log1p(jnp.exp(-jnp.abs(x)))


def _silu_and_grad(g):
    sg = jax.nn.sigmoid(g)
    return g * sg, sg * (1.0 + g * (1.0 - sg))


def _neg_expm1(x):
    series = -x * (1.0 + x * (0.5 + x * (1.0 / 6.0 + x * (1.0 / 24.0 + x * (1.0 / 120.0)))))
    return jnp.where(x > -0.1, series, 1.0 - jnp.exp(x))


def _rms_fwd(x, g, name):
    T, D = x.shape
    tt = _row_tile(T, 512)

    def body(x_ref, g_ref, o_ref):
        xv = x_ref[...]
        r = lax.rsqrt(jnp.mean(xv * xv, axis=-1, keepdims=True) + EPS)
        o_ref[...] = ((xv * r) * g_ref[...]).astype(o_ref.dtype)

    return pl.pallas_call(
        body, name=name, grid=(T // tt,),
        in_specs=[pl.BlockSpec((tt, D), lambda i: (i, 0)), pl.BlockSpec((1, D), lambda i: (0, 0))],
        out_specs=pl.BlockSpec((tt, D), lambda i: (i, 0)),
        out_shape=jax.ShapeDtypeStruct((T, D), BF16),
        compiler_params=_params("parallel"),
    )(x, g.reshape(1, D))


def _rms_bwd(du, h, g, res, name):
    T, D = h.shape
    tt = _row_tile(T, 256)

    def body(du_ref, h_ref, g_ref, res_ref, dh_ref, dg_ref):
        @pl.when(pl.program_id(0) == 0)
        def _():
            dg_ref[...] = jnp.zeros_like(dg_ref)

        hv = h_ref[...]
        duv = du_ref[...]
        r = lax.rsqrt(jnp.mean(hv * hv, axis=-1, keepdims=True) + EPS)
        xhat = hv * r
        dxhat = duv * g_ref[...]
        dh_ref[...] = res_ref[...] + r * (dxhat - xhat * jnp.mean(dxhat * xhat, axis=-1, keepdims=True))
        dg_ref[...] += jnp.sum(duv * xhat, axis=0, keepdims=True)

    row = pl.BlockSpec((tt, D), lambda i: (i, 0))
    vec = pl.BlockSpec((1, D), lambda i: (0, 0))
    return pl.pallas_call(
        body, name=name, grid=(T // tt,),
        in_specs=[row, row, vec, row],
        out_specs=[row, vec],
        out_shape=[jax.ShapeDtypeStruct((T, D), F32), jax.ShapeDtypeStruct((1, D), F32)],
        compiler_params=_params("arbitrary"),
    )(du, h, g.reshape(1, D), res)


def _mm_nn(a, b3, res=None, out_dtype=F32, name=None):
    M, K = a.shape
    S, _, ns = b3.shape
    tm = _row_tile(M, 1024 if a.dtype == BF16 else 512)
    tn = _tile(ns, 512)
    nj = ns // tn

    def body(*refs):
        if res is None:
            a_ref, b_ref, o_ref = refs
        else:
            a_ref, b_ref, r_ref, o_ref = refs
        acc = _dot(a_ref[...].astype(BF16), b_ref[...])
        if res is not None:
            acc = acc + r_ref[...]
        o_ref[...] = acc.astype(o_ref.dtype)

    in_specs = [pl.BlockSpec((tm, K), lambda i, j: (i, 0)),
                pl.BlockSpec((None, K, tn), lambda i, j: (j // nj, 0, j % nj))]
    args = [a, b3]
    if res is not None:
        in_specs.append(pl.BlockSpec((tm, tn), lambda i, j: (i, j)))
        args.append(res)
    return pl.pallas_call(
        body, name=name, grid=(M // tm, S * nj),
        in_specs=in_specs,
        out_specs=pl.BlockSpec((tm, tn), lambda i, j: (i, j)),
        out_shape=jax.ShapeDtypeStruct((M, S * ns), out_dtype),
        compiler_params=_params("parallel", "parallel"),
    )(*args)


def _mm_nt(a, b3, out_dtype=F32, name=None):
    M = a.shape[0]
    S, N, ns = b3.shape
    tm = _row_tile(M, 1024 if ns <= 1024 else 512)
    tn = _tile(N, 512)

    def body(a_ref, b_ref, o_ref, acc_ref):
        s = pl.program_id(2)
        p = _dot_nt(a_ref[...].astype(BF16), b_ref[...])

        @pl.when(s == 0)
        def _():
            acc_ref[...] = p

        @pl.when(s > 0)
        def _():
            acc_ref[...] += p

        @pl.when(s == S - 1)
        def _():
            o_ref[...] = acc_ref[...].astype(o_ref.dtype)

    return pl.pallas_call(
        body, name=name, grid=(M // tm, N // tn, S),
        in_specs=[pl.BlockSpec((tm, ns), lambda i, j, s: (i, s)),
                  pl.BlockSpec((None, tn, ns), lambda i, j, s: (s, j, 0))],
        out_specs=pl.BlockSpec((tm, tn), lambda i, j, s: (i, j)),
        out_shape=jax.ShapeDtypeStruct((M, N), out_dtype),
        scratch_shapes=[pltpu.VMEM((tm, tn), F32)],
        compiler_params=_params("parallel", "parallel", "arbitrary"),
    )(a, b3)


def _mm_tn(a, b, S, name=None):
    T, K = a.shape
    ns = b.shape[1] // S
    tk = _tile(K, 1024)
    tn = _tile(ns, 512)
    nj = ns // tn
    tt = _row_tile(T, 512)
    nt = T // tt

    def body(a_ref, b_ref, o_ref, acc_ref):
        t = pl.program_id(2)
        p = _dot_tn(a_ref[...].astype(BF16), b_ref[...].astype(BF16))

        @pl.when(t == 0)
        def _():
            acc_ref[...] = p

        @pl.when(t > 0)
        def _():
            acc_ref[...] += p

        @pl.when(t == nt - 1)
        def _():
            o_ref[...] = acc_ref[...].astype(o_ref.dtype)

    return pl.pallas_call(
        body, name=name, grid=(K // tk, S * nj, nt),
        in_specs=[pl.BlockSpec((tt, tk), lambda i, j, t: (t, i)),
                  pl.BlockSpec((tt, tn), lambda i, j, t: (t, j))],
        out_specs=pl.BlockSpec((None, tk, tn), lambda i, j, t: (j // nj, i, j % nj)),
        out_shape=jax.ShapeDtypeStruct((S, K, ns), BF16),
        scratch_shapes=[pltpu.VMEM((tk, tn), F32)],
        compiler_params=_params("parallel", "parallel", "arbitrary"),
    )(a, b)


def _lru_coeffs(lam_row, r):
    cl = -LRU_C * _softplus(-lam_row)
    log_a = cl * r
    a = jnp.exp(log_a)
    em = _neg_expm1(2.0 * log_a)
    return cl, a, em, jnp.sqrt(em)


def _rglru_fwd(proj, conv_w, conv_b, w_r, b_r, w_i, b_i, lam, name):
    T = proj.shape[0]
    DR = conv_w.shape[1]
    NB = DR // HEAD
    tt = _row_tile(T, 256)
    PAD = 8

    def body(x_ref, g_ref, cw_ref, cb_ref, wr_ref, br_ref, wi_ref, bi_ref, lam_ref,
             y_ref, xc_ref, r_ref, i_ref, h_ref, xbuf, hcar):
        @pl.when(pl.program_id(0) == 0)
        def _():
            xbuf[0:PAD, :] = jnp.zeros((PAD, DR), F32)
            hcar[...] = jnp.zeros_like(hcar)

        xbuf[PAD:PAD + tt, :] = x_ref[...]
        xc = cb_ref[...] + cw_ref[0:1, :] * xbuf[pl.ds(PAD - 3, tt), :]
        for k in range(1, CONV_WIDTH):
            xc = xc + cw_ref[k:k + 1, :] * xbuf[pl.ds(PAD - 3 + k, tt), :]
        xbuf[0:PAD, :] = xbuf[tt:tt + PAD, :]
        xc_ref[...] = xc
        xcb = xc.astype(BF16)
        for n in range(NB):
            sl = slice(n * HEAD, (n + 1) * HEAD)
            r_ref[:, sl] = jax.nn.sigmoid(_dot(xcb[:, sl], wr_ref[n]) + br_ref[:, sl])
            i_ref[:, sl] = jax.nn.sigmoid(_dot(xcb[:, sl], wi_ref[n]) + bi_ref[:, sl])
        _, a, _, mult = _lru_coeffs(lam_ref[...], r_ref[...])
        hs = mult * (i_ref[...] * xc)
        row = lax.broadcasted_iota(jnp.int32, (tt, DR), 0)
        d = 1
        while d < tt:
            keep = row >= d
            a_sh = jnp.where(keep, pltpu.roll(a, d, 0), 1.0)
            h_sh = jnp.where(keep, pltpu.roll(hs, d, 0), 0.0)
            hs = a * h_sh + hs
            a = a * a_sh
            d *= 2
        h = hs + a * hcar[...]
        h_ref[...] = h
        hcar[...] = h_ref[tt - 1:tt, :]
        g = g_ref[...]
        y_ref[...] = (h * (g * jax.nn.sigmoid(g))).astype(y_ref.dtype)

    col = lambda j: pl.BlockSpec((tt, DR), lambda c: (c, j))
    vec = pl.BlockSpec((1, DR), lambda c: (0, 0))
    gate = pl.BlockSpec((NB, HEAD, HEAD), lambda c: (0, 0, 0))
    f32_out = jax.ShapeDtypeStruct((T, DR), F32)
    return pl.pallas_call(
        body, name=name, grid=(T // tt,),
        in_specs=[col(0), col(1), pl.BlockSpec((CONV_WIDTH, DR), lambda c: (0, 0)), vec, gate, vec, gate, vec, vec],
        out_specs=[col(0)] * 5,
        out_shape=[jax.ShapeDtypeStruct((T, DR), BF16), f32_out, f32_out, f32_out, f32_out],
        scratch_shapes=[pltpu.VMEM((tt + PAD, DR), F32), pltpu.VMEM((1, DR), F32)],
        compiler_params=_params("arbitrary"),
    )(proj, proj, conv_w, conv_b.reshape(1, DR), w_r, b_r.reshape(1, DR), w_i, b_i.reshape(1, DR),
      lam.reshape(1, DR))


def _rglru_bwd(dy, proj, xc, r, i, h, conv_w, w_r, w_i, lam, name):
    T = proj.shape[0]
    DR = conv_w.shape[1]
    NB = DR // HEAD
    tt = _row_tile(T, 128)
    nc = T // tt
    PAD = 8
    per = tt // PAD

    def body(dy_ref, x_ref, g_ref, xc_ref, r_ref, i_ref, h_ref, xprev_ref, hprev_ref,
             cw_ref, wr_ref, wi_ref, lam_ref,
             dxg_ref, dcw_ref, dcb_ref, dwr_ref, dbr_ref, dwi_ref, dbi_ref, dlam_ref,
             xbuf, dxcbuf, gcar, acar):
        step = pl.program_id(0)
        chunk = nc - 1 - step

        @pl.when(step == 0)
        def _():
            for ref in (dcw_ref, dcb_ref, dwr_ref, dbr_ref, dwi_ref, dbi_ref, dlam_ref, gcar, acar):
                ref[...] = jnp.zeros_like(ref)
            dxcbuf[tt:tt + PAD, :] = jnp.zeros((PAD, DR), F32)

        not_first = (chunk > 0).astype(F32)
        row = lax.broadcasted_iota(jnp.int32, (tt, DR), 0)
        silu, dsilu = _silu_and_grad(g_ref[...])
        dyv = dy_ref[...]
        hv = h_ref[...]
        dxg_ref[:, DR:2 * DR] = (dyv * hv * dsilu).astype(dxg_ref.dtype)
        dh = dyv * silu
        rv = r_ref[...]
        iv = i_ref[...]
        xcv = xc_ref[...]
        lam_row = lam_ref[...]
        cl, a, em, mult = _lru_coeffs(lam_row, rv)
        b = jnp.where(row == tt - 1, acar[...], pltpu.roll(a, tt - 1, 0))
        gs = dh
        d = 1
        while d < tt:
            keep = row < tt - d
            b_sh = jnp.where(keep, pltpu.roll(b, tt - d, 0), 1.0)
            g_sh = jnp.where(keep, pltpu.roll(gs, tt - d, 0), 0.0)
            gs = gs + b * g_sh
            b = b * b_sh
            d *= 2
        gt = gs + b * gcar[...]
        xbuf[0:tt, :] = gt
        gcar[...] = xbuf[0:1, :]
        acar[...] = _lru_coeffs(lam_row, r_ref[0:1, :])[1]
        h_before = hprev_ref[PAD - 1:PAD, :] * not_first
        hprev = jnp.where(row == 0, h_before, pltpu.roll(hv, 1, 0))
        da = gt * hprev
        dmult = gt * (iv * xcv)
        di = gt * mult * xcv
        dxc = gt * mult * iv
        dlog_a = da * a - dmult * (1.0 - em) / mult
        dr = dlog_a * cl
        dlam_ref[...] += jnp.sum(dlog_a * rv, axis=0, keepdims=True) * (LRU_C * jax.nn.sigmoid(-lam_row))
        drp = dr * rv * (1.0 - rv)
        dip = di * iv * (1.0 - iv)
        dbr_ref[...] += jnp.sum(drp, axis=0, keepdims=True)
        dbi_ref[...] += jnp.sum(dip, axis=0, keepdims=True)
        drpb = drp.astype(BF16)
        dipb = dip.astype(BF16)
        xcb = xcv.astype(BF16)
        for n in range(NB):
            sl = slice(n * HEAD, (n + 1) * HEAD)
            dxcbuf[0:tt, sl] = dxc[:, sl] + _dot_nt(drpb[:, sl], wr_ref[n]) + _dot_nt(dipb[:, sl], wi_ref[n])
            dwr_ref[n] += _dot_tn(xcb[:, sl], drpb[:, sl])
            dwi_ref[n] += _dot_tn(xcb[:, sl], dipb[:, sl])
        dxc_all = dxcbuf[0:tt, :]
        dcb_ref[...] += jnp.sum(dxc_all, axis=0, keepdims=True)
        xbuf[0:PAD, :] = xprev_ref[...] * not_first
        xbuf[PAD:PAD + tt, :] = x_ref[...]
        dx = cw_ref[0:1, :] * dxcbuf[pl.ds(3, tt), :]
        for k in range(1, CONV_WIDTH):
            dx = dx + cw_ref[k:k + 1, :] * dxcbuf[pl.ds(3 - k, tt), :]
        dxg_ref[:, 0:DR] = dx.astype(dxg_ref.dtype)
        for k in range(CONV_WIDTH):
            dcw_ref[k:k + 1, :] += jnp.sum(xbuf[pl.ds(PAD - 3 + k, tt), :] * dxc_all, axis=0, keepdims=True)
        dxcbuf[tt:tt + PAD, :] = dxcbuf[0:PAD, :]

    rev = lambda j: pl.BlockSpec((tt, DR), lambda s: (nc - 1 - s, j))
    prev = pl.BlockSpec((PAD, DR), lambda s: (jnp.maximum((nc - 1 - s) * per - 1, 0), 0))
    vec = pl.BlockSpec((1, DR), lambda s: (0, 0))
    gate = pl.BlockSpec((NB, HEAD, HEAD), lambda s: (0, 0, 0))
    taps = pl.BlockSpec((CONV_WIDTH, DR), lambda s: (0, 0))
    vec_out = jax.ShapeDtypeStruct((1, DR), F32)
    gate_out = jax.ShapeDtypeStruct((NB, HEAD, HEAD), F32)
    return pl.pallas_call(
        body, name=name, grid=(nc,),
        in_specs=[rev(0), rev(0), rev(1), rev(0), rev(0), rev(0), rev(0), prev, prev, taps, gate, gate, vec],
        out_specs=[pl.BlockSpec((tt, 2 * DR), lambda s: (nc - 1 - s, 0)), taps, vec, gate, vec, gate, vec, vec],
        out_shape=[jax.ShapeDtypeStruct((T, 2 * DR), BF16), jax.ShapeDtypeStruct((CONV_WIDTH, DR), F32),
                   vec_out, gate_out, vec_out, gate_out, vec_out, vec_out],
        scratch_shapes=[pltpu.VMEM((tt + PAD, DR), F32), pltpu.VMEM((tt + PAD, DR), F32),
                        pltpu.VMEM((1, DR), F32), pltpu.VMEM((1, DR), F32)],
        compiler_params=_params("arbitrary"),
    )(dy, proj, proj, xc, r, i, h, proj, h, conv_w, w_r, w_i, lam.reshape(1, DR))


def _mem_probs(q, k, scale):
    s = _dot_nt(q, k) * scale
    p = jnp.exp(s - jnp.max(s, axis=-1, keepdims=True))
    return p * (1.0 / jnp.sum(p, axis=-1, keepdims=True))


def _memattn_fwd(proj, mkv, DR, DM, name):
    T = proj.shape[0]
    M = mkv.shape[0]
    NH = DM // HEAD
    tt = _row_tile(T, 512)
    qcol = 2 * DR // DM
    scale = HEAD ** -0.5

    def body(q_ref, g_ref, k_ref, v_ref, y_ref):
        for n in range(NH):
            sl = slice(n * HEAD, (n + 1) * HEAD)
            p = _mem_probs(q_ref[:, sl].astype(BF16), k_ref[:, sl], scale)
            o = _dot(p.astype(BF16), v_ref[:, sl])
            g = g_ref[:, sl]
            y_ref[:, sl] = (o * (g * jax.nn.sigmoid(g))).astype(y_ref.dtype)

    return pl.pallas_call(
        body, name=name, grid=(T // tt,),
        in_specs=[pl.BlockSpec((tt, DM), lambda t: (t, qcol)), pl.BlockSpec((tt, DM), lambda t: (t, qcol + 1)),
                  pl.BlockSpec((M, DM), lambda t: (0, 0)), pl.BlockSpec((M, DM), lambda t: (0, 1))],
        out_specs=pl.BlockSpec((tt, DM), lambda t: (t, 0)),
        out_shape=jax.ShapeDtypeStruct((T, DM), BF16),
        compiler_params=_params("parallel"),
    )(proj, proj, mkv, mkv)


def _memattn_bwd(dy, proj, mkv, DR, DM, name):
    T = proj.shape[0]
    M = mkv.shape[0]
    NH = DM // HEAD
    tt = _row_tile(T, 512)
    qcol = 2 * DR // DM
    scale = HEAD ** -0.5

    def body(dy_ref, q_ref, g_ref, k_ref, v_ref, dqg_ref, dkv_ref):
        @pl.when(pl.program_id(0) == 0)
        def _():
            dkv_ref[...] = jnp.zeros_like(dkv_ref)

        for n in range(NH):
            sl = slice(n * HEAD, (n + 1) * HEAD)
            qb = q_ref[:, sl].astype(BF16)
            kb = k_ref[:, sl]
            vb = v_ref[:, sl]
            p = _mem_probs(qb, kb, scale)
            pb = p.astype(BF16)
            o = _dot(pb, vb)
            silu, dsilu = _silu_and_grad(g_ref[:, sl])
            dyv = dy_ref[:, sl]
            dqg_ref[:, DM + n * HEAD:DM + (n + 1) * HEAD] = (dyv * o * dsilu).astype(dqg_ref.dtype)
            dob = (dyv * silu).astype(BF16)
            dp = _dot_nt(dob, vb)
            ds = (p * (dp - jnp.sum(dp * p, axis=-1, keepdims=True)) * scale).astype(BF16)
            dqg_ref[:, sl] = _dot(ds, kb).astype(dqg_ref.dtype)
            dkv_ref[:, sl] += _dot_tn(ds, qb)
            dkv_ref[:, DM + n * HEAD:DM + (n + 1) * HEAD] += _dot_tn(pb, dob)

    return pl.pallas_call(
        body, name=name, grid=(T // tt,),
        in_specs=[pl.BlockSpec((tt, DM), lambda t: (t, DR // DM)),
                  pl.BlockSpec((tt, DM), lambda t: (t, qcol)), pl.BlockSpec((tt, DM), lambda t: (t, qcol + 1)),
                  pl.BlockSpec((M, DM), lambda t: (0, 0)), pl.BlockSpec((M, DM), lambda t: (0, 1))],
        out_specs=[pl.BlockSpec((tt, 2 * DM), lambda t: (t, 0)), pl.BlockSpec((M, 2 * DM), lambda t: (0, 0))],
        out_shape=[jax.ShapeDtypeStruct((T, 2 * DM), BF16), jax.ShapeDtypeStruct((M, 2 * DM), F32)],
        compiler_params=_params("arbitrary"),
    )(dy, proj, proj, mkv, mkv)


def _sb_blocks(T):
    tk = _row_tile(T // 2, 256)
    tq = 2 * tk
    assert T % tq == 0
    return tq, tk


def _sb_upper(tk):
    row = lax.broadcasted_iota(jnp.int32, (tk, tk), 0)
    col = lax.broadcasted_iota(jnp.int32, (tk, tk), 1)
    return (row > col).astype(BF16)


def _sb_causal(tq, tk, d):
    row = lax.broadcasted_iota(jnp.int32, (tq, tk), 0)
    col = lax.broadcasted_iota(jnp.int32, (tq, tk), 1)
    return col + d * tk < row


def _later_sum(x, upper):
    n = x.shape[0]
    hi, lo = _split_bf16(x)
    both = _dot(jnp.concatenate([hi, lo], axis=0), upper)
    return both[0:n] + both[n:2 * n]


def _sb_weights(q, kb, later_c, causal, upper, scale):
    z = _dot_nt(q, kb) * scale
    sp = jnp.maximum(z, 0.0) + jnp.log(1.0 + jnp.exp(-jnp.abs(z)))
    lk = -sp if causal is None else jnp.where(causal, -sp, 0.0)
    later = _later_sum(lk, upper) + later_c
    w = jnp.exp(z - sp + later)
    if causal is not None:
        w = jnp.where(causal, w, 0.0)
    return z, sp, lk, w


def _sb_fwd(proj, kv, DR, name):
    T = proj.shape[0]
    NH = DR // HEAD
    tq, tk = _sb_blocks(T)
    span = tq // tk
    scale = HEAD ** -0.5

    def body(q_ref, g_ref, k_ref, v_ref, y_ref, o_ref):
        i = pl.program_id(1)
        q = q_ref[...].astype(BF16)
        upper = _sb_upper(tk)

        def tile(jb, carry, causal):
            later_c, acc = carry
            start = pl.multiple_of(jb * tk, tk)
            kb = k_ref[pl.ds(start, tk), :]
            vb = v_ref[pl.ds(start, tk), :]
            _, _, lk, w = _sb_weights(q, kb, later_c, causal, upper, scale)
            hi, lo = _split_bf16(w)
            pv = _dot(jnp.concatenate([hi, lo], axis=0), vb)
            return later_c + jnp.sum(lk, axis=-1, keepdims=True), acc + pv[0:tq] + pv[tq:2 * tq]

        carry = (jnp.zeros((tq, 1), F32), jnp.zeros((tq, HEAD), F32))
        for d in reversed(range(span)):
            carry = tile(span * i + d, carry, _sb_causal(tq, tk, d))
        _, acc = lax.fori_loop(0, span * i, lambda jj, c: tile(span * i - 1 - jj, c, None), carry)
        o_ref[...] = acc
        g = g_ref[...]
        y_ref[...] = (acc * (g * jax.nn.sigmoid(g))).astype(y_ref.dtype)

    blk = lambda off: pl.BlockSpec((tq, HEAD), lambda h, i: (i, off + h))
    whole = lambda off: pl.BlockSpec((T, HEAD), lambda h, i: (0, off + h))
    return pl.pallas_call(
        body, name=name, grid=(NH, T // tq),
        in_specs=[blk(0), blk(NH), whole(0), whole(NH)],
        out_specs=[blk(0), blk(0)],
        out_shape=[jax.ShapeDtypeStruct((T, DR), BF16), jax.ShapeDtypeStruct((T, DR), F32)],
        compiler_params=_params("parallel", "arbitrary"),
    )(proj, proj, kv, kv)


def _sb_bwd(dy, proj, kv, o, DR, name):
    T = proj.shape[0]
    NH = DR // HEAD
    tq, tk = _sb_blocks(T)
    span = tq // tk
    scale = HEAD ** -0.5

    def body(dy_ref, q_ref, g_ref, k_ref, v_ref, o_ref, dq_ref, dg_ref, dk_ref, dv_ref):
        i = pl.program_id(1)

        @pl.when(i == 0)
        def _():
            dk_ref[...] = jnp.zeros_like(dk_ref)
            dv_ref[...] = jnp.zeros_like(dv_ref)

        q = q_ref[...].astype(BF16)
        upper = _sb_upper(tk)
        silu, dsilu = _silu_and_grad(g_ref[...])
        dyv = dy_ref[...]
        ov = o_ref[...]
        dg_ref[...] = (dyv * ov * dsilu).astype(dg_ref.dtype)
        dob = (dyv * silu).astype(BF16)
        total = jnp.sum(dob.astype(F32) * ov, axis=-1, keepdims=True)

        def tile(jb, carry, causal):
            later_c, e_after, dq = carry
            start = pl.multiple_of(jb * tk, tk)
            kb = k_ref[pl.ds(start, tk), :]
            vb = v_ref[pl.ds(start, tk), :]
            z, sp, lk, w = _sb_weights(q, kb, later_c, causal, upper, scale)
            e = _dot_nt(dob, vb) * w
            before = total - e - (_later_sum(e, upper) + e_after)
            sig = jnp.exp(z - sp)
            dz = (e * (1.0 - sig) - before * sig) * scale
            if causal is not None:
                dz = jnp.where(causal, dz, 0.0)
            dzb = dz.astype(BF16)
            dk_ref[pl.ds(start, tk), :] += _dot_tn(dzb, q)
            dv_ref[pl.ds(start, tk), :] += _dot_tn(w.astype(BF16), dob)
            return (later_c + jnp.sum(lk, axis=-1, keepdims=True), e_after + jnp.sum(e, axis=-1, keepdims=True),
                    dq + _dot(dzb, kb))

        carry = (jnp.zeros((tq, 1), F32), jnp.zeros((tq, 1), F32), jnp.zeros((tq, HEAD), F32))
        for d in reversed(range(span)):
            carry = tile(span * i + d, carry, _sb_causal(tq, tk, d))
        _, _, dq = lax.fori_loop(0, span * i, lambda jj, c: tile(span * i - 1 - jj, c, None), carry)
        dq_ref[...] = dq.astype(dq_ref.dtype)

    blk = lambda off: pl.BlockSpec((tq, HEAD), lambda h, i: (i, off + h))
    whole = lambda off: pl.BlockSpec((T, HEAD), lambda h, i: (0, off + h))
    return pl.pallas_call(
        body, name=name, grid=(NH, T // tq),
        in_specs=[blk(0), blk(0), blk(NH), whole(0), whole(NH), blk(0)],
        out_specs=[blk(0), blk(0), whole(0), whole(0)],
        out_shape=[jax.ShapeDtypeStruct((T, DR), BF16), jax.ShapeDtypeStruct((T, DR), BF16),
                   jax.ShapeDtypeStruct((T, DR), F32), jax.ShapeDtypeStruct((T, DR), F32)],
        compiler_params=_params("parallel", "arbitrary"),
    )(dy, proj, proj, kv, kv, o)


def _merge_dkv(parts, name):
    T, DR = parts[0][0].shape
    tt = _row_tile(T, 256)
    n = len(parts)

    def body(*refs):
        o_ref = refs[-1]
        dk = refs[0][...]
        dv = refs[1][...]
        for p in range(1, n):
            dk = dk + refs[2 * p][...]
            dv = dv + refs[2 * p + 1][...]
        o_ref[:, 0:DR] = dk.astype(o_ref.dtype)
        o_ref[:, DR:2 * DR] = dv.astype(o_ref.dtype)

    row = pl.BlockSpec((tt, DR), lambda t: (t, 0))
    return pl.pallas_call(
        body, name=name, grid=(T // tt,),
        in_specs=[row] * (2 * n),
        out_specs=pl.BlockSpec((tt, 2 * DR), lambda t: (t, 0)),
        out_shape=jax.ShapeDtypeStruct((T, 2 * DR), BF16),
        compiler_params=_params("parallel"),
    )(*[a for pair in parts for a in pair])


def _final_loss(h, g, target, name):
    T, D = h.shape
    tt = _row_tile(T, 256)

    def body(h_ref, g_ref, t_ref, dh_ref, dg_ref, sq_ref):
        @pl.when(pl.program_id(0) == 0)
        def _():
            dg_ref[...] = jnp.zeros_like(dg_ref)
            sq_ref[...] = jnp.zeros_like(sq_ref)

        hv = h_ref[...]
        gv = g_ref[...]
        r = lax.rsqrt(jnp.mean(hv * hv, axis=-1, keepdims=True) + EPS)
        xhat = hv * r
        err = xhat * gv - t_ref[...]
        sq_ref[...] += jnp.sum(err * err, axis=0, keepdims=True)
        dy = err * (1.0 / D)
        dxhat = dy * gv
        dh_ref[...] = r * (dxhat - xhat * jnp.mean(dxhat * xhat, axis=-1, keepdims=True))
        dg_ref[...] += jnp.sum(dy * xhat, axis=0, keepdims=True)

    row = pl.BlockSpec((tt, D), lambda i: (i, 0))
    vec = pl.BlockSpec((1, D), lambda i: (0, 0))
    return pl.pallas_call(
        body, name=name, grid=(T // tt,),
        in_specs=[row, vec, row],
        out_specs=[row, vec, vec],
        out_shape=[jax.ShapeDtypeStruct((T, D), F32), jax.ShapeDtypeStruct((1, D), F32),
                   jax.ShapeDtypeStruct((1, D), F32)],
        compiler_params=_params("arbitrary"),
    )(h, g.reshape(1, D), target)


def _sum_parts(parts_ref):
    g = parts_ref[0].astype(F32)
    for s in range(1, parts_ref.shape[0]):
        g = g + parts_ref[s].astype(F32)
    return g


def _adamw(parts, w, m, v, name):
    P, R, C = parts.shape
    tr = _rows_for(R, P * C * 4)

    def body(p_ref, w_ref, m_ref, v_ref, g_ref, d_ref, nm_ref, nv_ref):
        g = _sum_parts(p_ref)
        nm = ADAM_B1 * m_ref[...] + (1.0 - ADAM_B1) * g
        nv = ADAM_B2 * v_ref[...] + (1.0 - ADAM_B2) * jnp.square(g)
        m_hat = nm / (1.0 - ADAM_B1 ** ADAM_STEP)
        v_hat = nv / (1.0 - ADAM_B2 ** ADAM_STEP)
        g_ref[...] = g
        d_ref[...] = -ADAM_LR * (m_hat / (jnp.sqrt(v_hat) + ADAM_EPS) + ADAM_WD * w_ref[...])
        nm_ref[...] = nm
        nv_ref[...] = nv

    row = pl.BlockSpec((tr, C), lambda i: (i, 0))
    out = jax.ShapeDtypeStruct((R, C), F32)
    return pl.pallas_call(
        body, name=name, grid=(R // tr,),
        in_specs=[pl.BlockSpec((P, tr, C), lambda i: (0, i, 0)), row, row, row],
        out_specs=[row] * 4,
        out_shape=[out] * 4,
        compiler_params=_params("parallel"),
    )(parts, w, m, v)


def _sum_devices(parts, name):
    P, R, C = parts.shape
    tr = _rows_for(R, P * C * 4)

    def body(p_ref, o_ref):
        o_ref[...] = _sum_parts(p_ref)

    return pl.pallas_call(
        body, name=name, grid=(R // tr,),
        in_specs=[pl.BlockSpec((P, tr, C), lambda i: (0, i, 0))],
        out_specs=pl.BlockSpec((tr, C), lambda i: (i, 0)),
        out_shape=jax.ShapeDtypeStruct((R, C), F32),
        compiler_params=_params("parallel"),
    )(parts)


def _mesh_position():
    return lax.axis_index("x"), lax.axis_index("y"), lax.axis_index("c")


def _device_index(p):
    return 4 * p[0] + 2 * p[1] + p[2]


def _all_gather(arrs, name):
    n = len(arrs)

    def body(*refs):
        ins, outs = refs[:n], refs[n:2 * n]
        send_sems, recv_sems, local_sems = refs[2 * n:]
        x, y, c = _mesh_position()
        me, sibling = (x, y, c), (x, y, 1 - c)
        chips = [(1 - x, y), (x, 1 - y), (1 - x, 1 - y)]

        def slot(a, p):
            return outs[a].at[_device_index(p)]

        def copy(a, k, block, to, src=None):
            return pltpu.make_async_remote_copy(
                src_ref=slot(a, block) if src is None else src, dst_ref=slot(a, block),
                send_sem=send_sems.at[a, k], recv_sem=recv_sems.at[a, k],
                device_id=to, device_id_type=pl.DeviceIdType.MESH)

        mine = [pltpu.make_async_copy(ins[a], slot(a, me), local_sems.at[a]) for a in range(n)]
        for cp in mine:
            cp.start()
        first = []
        for a in range(n):
            first.append(copy(a, 0, me, sibling, src=ins[a]))
            first += [copy(a, 1 + j, me, (*chip, c), src=ins[a]) for j, chip in enumerate(chips)]
        for cp in first:
            cp.start()
        passed = []
        for a in range(n):
            for j, chip in enumerate(chips):
                copy(a, 1 + j, (*chip, c), me).wait_recv()
                fwd = copy(a, 4 + j, (*chip, c), sibling)
                fwd.start()
                passed.append(fwd)
        for a in range(n):
            copy(a, 0, sibling, me).wait_recv()
            for j, chip in enumerate(chips):
                copy(a, 4 + j, (*chip, 1 - c), me).wait_recv()
        for cp in first + passed:
            cp.wait_send()
        for cp in mine:
            cp.wait()

    return pl.pallas_call(
        body, name=name,
        in_specs=[HBM_SPEC] * n, out_specs=[HBM_SPEC] * n,
        out_shape=[jax.ShapeDtypeStruct((N_DEV,) + a.shape, a.dtype) for a in arrs],
        scratch_shapes=[pltpu.SemaphoreType.DMA((n, 7)), pltpu.SemaphoreType.DMA((n, 7)),
                        pltpu.SemaphoreType.DMA((n,))],
    )(*arrs)


def _all_to_all(arrs, name):
    n = len(arrs)

    def body(*refs):
        ins, outs = refs[:n], refs[n:2 * n]
        send_sems, recv_sems, local_sems = refs[2 * n:]
        x, y, c = _mesh_position()
        me = _device_index((x, y, c))
        peers = [(1 - x if k & 4 else x, 1 - y if k & 2 else y, 1 - c if k & 1 else c) for k in range(1, N_DEV)]
        local = [pltpu.make_async_copy(ins[a].at[me], outs[a].at[me], local_sems.at[a]) for a in range(n)]
        for cp in local:
            cp.start()
        sends, recvs = [], []
        for a in range(n):
            for k, peer in enumerate(peers):
                there = _device_index(peer)
                sems = dict(send_sem=send_sems.at[a, k], recv_sem=recv_sems.at[a, k],
                            device_id=peer, device_id_type=pl.DeviceIdType.MESH)
                sends.append(pltpu.make_async_remote_copy(src_ref=ins[a].at[there], dst_ref=outs[a].at[me], **sems))
                recvs.append(pltpu.make_async_remote_copy(src_ref=ins[a].at[there], dst_ref=outs[a].at[there], **sems))
        for cp in sends:
            cp.start()
        for cp in recvs:
            cp.wait_recv()
        for cp in sends:
            cp.wait_send()
        for cp in local:
            cp.wait()

    return pl.pallas_call(
        body, name=name,
        in_specs=[HBM_SPEC] * n, out_specs=[HBM_SPEC] * n,
        out_shape=[jax.ShapeDtypeStruct(a.shape, a.dtype) for a in arrs],
        scratch_shapes=[pltpu.SemaphoreType.DMA((n, 7)), pltpu.SemaphoreType.DMA((n, 7)),
                        pltpu.SemaphoreType.DMA((n,))],
    )(*arrs)


def _pack(arrs, row_multiple):
    parts = []
    rows = 0
    for a in arrs:
        flat = a.reshape(-1).astype(F32)
        r = -(-flat.shape[0] // (8 * LANES)) * 8
        parts.append(jnp.pad(flat, (0, r * LANES - flat.shape[0])).reshape(r, LANES))
        rows += r
    pad = -rows % row_multiple
    if pad:
        parts.append(jnp.zeros((pad, LANES), F32))
    return jnp.concatenate(parts, axis=0)


def _unpack(buf, shapes, lead=()):
    out = []
    r0 = 0
    for shape in shapes:
        size = 1
        for s in shape:
            size *= s
        r = -(-size // (8 * LANES)) * 8
        part = buf[..., r0:r0 + r, :].reshape(lead + (r * LANES,))[..., :size]
        out.append(part.reshape(lead + tuple(shape)))
        r0 += r
    return out


def _gathered_cols(g):
    g = jnp.moveaxis(g, 0, -2)
    return g.reshape(g.shape[:-2] + (g.shape[-2] * g.shape[-1],))


def kernel(x, mem, mem_norm, w_mem_kv, norm_a, w_in_a, conv_w, conv_b, w_rec_gate, b_rec_gate, w_in_gate, b_in_gate, lru_lambda, w_out_a, kv_norm, w_kv, norm_b, w_in_b, w_out_b, final_norm, loss_target, m_mem_norm, m_w_mem_kv, m_norm_a, m_w_in_a, m_conv_w, m_conv_b, m_w_rec_gate, m_b_rec_gate, m_w_in_gate, m_b_in_gate, m_lru_lambda, m_w_out_a, m_kv_norm, m_w_kv, m_norm_b, m_w_in_b, m_w_out_b, m_final_norm, v_mem_norm, v_w_mem_kv, v_norm_a, v_w_in_a, v_conv_w, v_conv_b, v_w_rec_gate, v_b_rec_gate, v_w_in_gate, v_b_in_gate, v_lru_lambda, v_w_out_a, v_kv_norm, v_w_kv, v_norm_b, v_w_in_b, v_w_out_b, v_final_norm):
    xs = x[0]
    T, D = xs.shape
    L = w_mem_kv.shape[0]
    NA = w_in_a.shape[0]
    NB = w_in_b.shape[0]
    DM2 = w_mem_kv.shape[2]
    DM = DM2 // 2
    DR = w_rec_gate.shape[1] * w_rec_gate.shape[2]
    me = _device_index(_mesh_position())

    small_sharded = [norm_a, conv_w, conv_b, b_rec_gate, b_in_gate, lru_lambda]
    big = [w_mem_kv[l] for l in range(L)]
    big += [w_in_a[l] for l in range(NA)] + [w_out_a[l] for l in range(NA)] + [w_kv]
    big += [w_in_b[l] for l in range(NB)] + [w_out_b[l] for l in range(NB)]
    gathered = _all_gather([w.astype(BF16) for w in big] + [_pack(small_sharded, 8)], "gather_params")
    it = iter(gathered[:-1])
    g_mem = [next(it).reshape(1, D, DM2) for _ in range(L)]
    g_in_a = [next(it) for _ in range(NA)]
    g_out_a = [next(it).reshape(1, -1, D) for _ in range(NA)]
    g_kv = next(it)
    g_in_b = [next(it) for _ in range(NB)]
    g_out_b = [next(it).reshape(1, -1, D) for _ in range(NB)]
    norm_a_f, conv_w_f, conv_b_f, b_r_f, b_i_f, lam_f = [
        _gathered_cols(s) for s in _unpack(gathered[-1], [s.shape for s in small_sharded], lead=(N_DEV,))]
    w_r_bf = w_rec_gate.astype(BF16)
    w_i_bf = w_in_gate.astype(BF16)

    zeros_mem = jnp.zeros_like(mem[0])
    mem_n = _rms_fwd(mem[0], mem_norm, "rms_mem")
    mkv = [_mm_nn(mem_n, g_mem[l], out_dtype=BF16, name=f"mm_mem_kv{l}") for l in range(L)]

    h = xs
    saved_a = []
    for l in range(NA):
        u = _rms_fwd(h, norm_a_f[l], f"rms_a{l}")
        proj = _mm_nn(u, g_in_a[l], name=f"mm_in_a{l}")
        y_rnn, xc, r, i, hr = _rglru_fwd(proj, conv_w_f[l], conv_b_f[l], w_r_bf[l], b_r_f[l], w_i_bf[l], b_i_f[l],
                                         lam_f[l], f"rglru_fwd{l}")
        y_mem = _memattn_fwd(proj, mkv[l], DR, DM, f"memattn_fwd_a{l}")
        ycat = jnp.concatenate([y_rnn, y_mem], axis=-1)
        h_next = _mm_nn(ycat, g_out_a[l], res=h, name=f"mm_out_a{l}")
        saved_a.append((h, u, proj, xc, r, i, hr, ycat))
        h = h_next
    h_kv = h
    u_kv = _rms_fwd(h_kv, kv_norm, "rms_kv")
    kv = _mm_nn(u_kv, g_kv, out_dtype=BF16, name="mm_kv")
    saved_b = []
    for j in range(NB):
        u = _rms_fwd(h, norm_b[j], f"rms_b{j}")
        proj = _mm_nn(u, g_in_b[j], name=f"mm_in_b{j}")
        y_sb, o_sb = _sb_fwd(proj, kv, DR, f"sb_fwd{j}")
        y_mem = _memattn_fwd(proj, mkv[NA + j], DR, DM, f"memattn_fwd_b{j}")
        ycat = jnp.concatenate([y_sb, y_mem], axis=-1)
        h_next = _mm_nn(ycat, g_out_b[j], res=h, name=f"mm_out_b{j}")
        saved_b.append((h, u, proj, o_sb, ycat))
        h = h_next

    dh, d_final_norm, sq = _final_loss(h, final_norm, loss_target[0], "final_loss")
    loss = lax.psum(0.5 * jnp.sum(sq) / D, ("x", "y", "c"))

    big_grads = {}
    dmkv = [None] * L
    d_norm_b = [None] * NB
    dkv_parts = []
    for j in reversed(range(NB)):
        h_in, u, proj, o_sb, ycat = saved_b[j]
        dy = _mm_nt(dh, g_out_b[j], name=f"mm_dy_b{j}")
        big_grads[f"out_b{j}"] = _mm_tn(ycat, dh, 1, name=f"mm_dw_out_b{j}").reshape(N_DEV, -1, D)
        dq, dg, dk, dv = _sb_bwd(dy, proj, kv, o_sb, DR, f"sb_bwd{j}")
        dkv_parts.append((dk, dv))
        dqg_mem, dmkv[NA + j] = _memattn_bwd(dy, proj, mkv[NA + j], DR, DM, f"memattn_bwd_b{j}")
        dproj = jnp.concatenate([dq, dg, dqg_mem], axis=-1)
        du = _mm_nt(dproj, g_in_b[j], name=f"mm_du_b{j}")
        big_grads[f"in_b{j}"] = _mm_tn(u, dproj, N_DEV, name=f"mm_dw_in_b{j}")
        dh, d_norm_b[j] = _rms_bwd(du, h_in, norm_b[j], dh, f"rms_bwd_b{j}")

    dkv = _merge_dkv(dkv_parts, "merge_dkv")
    du_kv = _mm_nt(dkv, g_kv, name="mm_du_kv")
    big_grads["kv"] = _mm_tn(u_kv, dkv, N_DEV, name="mm_dw_kv")
    dh, d_kv_norm = _rms_bwd(du_kv, h_kv, kv_norm, dh, "rms_bwd_kv")

    d_norm_a, d_conv_w, d_conv_b, d_w_r, d_b_r, d_w_i, d_b_i, d_lam = ([None] * NA for _ in range(8))
    for l in reversed(range(NA)):
        h_in, u, proj, xc, r, i, hr, ycat = saved_a[l]
        dy = _mm_nt(dh, g_out_a[l], name=f"mm_dy_a{l}")
        big_grads[f"out_a{l}"] = _mm_tn(ycat, dh, 1, name=f"mm_dw_out_a{l}").reshape(N_DEV, -1, D)
        (dxg, d_conv_w[l], d_conv_b[l], d_w_r[l], d_b_r[l], d_w_i[l], d_b_i[l], d_lam[l]) = _rglru_bwd(
            dy, proj, xc, r, i, hr, conv_w_f[l], w_r_bf[l], w_i_bf[l], lam_f[l], f"rglru_bwd{l}")
        dqg_mem, dmkv[l] = _memattn_bwd(dy, proj, mkv[l], DR, DM, f"memattn_bwd_a{l}")
        dproj = jnp.concatenate([dxg, dqg_mem], axis=-1)
        du = _mm_nt(dproj, g_in_a[l], name=f"mm_du_a{l}")
        big_grads[f"in_a{l}"] = _mm_tn(u, dproj, N_DEV, name=f"mm_dw_in_a{l}")
        dh, d_norm_a[l] = _rms_bwd(du, h_in, norm_a_f[l], dh, f"rms_bwd_a{l}")
    grad_x = dh.reshape(x.shape)

    dmkv_all = jnp.concatenate(dmkv, axis=-1)
    dmem_n = _mm_nt(dmkv_all, jnp.concatenate(g_mem, axis=0), name="mm_dmem")
    for l in range(L):
        big_grads[f"mem_kv{l}"] = _mm_tn(mem_n, dmkv[l], 1, name=f"mm_dw_mem_kv{l}").reshape(N_DEV, -1, DM2)
    _, d_mem_norm = _rms_bwd(dmem_n, mem[0], mem_norm, zeros_mem, "rms_bwd_mem")

    names = ([f"mem_kv{l}" for l in range(L)] + [f"in_a{l}" for l in range(NA)] + [f"out_a{l}" for l in range(NA)] + ["kv"]
             + [f"in_b{j}" for j in range(NB)] + [f"out_b{j}" for j in range(NB)])
    received = dict(zip(names, _all_to_all([big_grads[k] for k in names], "scatter_grads")))

    def update(key, w, m, v):
        shape = w.shape
        two_d = (-1, shape[-1])
        return [o.reshape(shape) for o in _adamw(received[key], w.reshape(two_d), m.reshape(two_d),
                                                 v.reshape(two_d), f"adamw_{key}")]

    def update_layers(prefix, w, m, v):
        per_layer = [update(f"{prefix}{l}", w[l], m[l], v[l]) for l in range(w.shape[0])]
        return [jnp.stack([per_layer[l][k] for l in range(w.shape[0])]) for k in range(4)]

    upd = {
        "w_mem_kv": update_layers("mem_kv", w_mem_kv, m_w_mem_kv, v_w_mem_kv),
        "w_in_a": update_layers("in_a", w_in_a, m_w_in_a, v_w_in_a),
        "w_out_a": update_layers("out_a", w_out_a, m_w_out_a, v_w_out_a),
        "w_kv": update("kv", w_kv, m_w_kv, v_w_kv),
        "w_in_b": update_layers("in_b", w_in_b, m_w_in_b, v_w_in_b),
        "w_out_b": update_layers("out_b", w_out_b, m_w_out_b, v_w_out_b),
    }

    small_full = {
        "mem_norm": d_mem_norm.reshape(-1),
        "norm_a": jnp.concatenate(d_norm_a, axis=0),
        "conv_w": jnp.stack(d_conv_w),
        "conv_b": jnp.concatenate(d_conv_b, axis=0),
        "w_rec_gate": jnp.stack(d_w_r),
        "b_rec_gate": jnp.concatenate(d_b_r, axis=0),
        "w_in_gate": jnp.stack(d_w_i),
        "b_in_gate": jnp.concatenate(d_b_i, axis=0),
        "lru_lambda": jnp.concatenate(d_lam, axis=0),
        "kv_norm": d_kv_norm.reshape(-1),
        "norm_b": jnp.concatenate(d_norm_b, axis=0),
        "final_norm": d_final_norm.reshape(-1),
    }
    small_names = list(small_full)
    (small_gathered,) = _all_gather([_pack([small_full[k] for k in small_names], 256)], "gather_small_grads")
    small_sum = _sum_devices(small_gathered, "sum_small_grads")
    small_grad = dict(zip(small_names, _unpack(small_sum, [small_full[k].shape for k in small_names])))
    small_w = {"mem_norm": (mem_norm, m_mem_norm, v_mem_norm), "norm_a": (norm_a, m_norm_a, v_norm_a),
               "conv_w": (conv_w, m_conv_w, v_conv_w), "conv_b": (conv_b, m_conv_b, v_conv_b),
               "w_rec_gate": (w_rec_gate, m_w_rec_gate, v_w_rec_gate),
               "b_rec_gate": (b_rec_gate, m_b_rec_gate, v_b_rec_gate),
               "w_in_gate": (w_in_gate, m_w_in_gate, v_w_in_gate), "b_in_gate": (b_in_gate, m_b_in_gate, v_b_in_gate),
               "lru_lambda": (lru_lambda, m_lru_lambda, v_lru_lambda), "kv_norm": (kv_norm, m_kv_norm, v_kv_norm),
               "norm_b": (norm_b, m_norm_b, v_norm_b), "final_norm": (final_norm, m_final_norm, v_final_norm)}
    for k in small_names:
        w = small_w[k][0]
        if small_grad[k].shape != w.shape:
            n = w.shape[-1]
            small_grad[k] = lax.dynamic_slice_in_dim(small_grad[k], me * n, n, axis=-1)
    small_shapes = [small_w[k][0].shape for k in small_names]
    packed = [_pack([small_grad[k] for k in small_names], 256)[None]]
    packed += [_pack([small_w[k][t] for k in small_names], 256) for t in range(3)]
    small_out = [_unpack(o, small_shapes) for o in _adamw(*packed, "adamw_small")]
    for idx, k in enumerate(small_names):
        upd[k] = [small_out[t][idx] for t in range(4)]

    order = ["mem_norm", "w_mem_kv", "norm_a", "w_in_a", "conv_w", "conv_b", "w_rec_gate", "b_rec_gate", "w_in_gate",
             "b_in_gate", "lru_lambda", "w_out_a", "kv_norm", "w_kv", "norm_b", "w_in_b", "w_out_b", "final_norm"]
    return (loss, grad_x, *[upd[k][0] for k in order], *[upd[k][1] for k in order],
            *[upd[k][2] for k in order], *[upd[k][3] for k in order])
```

```python
import functools

import jax
import jax.numpy as jnp
from jax import lax
from jax.experimental import pallas as pl
from jax.experimental.pallas import tpu as pltpu

F32 = jnp.float32
BF16 = jnp.bfloat16

N_DEV = 8
EPS = 1e-6
LRU_C = 8.0
HEAD = 128
CONV_WIDTH = 4
LANES = 128
VMEM_LIMIT = 56 * 1024 * 1024

ADAM_LR = 0.001
ADAM_B1 = 0.9
ADAM_B2 = 0.999
ADAM_EPS = 1e-08
ADAM_WD = 0.01
ADAM_STEP = 10

HBM_SPEC = pl.BlockSpec(memory_space=pltpu.HBM)


def _params(*semantics):
    return pltpu.CompilerParams(dimension_semantics=semantics, vmem_limit_bytes=VMEM_LIMIT)


def _tile(n, cap):
    if n <= cap:
        return n
    t = cap - cap % LANES
    while n % t:
        t -= LANES
    return t


def _row_tile(n, cap):
    if n <= cap:
        return n
    t = cap - cap % 8
    while t >= 8:
        if n % t == 0:
            return t
        t -= 8
    return n


def _rows_for(n, bytes_per_row, budget=2 * 1024 * 1024):
    return _row_tile(n, max(8, budget // bytes_per_row))


def _dot(a, b):
    return lax.dot_general(a, b, (((1,), (0,)), ((), ())), preferred_element_type=F32)


def _dot_nt(a, b):
    return lax.dot_general(a, b, (((1,), (1,)), ((), ())), preferred_element_type=F32)


def _dot_tn(a, b):
    return lax.dot_general(a, b, (((0,), (0,)), ((), ())), preferred_element_type=F32)


def _split_bf16(x):
    hi = x.astype(BF16)
    lo = (x - hi.astype(F32)).astype(BF16)
    return hi, lo


def _softplus(x):
    return jnp.maximum(x, 0.0) + jnp.log1p(jnp.exp(-jnp.abs(x)))


def _silu_and_grad(g):
    sg = jax.nn.sigmoid(g)
    return g * sg, sg * (1.0 + g * (1.0 - sg))


def _neg_expm1(x):
    series = -x * (1.0 + x * (0.5 + x * (1.0 / 6.0 + x * (1.0 / 24.0 + x * (1.0 / 120.0)))))
    return jnp.where(x > -0.1, series, 1.0 - jnp.exp(x))


def _rms_fwd(x, g, name):
    T, D = x.shape
    tt = _row_tile(T, 512)

    def body(x_ref, g_ref, o_ref):
        xv = x_ref[...]
        r = lax.rsqrt(jnp.mean(xv * xv, axis=-1, keepdims=True) + EPS)
        o_ref[...] = ((xv * r) * g_ref[...]).astype(o_ref.dtype)

    return pl.pallas_call(
        body, name=name, grid=(T // tt,),
        in_specs=[pl.BlockSpec((tt, D), lambda i: (i, 0)), pl.BlockSpec((1, D), lambda i: (0, 0))],
        out_specs=pl.BlockSpec((tt, D), lambda i: (i, 0)),
        out_shape=jax.ShapeDtypeStruct((T, D), BF16),
        compiler_params=_params("parallel"),
    )(x, g.reshape(1, D))


def _rms_bwd(du, h, g, res, name):
    T, D = h.shape
    tt = _row_tile(T, 256)

    def body(du_ref, h_ref, g_ref, res_ref, dh_ref, dg_ref):
        @pl.when(pl.program_id(0) == 0)
        def _():
            dg_ref[...] = jnp.zeros_like(dg_ref)

        hv = h_ref[...]
        duv = du_ref[...]
        r = lax.rsqrt(jnp.mean(hv * hv, axis=-1, keepdims=True) + EPS)
        xhat = hv * r
        dxhat = duv * g_ref[...]
        dh_ref[...] = res_ref[...] + r * (dxhat - xhat * jnp.mean(dxhat * xhat, axis=-1, keepdims=True))
        dg_ref[...] += jnp.sum(duv * xhat, axis=0, keepdims=True)

    row = pl.BlockSpec((tt, D), lambda i: (i, 0))
    vec = pl.BlockSpec((1, D), lambda i: (0, 0))
    return pl.pallas_call(
        body, name=name, grid=(T // tt,),
        in_specs=[row, row, vec, row],
        out_specs=[row, vec],
        out_shape=[jax.ShapeDtypeStruct((T, D), F32), jax.ShapeDtypeStruct((1, D), F32)],
        compiler_params=_params("arbitrary"),
    )(du, h, g.reshape(1, D), res)


def _mm_nn(a, b3, res=None, out_dtype=F32, name=None):
    M, K = a.shape
    S, _, ns = b3.shape
    tm = _row_tile(M, 1024 if a.dtype == BF16 else 512)
    tn = _tile(ns, 512)
    nj = ns // tn

    def body(*refs):
        if res is None:
            a_ref, b_ref, o_ref = refs
        else:
            a_ref, b_ref, r_ref, o_ref = refs
        acc = _dot(a_ref[...].astype(BF16), b_ref[...])
        if res is not None:
            acc = acc + r_ref[...]
        o_ref[...] = acc.astype(o_ref.dtype)

    in_specs = [pl.BlockSpec((tm, K), lambda i, j: (i, 0)),
                pl.BlockSpec((None, K, tn), lambda i, j: (j // nj, 0, j % nj))]
    args = [a, b3]
    if res is not None:
        in_specs.append(pl.BlockSpec((tm, tn), lambda i, j: (i, j)))
        args.append(res)
    return pl.pallas_call(
        body, name=name, grid=(M // tm, S * nj),
        in_specs=in_specs,
        out_specs=pl.BlockSpec((tm, tn), lambda i, j: (i, j)),
        out_shape=jax.ShapeDtypeStruct((M, S * ns), out_dtype),
        compiler_params=_params("parallel", "parallel"),
    )(*args)


def _mm_nt(a, b3, out_dtype=F32, name=None):
    M = a.shape[0]
    S, N, ns = b3.shape
    tm = _row_tile(M, 1024 if a.dtype == BF16 else 512)
    tn = _tile(N, 512)

    def body(a_ref, b_ref, o_ref):
        acc = _dot_nt(a_ref[:, 0:ns].astype(BF16), b_ref[0])
        for s in range(1, S):
            acc = acc + _dot_nt(a_ref[:, s * ns:(s + 1) * ns].astype(BF16), b_ref[s])
        o_ref[...] = acc.astype(o_ref.dtype)

    return pl.pallas_call(
        body, name=name, grid=(M // tm, N // tn),
        in_specs=[pl.BlockSpec((tm, S * ns), lambda i, j: (i, 0)),
                  pl.BlockSpec((S, tn, ns), lambda i, j: (0, j, 0))],
        out_specs=pl.BlockSpec((tm, tn), lambda i, j: (i, j)),
        out_shape=jax.ShapeDtypeStruct((M, N), out_dtype),
        compiler_params=_params("parallel", "parallel"),
    )(a, b3)


def _mm_tn(a, b, S, name=None):
    T, K = a.shape
    ns = b.shape[1] // S
    tk = _tile(K, 1024)
    tn = _tile(ns, 512)
    nj = ns // tn
    tt = _row_tile(T, 2048)
    nt = T // tt

    def body(a_ref, b_ref, o_ref, acc_ref):
        t = pl.program_id(2)
        p = _dot_tn(a_ref[...].astype(BF16), b_ref[...].astype(BF16))

        @pl.when(t == 0)
        def _():
            acc_ref[...] = p

        @pl.when(t > 0)
        def _():
            acc_ref[...] += p

        @pl.when(t == nt - 1)
        def _():
            o_ref[...] = acc_ref[...].astype(o_ref.dtype)

    return pl.pallas_call(
        body, name=name, grid=(K // tk, S * nj, nt),
        in_specs=[pl.BlockSpec((tt, tk), lambda i, j, t: (t, i)),
                  pl.BlockSpec((tt, tn), lambda i, j, t: (t, j))],
        out_specs=pl.BlockSpec((None, tk, tn), lambda i, j, t: (j // nj, i, j % nj)),
        out_shape=jax.ShapeDtypeStruct((S, K, ns), BF16),
        scratch_shapes=[pltpu.VMEM((tk, tn), F32)],
        compiler_params=_params("parallel", "parallel", "arbitrary"),
    )(a, b)


def _lru_coeffs(lam_row, r):
    cl = -LRU_C * _softplus(-lam_row)
    log_a = cl * r
    a = jnp.exp(log_a)
    em = _neg_expm1(2.0 * log_a)
    return cl, a, em, jnp.sqrt(em)


def _rglru_fwd(proj, conv_w, conv_b, w_r, b_r, w_i, b_i, lam, name):
    T = proj.shape[0]
    DR = conv_w.shape[1]
    NB = DR // HEAD
    tt = _row_tile(T, 256)
    PAD = 8

    def body(x_ref, g_ref, cw_ref, cb_ref, wr_ref, br_ref, wi_ref, bi_ref, lam_ref,
             y_ref, xc_ref, r_ref, i_ref, h_ref, xbuf, hcar):
        @pl.when(pl.program_id(0) == 0)
        def _():
            xbuf[0:PAD, :] = jnp.zeros((PAD, DR), F32)
            hcar[...] = jnp.zeros_like(hcar)

        xbuf[PAD:PAD + tt, :] = x_ref[...]
        xc = cb_ref[...] + cw_ref[0:1, :] * xbuf[pl.ds(PAD - 3, tt), :]
        for k in range(1, CONV_WIDTH):
            xc = xc + cw_ref[k:k + 1, :] * xbuf[pl.ds(PAD - 3 + k, tt), :]
        xbuf[0:PAD, :] = xbuf[tt:tt + PAD, :]
        xc_ref[...] = xc
        xcb = xc.astype(BF16)
        for n in range(NB):
            sl = slice(n * HEAD, (n + 1) * HEAD)
            r_ref[:, sl] = jax.nn.sigmoid(_dot(xcb[:, sl], wr_ref[n]) + br_ref[:, sl])
            i_ref[:, sl] = jax.nn.sigmoid(_dot(xcb[:, sl], wi_ref[n]) + bi_ref[:, sl])
        _, a, _, mult = _lru_coeffs(lam_ref[...], r_ref[...])
        hs = mult * (i_ref[...] * xc)
        row = lax.broadcasted_iota(jnp.int32, (tt, DR), 0)
        d = 1
        while d < tt:
            keep = row >= d
            a_sh = jnp.where(keep, pltpu.roll(a, d, 0), 1.0)
            h_sh = jnp.where(keep, pltpu.roll(hs, d, 0), 0.0)
            hs = a * h_sh + hs
            a = a * a_sh
            d *= 2
        h = hs + a * hcar[...]
        h_ref[...] = h
        hcar[...] = h_ref[tt - 1:tt, :]
        g = g_ref[...]
        y_ref[...] = (h * (g * jax.nn.sigmoid(g))).astype(y_ref.dtype)

    col = lambda j: pl.BlockSpec((tt, DR), lambda c: (c, j))
    vec = pl.BlockSpec((1, DR), lambda c: (0, 0))
    gate = pl.BlockSpec((NB, HEAD, HEAD), lambda c: (0, 0, 0))
    f32_out = jax.ShapeDtypeStruct((T, DR), F32)
    return pl.pallas_call(
        body, name=name, grid=(T // tt,),
        in_specs=[col(0), col(1), pl.BlockSpec((CONV_WIDTH, DR), lambda c: (0, 0)), vec, gate, vec, gate, vec, vec],
        out_specs=[col(0)] * 5,
        out_shape=[jax.ShapeDtypeStruct((T, DR), BF16), f32_out, f32_out, f32_out, f32_out],
        scratch_shapes=[pltpu.VMEM((tt + PAD, DR), F32), pltpu.VMEM((1, DR), F32)],
        compiler_params=_params("arbitrary"),
    )(proj, proj, conv_w, conv_b.reshape(1, DR), w_r, b_r.reshape(1, DR), w_i, b_i.reshape(1, DR),
      lam.reshape(1, DR))


def _rglru_bwd(dy, proj, xc, r, i, h, conv_w, w_r, w_i, lam, name):
    T = proj.shape[0]
    DR = conv_w.shape[1]
    NB = DR // HEAD
    tt = _row_tile(T, 128)
    nc = T // tt
    PAD = 8
    per = tt // PAD

    def body(dy_ref, x_ref, g_ref, xc_ref, r_ref, i_ref, h_ref, xprev_ref, hprev_ref,
             cw_ref, wr_ref, wi_ref, lam_ref,
             dxg_ref, dcw_ref, dcb_ref, dwr_ref, dbr_ref, dwi_ref, dbi_ref, dlam_ref,
             xbuf, dxcbuf, gcar, acar):
        step = pl.program_id(0)
        chunk = nc - 1 - step

        @pl.when(step == 0)
        def _():
            for ref in (dcw_ref, dcb_ref, dwr_ref, dbr_ref, dwi_ref, dbi_ref, dlam_ref, gcar, acar):
                ref[...] = jnp.zeros_like(ref)
            dxcbuf[tt:tt + PAD, :] = jnp.zeros((PAD, DR), F32)

        not_first = (chunk > 0).astype(F32)
        row = lax.broadcasted_iota(jnp.int32, (tt, DR), 0)
        silu, dsilu = _silu_and_grad(g_ref[...])
        dyv = dy_ref[...]
        hv = h_ref[...]
        dxg_ref[:, DR:2 * DR] = (dyv * hv * dsilu).astype(dxg_ref.dtype)
        dh = dyv * silu
        rv = r_ref[...]
        iv = i_ref[...]
        xcv = xc_ref[...]
        lam_row = lam_ref[...]
        cl, a, em, mult = _lru_coeffs(lam_row, rv)
        b = jnp.where(row == tt - 1, acar[...], pltpu.roll(a, tt - 1, 0))
        gs = dh
        d = 1
        while d < tt:
            keep = row < tt - d
            b_sh = jnp.where(keep, pltpu.roll(b, tt - d, 0), 1.0)
            g_sh = jnp.where(keep, pltpu.roll(gs, tt - d, 0), 0.0)
            gs = gs + b * g_sh
            b = b * b_sh
            d *= 2
        gt = gs + b * gcar[...]
        xbuf[0:tt, :] = gt
        gcar[...] = xbuf[0:1, :]
        acar[...] = _lru_coeffs(lam_row, r_ref[0:1, :])[1]
        h_before = hprev_ref[PAD - 1:PAD, :] * not_first
        hprev = jnp.where(row == 0, h_before, pltpu.roll(hv, 1, 0))
        da = gt * hprev
        dmult = gt * (iv * xcv)
        di = gt * mult * xcv
        dxc = gt * mult * iv
        dlog_a = da * a - dmult * (1.0 - em) / mult
        dr = dlog_a * cl
        dlam_ref[...] += jnp.sum(dlog_a * rv, axis=0, keepdims=True) * (LRU_C * jax.nn.sigmoid(-lam_row))
        drp = dr * rv * (1.0 - rv)
        dip = di * iv * (1.0 - iv)
        dbr_ref[...] += jnp.sum(drp, axis=0, keepdims=True)
        dbi_ref[...] += jnp.sum(dip, axis=0, keepdims=True)
        drpb = drp.astype(BF16)
        dipb = dip.astype(BF16)
        xcb = xcv.astype(BF16)
        for n in range(NB):
            sl = slice(n * HEAD, (n + 1) * HEAD)
            dxcbuf[0:tt, sl] = dxc[:, sl] + _dot_nt(drpb[:, sl], wr_ref[n]) + _dot_nt(dipb[:, sl], wi_ref[n])
            dwr_ref[n] += _dot_tn(xcb[:, sl], drpb[:, sl])
            dwi_ref[n] += _dot_tn(xcb[:, sl], dipb[:, sl])
        dxc_all = dxcbuf[0:tt, :]
        dcb_ref[...] += jnp.sum(dxc_all, axis=0, keepdims=True)
        xbuf[0:PAD, :] = xprev_ref[...] * not_first
        xbuf[PAD:PAD + tt, :] = x_ref[...]
        dx = cw_ref[0:1, :] * dxcbuf[pl.ds(3, tt), :]
        for k in range(1, CONV_WIDTH):
            dx = dx + cw_ref[k:k + 1, :] * dxcbuf[pl.ds(3 - k, tt), :]
        dxg_ref[:, 0:DR] = dx.astype(dxg_ref.dtype)
        for k in range(CONV_WIDTH):
            dcw_ref[k:k + 1, :] += jnp.sum(xbuf[pl.ds(PAD - 3 + k, tt), :] * dxc_all, axis=0, keepdims=True)
        dxcbuf[tt:tt + PAD, :] = dxcbuf[0:PAD, :]

    rev = lambda j: pl.BlockSpec((tt, DR), lambda s: (nc - 1 - s, j))
    prev = pl.BlockSpec((PAD, DR), lambda s: (jnp.maximum((nc - 1 - s) * per - 1, 0), 0))
    vec = pl.BlockSpec((1, DR), lambda s: (0, 0))
    gate = pl.BlockSpec((NB, HEAD, HEAD), lambda s: (0, 0, 0))
    taps = pl.BlockSpec((CONV_WIDTH, DR), lambda s: (0, 0))
    vec_out = jax.ShapeDtypeStruct((1, DR), F32)
    gate_out = jax.ShapeDtypeStruct((NB, HEAD, HEAD), F32)
    return pl.pallas_call(
        body, name=name, grid=(nc,),
        in_specs=[rev(0), rev(0), rev(1), rev(0), rev(0), rev(0), rev(0), prev, prev, taps, gate, gate, vec],
        out_specs=[pl.BlockSpec((tt, 2 * DR), lambda s: (nc - 1 - s, 0)), taps, vec, gate, vec, gate, vec, vec],
        out_shape=[jax.ShapeDtypeStruct((T, 2 * DR), BF16), jax.ShapeDtypeStruct((CONV_WIDTH, DR), F32),
                   vec_out, gate_out, vec_out, gate_out, vec_out, vec_out],
        scratch_shapes=[pltpu.VMEM((tt + PAD, DR), F32), pltpu.VMEM((tt + PAD, DR), F32),
                        pltpu.VMEM((1, DR), F32), pltpu.VMEM((1, DR), F32)],
        compiler_params=_params("arbitrary"),
    )(dy, proj, proj, xc, r, i, h, proj, h, conv_w, w_r, w_i, lam.reshape(1, DR))


def _mem_probs(q, k, scale):
    s = _dot_nt(q, k) * scale
    p = jnp.exp(s - jnp.max(s, axis=-1, keepdims=True))
    return p * (1.0 / jnp.sum(p, axis=-1, keepdims=True))


def _memattn_fwd(proj, mkv, DR, DM, name):
    T = proj.shape[0]
    M = mkv.shape[0]
    NH = DM // HEAD
    tt = _row_tile(T, 512)
    qcol = 2 * DR // DM
    scale = HEAD ** -0.5

    def body(q_ref, g_ref, k_ref, v_ref, y_ref):
        for n in range(NH):
            sl = slice(n * HEAD, (n + 1) * HEAD)
            p = _mem_probs(q_ref[:, sl].astype(BF16), k_ref[:, sl], scale)
            o = _dot(p.astype(BF16), v_ref[:, sl])
            g = g_ref[:, sl]
            y_ref[:, sl] = (o * (g * jax.nn.sigmoid(g))).astype(y_ref.dtype)

    return pl.pallas_call(
        body, name=name, grid=(T // tt,),
        in_specs=[pl.BlockSpec((tt, DM), lambda t: (t, qcol)), pl.BlockSpec((tt, DM), lambda t: (t, qcol + 1)),
                  pl.BlockSpec((M, DM), lambda t: (0, 0)), pl.BlockSpec((M, DM), lambda t: (0, 1))],
        out_specs=pl.BlockSpec((tt, DM), lambda t: (t, 0)),
        out_shape=jax.ShapeDtypeStruct((T, DM), BF16),
        compiler_params=_params("parallel"),
    )(proj, proj, mkv, mkv)


def _memattn_bwd(dy, proj, mkv, DR, DM, name):
    T = proj.shape[0]
    M = mkv.shape[0]
    NH = DM // HEAD
    tt = _row_tile(T, 512)
    qcol = 2 * DR // DM
    scale = HEAD ** -0.5

    def body(dy_ref, q_ref, g_ref, k_ref, v_ref, dqg_ref, dkv_ref):
        @pl.when(pl.program_id(0) == 0)
        def _():
            dkv_ref[...] = jnp.zeros_like(dkv_ref)

        for n in range(NH):
            sl = slice(n * HEAD, (n + 1) * HEAD)
            qb = q_ref[:, sl].astype(BF16)
            kb = k_ref[:, sl]
            vb = v_ref[:, sl]
            p = _mem_probs(qb, kb, scale)
            pb = p.astype(BF16)
            o = _dot(pb, vb)
            silu, dsilu = _silu_and_grad(g_ref[:, sl])
            dyv = dy_ref[:, sl]
            dqg_ref[:, DM + n * HEAD:DM + (n + 1) * HEAD] = (dyv * o * dsilu).astype(dqg_ref.dtype)
            dob = (dyv * silu).astype(BF16)
            dp = _dot_nt(dob, vb)
            ds = (p * (dp - jnp.sum(dp * p, axis=-1, keepdims=True)) * scale).astype(BF16)
            dqg_ref[:, sl] = _dot(ds, kb).astype(dqg_ref.dtype)
            dkv_ref[:, sl] += _dot_tn(ds, qb)
            dkv_ref[:, DM + n * HEAD:DM + (n + 1) * HEAD] += _dot_tn(pb, dob)

    return pl.pallas_call(
        body, name=name, grid=(T // tt,),
        in_specs=[pl.BlockSpec((tt, DM), lambda t: (t, DR // DM)),
                  pl.BlockSpec((tt, DM), lambda t: (t, qcol)), pl.BlockSpec((tt, DM), lambda t: (t, qcol + 1)),
                  pl.BlockSpec((M, DM), lambda t: (0, 0)), pl.BlockSpec((M, DM), lambda t: (0, 1))],
        out_specs=[pl.BlockSpec((tt, 2 * DM), lambda t: (t, 0)), pl.BlockSpec((M, 2 * DM), lambda t: (0, 0))],
        out_shape=[jax.ShapeDtypeStruct((T, 2 * DM), BF16), jax.ShapeDtypeStruct((M, 2 * DM), F32)],
        compiler_params=_params("arbitrary"),
    )(dy, proj, proj, mkv, mkv)


def _sb_blocks(T):
    tk = _row_tile(T // 2, 256)
    tq = 2 * tk
    assert T % tq == 0
    return tq, tk


def _sb_upper(tk):
    row = lax.broadcasted_iota(jnp.int32, (tk, tk), 0)
    col = lax.broadcasted_iota(jnp.int32, (tk, tk), 1)
    return (row > col).astype(BF16)


def _sb_causal(tq, tk, d):
    row = lax.broadcasted_iota(jnp.int32, (tq, tk), 0)
    col = lax.broadcasted_iota(jnp.int32, (tq, tk), 1)
    return col + d * tk < row


def _later_sum(x, upper):
    n = x.shape[0]
    hi, lo = _split_bf16(x)
    both = _dot(jnp.concatenate([hi, lo], axis=0), upper)
    return both[0:n] + both[n:2 * n]


def _sb_weights(q, kb, later_c, causal, upper, scale):
    return _sb_weights_of(_dot_nt(q, kb) * scale, later_c, causal, upper)


def _sb_weights_of(z, later_c, causal, upper):
    sp = jnp.maximum(z, 0.0) + jnp.log(1.0 + jnp.exp(-jnp.abs(z)))
    lk = -sp if causal is None else jnp.where(causal, -sp, 0.0)
    later = _later_sum(lk, upper) + later_c
    w = jnp.exp(z - sp + later)
    if causal is not None:
        w = jnp.where(causal, w, 0.0)
    return z, sp, lk, w


def _sb_fwd(proj, kv, DR, name):
    T = proj.shape[0]
    NH = DR // HEAD
    tq, tk = _sb_blocks(T)
    span = tq // tk
    scale = HEAD ** -0.5

    def body(q_ref, g_ref, k_ref, v_ref, y_ref, o_ref):
        i = pl.program_id(1)
        q = q_ref[...].astype(BF16)
        upper = _sb_upper(tk)

        def tile(jb, carry, causal):
            later_c, acc = carry
            start = pl.multiple_of(jb * tk, tk)
            kb = k_ref[pl.ds(start, tk), :]
            vb = v_ref[pl.ds(start, tk), :]
            _, _, lk, w = _sb_weights(q, kb, later_c, causal, upper, scale)
            hi, lo = _split_bf16(w)
            pv = _dot(jnp.concatenate([hi, lo], axis=0), vb)
            return later_c + jnp.sum(lk, axis=-1, keepdims=True), acc + pv[0:tq] + pv[tq:2 * tq]

        carry = (jnp.zeros((tq, 1), F32), jnp.zeros((tq, HEAD), F32))
        for d in reversed(range(span)):
            carry = tile(span * i + d, carry, _sb_causal(tq, tk, d))

        half = tq // 2
        q_a, q_b = q[0:half], q[half:tq]
        n = span * i

        def scores(qx, jb):
            start = pl.multiple_of(jb * tk, tk)
            return _dot_nt(qx, k_ref[pl.ds(start, tk), :]) * scale

        def finish(z, jb, c):
            later_c, acc = c
            start = pl.multiple_of(jb * tk, tk)
            _, _, lk, w = _sb_weights_of(z, later_c, None, upper)
            hi, lo = _split_bf16(w)
            pv = _dot(jnp.concatenate([hi, lo], axis=0), v_ref[pl.ds(start, tk), :])
            return later_c + jnp.sum(lk, axis=-1, keepdims=True), acc + pv[0:half] + pv[half:tq]

        def one(jb, c):
            c_a, c_b, z_b = c
            c_a = finish(scores(q_a, jb), jb, c_a)
            c_b = finish(z_b, jb, c_b)
            return c_a, c_b, scores(q_b, jnp.maximum(jb - 1, 0))

        def step(jj, c):
            for d in range(span):
                c = one(n - 1 - span * jj - d, c)
            return c

        later_c, acc = carry
        c_a, c_b, _ = lax.fori_loop(
            0, i, step, ((later_c[0:half], acc[0:half]), (later_c[half:tq], acc[half:tq]),
                         scores(q_b, jnp.maximum(n - 1, 0))))
        o_ref[0:half, :] = c_a[1]
        o_ref[half:tq, :] = c_b[1]
        g = g_ref[...]
        y_ref[...] = (o_ref[...] * (g * jax.nn.sigmoid(g))).astype(y_ref.dtype)

    blk = lambda off: pl.BlockSpec((tq, HEAD), lambda h, i: (i, off + h))
    whole = lambda off: pl.BlockSpec((T, HEAD), lambda h, i: (0, off + h))
    return pl.pallas_call(
        body, name=name, grid=(NH, T // tq),
        in_specs=[blk(0), blk(NH), whole(0), whole(NH)],
        out_specs=[blk(0), blk(0)],
        out_shape=[jax.ShapeDtypeStruct((T, DR), BF16), jax.ShapeDtypeStruct((T, DR), F32)],
        compiler_params=_params("parallel", "arbitrary"),
    )(proj, proj, kv, kv)


def _sb_bwd(dy, proj, kv, o, DR, name):
    T = proj.shape[0]
    NH = DR // HEAD
    tq, tk = _sb_blocks(T)
    span = tq // tk
    scale = HEAD ** -0.5

    def body(dy_ref, q_ref, g_ref, k_ref, v_ref, o_ref, dq_ref, dg_ref, dk_ref, dv_ref):
        i = pl.program_id(1)

        @pl.when(i == 0)
        def _():
            dk_ref[...] = jnp.zeros_like(dk_ref)
            dv_ref[...] = jnp.zeros_like(dv_ref)

        qf = q_ref[...]
        q = qf.astype(BF16)
        q_t = qf.T.astype(BF16)
        upper = _sb_upper(tk)
        silu, dsilu = _silu_and_grad(g_ref[...])
        dyv = dy_ref[...]
        ov = o_ref[...]
        dg_ref[...] = (dyv * ov * dsilu).astype(dg_ref.dtype)
        do = dyv * silu
        dob = do.astype(BF16)
        do_t = do.T.astype(BF16)
        total = jnp.sum(dob.astype(F32) * ov, axis=-1, keepdims=True)

        def tile(jb, carry, causal):
            later_c, e_after, dq = carry
            start = pl.multiple_of(jb * tk, tk)
            kb = k_ref[pl.ds(start, tk), :]
            vb = v_ref[pl.ds(start, tk), :]
            z, sp, lk, w = _sb_weights(q, kb, later_c, causal, upper, scale)
            e = _dot_nt(dob, vb) * w
            dz = e - (total - (_later_sum(e, upper) + e_after)) * jnp.exp(z - sp)
            if causal is not None:
                dz = jnp.where(causal, dz, 0.0)
            dzb = dz.astype(BF16)
            dk_ref[jb] += _dot(q_t, dzb) * scale
            dv_ref[jb] += _dot(do_t, w.astype(BF16))
            return (later_c + jnp.sum(lk, axis=-1, keepdims=True), e_after + jnp.sum(e, axis=-1, keepdims=True),
                    dq + _dot(dzb, kb))

        carry = (jnp.zeros((tq, 1), F32), jnp.zeros((tq, 1), F32), jnp.zeros((tq, HEAD), F32))
        for d in reversed(range(span)):
            carry = tile(span * i + d, carry, _sb_causal(tq, tk, d))

        def step(jj, c):
            for d in range(span):
                c = tile(span * (i - jj) - 1 - d, c, None)
            return c

        _, _, dq = lax.fori_loop(0, i, step, carry)
        dq_ref[...] = (dq * scale).astype(dq_ref.dtype)

    blk = lambda off: pl.BlockSpec((tq, HEAD), lambda h, i: (i, off + h))
    whole = lambda off: pl.BlockSpec((T, HEAD), lambda h, i: (0, off + h))
    keys_t = pl.BlockSpec((None, T // tk, HEAD, tk), lambda h, i: (h, 0, 0, 0))
    keys_t_shape = jax.ShapeDtypeStruct((NH, T // tk, HEAD, tk), F32)
    return pl.pallas_call(
        body, name=name, grid=(NH, T // tq),
        in_specs=[blk(0), blk(0), blk(NH), whole(0), whole(NH), blk(0)],
        out_specs=[blk(0), blk(0), keys_t, keys_t],
        out_shape=[jax.ShapeDtypeStruct((T, DR), BF16), jax.ShapeDtypeStruct((T, DR), BF16),
                   keys_t_shape, keys_t_shape],
        compiler_params=_params("parallel", "arbitrary"),
    )(dy, proj, proj, kv, kv, o)


def _merge_dkv(parts, name):
    NH, nblk, _, tk = parts[0][0].shape
    DR = NH * HEAD
    n = len(parts)

    def body(*refs):
        o_ref = refs[-1]
        for which in range(2):
            for h in range(NH):
                acc = refs[which][h]
                for p in range(1, n):
                    acc = acc + refs[2 * p + which][h]
                col = which * DR + h * HEAD
                o_ref[:, col:col + HEAD] = acc.T.astype(o_ref.dtype)

    blk = pl.BlockSpec((NH, None, HEAD, tk), lambda t: (0, t, 0, 0))
    return pl.pallas_call(
        body, name=name, grid=(nblk,),
        in_specs=[blk] * (2 * n),
        out_specs=pl.BlockSpec((tk, 2 * DR), lambda t: (t, 0)),
        out_shape=jax.ShapeDtypeStruct((nblk * tk, 2 * DR), BF16),
        compiler_params=_params("parallel"),
    )(*[a for pair in parts for a in pair])


def _final_loss(h, g, target, name):
    T, D = h.shape
    tt = _row_tile(T, 256)

    def body(h_ref, g_ref, t_ref, dh_ref, dg_ref, sq_ref):
        @pl.when(pl.program_id(0) == 0)
        def _():
            dg_ref[...] = jnp.zeros_like(dg_ref)
            sq_ref[...] = jnp.zeros_like(sq_ref)

        hv = h_ref[...]
        gv = g_ref[...]
        r = lax.rsqrt(jnp.mean(hv * hv, axis=-1, keepdims=True) + EPS)
        xhat = hv * r
        err = xhat * gv - t_ref[...]
        sq_ref[...] += jnp.sum(err * err, axis=0, keepdims=True)
        dy = err * (1.0 / D)
        dxhat = dy * gv
        dh_ref[...] = r * (dxhat - xhat * jnp.mean(dxhat * xhat, axis=-1, keepdims=True))
        dg_ref[...] += jnp.sum(dy * xhat, axis=0, keepdims=True)

    row = pl.BlockSpec((tt, D), lambda i: (i, 0))
    vec = pl.BlockSpec((1, D), lambda i: (0, 0))
    return pl.pallas_call(
        body, name=name, grid=(T // tt,),
        in_specs=[row, vec, row],
        out_specs=[row, vec, vec],
        out_shape=[jax.ShapeDtypeStruct((T, D), F32), jax.ShapeDtypeStruct((1, D), F32),
                   jax.ShapeDtypeStruct((1, D), F32)],
        compiler_params=_params("arbitrary"),
    )(h, g.reshape(1, D), target)


def _sum_parts(parts_ref):
    g = parts_ref[0].astype(F32)
    for s in range(1, parts_ref.shape[0]):
        g = g + parts_ref[s].astype(F32)
    return g


def _adamw(parts, w, m, v, name):
    P, R, C = parts.shape
    tr = _rows_for(R, P * C * 4)

    def body(p_ref, w_ref, m_ref, v_ref, g_ref, d_ref, nm_ref, nv_ref):
        g = _sum_parts(p_ref)
        nm = ADAM_B1 * m_ref[...] + (1.0 - ADAM_B1) * g
        nv = ADAM_B2 * v_ref[...] + (1.0 - ADAM_B2) * jnp.square(g)
        m_hat = nm / (1.0 - ADAM_B1 ** ADAM_STEP)
        v_hat = nv / (1.0 - ADAM_B2 ** ADAM_STEP)
        g_ref[...] = g
        d_ref[...] = -ADAM_LR * (m_hat / (jnp.sqrt(v_hat) + ADAM_EPS) + ADAM_WD * w_ref[...])
        nm_ref[...] = nm
        nv_ref[...] = nv

    row = pl.BlockSpec((tr, C), lambda i: (i, 0))
    out = jax.ShapeDtypeStruct((R, C), F32)
    return pl.pallas_call(
        body, name=name, grid=(R // tr,),
        in_specs=[pl.BlockSpec((P, tr, C), lambda i: (0, i, 0)), row, row, row],
        out_specs=[row] * 4,
        out_shape=[out] * 4,
        compiler_params=_params("parallel"),
    )(parts, w, m, v)


def _sum_devices(parts, name):
    P, R, C = parts.shape
    tr = _rows_for(R, P * C * 4)

    def body(p_ref, o_ref):
        o_ref[...] = _sum_parts(p_ref)

    return pl.pallas_call(
        body, name=name, grid=(R // tr,),
        in_specs=[pl.BlockSpec((P, tr, C), lambda i: (0, i, 0))],
        out_specs=pl.BlockSpec((tr, C), lambda i: (i, 0)),
        out_shape=jax.ShapeDtypeStruct((R, C), F32),
        compiler_params=_params("parallel"),
    )(parts)


def _mesh_position():
    return lax.axis_index("x"), lax.axis_index("y"), lax.axis_index("c")


def _device_index(p):
    return 4 * p[0] + 2 * p[1] + p[2]


def _all_gather(arrs, name):
    n = len(arrs)

    def body(*refs):
        ins, outs = refs[:n], refs[n:2 * n]
        send_sems, recv_sems, local_sems = refs[2 * n:]
        x, y, c = _mesh_position()
        me, sibling = (x, y, c), (x, y, 1 - c)
        chips = [(1 - x, y), (x, 1 - y), (1 - x, 1 - y)]

        def slot(a, p):
            return outs[a].at[_device_index(p)]

        def copy(a, k, block, to, src=None):
            return pltpu.make_async_remote_copy(
                src_ref=slot(a, block) if src is None else src, dst_ref=slot(a, block),
                send_sem=send_sems.at[a, k], recv_sem=recv_sems.at[a, k],
                device_id=to, device_id_type=pl.DeviceIdType.MESH)

        mine = [pltpu.make_async_copy(ins[a], slot(a, me), local_sems.at[a]) for a in range(n)]
        for cp in mine:
            cp.start()
        first = []
        for a in range(n):
            first.append(copy(a, 0, me, sibling, src=ins[a]))
            first += [copy(a, 1 + j, me, (*chip, c), src=ins[a]) for j, chip in enumerate(chips)]
        for cp in first:
            cp.start()
        passed = []
        for a in range(n):
            for j, chip in enumerate(chips):
                copy(a, 1 + j, (*chip, c), me).wait_recv()
                fwd = copy(a, 4 + j, (*chip, c), sibling)
                fwd.start()
                passed.append(fwd)
        for a in range(n):
            copy(a, 0, sibling, me).wait_recv()
            for j, chip in enumerate(chips):
                copy(a, 4 + j, (*chip, 1 - c), me).wait_recv()
        for cp in first + passed:
            cp.wait_send()
        for cp in mine:
            cp.wait()

    return pl.pallas_call(
        body, name=name,
        in_specs=[HBM_SPEC] * n, out_specs=[HBM_SPEC] * n,
        out_shape=[jax.ShapeDtypeStruct((N_DEV,) + a.shape, a.dtype) for a in arrs],
        scratch_shapes=[pltpu.SemaphoreType.DMA((n, 7)), pltpu.SemaphoreType.DMA((n, 7)),
                        pltpu.SemaphoreType.DMA((n,))],
    )(*arrs)


def _all_to_all(arrs, name):
    n = len(arrs)

    def body(*refs):
        ins, outs = refs[:n], refs[n:2 * n]
        send_sems, recv_sems, local_sems = refs[2 * n:]
        x, y, c = _mesh_position()
        me = _device_index((x, y, c))
        peers = [(1 - x if k & 4 else x, 1 - y if k & 2 else y, 1 - c if k & 1 else c) for k in range(1, N_DEV)]
        local = [pltpu.make_async_copy(ins[a].at[me], outs[a].at[me], local_sems.at[a]) for a in range(n)]
        for cp in local:
            cp.start()
        sends, recvs = [], []
        for a in range(n):
            for k, peer in enumerate(peers):
                there = _device_index(peer)
                sems = dict(send_sem=send_sems.at[a, k], recv_sem=recv_sems.at[a, k],
                            device_id=peer, device_id_type=pl.DeviceIdType.MESH)
                sends.append(pltpu.make_async_remote_copy(src_ref=ins[a].at[there], dst_ref=outs[a].at[me], **sems))
                recvs.append(pltpu.make_async_remote_copy(src_ref=ins[a].at[there], dst_ref=outs[a].at[there], **sems))
        for cp in sends:
            cp.start()
        for cp in recvs:
            cp.wait_recv()
        for cp in sends:
            cp.wait_send()
        for cp in local:
            cp.wait()

    return pl.pallas_call(
        body, name=name,
        in_specs=[HBM_SPEC] * n, out_specs=[HBM_SPEC] * n,
        out_shape=[jax.ShapeDtypeStruct(a.shape, a.dtype) for a in arrs],
        scratch_shapes=[pltpu.SemaphoreType.DMA((n, 7)), pltpu.SemaphoreType.DMA((n, 7)),
                        pltpu.SemaphoreType.DMA((n,))],
    )(*arrs)


def _pack(arrs, row_multiple):
    parts = []
    rows = 0
    for a in arrs:
        flat = a.reshape(-1).astype(F32)
        r = -(-flat.shape[0] // (8 * LANES)) * 8
        parts.append(jnp.pad(flat, (0, r * LANES - flat.shape[0])).reshape(r, LANES))
        rows += r
    pad = -rows % row_multiple
    if pad:
        parts.append(jnp.zeros((pad, LANES), F32))
    return jnp.concatenate(parts, axis=0)


def _unpack(buf, shapes, lead=()):
    out = []
    r0 = 0
    for shape in shapes:
        size = 1
        for s in shape:
            size *= s
        r = -(-size // (8 * LANES)) * 8
        part = buf[..., r0:r0 + r, :].reshape(lead + (r * LANES,))[..., :size]
        out.append(part.reshape(lead + tuple(shape)))
        r0 += r
    return out


def _gathered_cols(g):
    g = jnp.moveaxis(g, 0, -2)
    return g.reshape(g.shape[:-2] + (g.shape[-2] * g.shape[-1],))


def kernel(x, mem, mem_norm, w_mem_kv, norm_a, w_in_a, conv_w, conv_b, w_rec_gate, b_rec_gate, w_in_gate, b_in_gate, lru_lambda, w_out_a, kv_norm, w_kv, norm_b, w_in_b, w_out_b, final_norm, loss_target, m_mem_norm, m_w_mem_kv, m_norm_a, m_w_in_a, m_conv_w, m_conv_b, m_w_rec_gate, m_b_rec_gate, m_w_in_gate, m_b_in_gate, m_lru_lambda, m_w_out_a, m_kv_norm, m_w_kv, m_norm_b, m_w_in_b, m_w_out_b, m_final_norm, v_mem_norm, v_w_mem_kv, v_norm_a, v_w_in_a, v_conv_w, v_conv_b, v_w_rec_gate, v_b_rec_gate, v_w_in_gate, v_b_in_gate, v_lru_lambda, v_w_out_a, v_kv_norm, v_w_kv, v_norm_b, v_w_in_b, v_w_out_b, v_final_norm):
    xs = x[0]
    T, D = xs.shape
    L = w_mem_kv.shape[0]
    NA = w_in_a.shape[0]
    NB = w_in_b.shape[0]
    DM2 = w_mem_kv.shape[2]
    DM = DM2 // 2
    DR = w_rec_gate.shape[1] * w_rec_gate.shape[2]
    me = _device_index(_mesh_position())

    small_sharded = [norm_a, conv_w, conv_b, b_rec_gate, b_in_gate, lru_lambda]
    big = [w_mem_kv[l] for l in range(L)]
    big += [w_in_a[l] for l in range(NA)] + [w_out_a[l] for l in range(NA)] + [w_kv]
    big += [w_in_b[l] for l in range(NB)] + [w_out_b[l] for l in range(NB)]
    gathered = _all_gather([w.astype(BF16) for w in big] + [_pack(small_sharded, 8)], "gather_params")
    it = iter(gathered[:-1])
    g_mem = [next(it).reshape(1, D, DM2) for _ in range(L)]
    g_in_a = [next(it) for _ in range(NA)]
    g_out_a = [next(it).reshape(1, -1, D) for _ in range(NA)]
    g_kv = next(it)
    g_in_b = [next(it) for _ in range(NB)]
    g_out_b = [next(it).reshape(1, -1, D) for _ in range(NB)]
    norm_a_f, conv_w_f, conv_b_f, b_r_f, b_i_f, lam_f = [
        _gathered_cols(s) for s in _unpack(gathered[-1], [s.shape for s in small_sharded], lead=(N_DEV,))]
    w_r_bf = w_rec_gate.astype(BF16)
    w_i_bf = w_in_gate.astype(BF16)

    zeros_mem = jnp.zeros_like(mem[0])
    mem_n = _rms_fwd(mem[0], mem_norm, "rms_mem")
    mkv = [_mm_nn(mem_n, g_mem[l], out_dtype=BF16, name=f"mm_mem_kv{l}") for l in range(L)]

    h = xs
    saved_a = []
    for l in range(NA):
        u = _rms_fwd(h, norm_a_f[l], f"rms_a{l}")
        proj = _mm_nn(u, g_in_a[l], name=f"mm_in_a{l}")
        y_rnn, xc, r, i, hr = _rglru_fwd(proj, conv_w_f[l], conv_b_f[l], w_r_bf[l], b_r_f[l], w_i_bf[l], b_i_f[l],
                                         lam_f[l], f"rglru_fwd{l}")
        y_mem = _memattn_fwd(proj, mkv[l], DR, DM, f"memattn_fwd_a{l}")
        ycat = jnp.concatenate([y_rnn, y_mem], axis=-1)
        h_next = _mm_nn(ycat, g_out_a[l], res=h, name=f"mm_out_a{l}")
        saved_a.append((h, u, proj, xc, r, i, hr, ycat))
        h = h_next
    h_kv = h
    u_kv = _rms_fwd(h_kv, kv_norm, "rms_kv")
    kv = _mm_nn(u_kv, g_kv, out_dtype=BF16, name="mm_kv")
    saved_b = []
    for j in range(NB):
        u = _rms_fwd(h, norm_b[j], f"rms_b{j}")
        proj = _mm_nn(u, g_in_b[j], name=f"mm_in_b{j}")
        y_sb, o_sb = _sb_fwd(proj, kv, DR, f"sb_fwd{j}")
        y_mem = _memattn_fwd(proj, mkv[NA + j], DR, DM, f"memattn_fwd_b{j}")
        ycat = jnp.concatenate([y_sb, y_mem], axis=-1)
        h_next = _mm_nn(ycat, g_out_b[j], res=h, name=f"mm_out_b{j}")
        saved_b.append((h, u, proj, o_sb, ycat))
        h = h_next

    dh, d_final_norm, sq = _final_loss(h, final_norm, loss_target[0], "final_loss")
    loss = lax.psum(0.5 * jnp.sum(sq) / D, ("x", "y", "c"))

    big_grads = {}
    dmkv = [None] * L
    d_norm_b = [None] * NB
    dkv_parts = []
    for j in reversed(range(NB)):
        h_in, u, proj, o_sb, ycat = saved_b[j]
        dy = _mm_nt(dh, g_out_b[j], name=f"mm_dy_b{j}")
        big_grads[f"out_b{j}"] = _mm_tn(ycat, dh, 1, name=f"mm_dw_out_b{j}").reshape(N_DEV, -1, D)
        dq, dg, dk, dv = _sb_bwd(dy, proj, kv, o_sb, DR, f"sb_bwd{j}")
        dkv_parts.append((dk, dv))
        dqg_mem, dmkv[NA + j] = _memattn_bwd(dy, proj, mkv[NA + j], DR, DM, f"memattn_bwd_b{j}")
        dproj = jnp.concatenate([dq, dg, dqg_mem], axis=-1)
        du = _mm_nt(dproj, g_in_b[j], name=f"mm_du_b{j}")
        big_grads[f"in_b{j}"] = _mm_tn(u, dproj, N_DEV, name=f"mm_dw_in_b{j}")
        dh, d_norm_b[j] = _rms_bwd(du, h_in, norm_b[j], dh, f"rms_bwd_b{j}")

    dkv = _merge_dkv(dkv_parts, "merge_dkv")
    du_kv = _mm_nt(dkv, g_kv, name="mm_du_kv")
    big_grads["kv"] = _mm_tn(u_kv, dkv, N_DEV, name="mm_dw_kv")
    dh, d_kv_norm = _rms_bwd(du_kv, h_kv, kv_norm, dh, "rms_bwd_kv")

    d_norm_a, d_conv_w, d_conv_b, d_w_r, d_b_r, d_w_i, d_b_i, d_lam = ([None] * NA for _ in range(8))
    for l in reversed(range(NA)):
        h_in, u, proj, xc, r, i, hr, ycat = saved_a[l]
        dy = _mm_nt(dh, g_out_a[l], name=f"mm_dy_a{l}")
        big_grads[f"out_a{l}"] = _mm_tn(ycat, dh, 1, name=f"mm_dw_out_a{l}").reshape(N_DEV, -1, D)
        (dxg, d_conv_w[l], d_conv_b[l], d_w_r[l], d_b_r[l], d_w_i[l], d_b_i[l], d_lam[l]) = _rglru_bwd(
            dy, proj, xc, r, i, hr, conv_w_f[l], w_r_bf[l], w_i_bf[l], lam_f[l], f"rglru_bwd{l}")
        dqg_mem, dmkv[l] = _memattn_bwd(dy, proj, mkv[l], DR, DM, f"memattn_bwd_a{l}")
        dproj = jnp.concatenate([dxg, dqg_mem], axis=-1)
        du = _mm_nt(dproj, g_in_a[l], name=f"mm_du_a{l}")
        big_grads[f"in_a{l}"] = _mm_tn(u, dproj, N_DEV, name=f"mm_dw_in_a{l}")
        dh, d_norm_a[l] = _rms_bwd(du, h_in, norm_a_f[l], dh, f"rms_bwd_a{l}")
    grad_x = dh.reshape(x.shape)

    dmkv_all = jnp.concatenate(dmkv, axis=-1)
    dmem_n = _mm_nt(dmkv_all, jnp.concatenate(g_mem, axis=0), name="mm_dmem")
    for l in range(L):
        big_grads[f"mem_kv{l}"] = _mm_tn(mem_n, dmkv[l], 1, name=f"mm_dw_mem_kv{l}").reshape(N_DEV, -1, DM2)
    _, d_mem_norm = _rms_bwd(dmem_n, mem[0], mem_norm, zeros_mem, "rms_bwd_mem")

    names = ([f"mem_kv{l}" for l in range(L)] + [f"in_a{l}" for l in range(NA)] + [f"out_a{l}" for l in range(NA)] + ["kv"]
             + [f"in_b{j}" for j in range(NB)] + [f"out_b{j}" for j in range(NB)])
    received = dict(zip(names, _all_to_all([big_grads[k] for k in names], "scatter_grads")))

    def update(key, w, m, v):
        shape = w.shape
        two_d = (-1, shape[-1])
        return [o.reshape(shape) for o in _adamw(received[key], w.reshape(two_d), m.reshape(two_d),
                                                 v.reshape(two_d), f"adamw_{key}")]

    def update_layers(prefix, w, m, v):
        per_layer = [update(f"{prefix}{l}", w[l], m[l], v[l]) for l in range(w.shape[0])]
        return [jnp.stack([per_layer[l][k] for l in range(w.shape[0])]) for k in range(4)]

    upd = {
        "w_mem_kv": update_layers("mem_kv", w_mem_kv, m_w_mem_kv, v_w_mem_kv),
        "w_in_a": update_layers("in_a", w_in_a, m_w_in_a, v_w_in_a),
        "w_out_a": update_layers("out_a", w_out_a, m_w_out_a, v_w_out_a),
        "w_kv": update("kv", w_kv, m_w_kv, v_w_kv),
        "w_in_b": update_layers("in_b", w_in_b, m_w_in_b, v_w_in_b),
        "w_out_b": update_layers("out_b", w_out_b, m_w_out_b, v_w_out_b),
    }

    small_full = {
        "mem_norm": d_mem_norm.reshape(-1),
        "norm_a": jnp.concatenate(d_norm_a, axis=0),
        "conv_w": jnp.stack(d_conv_w),
        "conv_b": jnp.concatenate(d_conv_b, axis=0),
        "w_rec_gate": jnp.stack(d_w_r),
        "b_rec_gate": jnp.concatenate(d_b_r, axis=0),
        "w_in_gate": jnp.stack(d_w_i),
        "b_in_gate": jnp.concatenate(d_b_i, axis=0),
        "lru_lambda": jnp.concatenate(d_lam, axis=0),
        "kv_norm": d_kv_norm.reshape(-1),
        "norm_b": jnp.concatenate(d_norm_b, axis=0),
        "final_norm": d_final_norm.reshape(-1),
    }
    small_names = list(small_full)
    (small_gathered,) = _all_gather([_pack([small_full[k] for k in small_names], 256)], "gather_small_grads")
    small_sum = _sum_devices(small_gathered, "sum_small_grads")
    small_grad = dict(zip(small_names, _unpack(small_sum, [small_full[k].shape for k in small_names])))
    small_w = {"mem_norm": (mem_norm, m_mem_norm, v_mem_norm), "norm_a": (norm_a, m_norm_a, v_norm_a),
               "conv_w": (conv_w, m_conv_w, v_conv_w), "conv_b": (conv_b, m_conv_b, v_conv_b),
               "w_rec_gate": (w_rec_gate, m_w_rec_gate, v_w_rec_gate),
               "b_rec_gate": (b_rec_gate, m_b_rec_gate, v_b_rec_gate),
               "w_in_gate": (w_in_gate, m_w_in_gate, v_w_in_gate), "b_in_gate": (b_in_gate, m_b_in_gate, v_b_in_gate),
               "lru_lambda": (lru_lambda, m_lru_lambda, v_lru_lambda), "kv_norm": (kv_norm, m_kv_norm, v_kv_norm),
               "norm_b": (norm_b, m_norm_b, v_norm_b), "final_norm": (final_norm, m_final_norm, v_final_norm)}
    for k in small_names:
        w = small_w[k][0]
        if small_grad[k].shape != w.shape:
            n = w.shape[-1]
            small_grad[k] = lax.dynamic_slice_in_dim(small_grad[k], me * n, n, axis=-1)
    small_shapes = [small_w[k][0].shape for k in small_names]
    packed = [_pack([small_grad[k] for k in small_names], 256)[None]]
    packed += [_pack([small_w[k][t] for k in small_names], 256) for t in range(3)]
    small_out = [_unpack(o, small_shapes) for o in _adamw(*packed, "adamw_small")]
    for idx, k in enumerate(small_names):
        upd[k] = [small_out[t][idx] for t in range(4)]

    order = ["mem_norm", "w_mem_kv", "norm_a", "w_in_a", "conv_w", "conv_b", "w_rec_gate", "b_rec_gate", "w_in_gate",
             "b_in_gate", "lru_lambda", "w_out_a", "kv_norm", "w_kv", "norm_b", "w_in_b", "w_out_b", "final_norm"]
    return (loss, grad_x, *[upd[k][0] for k in order], *[upd[k][1] for k in order],
            *[upd[k][2] for k in order], *[upd[k][3] for k in order])
```

```python
import functools

import jax
import jax.numpy as jnp
from jax import lax
from jax.experimental import pallas as pl
from jax.experimental.pallas import tpu as pltpu

F32 = jnp.float32
BF16 = jnp.bfloat16

N_DEV = 8
EPS = 1e-6
LRU_C = 8.0
HEAD = 128
CONV_WIDTH = 4
LANES = 128
SB_EXHAUSTED = 110.0
VMEM_LIMIT = 56 * 1024 * 1024

ADAM_LR = 0.001
ADAM_B1 = 0.9
ADAM_B2 = 0.999
ADAM_EPS = 1e-08
ADAM_WD = 0.01
ADAM_STEP = 10

HBM_SPEC = pl.BlockSpec(memory_space=pltpu.HBM)


def _params(*semantics):
    return pltpu.CompilerParams(dimension_semantics=semantics, vmem_limit_bytes=VMEM_LIMIT)


def _tile(n, cap):
    if n <= cap:
        return n
    t = cap - cap % LANES
    while n % t:
        t -= LANES
    return t


def _row_tile(n, cap):
    if n <= cap:
        return n
    t = cap - cap % 8
    while t >= 8:
        if n % t == 0:
            return t
        t -= 8
    return n


def _rows_for(n, bytes_per_row, budget=2 * 1024 * 1024):
    return _row_tile(n, max(8, budget // bytes_per_row))


def _dot(a, b):
    return lax.dot_general(a, b, (((1,), (0,)), ((), ())), preferred_element_type=F32)


def _dot_nt(a, b):
    return lax.dot_general(a, b, (((1,), (1,)), ((), ())), preferred_element_type=F32)


def _dot_tn(a, b):
    return lax.dot_general(a, b, (((0,), (0,)), ((), ())), preferred_element_type=F32)


def _split_bf16(x):
    hi = x.astype(BF16)
    lo = (x - hi.astype(F32)).astype(BF16)
    return hi, lo


def _softplus(x):
    return jnp.maximum(x, 0.0) + jnp.log1p(jnp.exp(-jnp.abs(x)))


def _silu_and_grad(g):
    sg = jax.nn.sigmoid(g)
    return g * sg, sg * (1.0 + g * (1.0 - sg))


def _neg_expm1(x):
    series = -x * (1.0 + x * (0.5 + x * (1.0 / 6.0 + x * (1.0 / 24.0 + x * (1.0 / 120.0)))))
    return jnp.where(x > -0.1, series, 1.0 - jnp.exp(x))


def _rms_fwd(x, g, name):
    T, D = x.shape
    tt = _row_tile(T, 512)

    def body(x_ref, g_ref, o_ref):
        xv = x_ref[...]
        r = lax.rsqrt(jnp.mean(xv * xv, axis=-1, keepdims=True) + EPS)
        o_ref[...] = ((xv * r) * g_ref[...]).astype(o_ref.dtype)

    return pl.pallas_call(
        body, name=name, grid=(T // tt,),
        in_specs=[pl.BlockSpec((tt, D), lambda i: (i, 0)), pl.BlockSpec((1, D), lambda i: (0, 0))],
        out_specs=pl.BlockSpec((tt, D), lambda i: (i, 0)),
        out_shape=jax.ShapeDtypeStruct((T, D), BF16),
        compiler_params=_params("parallel"),
    )(x, g.reshape(1, D))


def _rms_bwd(du, h, g, res, name):
    T, D = h.shape
    tt = _row_tile(T, 256)

    def body(du_ref, h_ref, g_ref, res_ref, dh_ref, dg_ref):
        @pl.when(pl.program_id(0) == 0)
        def _():
            dg_ref[...] = jnp.zeros_like(dg_ref)

        hv = h_ref[...]
        duv = du_ref[...]
        r = lax.rsqrt(jnp.mean(hv * hv, axis=-1, keepdims=True) + EPS)
        xhat = hv * r
        dxhat = duv * g_ref[...]
        dh_ref[...] = res_ref[...] + r * (dxhat - xhat * jnp.mean(dxhat * xhat, axis=-1, keepdims=True))
        dg_ref[...] += jnp.sum(duv * xhat, axis=0, keepdims=True)

    row = pl.BlockSpec((tt, D), lambda i: (i, 0))
    vec = pl.BlockSpec((1, D), lambda i: (0, 0))
    return pl.pallas_call(
        body, name=name, grid=(T // tt,),
        in_specs=[row, row, vec, row],
        out_specs=[row, vec],
        out_shape=[jax.ShapeDtypeStruct((T, D), F32), jax.ShapeDtypeStruct((1, D), F32)],
        compiler_params=_params("arbitrary"),
    )(du, h, g.reshape(1, D), res)


def _mm_nn(a, b3, res=None, out_dtype=F32, name=None):
    M, K = a.shape
    S, _, ns = b3.shape
    tm = _row_tile(M, 1024 if a.dtype == BF16 else 512)
    tn = _tile(ns, 512)
    nj = ns // tn

    def body(*refs):
        if res is None:
            a_ref, b_ref, o_ref = refs
        else:
            a_ref, b_ref, r_ref, o_ref = refs
        acc = _dot(a_ref[...].astype(BF16), b_ref[...])
        if res is not None:
            acc = acc + r_ref[...]
        o_ref[...] = acc.astype(o_ref.dtype)

    in_specs = [pl.BlockSpec((tm, K), lambda i, j: (i, 0)),
                pl.BlockSpec((None, K, tn), lambda i, j: (j // nj, 0, j % nj))]
    args = [a, b3]
    if res is not None:
        in_specs.append(pl.BlockSpec((tm, tn), lambda i, j: (i, j)))
        args.append(res)
    return pl.pallas_call(
        body, name=name, grid=(M // tm, S * nj),
        in_specs=in_specs,
        out_specs=pl.BlockSpec((tm, tn), lambda i, j: (i, j)),
        out_shape=jax.ShapeDtypeStruct((M, S * ns), out_dtype),
        compiler_params=_params("parallel", "parallel"),
    )(*args)


def _mm_nt(a, b3, out_dtype=F32, name=None):
    M = a.shape[0]
    S, N, ns = b3.shape
    tm = _row_tile(M, 1024 if a.dtype == BF16 else 512)
    tn = _tile(N, 512)

    def body(a_ref, b_ref, o_ref):
        acc = _dot_nt(a_ref[:, 0:ns].astype(BF16), b_ref[0])
        for s in range(1, S):
            acc = acc + _dot_nt(a_ref[:, s * ns:(s + 1) * ns].astype(BF16), b_ref[s])
        o_ref[...] = acc.astype(o_ref.dtype)

    return pl.pallas_call(
        body, name=name, grid=(M // tm, N // tn),
        in_specs=[pl.BlockSpec((tm, S * ns), lambda i, j: (i, 0)),
                  pl.BlockSpec((S, tn, ns), lambda i, j: (0, j, 0))],
        out_specs=pl.BlockSpec((tm, tn), lambda i, j: (i, j)),
        out_shape=jax.ShapeDtypeStruct((M, N), out_dtype),
        compiler_params=_params("parallel", "parallel"),
    )(a, b3)


def _mm_tn(a, b, S, name=None):
    T, K = a.shape
    ns = b.shape[1] // S
    tk = _tile(K, 1024)
    tn = _tile(ns, 512)
    nj = ns // tn
    tt = _row_tile(T, 2048)
    nt = T // tt

    def body(a_ref, b_ref, o_ref, acc_ref):
        t = pl.program_id(2)
        p = _dot_tn(a_ref[...].astype(BF16), b_ref[...].astype(BF16))

        @pl.when(t == 0)
        def _():
            acc_ref[...] = p

        @pl.when(t > 0)
        def _():
            acc_ref[...] += p

        @pl.when(t == nt - 1)
        def _():
            o_ref[...] = acc_ref[...].astype(o_ref.dtype)

    return pl.pallas_call(
        body, name=name, grid=(K // tk, S * nj, nt),
        in_specs=[pl.BlockSpec((tt, tk), lambda i, j, t: (t, i)),
                  pl.BlockSpec((tt, tn), lambda i, j, t: (t, j))],
        out_specs=pl.BlockSpec((None, tk, tn), lambda i, j, t: (j // nj, i, j % nj)),
        out_shape=jax.ShapeDtypeStruct((S, K, ns), BF16),
        scratch_shapes=[pltpu.VMEM((tk, tn), F32)],
        compiler_params=_params("parallel", "parallel", "arbitrary"),
    )(a, b)


def _lru_coeffs(lam_row, r):
    cl = -LRU_C * _softplus(-lam_row)
    log_a = cl * r
    a = jnp.exp(log_a)
    em = _neg_expm1(2.0 * log_a)
    return cl, a, em, jnp.sqrt(em)


def _rglru_fwd(proj, conv_w, conv_b, w_r, b_r, w_i, b_i, lam, name):
    T = proj.shape[0]
    DR = conv_w.shape[1]
    NB = DR // HEAD
    tt = _row_tile(T, 256)
    PAD = 8

    def body(x_ref, g_ref, cw_ref, cb_ref, wr_ref, br_ref, wi_ref, bi_ref, lam_ref,
             y_ref, xc_ref, r_ref, i_ref, h_ref, xbuf, hcar):
        @pl.when(pl.program_id(0) == 0)
        def _():
            xbuf[0:PAD, :] = jnp.zeros((PAD, DR), F32)
            hcar[...] = jnp.zeros_like(hcar)

        xbuf[PAD:PAD + tt, :] = x_ref[...]
        xc = cb_ref[...] + cw_ref[0:1, :] * xbuf[pl.ds(PAD - 3, tt), :]
        for k in range(1, CONV_WIDTH):
            xc = xc + cw_ref[k:k + 1, :] * xbuf[pl.ds(PAD - 3 + k, tt), :]
        xbuf[0:PAD, :] = xbuf[tt:tt + PAD, :]
        xc_ref[...] = xc
        xcb = xc.astype(BF16)
        for n in range(NB):
            sl = slice(n * HEAD, (n + 1) * HEAD)
            r_ref[:, sl] = jax.nn.sigmoid(_dot(xcb[:, sl], wr_ref[n]) + br_ref[:, sl])
            i_ref[:, sl] = jax.nn.sigmoid(_dot(xcb[:, sl], wi_ref[n]) + bi_ref[:, sl])
        _, a, _, mult = _lru_coeffs(lam_ref[...], r_ref[...])
        hs = mult * (i_ref[...] * xc)
        row = lax.broadcasted_iota(jnp.int32, (tt, DR), 0)
        d = 1
        while d < tt:
            keep = row >= d
            a_sh = jnp.where(keep, pltpu.roll(a, d, 0), 1.0)
            h_sh = jnp.where(keep, pltpu.roll(hs, d, 0), 0.0)
            hs = a * h_sh + hs
            a = a * a_sh
            d *= 2
        h = hs + a * hcar[...]
        h_ref[...] = h
        hcar[...] = h_ref[tt - 1:tt, :]
        g = g_ref[...]
        y_ref[...] = (h * (g * jax.nn.sigmoid(g))).astype(y_ref.dtype)

    col = lambda j: pl.BlockSpec((tt, DR), lambda c: (c, j))
    vec = pl.BlockSpec((1, DR), lambda c: (0, 0))
    gate = pl.BlockSpec((NB, HEAD, HEAD), lambda c: (0, 0, 0))
    f32_out = jax.ShapeDtypeStruct((T, DR), F32)
    return pl.pallas_call(
        body, name=name, grid=(T // tt,),
        in_specs=[col(0), col(1), pl.BlockSpec((CONV_WIDTH, DR), lambda c: (0, 0)), vec, gate, vec, gate, vec, vec],
        out_specs=[col(0)] * 5,
        out_shape=[jax.ShapeDtypeStruct((T, DR), BF16), f32_out, f32_out, f32_out, f32_out],
        scratch_shapes=[pltpu.VMEM((tt + PAD, DR), F32), pltpu.VMEM((1, DR), F32)],
        compiler_params=_params("arbitrary"),
    )(proj, proj, conv_w, conv_b.reshape(1, DR), w_r, b_r.reshape(1, DR), w_i, b_i.reshape(1, DR),
      lam.reshape(1, DR))


def _rglru_bwd(dy, proj, xc, r, i, h, conv_w, w_r, w_i, lam, name):
    T = proj.shape[0]
    DR = conv_w.shape[1]
    NB = DR // HEAD
    tt = _row_tile(T, 128)
    nc = T // tt
    PAD = 8
    per = tt // PAD

    def body(dy_ref, x_ref, g_ref, xc_ref, r_ref, i_ref, h_ref, xprev_ref, hprev_ref,
             cw_ref, wr_ref, wi_ref, lam_ref,
             dxg_ref, dcw_ref, dcb_ref, dwr_ref, dbr_ref, dwi_ref, dbi_ref, dlam_ref,
             xbuf, dxcbuf, gcar, acar):
        step = pl.program_id(0)
        chunk = nc - 1 - step

        @pl.when(step == 0)
        def _():
            for ref in (dcw_ref, dcb_ref, dwr_ref, dbr_ref, dwi_ref, dbi_ref, dlam_ref, gcar, acar):
                ref[...] = jnp.zeros_like(ref)
            dxcbuf[tt:tt + PAD, :] = jnp.zeros((PAD, DR), F32)

        not_first = (chunk > 0).astype(F32)
        row = lax.broadcasted_iota(jnp.int32, (tt, DR), 0)
        silu, dsilu = _silu_and_grad(g_ref[...])
        dyv = dy_ref[...]
        hv = h_ref[...]
        dxg_ref[:, DR:2 * DR] = (dyv * hv * dsilu).astype(dxg_ref.dtype)
        dh = dyv * silu
        rv = r_ref[...]
        iv = i_ref[...]
        xcv = xc_ref[...]
        lam_row = lam_ref[...]
        cl, a, em, mult = _lru_coeffs(lam_row, rv)
        b = jnp.where(row == tt - 1, acar[...], pltpu.roll(a, tt - 1, 0))
        gs = dh
        d = 1
        while d < tt:
            keep = row < tt - d
            b_sh = jnp.where(keep, pltpu.roll(b, tt - d, 0), 1.0)
            g_sh = jnp.where(keep, pltpu.roll(gs, tt - d, 0), 0.0)
            gs = gs + b * g_sh
            b = b * b_sh
            d *= 2
        gt = gs + b * gcar[...]
        xbuf[0:tt, :] = gt
        gcar[...] = xbuf[0:1, :]
        acar[...] = _lru_coeffs(lam_row, r_ref[0:1, :])[1]
        h_before = hprev_ref[PAD - 1:PAD, :] * not_first
        hprev = jnp.where(row == 0, h_before, pltpu.roll(hv, 1, 0))
        da = gt * hprev
        dmult = gt * (iv * xcv)
        di = gt * mult * xcv
        dxc = gt * mult * iv
        dlog_a = da * a - dmult * (1.0 - em) / mult
        dr = dlog_a * cl
        dlam_ref[...] += jnp.sum(dlog_a * rv, axis=0, keepdims=True) * (LRU_C * jax.nn.sigmoid(-lam_row))
        drp = dr * rv * (1.0 - rv)
        dip = di * iv * (1.0 - iv)
        dbr_ref[...] += jnp.sum(drp, axis=0, keepdims=True)
        dbi_ref[...] += jnp.sum(dip, axis=0, keepdims=True)
        drpb = drp.astype(BF16)
        dipb = dip.astype(BF16)
        xcb = xcv.astype(BF16)
        for n in range(NB):
            sl = slice(n * HEAD, (n + 1) * HEAD)
            dxcbuf[0:tt, sl] = dxc[:, sl] + _dot_nt(drpb[:, sl], wr_ref[n]) + _dot_nt(dipb[:, sl], wi_ref[n])
            dwr_ref[n] += _dot_tn(xcb[:, sl], drpb[:, sl])
            dwi_ref[n] += _dot_tn(xcb[:, sl], dipb[:, sl])
        dxc_all = dxcbuf[0:tt, :]
        dcb_ref[...] += jnp.sum(dxc_all, axis=0, keepdims=True)
        xbuf[0:PAD, :] = xprev_ref[...] * not_first
        xbuf[PAD:PAD + tt, :] = x_ref[...]
        dx = cw_ref[0:1, :] * dxcbuf[pl.ds(3, tt), :]
        for k in range(1, CONV_WIDTH):
            dx = dx + cw_ref[k:k + 1, :] * dxcbuf[pl.ds(3 - k, tt), :]
        dxg_ref[:, 0:DR] = dx.astype(dxg_ref.dtype)
        for k in range(CONV_WIDTH):
            dcw_ref[k:k + 1, :] += jnp.sum(xbuf[pl.ds(PAD - 3 + k, tt), :] * dxc_all, axis=0, keepdims=True)
        dxcbuf[tt:tt + PAD, :] = dxcbuf[0:PAD, :]

    rev = lambda j: pl.BlockSpec((tt, DR), lambda s: (nc - 1 - s, j))
    prev = pl.BlockSpec((PAD, DR), lambda s: (jnp.maximum((nc - 1 - s) * per - 1, 0), 0))
    vec = pl.BlockSpec((1, DR), lambda s: (0, 0))
    gate = pl.BlockSpec((NB, HEAD, HEAD), lambda s: (0, 0, 0))
    taps = pl.BlockSpec((CONV_WIDTH, DR), lambda s: (0, 0))
    vec_out = jax.ShapeDtypeStruct((1, DR), F32)
    gate_out = jax.ShapeDtypeStruct((NB, HEAD, HEAD), F32)
    return pl.pallas_call(
        body, name=name, grid=(nc,),
        in_specs=[rev(0), rev(0), rev(1), rev(0), rev(0), rev(0), rev(0), prev, prev, taps, gate, gate, vec],
        out_specs=[pl.BlockSpec((tt, 2 * DR), lambda s: (nc - 1 - s, 0)), taps, vec, gate, vec, gate, vec, vec],
        out_shape=[jax.ShapeDtypeStruct((T, 2 * DR), BF16), jax.ShapeDtypeStruct((CONV_WIDTH, DR), F32),
                   vec_out, gate_out, vec_out, gate_out, vec_out, vec_out],
        scratch_shapes=[pltpu.VMEM((tt + PAD, DR), F32), pltpu.VMEM((tt + PAD, DR), F32),
                        pltpu.VMEM((1, DR), F32), pltpu.VMEM((1, DR), F32)],
        compiler_params=_params("arbitrary"),
    )(dy, proj, proj, xc, r, i, h, proj, h, conv_w, w_r, w_i, lam.reshape(1, DR))


def _mem_probs(q, k, scale):
    s = _dot_nt(q, k) * scale
    p = jnp.exp(s - jnp.max(s, axis=-1, keepdims=True))
    return p * (1.0 / jnp.sum(p, axis=-1, keepdims=True))


def _memattn_fwd(proj, mkv, DR, DM, name):
    T = proj.shape[0]
    M = mkv.shape[0]
    NH = DM // HEAD
    tt = _row_tile(T, 512)
    qcol = 2 * DR // DM
    scale = HEAD ** -0.5

    def body(q_ref, g_ref, k_ref, v_ref, y_ref):
        for n in range(NH):
            sl = slice(n * HEAD, (n + 1) * HEAD)
            p = _mem_probs(q_ref[:, sl].astype(BF16), k_ref[:, sl], scale)
            o = _dot(p.astype(BF16), v_ref[:, sl])
            g = g_ref[:, sl]
            y_ref[:, sl] = (o * (g * jax.nn.sigmoid(g))).astype(y_ref.dtype)

    return pl.pallas_call(
        body, name=name, grid=(T // tt,),
        in_specs=[pl.BlockSpec((tt, DM), lambda t: (t, qcol)), pl.BlockSpec((tt, DM), lambda t: (t, qcol + 1)),
                  pl.BlockSpec((M, DM), lambda t: (0, 0)), pl.BlockSpec((M, DM), lambda t: (0, 1))],
        out_specs=pl.BlockSpec((tt, DM), lambda t: (t, 0)),
        out_shape=jax.ShapeDtypeStruct((T, DM), BF16),
        compiler_params=_params("parallel"),
    )(proj, proj, mkv, mkv)


def _memattn_bwd(dy, proj, mkv, DR, DM, name):
    T = proj.shape[0]
    M = mkv.shape[0]
    NH = DM // HEAD
    tt = _row_tile(T, 512)
    qcol = 2 * DR // DM
    scale = HEAD ** -0.5

    def body(dy_ref, q_ref, g_ref, k_ref, v_ref, dqg_ref, dkv_ref):
        @pl.when(pl.program_id(0) == 0)
        def _():
            dkv_ref[...] = jnp.zeros_like(dkv_ref)

        for n in range(NH):
            sl = slice(n * HEAD, (n + 1) * HEAD)
            qb = q_ref[:, sl].astype(BF16)
            kb = k_ref[:, sl]
            vb = v_ref[:, sl]
            p = _mem_probs(qb, kb, scale)
            pb = p.astype(BF16)
            o = _dot(pb, vb)
            silu, dsilu = _silu_and_grad(g_ref[:, sl])
            dyv = dy_ref[:, sl]
            dqg_ref[:, DM + n * HEAD:DM + (n + 1) * HEAD] = (dyv * o * dsilu).astype(dqg_ref.dtype)
            dob = (dyv * silu).astype(BF16)
            dp = _dot_nt(dob, vb)
            ds = (p * (dp - jnp.sum(dp * p, axis=-1, keepdims=True)) * scale).astype(BF16)
            dqg_ref[:, sl] = _dot(ds, kb).astype(dqg_ref.dtype)
            dkv_ref[:, sl] += _dot_tn(ds, qb)
            dkv_ref[:, DM + n * HEAD:DM + (n + 1) * HEAD] += _dot_tn(pb, dob)

    return pl.pallas_call(
        body, name=name, grid=(T // tt,),
        in_specs=[pl.BlockSpec((tt, DM), lambda t: (t, DR // DM)),
                  pl.BlockSpec((tt, DM), lambda t: (t, qcol)), pl.BlockSpec((tt, DM), lambda t: (t, qcol + 1)),
                  pl.BlockSpec((M, DM), lambda t: (0, 0)), pl.BlockSpec((M, DM), lambda t: (0, 1))],
        out_specs=[pl.BlockSpec((tt, 2 * DM), lambda t: (t, 0)), pl.BlockSpec((M, 2 * DM), lambda t: (0, 0))],
        out_shape=[jax.ShapeDtypeStruct((T, 2 * DM), BF16), jax.ShapeDtypeStruct((M, 2 * DM), F32)],
        compiler_params=_params("arbitrary"),
    )(dy, proj, proj, mkv, mkv)


def _sb_blocks(T):
    tk = _row_tile(T // 2, 256)
    tq = 2 * tk
    assert T % tq == 0
    return tq, tk


def _sb_upper(tk):
    row = lax.broadcasted_iota(jnp.int32, (tk, tk), 0)
    col = lax.broadcasted_iota(jnp.int32, (tk, tk), 1)
    return (row > col).astype(BF16)


def _sb_causal(tq, tk, d):
    row = lax.broadcasted_iota(jnp.int32, (tq, tk), 0)
    col = lax.broadcasted_iota(jnp.int32, (tq, tk), 1)
    return col + d * tk < row


def _later_sum(x, upper):
    n = x.shape[0]
    hi, lo = _split_bf16(x)
    both = _dot(jnp.concatenate([hi, lo], axis=0), upper)
    return both[0:n] + both[n:2 * n]


def _sb_alive(gone_c):
    return (jnp.min(gone_c) < SB_EXHAUSTED).astype(jnp.int32)


def _sb_weights(q, kb, gone_c, causal, upper, scale):
    return _sb_weights_of(_dot_nt(q, kb) * scale, gone_c, causal, upper)


def _sb_weights_of(z, gone_c, causal, upper):
    sp = jnp.where(z > 20.0, z, jnp.log(1.0 + jnp.exp(z)))
    spm = sp if causal is None else jnp.where(causal, sp, 0.0)
    gone = _later_sum(spm, upper) + gone_c
    w = jnp.exp(z - sp - gone)
    if causal is not None:
        w = jnp.where(causal, w, 0.0)
    return z, sp, spm, w


def _sb_fwd(proj, kv, DR, name):
    T = proj.shape[0]
    NH = DR // HEAD
    tq, tk = _sb_blocks(T)
    span = tq // tk
    scale = HEAD ** -0.5

    def body(q_ref, g_ref, k_ref, v_ref, y_ref, o_ref):
        i = pl.program_id(1)
        q = q_ref[...].astype(BF16)
        upper = _sb_upper(tk)

        def tile(jb, carry, causal):
            gone_c, acc = carry
            start = pl.multiple_of(jb * tk, tk)
            kb = k_ref[pl.ds(start, tk), :]
            vb = v_ref[pl.ds(start, tk), :]
            _, _, spm, w = _sb_weights(q, kb, gone_c, causal, upper, scale)
            hi, lo = _split_bf16(w)
            pv = _dot(jnp.concatenate([hi, lo], axis=0), vb)
            return gone_c + jnp.sum(spm, axis=-1, keepdims=True), acc + pv[0:tq] + pv[tq:2 * tq]

        carry = (jnp.zeros((tq, 1), F32), jnp.zeros((tq, HEAD), F32))
        for d in reversed(range(span)):
            carry = tile(span * i + d, carry, _sb_causal(tq, tk, d))

        def more(c):
            return jnp.logical_and(c[0] >= 0, c[1] > 0)

        def step(c):
            jb, _, gone_c, acc = c
            gone_c, acc = tile(jb, (gone_c, acc), None)
            return jb - 1, _sb_alive(gone_c), gone_c, acc

        _, _, _, acc = lax.while_loop(more, step, (span * i - 1, _sb_alive(carry[0]), *carry))
        o_ref[...] = acc
        g = g_ref[...]
        y_ref[...] = (acc * (g * jax.nn.sigmoid(g))).astype(y_ref.dtype)

    blk = lambda off: pl.BlockSpec((tq, HEAD), lambda h, i: (i, off + h))
    whole = lambda off: pl.BlockSpec((T, HEAD), lambda h, i: (0, off + h))
    return pl.pallas_call(
        body, name=name, grid=(NH, T // tq),
        in_specs=[blk(0), blk(NH), whole(0), whole(NH)],
        out_specs=[blk(0), blk(0)],
        out_shape=[jax.ShapeDtypeStruct((T, DR), BF16), jax.ShapeDtypeStruct((T, DR), F32)],
        compiler_params=_params("parallel", "arbitrary"),
    )(proj, proj, kv, kv)


def _sb_bwd(dy, proj, kv, o, DR, name):
    T = proj.shape[0]
    NH = DR // HEAD
    tq, tk = _sb_blocks(T)
    span = tq // tk
    scale = HEAD ** -0.5

    def body(dy_ref, q_ref, g_ref, k_ref, v_ref, o_ref, dq_ref, dg_ref, dk_ref, dv_ref):
        i = pl.program_id(1)

        @pl.when(i == 0)
        def _():
            dk_ref[...] = jnp.zeros_like(dk_ref)
            dv_ref[...] = jnp.zeros_like(dv_ref)

        qf = q_ref[...]
        q = qf.astype(BF16)
        q_t = qf.T.astype(BF16)
        upper = _sb_upper(tk)
        silu, dsilu = _silu_and_grad(g_ref[...])
        dyv = dy_ref[...]
        ov = o_ref[...]
        dg_ref[...] = (dyv * ov * dsilu).astype(dg_ref.dtype)
        do = dyv * silu
        dob = do.astype(BF16)
        do_t = do.T.astype(BF16)
        total = jnp.sum(dob.astype(F32) * ov, axis=-1, keepdims=True)

        def tile(jb, carry, causal):
            gone_c, e_after, dq = carry
            start = pl.multiple_of(jb * tk, tk)
            kb = k_ref[pl.ds(start, tk), :]
            vb = v_ref[pl.ds(start, tk), :]
            z, sp, spm, w = _sb_weights(q, kb, gone_c, causal, upper, scale)
            e = _dot_nt(dob, vb) * w
            dz = e - (total - (_later_sum(e, upper) + e_after)) * jnp.exp(z - sp)
            if causal is not None:
                dz = jnp.where(causal, dz, 0.0)
            dzb = dz.astype(BF16)
            dk_ref[jb] += _dot(q_t, dzb) * scale
            dv_ref[jb] += _dot(do_t, w.astype(BF16))
            return (gone_c + jnp.sum(spm, axis=-1, keepdims=True), e_after + jnp.sum(e, axis=-1, keepdims=True),
                    dq + _dot(dzb, kb))

        carry = (jnp.zeros((tq, 1), F32), jnp.zeros((tq, 1), F32), jnp.zeros((tq, HEAD), F32))
        for d in reversed(range(span)):
            carry = tile(span * i + d, carry, _sb_causal(tq, tk, d))

        def more(c):
            return jnp.logical_and(c[0] >= 0, c[1] > 0)

        def step(c):
            jb, _, *rest = c
            rest = tile(jb, tuple(rest), None)
            return jb - 1, _sb_alive(rest[0]), *rest

        *_, dq = lax.while_loop(more, step, (span * i - 1, _sb_alive(carry[0]), *carry))
        dq_ref[...] = (dq * scale).astype(dq_ref.dtype)

    blk = lambda off: pl.BlockSpec((tq, HEAD), lambda h, i: (i, off + h))
    whole = lambda off: pl.BlockSpec((T, HEAD), lambda h, i: (0, off + h))
    keys_t = pl.BlockSpec((None, T // tk, HEAD, tk), lambda h, i: (h, 0, 0, 0))
    keys_t_shape = jax.ShapeDtypeStruct((NH, T // tk, HEAD, tk), F32)
    return pl.pallas_call(
        body, name=name, grid=(NH, T // tq),
        in_specs=[blk(0), blk(0), blk(NH), whole(0), whole(NH), blk(0)],
        out_specs=[blk(0), blk(0), keys_t, keys_t],
        out_shape=[jax.ShapeDtypeStruct((T, DR), BF16), jax.ShapeDtypeStruct((T, DR), BF16),
                   keys_t_shape, keys_t_shape],
        compiler_params=_params("parallel", "arbitrary"),
    )(dy, proj, proj, kv, kv, o)


def _merge_dkv(parts, name):
    NH, nblk, _, tk = parts[0][0].shape
    DR = NH * HEAD
    n = len(parts)

    def body(*refs):
        o_ref = refs[-1]
        for which in range(2):
            for h in range(NH):
                acc = refs[which][h]
                for p in range(1, n):
                    acc = acc + refs[2 * p + which][h]
                col = which * DR + h * HEAD
                o_ref[:, col:col + HEAD] = acc.T.astype(o_ref.dtype)

    blk = pl.BlockSpec((NH, None, HEAD, tk), lambda t: (0, t, 0, 0))
    return pl.pallas_call(
        body, name=name, grid=(nblk,),
        in_specs=[blk] * (2 * n),
        out_specs=pl.BlockSpec((tk, 2 * DR), lambda t: (t, 0)),
        out_shape=jax.ShapeDtypeStruct((nblk * tk, 2 * DR), BF16),
        compiler_params=_params("parallel"),
    )(*[a for pair in parts for a in pair])


def _final_loss(h, g, target, name):
    T, D = h.shape
    tt = _row_tile(T, 256)

    def body(h_ref, g_ref, t_ref, dh_ref, dg_ref, sq_ref):
        @pl.when(pl.program_id(0) == 0)
        def _():
            dg_ref[...] = jnp.zeros_like(dg_ref)
            sq_ref[...] = jnp.zeros_like(sq_ref)

        hv = h_ref[...]
        gv = g_ref[...]
        r = lax.rsqrt(jnp.mean(hv * hv, axis=-1, keepdims=True) + EPS)
        xhat = hv * r
        err = xhat * gv - t_ref[...]
        sq_ref[...] += jnp.sum(err * err, axis=0, keepdims=True)
        dy = err * (1.0 / D)
        dxhat = dy * gv
        dh_ref[...] = r * (dxhat - xhat * jnp.mean(dxhat * xhat, axis=-1, keepdims=True))
        dg_ref[...] += jnp.sum(dy * xhat, axis=0, keepdims=True)

    row = pl.BlockSpec((tt, D), lambda i: (i, 0))
    vec = pl.BlockSpec((1, D), lambda i: (0, 0))
    return pl.pallas_call(
        body, name=name, grid=(T // tt,),
        in_specs=[row, vec, row],
        out_specs=[row, vec, vec],
        out_shape=[jax.ShapeDtypeStruct((T, D), F32), jax.ShapeDtypeStruct((1, D), F32),
                   jax.ShapeDtypeStruct((1, D), F32)],
        compiler_params=_params("arbitrary"),
    )(h, g.reshape(1, D), target)


def _sum_parts(parts_ref):
    g = parts_ref[0].astype(F32)
    for s in range(1, parts_ref.shape[0]):
        g = g + parts_ref[s].astype(F32)
    return g


def _adamw(parts, w, m, v, name):
    P, R, C = parts.shape
    tr = _rows_for(R, P * C * 4)

    def body(p_ref, w_ref, m_ref, v_ref, g_ref, d_ref, nm_ref, nv_ref):
        g = _sum_parts(p_ref)
        nm = ADAM_B1 * m_ref[...] + (1.0 - ADAM_B1) * g
        nv = ADAM_B2 * v_ref[...] + (1.0 - ADAM_B2) * jnp.square(g)
        m_hat = nm / (1.0 - ADAM_B1 ** ADAM_STEP)
        v_hat = nv / (1.0 - ADAM_B2 ** ADAM_STEP)
        g_ref[...] = g
        d_ref[...] = -ADAM_LR * (m_hat / (jnp.sqrt(v_hat) + ADAM_EPS) + ADAM_WD * w_ref[...])
        nm_ref[...] = nm
        nv_ref[...] = nv

    row = pl.BlockSpec((tr, C), lambda i: (i, 0))
    out = jax.ShapeDtypeStruct((R, C), F32)
    return pl.pallas_call(
        body, name=name, grid=(R // tr,),
        in_specs=[pl.BlockSpec((P, tr, C), lambda i: (0, i, 0)), row, row, row],
        out_specs=[row] * 4,
        out_shape=[out] * 4,
        compiler_params=_params("parallel"),
    )(parts, w, m, v)


def _sum_devices(parts, name):
    P, R, C = parts.shape
    tr = _rows_for(R, P * C * 4)

    def body(p_ref, o_ref):
        o_ref[...] = _sum_parts(p_ref)

    return pl.pallas_call(
        body, name=name, grid=(R // tr,),
        in_specs=[pl.BlockSpec((P, tr, C), lambda i: (0, i, 0))],
        out_specs=pl.BlockSpec((tr, C), lambda i: (i, 0)),
        out_shape=jax.ShapeDtypeStruct((R, C), F32),
        compiler_params=_params("parallel"),
    )(parts)


def _mesh_position():
    return lax.axis_index("x"), lax.axis_index("y"), lax.axis_index("c")


def _device_index(p):
    return 4 * p[0] + 2 * p[1] + p[2]


def _all_gather(arrs, name):
    n = len(arrs)

    def body(*refs):
        ins, outs = refs[:n], refs[n:2 * n]
        send_sems, recv_sems, local_sems = refs[2 * n:]
        x, y, c = _mesh_position()
        me, sibling = (x, y, c), (x, y, 1 - c)
        chips = [(1 - x, y), (x, 1 - y), (1 - x, 1 - y)]

        def slot(a, p):
            return outs[a].at[_device_index(p)]

        def copy(a, k, block, to, src=None):
            return pltpu.make_async_remote_copy(
                src_ref=slot(a, block) if src is None else src, dst_ref=slot(a, block),
                send_sem=send_sems.at[a, k], recv_sem=recv_sems.at[a, k],
                device_id=to, device_id_type=pl.DeviceIdType.MESH)

        mine = [pltpu.make_async_copy(ins[a], slot(a, me), local_sems.at[a]) for a in range(n)]
        for cp in mine:
            cp.start()
        first = []
        for a in range(n):
            first.append(copy(a, 0, me, sibling, src=ins[a]))
            first += [copy(a, 1 + j, me, (*chip, c), src=ins[a]) for j, chip in enumerate(chips)]
        for cp in first:
            cp.start()
        passed = []
        for a in range(n):
            for j, chip in enumerate(chips):
                copy(a, 1 + j, (*chip, c), me).wait_recv()
                fwd = copy(a, 4 + j, (*chip, c), sibling)
                fwd.start()
                passed.append(fwd)
        for a in range(n):
            copy(a, 0, sibling, me).wait_recv()
            for j, chip in enumerate(chips):
                copy(a, 4 + j, (*chip, 1 - c), me).wait_recv()
        for cp in first + passed:
            cp.wait_send()
        for cp in mine:
            cp.wait()

    return pl.pallas_call(
        body, name=name,
        in_specs=[HBM_SPEC] * n, out_specs=[HBM_SPEC] * n,
        out_shape=[jax.ShapeDtypeStruct((N_DEV,) + a.shape, a.dtype) for a in arrs],
        scratch_shapes=[pltpu.SemaphoreType.DMA((n, 7)), pltpu.SemaphoreType.DMA((n, 7)),
                        pltpu.SemaphoreType.DMA((n,))],
    )(*arrs)


def _all_to_all(arrs, name):
    n = len(arrs)

    def body(*refs):
        ins, outs = refs[:n], refs[n:2 * n]
        send_sems, recv_sems, local_sems = refs[2 * n:]
        x, y, c = _mesh_position()
        me = _device_index((x, y, c))
        peers = [(1 - x if k & 4 else x, 1 - y if k & 2 else y, 1 - c if k & 1 else c) for k in range(1, N_DEV)]
        local = [pltpu.make_async_copy(ins[a].at[me], outs[a].at[me], local_sems.at[a]) for a in range(n)]
        for cp in local:
            cp.start()
        sends, recvs = [], []
        for a in range(n):
            for k, peer in enumerate(peers):
                there = _device_index(peer)
                sems = dict(send_sem=send_sems.at[a, k], recv_sem=recv_sems.at[a, k],
                            device_id=peer, device_id_type=pl.DeviceIdType.MESH)
                sends.append(pltpu.make_async_remote_copy(src_ref=ins[a].at[there], dst_ref=outs[a].at[me], **sems))
                recvs.append(pltpu.make_async_remote_copy(src_ref=ins[a].at[there], dst_ref=outs[a].at[there], **sems))
        for cp in sends:
            cp.start()
        for cp in recvs:
            cp.wait_recv()
        for cp in sends:
            cp.wait_send()
        for cp in local:
            cp.wait()

    return pl.pallas_call(
        body, name=name,
        in_specs=[HBM_SPEC] * n, out_specs=[HBM_SPEC] * n,
        out_shape=[jax.ShapeDtypeStruct(a.shape, a.dtype) for a in arrs],
        scratch_shapes=[pltpu.SemaphoreType.DMA((n, 7)), pltpu.SemaphoreType.DMA((n, 7)),
                        pltpu.SemaphoreType.DMA((n,))],
    )(*arrs)


def _pack(arrs, row_multiple):
    parts = []
    rows = 0
    for a in arrs:
        flat = a.reshape(-1).astype(F32)
        r = -(-flat.shape[0] // (8 * LANES)) * 8
        parts.append(jnp.pad(flat, (0, r * LANES - flat.shape[0])).reshape(r, LANES))
        rows += r
    pad = -rows % row_multiple
    if pad:
        parts.append(jnp.zeros((pad, LANES), F32))
    return jnp.concatenate(parts, axis=0)


def _unpack(buf, shapes, lead=()):
    out = []
    r0 = 0
    for shape in shapes:
        size = 1
        for s in shape:
            size *= s
        r = -(-size // (8 * LANES)) * 8
        part = buf[..., r0:r0 + r, :].reshape(lead + (r * LANES,))[..., :size]
        out.append(part.reshape(lead + tuple(shape)))
        r0 += r
    return out


def _gathered_cols(g):
    g = jnp.moveaxis(g, 0, -2)
    return g.reshape(g.shape[:-2] + (g.shape[-2] * g.shape[-1],))


def kernel(x, mem, mem_norm, w_mem_kv, norm_a, w_in_a, conv_w, conv_b, w_rec_gate, b_rec_gate, w_in_gate, b_in_gate, lru_lambda, w_out_a, kv_norm, w_kv, norm_b, w_in_b, w_out_b, final_norm, loss_target, m_mem_norm, m_w_mem_kv, m_norm_a, m_w_in_a, m_conv_w, m_conv_b, m_w_rec_gate, m_b_rec_gate, m_w_in_gate, m_b_in_gate, m_lru_lambda, m_w_out_a, m_kv_norm, m_w_kv, m_norm_b, m_w_in_b, m_w_out_b, m_final_norm, v_mem_norm, v_w_mem_kv, v_norm_a, v_w_in_a, v_conv_w, v_conv_b, v_w_rec_gate, v_b_rec_gate, v_w_in_gate, v_b_in_gate, v_lru_lambda, v_w_out_a, v_kv_norm, v_w_kv, v_norm_b, v_w_in_b, v_w_out_b, v_final_norm):
    xs = x[0]
    T, D = xs.shape
    L = w_mem_kv.shape[0]
    NA = w_in_a.shape[0]
    NB = w_in_b.shape[0]
    DM2 = w_mem_kv.shape[2]
    DM = DM2 // 2
    DR = w_rec_gate.shape[1] * w_rec_gate.shape[2]
    me = _device_index(_mesh_position())

    small_sharded = [norm_a, conv_w, conv_b, b_rec_gate, b_in_gate, lru_lambda]
    big = [w_mem_kv[l] for l in range(L)]
    big += [w_in_a[l] for l in range(NA)] + [w_out_a[l] for l in range(NA)] + [w_kv]
    big += [w_in_b[l] for l in range(NB)] + [w_out_b[l] for l in range(NB)]
    gathered = _all_gather([w.astype(BF16) for w in big] + [_pack(small_sharded, 8)], "gather_params")
    it = iter(gathered[:-1])
    g_mem = [next(it).reshape(1, D, DM2) for _ in range(L)]
    g_in_a = [next(it) for _ in range(NA)]
    g_out_a = [next(it).reshape(1, -1, D) for _ in range(NA)]
    g_kv = next(it)
    g_in_b = [next(it) for _ in range(NB)]
    g_out_b = [next(it).reshape(1, -1, D) for _ in range(NB)]
    norm_a_f, conv_w_f, conv_b_f, b_r_f, b_i_f, lam_f = [
        _gathered_cols(s) for s in _unpack(gathered[-1], [s.shape for s in small_sharded], lead=(N_DEV,))]
    w_r_bf = w_rec_gate.astype(BF16)
    w_i_bf = w_in_gate.astype(BF16)

    zeros_mem = jnp.zeros_like(mem[0])
    mem_n = _rms_fwd(mem[0], mem_norm, "rms_mem")
    mkv = [_mm_nn(mem_n, g_mem[l], out_dtype=BF16, name=f"mm_mem_kv{l}") for l in range(L)]

    h = xs
    saved_a = []
    for l in range(NA):
        u = _rms_fwd(h, norm_a_f[l], f"rms_a{l}")
        proj = _mm_nn(u, g_in_a[l], name=f"mm_in_a{l}")
        y_rnn, xc, r, i, hr = _rglru_fwd(proj, conv_w_f[l], conv_b_f[l], w_r_bf[l], b_r_f[l], w_i_bf[l], b_i_f[l],
                                         lam_f[l], f"rglru_fwd{l}")
        y_mem = _memattn_fwd(proj, mkv[l], DR, DM, f"memattn_fwd_a{l}")
        ycat = jnp.concatenate([y_rnn, y_mem], axis=-1)
        h_next = _mm_nn(ycat, g_out_a[l], res=h, name=f"mm_out_a{l}")
        saved_a.append((h, u, proj, xc, r, i, hr, ycat))
        h = h_next
    h_kv = h
    u_kv = _rms_fwd(h_kv, kv_norm, "rms_kv")
    kv = _mm_nn(u_kv, g_kv, out_dtype=BF16, name="mm_kv")
    saved_b = []
    for j in range(NB):
        u = _rms_fwd(h, norm_b[j], f"rms_b{j}")
        proj = _mm_nn(u, g_in_b[j], name=f"mm_in_b{j}")
        y_sb, o_sb = _sb_fwd(proj, kv, DR, f"sb_fwd{j}")
        y_mem = _memattn_fwd(proj, mkv[NA + j], DR, DM, f"memattn_fwd_b{j}")
        ycat = jnp.concatenate([y_sb, y_mem], axis=-1)
        h_next = _mm_nn(ycat, g_out_b[j], res=h, name=f"mm_out_b{j}")
        saved_b.append((h, u, proj, o_sb, ycat))
        h = h_next

    dh, d_final_norm, sq = _final_loss(h, final_norm, loss_target[0], "final_loss")
    loss = lax.psum(0.5 * jnp.sum(sq) / D, ("x", "y", "c"))

    big_grads = {}
    dmkv = [None] * L
    d_norm_b = [None] * NB
    dkv_parts = []
    for j in reversed(range(NB)):
        h_in, u, proj, o_sb, ycat = saved_b[j]
        dy = _mm_nt(dh, g_out_b[j], name=f"mm_dy_b{j}")
        big_grads[f"out_b{j}"] = _mm_tn(ycat, dh, 1, name=f"mm_dw_out_b{j}").reshape(N_DEV, -1, D)
        dq, dg, dk, dv = _sb_bwd(dy, proj, kv, o_sb, DR, f"sb_bwd{j}")
        dkv_parts.append((dk, dv))
        dqg_mem, dmkv[NA + j] = _memattn_bwd(dy, proj, mkv[NA + j], DR, DM, f"memattn_bwd_b{j}")
        dproj = jnp.concatenate([dq, dg, dqg_mem], axis=-1)
        du = _mm_nt(dproj, g_in_b[j], name=f"mm_du_b{j}")
        big_grads[f"in_b{j}"] = _mm_tn(u, dproj, N_DEV, name=f"mm_dw_in_b{j}")
        dh, d_norm_b[j] = _rms_bwd(du, h_in, norm_b[j], dh, f"rms_bwd_b{j}")

    dkv = _merge_dkv(dkv_parts, "merge_dkv")
    du_kv = _mm_nt(dkv, g_kv, name="mm_du_kv")
    big_grads["kv"] = _mm_tn(u_kv, dkv, N_DEV, name="mm_dw_kv")
    dh, d_kv_norm = _rms_bwd(du_kv, h_kv, kv_norm, dh, "rms_bwd_kv")

    d_norm_a, d_conv_w, d_conv_b, d_w_r, d_b_r, d_w_i, d_b_i, d_lam = ([None] * NA for _ in range(8))
    for l in reversed(range(NA)):
        h_in, u, proj, xc, r, i, hr, ycat = saved_a[l]
        dy = _mm_nt(dh, g_out_a[l], name=f"mm_dy_a{l}")
        big_grads[f"out_a{l}"] = _mm_tn(ycat, dh, 1, name=f"mm_dw_out_a{l}").reshape(N_DEV, -1, D)
        (dxg, d_conv_w[l], d_conv_b[l], d_w_r[l], d_b_r[l], d_w_i[l], d_b_i[l], d_lam[l]) = _rglru_bwd(
            dy, proj, xc, r, i, hr, conv_w_f[l], w_r_bf[l], w_i_bf[l], lam_f[l], f"rglru_bwd{l}")
        dqg_mem, dmkv[l] = _memattn_bwd(dy, proj, mkv[l], DR, DM, f"memattn_bwd_a{l}")
        dproj = jnp.concatenate([dxg, dqg_mem], axis=-1)
        du = _mm_nt(dproj, g_in_a[l], name=f"mm_du_a{l}")
        big_grads[f"in_a{l}"] = _mm_tn(u, dproj, N_DEV, name=f"mm_dw_in_a{l}")
        dh, d_norm_a[l] = _rms_bwd(du, h_in, norm_a_f[l], dh, f"rms_bwd_a{l}")
    grad_x = dh.reshape(x.shape)

    dmkv_all = jnp.concatenate(dmkv, axis=-1)
    dmem_n = _mm_nt(dmkv_all, jnp.concatenate(g_mem, axis=0), name="mm_dmem")
    for l in range(L):
        big_grads[f"mem_kv{l}"] = _mm_tn(mem_n, dmkv[l], 1, name=f"mm_dw_mem_kv{l}").reshape(N_DEV, -1, DM2)
    _, d_mem_norm = _rms_bwd(dmem_n, mem[0], mem_norm, zeros_mem, "rms_bwd_mem")

    names = ([f"mem_kv{l}" for l in range(L)] + [f"in_a{l}" for l in range(NA)] + [f"out_a{l}" for l in range(NA)] + ["kv"]
             + [f"in_b{j}" for j in range(NB)] + [f"out_b{j}" for j in range(NB)])
    received = dict(zip(names, _all_to_all([big_grads[k] for k in names], "scatter_grads")))

    def update(key, w, m, v):
        shape = w.shape
        two_d = (-1, shape[-1])
        return [o.reshape(shape) for o in _adamw(received[key], w.reshape(two_d), m.reshape(two_d),
                                                 v.reshape(two_d), f"adamw_{key}")]

    def update_layers(prefix, w, m, v):
        per_layer = [update(f"{prefix}{l}", w[l], m[l], v[l]) for l in range(w.shape[0])]
        return [jnp.stack([per_layer[l][k] for l in range(w.shape[0])]) for k in range(4)]

    upd = {
        "w_mem_kv": update_layers("mem_kv", w_mem_kv, m_w_mem_kv, v_w_mem_kv),
        "w_in_a": update_layers("in_a", w_in_a, m_w_in_a, v_w_in_a),
        "w_out_a": update_layers("out_a", w_out_a, m_w_out_a, v_w_out_a),
        "w_kv": update("kv", w_kv, m_w_kv, v_w_kv),
        "w_in_b": update_layers("in_b", w_in_b, m_w_in_b, v_w_in_b),
        "w_out_b": update_layers("out_b", w_out_b, m_w_out_b, v_w_out_b),
    }

    small_full = {
        "mem_norm": d_mem_norm.reshape(-1),
        "norm_a": jnp.concatenate(d_norm_a, axis=0),
        "conv_w": jnp.stack(d_conv_w),
        "conv_b": jnp.concatenate(d_conv_b, axis=0),
        "w_rec_gate": jnp.stack(d_w_r),
        "b_rec_gate": jnp.concatenate(d_b_r, axis=0),
        "w_in_gate": jnp.stack(d_w_i),
        "b_in_gate": jnp.concatenate(d_b_i, axis=0),
        "lru_lambda": jnp.concatenate(d_lam, axis=0),
        "kv_norm": d_kv_norm.reshape(-1),
        "norm_b": jnp.concatenate(d_norm_b, axis=0),
        "final_norm": d_final_norm.reshape(-1),
    }
    small_names = list(small_full)
    (small_gathered,) = _all_gather([_pack([small_full[k] for k in small_names], 256)], "gather_small_grads")
    small_sum = _sum_devices(small_gathered, "sum_small_grads")
    small_grad = dict(zip(small_names, _unpack(small_sum, [small_full[k].shape for k in small_names])))
    small_w = {"mem_norm": (mem_norm, m_mem_norm, v_mem_norm), "norm_a": (norm_a, m_norm_a, v_norm_a),
               "conv_w": (conv_w, m_conv_w, v_conv_w), "conv_b": (conv_b, m_conv_b, v_conv_b),
               "w_rec_gate": (w_rec_gate, m_w_rec_gate, v_w_rec_gate),
               "b_rec_gate": (b_rec_gate, m_b_rec_gate, v_b_rec_gate),
               "w_in_gate": (w_in_gate, m_w_in_gate, v_w_in_gate), "b_in_gate": (b_in_gate, m_b_in_gate, v_b_in_gate),
               "lru_lambda": (lru_lambda, m_lru_lambda, v_lru_lambda), "kv_norm": (kv_norm, m_kv_norm, v_kv_norm),
               "norm_b": (norm_b, m_norm_b, v_norm_b), "final_norm": (final_norm, m_final_norm, v_final_norm)}
    for k in small_names:
        w = small_w[k][0]
        if small_grad[k].shape != w.shape:
            n = w.shape[-1]
            small_grad[k] = lax.dynamic_slice_in_dim(small_grad[k], me * n, n, axis=-1)
    small_shapes = [small_w[k][0].shape for k in small_names]
    packed = [_pack([small_grad[k] for k in small_names], 256)[None]]
    packed += [_pack([small_w[k][t] for k in small_names], 256) for t in range(3)]
    small_out = [_unpack(o, small_shapes) for o in _adamw(*packed, "adamw_small")]
    for idx, k in enumerate(small_names):
        upd[k] = [small_out[t][idx] for t in range(4)]

    order = ["mem_norm", "w_mem_kv", "norm_a", "w_in_a", "conv_w", "conv_b", "w_rec_gate", "b_rec_gate", "w_in_gate",
             "b_in_gate", "lru_lambda", "w_out_a", "kv_norm", "w_kv", "norm_b", "w_in_b", "w_out_b", "final_norm"]
    return (loss, grad_x, *[upd[k][0] for k in order], *[upd[k][1] for k in order],
            *[upd[k][2] for k in order], *[upd[k][3] for k in order])
```

```python
import functools

import jax
import jax.numpy as jnp
from jax import lax
from jax.experimental import pallas as pl
from jax.experimental.pallas import tpu as pltpu

F32 = jnp.float32
BF16 = jnp.bfloat16

N_DEV = 8
EPS = 1e-6
LRU_C = 8.0
HEAD = 128
CONV_WIDTH = 4
LANES = 128
SB_EXHAUSTED = 110.0
VMEM_LIMIT = 56 * 1024 * 1024

ADAM_LR = 0.001
ADAM_B1 = 0.9
ADAM_B2 = 0.999
ADAM_EPS = 1e-08
ADAM_WD = 0.01
ADAM_STEP = 10

HBM_SPEC = pl.BlockSpec(memory_space=pltpu.HBM)


def _params(*semantics):
    return pltpu.CompilerParams(dimension_semantics=semantics, vmem_limit_bytes=VMEM_LIMIT)


def _tile(n, cap):
    if n <= cap:
        return n
    t = cap - cap % LANES
    while n % t:
        t -= LANES
    return t


def _row_tile(n, cap):
    if n <= cap:
        return n
    t = cap - cap % 8
    while t >= 8:
        if n % t == 0:
            return t
        t -= 8
    return n


def _rows_for(n, bytes_per_row, budget=2 * 1024 * 1024):
    return _row_tile(n, max(8, budget // bytes_per_row))


def _dot(a, b):
    return lax.dot_general(a, b, (((1,), (0,)), ((), ())), preferred_element_type=F32)


def _dot_nt(a, b):
    return lax.dot_general(a, b, (((1,), (1,)), ((), ())), preferred_element_type=F32)


def _dot_tn(a, b):
    return lax.dot_general(a, b, (((0,), (0,)), ((), ())), preferred_element_type=F32)


def _split_bf16(x):
    hi = x.astype(BF16)
    lo = (x - hi.astype(F32)).astype(BF16)
    return hi, lo


def _softplus(x):
    return jnp.maximum(x, 0.0) + jnp.log1p(jnp.exp(-jnp.abs(x)))


def _silu_and_grad(g):
    sg = jax.nn.sigmoid(g)
    return g * sg, sg * (1.0 + g * (1.0 - sg))


def _neg_expm1(x):
    series = -x * (1.0 + x * (0.5 + x * (1.0 / 6.0 + x * (1.0 / 24.0 + x * (1.0 / 120.0)))))
    return jnp.where(x > -0.1, series, 1.0 - jnp.exp(x))


def _rms_fwd(x, g, name):
    T, D = x.shape
    tt = _row_tile(T, 512)

    def body(x_ref, g_ref, o_ref):
        xv = x_ref[...]
        r = lax.rsqrt(jnp.mean(xv * xv, axis=-1, keepdims=True) + EPS)
        o_ref[...] = ((xv * r) * g_ref[...]).astype(o_ref.dtype)

    return pl.pallas_call(
        body, name=name, grid=(T // tt,),
        in_specs=[pl.BlockSpec((tt, D), lambda i: (i, 0)), pl.BlockSpec((1, D), lambda i: (0, 0))],
        out_specs=pl.BlockSpec((tt, D), lambda i: (i, 0)),
        out_shape=jax.ShapeDtypeStruct((T, D), BF16),
        compiler_params=_params("parallel"),
    )(x, g.reshape(1, D))


def _rms_bwd(du, h, g, res, name):
    T, D = h.shape
    tt = _row_tile(T, 256)

    def body(du_ref, h_ref, g_ref, res_ref, dh_ref, dg_ref):
        @pl.when(pl.program_id(0) == 0)
        def _():
            dg_ref[...] = jnp.zeros_like(dg_ref)

        hv = h_ref[...]
        duv = du_ref[...]
        r = lax.rsqrt(jnp.mean(hv * hv, axis=-1, keepdims=True) + EPS)
        xhat = hv * r
        dxhat = duv * g_ref[...]
        dh_ref[...] = res_ref[...] + r * (dxhat - xhat * jnp.mean(dxhat * xhat, axis=-1, keepdims=True))
        dg_ref[...] += jnp.sum(duv * xhat, axis=0, keepdims=True)

    row = pl.BlockSpec((tt, D), lambda i: (i, 0))
    vec = pl.BlockSpec((1, D), lambda i: (0, 0))
    return pl.pallas_call(
        body, name=name, grid=(T // tt,),
        in_specs=[row, row, vec, row],
        out_specs=[row, vec],
        out_shape=[jax.ShapeDtypeStruct((T, D), F32), jax.ShapeDtypeStruct((1, D), F32)],
        compiler_params=_params("arbitrary"),
    )(du, h, g.reshape(1, D), res)


def _mm_nn(a, b3, res=None, out_dtype=F32, name=None):
    M, K = a.shape
    S, _, ns = b3.shape
    tm = _row_tile(M, 1024 if a.dtype == BF16 else 512)
    tn = _tile(ns, 512)
    nj = ns // tn

    def body(*refs):
        if res is None:
            a_ref, b_ref, o_ref = refs
        else:
            a_ref, b_ref, r_ref, o_ref = refs
        acc = _dot(a_ref[...].astype(BF16), b_ref[...])
        if res is not None:
            acc = acc + r_ref[...]
        o_ref[...] = acc.astype(o_ref.dtype)

    in_specs = [pl.BlockSpec((tm, K), lambda i, j: (i, 0)),
                pl.BlockSpec((None, K, tn), lambda i, j: (j // nj, 0, j % nj))]
    args = [a, b3]
    if res is not None:
        in_specs.append(pl.BlockSpec((tm, tn), lambda i, j: (i, j)))
        args.append(res)
    return pl.pallas_call(
        body, name=name, grid=(M // tm, S * nj),
        in_specs=in_specs,
        out_specs=pl.BlockSpec((tm, tn), lambda i, j: (i, j)),
        out_shape=jax.ShapeDtypeStruct((M, S * ns), out_dtype),
        compiler_params=_params("parallel", "parallel"),
    )(*args)


def _mm_nt(a, b3, out_dtype=F32, name=None):
    M = a.shape[0]
    S, N, ns = b3.shape
    tm = _row_tile(M, 1024 if a.dtype == BF16 else 512)
    tn = _tile(N, 512)

    def body(a_ref, b_ref, o_ref):
        acc = _dot_nt(a_ref[:, 0:ns].astype(BF16), b_ref[0])
        for s in range(1, S):
            acc = acc + _dot_nt(a_ref[:, s * ns:(s + 1) * ns].astype(BF16), b_ref[s])
        o_ref[...] = acc.astype(o_ref.dtype)

    return pl.pallas_call(
        body, name=name, grid=(M // tm, N // tn),
        in_specs=[pl.BlockSpec((tm, S * ns), lambda i, j: (i, 0)),
                  pl.BlockSpec((S, tn, ns), lambda i, j: (0, j, 0))],
        out_specs=pl.BlockSpec((tm, tn), lambda i, j: (i, j)),
        out_shape=jax.ShapeDtypeStruct((M, N), out_dtype),
        compiler_params=_params("parallel", "parallel"),
    )(a, b3)


def _mm_tn(a, b, S, name=None):
    T, K = a.shape
    ns = b.shape[1] // S
    tk = _tile(K, 1024)
    tn = _tile(ns, 512)
    nj = ns // tn
    tt = _row_tile(T, 2048)
    nt = T // tt

    def body(a_ref, b_ref, o_ref, acc_ref):
        t = pl.program_id(2)
        p = _dot_tn(a_ref[...].astype(BF16), b_ref[...].astype(BF16))

        @pl.when(t == 0)
        def _():
            acc_ref[...] = p

        @pl.when(t > 0)
        def _():
            acc_ref[...] += p

        @pl.when(t == nt - 1)
        def _():
            o_ref[...] = acc_ref[...].astype(o_ref.dtype)

    return pl.pallas_call(
        body, name=name, grid=(K // tk, S * nj, nt),
        in_specs=[pl.BlockSpec((tt, tk), lambda i, j, t: (t, i)),
                  pl.BlockSpec((tt, tn), lambda i, j, t: (t, j))],
        out_specs=pl.BlockSpec((None, tk, tn), lambda i, j, t: (j // nj, i, j % nj)),
        out_shape=jax.ShapeDtypeStruct((S, K, ns), BF16),
        scratch_shapes=[pltpu.VMEM((tk, tn), F32)],
        compiler_params=_params("parallel", "parallel", "arbitrary"),
    )(a, b)


def _lru_coeffs(lam_row, r):
    cl = -LRU_C * _softplus(-lam_row)
    log_a = cl * r
    a = jnp.exp(log_a)
    em = _neg_expm1(2.0 * log_a)
    return cl, a, em, jnp.sqrt(em)


def _rglru_fwd(proj, conv_w, conv_b, w_r, b_r, w_i, b_i, lam, name, rider=None):
    T = proj.shape[0]
    DR = conv_w.shape[1]
    NB = DR // HEAD
    tt = _row_tile(T, 256)
    PAD = 8

    def body(x_ref, g_ref, cw_ref, cb_ref, wr_ref, br_ref, wi_ref, bi_ref, lam_ref,
             y_ref, xc_ref, r_ref, i_ref, h_ref, xbuf, hcar):
        @pl.when(pl.program_id(0) == 0)
        def _():
            xbuf[0:PAD, :] = jnp.zeros((PAD, DR), F32)
            hcar[...] = jnp.zeros_like(hcar)

        xbuf[PAD:PAD + tt, :] = x_ref[...]
        xc = cb_ref[...] + cw_ref[0:1, :] * xbuf[pl.ds(PAD - 3, tt), :]
        for k in range(1, CONV_WIDTH):
            xc = xc + cw_ref[k:k + 1, :] * xbuf[pl.ds(PAD - 3 + k, tt), :]
        xbuf[0:PAD, :] = xbuf[tt:tt + PAD, :]
        xc_ref[...] = xc
        xcb = xc.astype(BF16)
        for n in range(NB):
            sl = slice(n * HEAD, (n + 1) * HEAD)
            r_ref[:, sl] = jax.nn.sigmoid(_dot(xcb[:, sl], wr_ref[n]) + br_ref[:, sl])
            i_ref[:, sl] = jax.nn.sigmoid(_dot(xcb[:, sl], wi_ref[n]) + bi_ref[:, sl])
        _, a, _, mult = _lru_coeffs(lam_ref[...], r_ref[...])
        hs = mult * (i_ref[...] * xc)
        row = lax.broadcasted_iota(jnp.int32, (tt, DR), 0)
        d = 1
        while d < tt:
            keep = row >= d
            a_sh = jnp.where(keep, pltpu.roll(a, d, 0), 1.0)
            h_sh = jnp.where(keep, pltpu.roll(hs, d, 0), 0.0)
            hs = a * h_sh + hs
            a = a * a_sh
            d *= 2
        h = hs + a * hcar[...]
        h_ref[...] = h
        hcar[...] = h_ref[tt - 1:tt, :]
        g = g_ref[...]
        y_ref[...] = (h * (g * jax.nn.sigmoid(g))).astype(y_ref.dtype)

    col = lambda j: pl.BlockSpec((tt, DR), lambda c: (c, j))
    vec = pl.BlockSpec((1, DR), lambda c: (0, 0))
    gate = pl.BlockSpec((NB, HEAD, HEAD), lambda c: (0, 0, 0))
    f32_out = jax.ShapeDtypeStruct((T, DR), F32)
    nc = T // tt
    return _call_with_rider(
        body, rider, lambda: pl.program_id(0) == 0, lambda: pl.program_id(0) == nc - 1,
        name=name, grid=(nc,),
        in_specs=[col(0), col(1), pl.BlockSpec((CONV_WIDTH, DR), lambda c: (0, 0)), vec, gate, vec, gate, vec, vec],
        out_specs=[col(0)] * 5,
        out_shape=[jax.ShapeDtypeStruct((T, DR), BF16), f32_out, f32_out, f32_out, f32_out],
        scratch_shapes=[pltpu.VMEM((tt + PAD, DR), F32), pltpu.VMEM((1, DR), F32)],
        semantics=("arbitrary",),
        args=(proj, proj, conv_w, conv_b.reshape(1, DR), w_r, b_r.reshape(1, DR), w_i, b_i.reshape(1, DR),
              lam.reshape(1, DR)))


def _rglru_bwd(dy, proj, xc, r, i, h, conv_w, w_r, w_i, lam, name, rider=None):
    T = proj.shape[0]
    DR = conv_w.shape[1]
    NB = DR // HEAD
    tt = _row_tile(T, 128)
    nc = T // tt
    PAD = 8
    per = tt // PAD

    def body(dy_ref, x_ref, g_ref, xc_ref, r_ref, i_ref, h_ref, xprev_ref, hprev_ref,
             cw_ref, wr_ref, wi_ref, lam_ref,
             dxg_ref, dcw_ref, dcb_ref, dwr_ref, dbr_ref, dwi_ref, dbi_ref, dlam_ref,
             xbuf, dxcbuf, gcar, acar):
        step = pl.program_id(0)
        chunk = nc - 1 - step

        @pl.when(step == 0)
        def _():
            for ref in (dcw_ref, dcb_ref, dwr_ref, dbr_ref, dwi_ref, dbi_ref, dlam_ref, gcar, acar):
                ref[...] = jnp.zeros_like(ref)
            dxcbuf[tt:tt + PAD, :] = jnp.zeros((PAD, DR), F32)

        not_first = (chunk > 0).astype(F32)
        row = lax.broadcasted_iota(jnp.int32, (tt, DR), 0)
        silu, dsilu = _silu_and_grad(g_ref[...])
        dyv = dy_ref[...]
        hv = h_ref[...]
        dxg_ref[:, DR:2 * DR] = (dyv * hv * dsilu).astype(dxg_ref.dtype)
        dh = dyv * silu
        rv = r_ref[...]
        iv = i_ref[...]
        xcv = xc_ref[...]
        lam_row = lam_ref[...]
        cl, a, em, mult = _lru_coeffs(lam_row, rv)
        b = jnp.where(row == tt - 1, acar[...], pltpu.roll(a, tt - 1, 0))
        gs = dh
        d = 1
        while d < tt:
            keep = row < tt - d
            b_sh = jnp.where(keep, pltpu.roll(b, tt - d, 0), 1.0)
            g_sh = jnp.where(keep, pltpu.roll(gs, tt - d, 0), 0.0)
            gs = gs + b * g_sh
            b = b * b_sh
            d *= 2
        gt = gs + b * gcar[...]
        xbuf[0:tt, :] = gt
        gcar[...] = xbuf[0:1, :]
        acar[...] = _lru_coeffs(lam_row, r_ref[0:1, :])[1]
        h_before = hprev_ref[PAD - 1:PAD, :] * not_first
        hprev = jnp.where(row == 0, h_before, pltpu.roll(hv, 1, 0))
        da = gt * hprev
        dmult = gt * (iv * xcv)
        di = gt * mult * xcv
        dxc = gt * mult * iv
        dlog_a = da * a - dmult * (1.0 - em) / mult
        dr = dlog_a * cl
        dlam_ref[...] += jnp.sum(dlog_a * rv, axis=0, keepdims=True) * (LRU_C * jax.nn.sigmoid(-lam_row))
        drp = dr * rv * (1.0 - rv)
        dip = di * iv * (1.0 - iv)
        dbr_ref[...] += jnp.sum(drp, axis=0, keepdims=True)
        dbi_ref[...] += jnp.sum(dip, axis=0, keepdims=True)
        drpb = drp.astype(BF16)
        dipb = dip.astype(BF16)
        xcb = xcv.astype(BF16)
        for n in range(NB):
            sl = slice(n * HEAD, (n + 1) * HEAD)
            dxcbuf[0:tt, sl] = dxc[:, sl] + _dot_nt(drpb[:, sl], wr_ref[n]) + _dot_nt(dipb[:, sl], wi_ref[n])
            dwr_ref[n] += _dot_tn(xcb[:, sl], drpb[:, sl])
            dwi_ref[n] += _dot_tn(xcb[:, sl], dipb[:, sl])
        dxc_all = dxcbuf[0:tt, :]
        dcb_ref[...] += jnp.sum(dxc_all, axis=0, keepdims=True)
        xbuf[0:PAD, :] = xprev_ref[...] * not_first
        xbuf[PAD:PAD + tt, :] = x_ref[...]
        dx = cw_ref[0:1, :] * dxcbuf[pl.ds(3, tt), :]
        for k in range(1, CONV_WIDTH):
            dx = dx + cw_ref[k:k + 1, :] * dxcbuf[pl.ds(3 - k, tt), :]
        dxg_ref[:, 0:DR] = dx.astype(dxg_ref.dtype)
        for k in range(CONV_WIDTH):
            dcw_ref[k:k + 1, :] += jnp.sum(xbuf[pl.ds(PAD - 3 + k, tt), :] * dxc_all, axis=0, keepdims=True)
        dxcbuf[tt:tt + PAD, :] = dxcbuf[0:PAD, :]

    rev = lambda j: pl.BlockSpec((tt, DR), lambda s: (nc - 1 - s, j))
    prev = pl.BlockSpec((PAD, DR), lambda s: (jnp.maximum((nc - 1 - s) * per - 1, 0), 0))
    vec = pl.BlockSpec((1, DR), lambda s: (0, 0))
    gate = pl.BlockSpec((NB, HEAD, HEAD), lambda s: (0, 0, 0))
    taps = pl.BlockSpec((CONV_WIDTH, DR), lambda s: (0, 0))
    vec_out = jax.ShapeDtypeStruct((1, DR), F32)
    gate_out = jax.ShapeDtypeStruct((NB, HEAD, HEAD), F32)
    return _call_with_rider(
        body, rider, lambda: pl.program_id(0) == 0, lambda: pl.program_id(0) == nc - 1,
        name=name, grid=(nc,),
        in_specs=[rev(0), rev(0), rev(1), rev(0), rev(0), rev(0), rev(0), prev, prev, taps, gate, gate, vec],
        out_specs=[pl.BlockSpec((tt, 2 * DR), lambda s: (nc - 1 - s, 0)), taps, vec, gate, vec, gate, vec, vec],
        out_shape=[jax.ShapeDtypeStruct((T, 2 * DR), BF16), jax.ShapeDtypeStruct((CONV_WIDTH, DR), F32),
                   vec_out, gate_out, vec_out, gate_out, vec_out, vec_out],
        scratch_shapes=[pltpu.VMEM((tt + PAD, DR), F32), pltpu.VMEM((tt + PAD, DR), F32),
                        pltpu.VMEM((1, DR), F32), pltpu.VMEM((1, DR), F32)],
        semantics=("arbitrary",),
        args=(dy, proj, proj, xc, r, i, h, proj, h, conv_w, w_r, w_i, lam.reshape(1, DR)))


def _mem_probs(q, k, scale):
    s = _dot_nt(q, k) * scale
    p = jnp.exp(s - jnp.max(s, axis=-1, keepdims=True))
    return p * (1.0 / jnp.sum(p, axis=-1, keepdims=True))


def _memattn_fwd(proj, mkv, DR, DM, name):
    T = proj.shape[0]
    M = mkv.shape[0]
    NH = DM // HEAD
    tt = _row_tile(T, 512)
    qcol = 2 * DR // DM
    scale = HEAD ** -0.5

    def body(q_ref, g_ref, k_ref, v_ref, y_ref):
        for n in range(NH):
            sl = slice(n * HEAD, (n + 1) * HEAD)
            p = _mem_probs(q_ref[:, sl].astype(BF16), k_ref[:, sl], scale)
            o = _dot(p.astype(BF16), v_ref[:, sl])
            g = g_ref[:, sl]
            y_ref[:, sl] = (o * (g * jax.nn.sigmoid(g))).astype(y_ref.dtype)

    return pl.pallas_call(
        body, name=name, grid=(T // tt,),
        in_specs=[pl.BlockSpec((tt, DM), lambda t: (t, qcol)), pl.BlockSpec((tt, DM), lambda t: (t, qcol + 1)),
                  pl.BlockSpec((M, DM), lambda t: (0, 0)), pl.BlockSpec((M, DM), lambda t: (0, 1))],
        out_specs=pl.BlockSpec((tt, DM), lambda t: (t, 0)),
        out_shape=jax.ShapeDtypeStruct((T, DM), BF16),
        compiler_params=_params("parallel"),
    )(proj, proj, mkv, mkv)


def _memattn_bwd(dy, proj, mkv, DR, DM, name):
    T = proj.shape[0]
    M = mkv.shape[0]
    NH = DM // HEAD
    tt = _row_tile(T, 512)
    qcol = 2 * DR // DM
    scale = HEAD ** -0.5

    def body(dy_ref, q_ref, g_ref, k_ref, v_ref, dqg_ref, dkv_ref):
        @pl.when(pl.program_id(0) == 0)
        def _():
            dkv_ref[...] = jnp.zeros_like(dkv_ref)

        for n in range(NH):
            sl = slice(n * HEAD, (n + 1) * HEAD)
            qb = q_ref[:, sl].astype(BF16)
            kb = k_ref[:, sl]
            vb = v_ref[:, sl]
            p = _mem_probs(qb, kb, scale)
            pb = p.astype(BF16)
            o = _dot(pb, vb)
            silu, dsilu = _silu_and_grad(g_ref[:, sl])
            dyv = dy_ref[:, sl]
            dqg_ref[:, DM + n * HEAD:DM + (n + 1) * HEAD] = (dyv * o * dsilu).astype(dqg_ref.dtype)
            dob = (dyv * silu).astype(BF16)
            dp = _dot_nt(dob, vb)
            ds = (p * (dp - jnp.sum(dp * p, axis=-1, keepdims=True)) * scale).astype(BF16)
            dqg_ref[:, sl] = _dot(ds, kb).astype(dqg_ref.dtype)
            dkv_ref[:, sl] += _dot_tn(ds, qb)
            dkv_ref[:, DM + n * HEAD:DM + (n + 1) * HEAD] += _dot_tn(pb, dob)

    return pl.pallas_call(
        body, name=name, grid=(T // tt,),
        in_specs=[pl.BlockSpec((tt, DM), lambda t: (t, DR // DM)),
                  pl.BlockSpec((tt, DM), lambda t: (t, qcol)), pl.BlockSpec((tt, DM), lambda t: (t, qcol + 1)),
                  pl.BlockSpec((M, DM), lambda t: (0, 0)), pl.BlockSpec((M, DM), lambda t: (0, 1))],
        out_specs=[pl.BlockSpec((tt, 2 * DM), lambda t: (t, 0)), pl.BlockSpec((M, 2 * DM), lambda t: (0, 0))],
        out_shape=[jax.ShapeDtypeStruct((T, 2 * DM), BF16), jax.ShapeDtypeStruct((M, 2 * DM), F32)],
        compiler_params=_params("arbitrary"),
    )(dy, proj, proj, mkv, mkv)


def _sb_blocks(T):
    tk = _row_tile(T // 2, 256)
    tq = 2 * tk
    assert T % tq == 0
    return tq, tk


def _sb_upper(tk):
    row = lax.broadcasted_iota(jnp.int32, (tk, tk), 0)
    col = lax.broadcasted_iota(jnp.int32, (tk, tk), 1)
    return (row > col).astype(BF16)


def _sb_causal(tq, tk, d):
    row = lax.broadcasted_iota(jnp.int32, (tq, tk), 0)
    col = lax.broadcasted_iota(jnp.int32, (tq, tk), 1)
    return col + d * tk < row


def _later_sum(x, upper):
    n = x.shape[0]
    hi, lo = _split_bf16(x)
    both = _dot(jnp.concatenate([hi, lo], axis=0), upper)
    return both[0:n] + both[n:2 * n]


def _sb_alive(gone_c):
    return (jnp.min(gone_c) < SB_EXHAUSTED).astype(jnp.int32)


def _sb_weights(q, kb, gone_c, causal, upper, scale):
    return _sb_weights_of(_dot_nt(q, kb) * scale, gone_c, causal, upper)


def _sb_weights_of(z, gone_c, causal, upper):
    sp = jnp.where(z > 20.0, z, jnp.log(1.0 + jnp.exp(z)))
    spm = sp if causal is None else jnp.where(causal, sp, 0.0)
    gone = _later_sum(spm, upper) + gone_c
    w = jnp.exp(z - sp - gone)
    if causal is not None:
        w = jnp.where(causal, w, 0.0)
    return z, sp, spm, w


def _sb_fwd(proj, kv, DR, name, rider=None):
    T = proj.shape[0]
    NH = DR // HEAD
    tq, tk = _sb_blocks(T)
    span = tq // tk
    scale = HEAD ** -0.5

    def body(q_ref, g_ref, k_ref, v_ref, y_ref, o_ref):
        i = pl.program_id(1)
        q = q_ref[...].astype(BF16)
        upper = _sb_upper(tk)

        def tile(jb, carry, causal):
            gone_c, acc = carry
            start = pl.multiple_of(jb * tk, tk)
            kb = k_ref[pl.ds(start, tk), :]
            vb = v_ref[pl.ds(start, tk), :]
            _, _, spm, w = _sb_weights(q, kb, gone_c, causal, upper, scale)
            hi, lo = _split_bf16(w)
            pv = _dot(jnp.concatenate([hi, lo], axis=0), vb)
            return gone_c + jnp.sum(spm, axis=-1, keepdims=True), acc + pv[0:tq] + pv[tq:2 * tq]

        carry = (jnp.zeros((tq, 1), F32), jnp.zeros((tq, HEAD), F32))
        for d in reversed(range(span)):
            carry = tile(span * i + d, carry, _sb_causal(tq, tk, d))

        def more(c):
            return jnp.logical_and(c[0] >= 0, c[1] > 0)

        def step(c):
            jb, _, gone_c, acc = c
            gone_c, acc = tile(jb, (gone_c, acc), None)
            return jb - 1, _sb_alive(gone_c), gone_c, acc

        _, _, _, acc = lax.while_loop(more, step, (span * i - 1, _sb_alive(carry[0]), *carry))
        o_ref[...] = acc
        g = g_ref[...]
        y_ref[...] = (acc * (g * jax.nn.sigmoid(g))).astype(y_ref.dtype)

    blk = lambda off: pl.BlockSpec((tq, HEAD), lambda h, i: (i, off + h))
    whole = lambda off: pl.BlockSpec((T, HEAD), lambda h, i: (0, off + h))
    nq = T // tq
    return _call_with_rider(
        body, rider, lambda: (pl.program_id(0) == 0) & (pl.program_id(1) == 0),
        lambda: (pl.program_id(0) == NH - 1) & (pl.program_id(1) == nq - 1),
        name=name, grid=(NH, nq),
        in_specs=[blk(0), blk(NH), whole(0), whole(NH)],
        out_specs=[blk(0), blk(0)],
        out_shape=[jax.ShapeDtypeStruct((T, DR), BF16), jax.ShapeDtypeStruct((T, DR), F32)],
        scratch_shapes=[], semantics=("parallel", "arbitrary"),
        args=(proj, proj, kv, kv))


def _sb_bwd(dy, proj, kv, o, DR, name, rider=None):
    T = proj.shape[0]
    NH = DR // HEAD
    tq, tk = _sb_blocks(T)
    span = tq // tk
    scale = HEAD ** -0.5

    def body(dy_ref, q_ref, g_ref, k_ref, v_ref, o_ref, dq_ref, dg_ref, dk_ref, dv_ref):
        i = pl.program_id(1)

        @pl.when(i == 0)
        def _():
            dk_ref[...] = jnp.zeros_like(dk_ref)
            dv_ref[...] = jnp.zeros_like(dv_ref)

        qf = q_ref[...]
        q = qf.astype(BF16)
        q_t = qf.T.astype(BF16)
        upper = _sb_upper(tk)
        silu, dsilu = _silu_and_grad(g_ref[...])
        dyv = dy_ref[...]
        ov = o_ref[...]
        dg_ref[...] = (dyv * ov * dsilu).astype(dg_ref.dtype)
        do = dyv * silu
        dob = do.astype(BF16)
        do_t = do.T.astype(BF16)
        total = jnp.sum(dob.astype(F32) * ov, axis=-1, keepdims=True)

        def tile(jb, carry, causal):
            gone_c, e_after, dq = carry
            start = pl.multiple_of(jb * tk, tk)
            kb = k_ref[pl.ds(start, tk), :]
            vb = v_ref[pl.ds(start, tk), :]
            z, sp, spm, w = _sb_weights(q, kb, gone_c, causal, upper, scale)
            e = _dot_nt(dob, vb) * w
            dz = e - (total - (_later_sum(e, upper) + e_after)) * jnp.exp(z - sp)
            if causal is not None:
                dz = jnp.where(causal, dz, 0.0)
            dzb = dz.astype(BF16)
            dk_ref[jb] += _dot(q_t, dzb) * scale
            dv_ref[jb] += _dot(do_t, w.astype(BF16))
            return (gone_c + jnp.sum(spm, axis=-1, keepdims=True), e_after + jnp.sum(e, axis=-1, keepdims=True),
                    dq + _dot(dzb, kb))

        carry = (jnp.zeros((tq, 1), F32), jnp.zeros((tq, 1), F32), jnp.zeros((tq, HEAD), F32))
        for d in reversed(range(span)):
            carry = tile(span * i + d, carry, _sb_causal(tq, tk, d))

        def more(c):
            return jnp.logical_and(c[0] >= 0, c[1] > 0)

        def step(c):
            jb, _, *rest = c
            rest = tile(jb, tuple(rest), None)
            return jb - 1, _sb_alive(rest[0]), *rest

        *_, dq = lax.while_loop(more, step, (span * i - 1, _sb_alive(carry[0]), *carry))
        dq_ref[...] = (dq * scale).astype(dq_ref.dtype)

    blk = lambda off: pl.BlockSpec((tq, HEAD), lambda h, i: (i, off + h))
    whole = lambda off: pl.BlockSpec((T, HEAD), lambda h, i: (0, off + h))
    keys_t = pl.BlockSpec((None, T // tk, HEAD, tk), lambda h, i: (h, 0, 0, 0))
    keys_t_shape = jax.ShapeDtypeStruct((NH, T // tk, HEAD, tk), F32)
    nq = T // tq
    return _call_with_rider(
        body, rider, lambda: (pl.program_id(0) == 0) & (pl.program_id(1) == 0),
        lambda: (pl.program_id(0) == NH - 1) & (pl.program_id(1) == nq - 1),
        name=name, grid=(NH, nq),
        in_specs=[blk(0), blk(0), blk(NH), whole(0), whole(NH), blk(0)],
        out_specs=[blk(0), blk(0), keys_t, keys_t],
        out_shape=[jax.ShapeDtypeStruct((T, DR), BF16), jax.ShapeDtypeStruct((T, DR), BF16),
                   keys_t_shape, keys_t_shape],
        scratch_shapes=[], semantics=("parallel", "arbitrary"),
        args=(dy, proj, proj, kv, kv, o))


def _merge_dkv(parts, name):
    NH, nblk, _, tk = parts[0][0].shape
    DR = NH * HEAD
    n = len(parts)

    def body(*refs):
        o_ref = refs[-1]
        for which in range(2):
            for h in range(NH):
                acc = refs[which][h]
                for p in range(1, n):
                    acc = acc + refs[2 * p + which][h]
                col = which * DR + h * HEAD
                o_ref[:, col:col + HEAD] = acc.T.astype(o_ref.dtype)

    blk = pl.BlockSpec((NH, None, HEAD, tk), lambda t: (0, t, 0, 0))
    return pl.pallas_call(
        body, name=name, grid=(nblk,),
        in_specs=[blk] * (2 * n),
        out_specs=pl.BlockSpec((tk, 2 * DR), lambda t: (t, 0)),
        out_shape=jax.ShapeDtypeStruct((nblk * tk, 2 * DR), BF16),
        compiler_params=_params("parallel"),
    )(*[a for pair in parts for a in pair])


def _final_loss(h, g, target, name):
    T, D = h.shape
    tt = _row_tile(T, 256)

    def body(h_ref, g_ref, t_ref, dh_ref, dg_ref, sq_ref):
        @pl.when(pl.program_id(0) == 0)
        def _():
            dg_ref[...] = jnp.zeros_like(dg_ref)
            sq_ref[...] = jnp.zeros_like(sq_ref)

        hv = h_ref[...]
        gv = g_ref[...]
        r = lax.rsqrt(jnp.mean(hv * hv, axis=-1, keepdims=True) + EPS)
        xhat = hv * r
        err = xhat * gv - t_ref[...]
        sq_ref[...] += jnp.sum(err * err, axis=0, keepdims=True)
        dy = err * (1.0 / D)
        dxhat = dy * gv
        dh_ref[...] = r * (dxhat - xhat * jnp.mean(dxhat * xhat, axis=-1, keepdims=True))
        dg_ref[...] += jnp.sum(dy * xhat, axis=0, keepdims=True)

    row = pl.BlockSpec((tt, D), lambda i: (i, 0))
    vec = pl.BlockSpec((1, D), lambda i: (0, 0))
    return pl.pallas_call(
        body, name=name, grid=(T // tt,),
        in_specs=[row, vec, row],
        out_specs=[row, vec, vec],
        out_shape=[jax.ShapeDtypeStruct((T, D), F32), jax.ShapeDtypeStruct((1, D), F32),
                   jax.ShapeDtypeStruct((1, D), F32)],
        compiler_params=_params("arbitrary"),
    )(h, g.reshape(1, D), target)


def _sum_parts(parts_ref):
    g = parts_ref[0].astype(F32)
    for s in range(1, parts_ref.shape[0]):
        g = g + parts_ref[s].astype(F32)
    return g


def _adamw(parts, w, m, v, name):
    P, R, C = parts.shape
    tr = _rows_for(R, P * C * 4)

    def body(p_ref, w_ref, m_ref, v_ref, g_ref, d_ref, nm_ref, nv_ref):
        g = _sum_parts(p_ref)
        nm = ADAM_B1 * m_ref[...] + (1.0 - ADAM_B1) * g
        nv = ADAM_B2 * v_ref[...] + (1.0 - ADAM_B2) * jnp.square(g)
        m_hat = nm / (1.0 - ADAM_B1 ** ADAM_STEP)
        v_hat = nv / (1.0 - ADAM_B2 ** ADAM_STEP)
        g_ref[...] = g
        d_ref[...] = -ADAM_LR * (m_hat / (jnp.sqrt(v_hat) + ADAM_EPS) + ADAM_WD * w_ref[...])
        nm_ref[...] = nm
        nv_ref[...] = nv

    row = pl.BlockSpec((tr, C), lambda i: (i, 0))
    out = jax.ShapeDtypeStruct((R, C), F32)
    return pl.pallas_call(
        body, name=name, grid=(R // tr,),
        in_specs=[pl.BlockSpec((P, tr, C), lambda i: (0, i, 0)), row, row, row],
        out_specs=[row] * 4,
        out_shape=[out] * 4,
        compiler_params=_params("parallel"),
    )(parts, w, m, v)


def _sum_devices(parts, name):
    P, R, C = parts.shape
    tr = _rows_for(R, P * C * 4)

    def body(p_ref, o_ref):
        o_ref[...] = _sum_parts(p_ref)

    return pl.pallas_call(
        body, name=name, grid=(R // tr,),
        in_specs=[pl.BlockSpec((P, tr, C), lambda i: (0, i, 0))],
        out_specs=pl.BlockSpec((tr, C), lambda i: (i, 0)),
        out_shape=jax.ShapeDtypeStruct((R, C), F32),
        compiler_params=_params("parallel"),
    )(parts)


def _mesh_position():
    return lax.axis_index("x"), lax.axis_index("y"), lax.axis_index("c")


def _device_index(p):
    return 4 * p[0] + 2 * p[1] + p[2]


def _all_gather(arrs, name):
    n = len(arrs)

    def body(*refs):
        ins, outs = refs[:n], refs[n:2 * n]
        send_sems, recv_sems, local_sems = refs[2 * n:]
        x, y, c = _mesh_position()
        me, sibling = (x, y, c), (x, y, 1 - c)
        chips = [(1 - x, y), (x, 1 - y), (1 - x, 1 - y)]

        def slot(a, p):
            return outs[a].at[_device_index(p)]

        def copy(a, k, block, to, src=None):
            return pltpu.make_async_remote_copy(
                src_ref=slot(a, block) if src is None else src, dst_ref=slot(a, block),
                send_sem=send_sems.at[a, k], recv_sem=recv_sems.at[a, k],
                device_id=to, device_id_type=pl.DeviceIdType.MESH)

        mine = [pltpu.make_async_copy(ins[a], slot(a, me), local_sems.at[a]) for a in range(n)]
        for cp in mine:
            cp.start()
        first = []
        for a in range(n):
            first.append(copy(a, 0, me, sibling, src=ins[a]))
            first += [copy(a, 1 + j, me, (*chip, c), src=ins[a]) for j, chip in enumerate(chips)]
        for cp in first:
            cp.start()
        passed = []
        for a in range(n):
            for j, chip in enumerate(chips):
                copy(a, 1 + j, (*chip, c), me).wait_recv()
                fwd = copy(a, 4 + j, (*chip, c), sibling)
                fwd.start()
                passed.append(fwd)
        for a in range(n):
            copy(a, 0, sibling, me).wait_recv()
            for j, chip in enumerate(chips):
                copy(a, 4 + j, (*chip, 1 - c), me).wait_recv()
        for cp in first + passed:
            cp.wait_send()
        for cp in mine:
            cp.wait()

    return pl.pallas_call(
        body, name=name,
        in_specs=[HBM_SPEC] * n, out_specs=[HBM_SPEC] * n,
        out_shape=[jax.ShapeDtypeStruct((N_DEV,) + a.shape, a.dtype) for a in arrs],
        scratch_shapes=[pltpu.SemaphoreType.DMA((n, 7)), pltpu.SemaphoreType.DMA((n, 7)),
                        pltpu.SemaphoreType.DMA((n,))],
    )(*arrs)


class _Exchange:
    def __init__(self, arrs, gather):
        self.arrs = list(arrs)
        self.gather = gather
        self.n = len(self.arrs)

    def out_shape(self):
        lead = (N_DEV,) if self.gather else ()
        return [jax.ShapeDtypeStruct(lead + a.shape, a.dtype) for a in self.arrs]

    def scratch(self):
        return [pltpu.SemaphoreType.DMA((self.n, 7)), pltpu.SemaphoreType.DMA((self.n, 7)),
                pltpu.SemaphoreType.DMA((self.n,))]

    def _copies(self, ins, outs, sems, arrivals):
        send_sems, recv_sems, local_sems = sems
        x, y, c = _mesh_position()
        me = _device_index((x, y, c))
        peers = [(1 - x if k & 4 else x, 1 - y if k & 2 else y, 1 - c if k & 1 else c) for k in range(1, N_DEV)]
        local, sends, recvs = [], [], []
        for a in range(self.n):
            mine = ins[a] if self.gather else ins[a].at[me]
            local.append(pltpu.make_async_copy(mine, outs[a].at[me], local_sems.at[a]))
            for k, peer in enumerate(peers):
                there = _device_index(peer)
                src = ins[a] if self.gather else ins[a].at[there]
                sem = dict(send_sem=send_sems.at[a, k], recv_sem=recv_sems.at[a, k],
                           device_id=peer, device_id_type=pl.DeviceIdType.MESH)
                sends.append(pltpu.make_async_remote_copy(src_ref=src, dst_ref=outs[a].at[me], **sem))
                if arrivals:
                    recvs.append(pltpu.make_async_remote_copy(src_ref=src, dst_ref=outs[a].at[there], **sem))
        return local, sends, recvs

    def start(self, ins, outs, sems):
        local, sends, _ = self._copies(ins, outs, sems, arrivals=False)
        for cp in local + sends:
            cp.start()

    def wait(self, ins, outs, sems):
        local, sends, recvs = self._copies(ins, outs, sems, arrivals=True)
        for cp in recvs:
            cp.wait_recv()
        for cp in sends:
            cp.wait_send()
        for cp in local:
            cp.wait()


def _all_to_all(arrs, name):
    ex = _Exchange(arrs, gather=False)
    n = ex.n

    def body(*refs):
        ins, outs, sems = refs[:n], refs[n:2 * n], refs[2 * n:]
        ex.start(ins, outs, sems)
        ex.wait(ins, outs, sems)

    return pl.pallas_call(
        body, name=name,
        in_specs=[HBM_SPEC] * n, out_specs=[HBM_SPEC] * n,
        out_shape=ex.out_shape(), scratch_shapes=ex.scratch(),
    )(*arrs)


def _call_with_rider(body, rider, first, last, *, name, grid, in_specs, out_specs, out_shape, scratch_shapes,
                     semantics, args):
    if rider is None:
        out = pl.pallas_call(body, name=name, grid=grid, in_specs=in_specs, out_specs=out_specs,
                             out_shape=out_shape, scratch_shapes=scratch_shapes,
                             compiler_params=_params(*semantics))(*args)
        return out, None
    n, n_in, n_out = rider.n, len(in_specs), len(out_specs)

    def riding(*refs):
        ins, r_in = refs[:n_in], refs[n_in:n_in + n]
        outs, r_out = refs[n_in + n:n_in + n + n_out], refs[n_in + n + n_out:n_in + 2 * n + n_out]
        scratch, sems = refs[n_in + 2 * n + n_out:-3], refs[-3:]

        @pl.when(first())
        def _():
            rider.start(r_in, r_out, sems)

        body(*ins, *outs, *scratch)

        @pl.when(last())
        def _():
            rider.wait(r_in, r_out, sems)

    out = pl.pallas_call(
        riding, name=name, grid=grid,
        in_specs=list(in_specs) + [HBM_SPEC] * n, out_specs=list(out_specs) + [HBM_SPEC] * n,
        out_shape=list(out_shape) + rider.out_shape(),
        scratch_shapes=list(scratch_shapes) + rider.scratch(),
        compiler_params=_params(*["arbitrary"] * len(grid)),
    )(*args, *rider.arrs)
    return out[:n_out], out[n_out:]


def _pack(arrs, row_multiple):
    parts = []
    rows = 0
    for a in arrs:
        flat = a.reshape(-1).astype(F32)
        r = -(-flat.shape[0] // (8 * LANES)) * 8
        parts.append(jnp.pad(flat, (0, r * LANES - flat.shape[0])).reshape(r, LANES))
        rows += r
    pad = -rows % row_multiple
    if pad:
        parts.append(jnp.zeros((pad, LANES), F32))
    return jnp.concatenate(parts, axis=0)


def _unpack(buf, shapes, lead=()):
    out = []
    r0 = 0
    for shape in shapes:
        size = 1
        for s in shape:
            size *= s
        r = -(-size // (8 * LANES)) * 8
        part = buf[..., r0:r0 + r, :].reshape(lead + (r * LANES,))[..., :size]
        out.append(part.reshape(lead + tuple(shape)))
        r0 += r
    return out


def _gathered_cols(g):
    g = jnp.moveaxis(g, 0, -2)
    return g.reshape(g.shape[:-2] + (g.shape[-2] * g.shape[-1],))


def kernel(x, mem, mem_norm, w_mem_kv, norm_a, w_in_a, conv_w, conv_b, w_rec_gate, b_rec_gate, w_in_gate, b_in_gate, lru_lambda, w_out_a, kv_norm, w_kv, norm_b, w_in_b, w_out_b, final_norm, loss_target, m_mem_norm, m_w_mem_kv, m_norm_a, m_w_in_a, m_conv_w, m_conv_b, m_w_rec_gate, m_b_rec_gate, m_w_in_gate, m_b_in_gate, m_lru_lambda, m_w_out_a, m_kv_norm, m_w_kv, m_norm_b, m_w_in_b, m_w_out_b, m_final_norm, v_mem_norm, v_w_mem_kv, v_norm_a, v_w_in_a, v_conv_w, v_conv_b, v_w_rec_gate, v_b_rec_gate, v_w_in_gate, v_b_in_gate, v_lru_lambda, v_w_out_a, v_kv_norm, v_w_kv, v_norm_b, v_w_in_b, v_w_out_b, v_final_norm):
    xs = x[0]
    T, D = xs.shape
    L = w_mem_kv.shape[0]
    NA = w_in_a.shape[0]
    NB = w_in_b.shape[0]
    DM2 = w_mem_kv.shape[2]
    DM = DM2 // 2
    DR = w_rec_gate.shape[1] * w_rec_gate.shape[2]
    me = _device_index(_mesh_position())

    small_sharded = [norm_a, conv_w, conv_b, b_rec_gate, b_in_gate, lru_lambda]
    shard = {f"mem_kv{l}": w_mem_kv[l] for l in range(L)}
    shard.update({f"in_a{l}": w_in_a[l] for l in range(NA)}, **{f"out_a{l}": w_out_a[l] for l in range(NA)})
    shard.update({f"in_b{j}": w_in_b[j] for j in range(NB)}, **{f"out_b{j}": w_out_b[j] for j in range(NB)})
    shard["kv"] = w_kv
    shard = {k: w.astype(BF16) for k, w in shard.items()}
    now = [f"mem_kv{l}" for l in range(L)] + ["in_a0", "out_a0"]
    gathered = _all_gather([shard[k] for k in now] + [_pack(small_sharded, 8)], "gather_params")
    full = dict(zip(now, gathered[:-1]))

    def gather_rider(keys):
        return _Exchange([shard[k] for k in keys], gather=True), keys

    def out_proj(key):
        return full[key].reshape(1, -1, D)

    g_mem = [full[f"mem_kv{l}"].reshape(1, D, DM2) for l in range(L)]
    norm_a_f, conv_w_f, conv_b_f, b_r_f, b_i_f, lam_f = [
        _gathered_cols(s) for s in _unpack(gathered[-1], [s.shape for s in small_sharded], lead=(N_DEV,))]
    w_r_bf = w_rec_gate.astype(BF16)
    w_i_bf = w_in_gate.astype(BF16)

    zeros_mem = jnp.zeros_like(mem[0])
    mem_n = _rms_fwd(mem[0], mem_norm, "rms_mem")
    mkv = [_mm_nn(mem_n, g_mem[l], out_dtype=BF16, name=f"mm_mem_kv{l}") for l in range(L)]

    h = xs
    saved_a = []
    for l in range(NA):
        u = _rms_fwd(h, norm_a_f[l], f"rms_a{l}")
        proj = _mm_nn(u, full[f"in_a{l}"], name=f"mm_in_a{l}")
        rider, keys = gather_rider([f"in_a{l + 1}", f"out_a{l + 1}"] if l + 1 < NA else ["kv", "in_b0"])
        (y_rnn, xc, r, i, hr), landed = _rglru_fwd(proj, conv_w_f[l], conv_b_f[l], w_r_bf[l], b_r_f[l], w_i_bf[l],
                                                   b_i_f[l], lam_f[l], f"rglru_fwd{l}", rider)
        full.update(zip(keys, landed))
        y_mem = _memattn_fwd(proj, mkv[l], DR, DM, f"memattn_fwd_a{l}")
        ycat = jnp.concatenate([y_rnn, y_mem], axis=-1)
        h_next = _mm_nn(ycat, out_proj(f"out_a{l}"), res=h, name=f"mm_out_a{l}")
        saved_a.append((h, u, proj, xc, r, i, hr, ycat))
        h = h_next
    h_kv = h
    u_kv = _rms_fwd(h_kv, kv_norm, "rms_kv")
    kv = _mm_nn(u_kv, full["kv"], out_dtype=BF16, name="mm_kv")
    saved_b = []
    for j in range(NB):
        u = _rms_fwd(h, norm_b[j], f"rms_b{j}")
        proj = _mm_nn(u, full[f"in_b{j}"], name=f"mm_in_b{j}")
        rider, keys = gather_rider([f"out_b{j}"] + ([f"in_b{j + 1}"] if j + 1 < NB else []))
        (y_sb, o_sb), landed = _sb_fwd(proj, kv, DR, f"sb_fwd{j}", rider)
        full.update(zip(keys, landed))
        y_mem = _memattn_fwd(proj, mkv[NA + j], DR, DM, f"memattn_fwd_b{j}")
        ycat = jnp.concatenate([y_sb, y_mem], axis=-1)
        h_next = _mm_nn(ycat, out_proj(f"out_b{j}"), res=h, name=f"mm_out_b{j}")
        saved_b.append((h, u, proj, o_sb, ycat))
        h = h_next

    dh, d_final_norm, sq = _final_loss(h, final_norm, loss_target[0], "final_loss")
    loss = lax.psum(0.5 * jnp.sum(sq) / D, ("x", "y", "c"))

    big_grads = {}
    received = {}

    def scatter_rider(keys):
        return _Exchange([big_grads[k] for k in keys], gather=False), keys

    dmkv = [None] * L
    d_norm_b = [None] * NB
    dkv_parts = []
    for j in reversed(range(NB)):
        h_in, u, proj, o_sb, ycat = saved_b[j]
        dy = _mm_nt(dh, out_proj(f"out_b{j}"), name=f"mm_dy_b{j}")
        big_grads[f"out_b{j}"] = _mm_tn(ycat, dh, 1, name=f"mm_dw_out_b{j}").reshape(N_DEV, -1, D)
        rider, keys = scatter_rider([f"out_b{j + 1}", f"in_b{j + 1}"]) if j + 1 < NB else (None, [])
        (dq, dg, dk, dv), landed = _sb_bwd(dy, proj, kv, o_sb, DR, f"sb_bwd{j}", rider)
        received.update(zip(keys, landed or []))
        dkv_parts.append((dk, dv))
        dqg_mem, dmkv[NA + j] = _memattn_bwd(dy, proj, mkv[NA + j], DR, DM, f"memattn_bwd_b{j}")
        dproj = jnp.concatenate([dq, dg, dqg_mem], axis=-1)
        du = _mm_nt(dproj, full[f"in_b{j}"], name=f"mm_du_b{j}")
        big_grads[f"in_b{j}"] = _mm_tn(u, dproj, N_DEV, name=f"mm_dw_in_b{j}")
        dh, d_norm_b[j] = _rms_bwd(du, h_in, norm_b[j], dh, f"rms_bwd_b{j}")

    dkv = _merge_dkv(dkv_parts, "merge_dkv")
    du_kv = _mm_nt(dkv, full["kv"], name="mm_du_kv")
    big_grads["kv"] = _mm_tn(u_kv, dkv, N_DEV, name="mm_dw_kv")
    dh, d_kv_norm = _rms_bwd(du_kv, h_kv, kv_norm, dh, "rms_bwd_kv")

    d_norm_a, d_conv_w, d_conv_b, d_w_r, d_b_r, d_w_i, d_b_i, d_lam = ([None] * NA for _ in range(8))
    for l in reversed(range(NA)):
        h_in, u, proj, xc, r, i, hr, ycat = saved_a[l]
        dy = _mm_nt(dh, out_proj(f"out_a{l}"), name=f"mm_dy_a{l}")
        big_grads[f"out_a{l}"] = _mm_tn(ycat, dh, 1, name=f"mm_dw_out_a{l}").reshape(N_DEV, -1, D)
        rider, keys = scatter_rider([f"out_a{l + 1}", f"in_a{l + 1}"] if l + 1 < NA else ["out_b0", "in_b0", "kv"])
        (dxg, d_conv_w[l], d_conv_b[l], d_w_r[l], d_b_r[l], d_w_i[l], d_b_i[l], d_lam[l]), landed = _rglru_bwd(
            dy, proj, xc, r, i, hr, conv_w_f[l], w_r_bf[l], w_i_bf[l], lam_f[l], f"rglru_bwd{l}", rider)
        received.update(zip(keys, landed))
        dqg_mem, dmkv[l] = _memattn_bwd(dy, proj, mkv[l], DR, DM, f"memattn_bwd_a{l}")
        dproj = jnp.concatenate([dxg, dqg_mem], axis=-1)
        du = _mm_nt(dproj, full[f"in_a{l}"], name=f"mm_du_a{l}")
        big_grads[f"in_a{l}"] = _mm_tn(u, dproj, N_DEV, name=f"mm_dw_in_a{l}")
        dh, d_norm_a[l] = _rms_bwd(du, h_in, norm_a_f[l], dh, f"rms_bwd_a{l}")
    grad_x = dh.reshape(x.shape)

    dmkv_all = jnp.concatenate(dmkv, axis=-1)
    dmem_n = _mm_nt(dmkv_all, jnp.concatenate(g_mem, axis=0), name="mm_dmem")
    for l in range(L):
        big_grads[f"mem_kv{l}"] = _mm_tn(mem_n, dmkv[l], 1, name=f"mm_dw_mem_kv{l}").reshape(N_DEV, -1, DM2)
    _, d_mem_norm = _rms_bwd(dmem_n, mem[0], mem_norm, zeros_mem, "rms_bwd_mem")

    last = ["out_a0", "in_a0"] + [f"mem_kv{l}" for l in range(L)]
    received.update(zip(last, _all_to_all([big_grads[k] for k in last], "scatter_grads")))

    def update(key, w, m, v):
        shape = w.shape
        two_d = (-1, shape[-1])
        return [o.reshape(shape) for o in _adamw(received[key], w.reshape(two_d), m.reshape(two_d),
                                                 v.reshape(two_d), f"adamw_{key}")]

    def update_layers(prefix, w, m, v):
        per_layer = [update(f"{prefix}{l}", w[l], m[l], v[l]) for l in range(w.shape[0])]
        return [jnp.stack([per_layer[l][k] for l in range(w.shape[0])]) for k in range(4)]

    upd = {
        "w_mem_kv": update_layers("mem_kv", w_mem_kv, m_w_mem_kv, v_w_mem_kv),
        "w_in_a": update_layers("in_a", w_in_a, m_w_in_a, v_w_in_a),
        "w_out_a": update_layers("out_a", w_out_a, m_w_out_a, v_w_out_a),
        "w_kv": update("kv", w_kv, m_w_kv, v_w_kv),
        "w_in_b": update_layers("in_b", w_in_b, m_w_in_b, v_w_in_b),
        "w_out_b": update_layers("out_b", w_out_b, m_w_out_b, v_w_out_b),
    }

    small_full = {
        "mem_norm": d_mem_norm.reshape(-1),
        "norm_a": jnp.concatenate(d_norm_a, axis=0),
        "conv_w": jnp.stack(d_conv_w),
        "conv_b": jnp.concatenate(d_conv_b, axis=0),
        "w_rec_gate": jnp.stack(d_w_r),
        "b_rec_gate": jnp.concatenate(d_b_r, axis=0),
        "w_in_gate": jnp.stack(d_w_i),
        "b_in_gate": jnp.concatenate(d_b_i, axis=0),
        "lru_lambda": jnp.concatenate(d_lam, axis=0),
        "kv_norm": d_kv_norm.reshape(-1),
        "norm_b": jnp.concatenate(d_norm_b, axis=0),
        "final_norm": d_final_norm.reshape(-1),
    }
    small_names = list(small_full)
    (small_gathered,) = _all_gather([_pack([small_full[k] for k in small_names], 256)], "gather_small_grads")
    small_sum = _sum_devices(small_gathered, "sum_small_grads")
    small_grad = dict(zip(small_names, _unpack(small_sum, [small_full[k].shape for k in small_names])))
    small_w = {"mem_norm": (mem_norm, m_mem_norm, v_mem_norm), "norm_a": (norm_a, m_norm_a, v_norm_a),
               "conv_w": (conv_w, m_conv_w, v_conv_w), "conv_b": (conv_b, m_conv_b, v_conv_b),
               "w_rec_gate": (w_rec_gate, m_w_rec_gate, v_w_rec_gate),
               "b_rec_gate": (b_rec_gate, m_b_rec_gate, v_b_rec_gate),
               "w_in_gate": (w_in_gate, m_w_in_gate, v_w_in_gate), "b_in_gate": (b_in_gate, m_b_in_gate, v_b_in_gate),
               "lru_lambda": (lru_lambda, m_lru_lambda, v_lru_lambda), "kv_norm": (kv_norm, m_kv_norm, v_kv_norm),
               "norm_b": (norm_b, m_norm_b, v_norm_b), "final_norm": (final_norm, m_final_norm, v_final_norm)}
    for k in small_names:
        w = small_w[k][0]
        if small_grad[k].shape != w.shape:
            n = w.shape[-1]
            small_grad[k] = lax.dynamic_slice_in_dim(small_grad[k], me * n, n, axis=-1)
    small_shapes = [small_w[k][0].shape for k in small_names]
    packed = [_pack([small_grad[k] for k in small_names], 256)[None]]
    packed += [_pack([small_w[k][t] for k in small_names], 256) for t in range(3)]
    small_out = [_unpack(o, small_shapes) for o in _adamw(*packed, "adamw_small")]
    for idx, k in enumerate(small_names):
        upd[k] = [small_out[t][idx] for t in range(4)]

    order = ["mem_norm", "w_mem_kv", "norm_a", "w_in_a", "conv_w", "conv_b", "w_rec_gate", "b_rec_gate", "w_in_gate",
             "b_in_gate", "lru_lambda", "w_out_a", "kv_norm", "w_kv", "norm_b", "w_in_b", "w_out_b", "final_norm"]
    return (loss, grad_x, *[upd[k][0] for k in order], *[upd[k][1] for k in order],
            *[upd[k][2] for k in order], *[upd[k][3] for k in order])
```

```python
import functools

import jax
import jax.numpy as jnp
from jax import lax
from jax.experimental import pallas as pl
from jax.experimental.pallas import tpu as pltpu

F32 = jnp.float32
BF16 = jnp.bfloat16

N_DEV = 8
EPS = 1e-6
LRU_C = 8.0
HEAD = 128
CONV_WIDTH = 4
LANES = 128
SUB = 8
SB_EXHAUSTED = 110.0
VMEM_LIMIT = 56 * 1024 * 1024

ADAM_LR = 0.001
ADAM_B1 = 0.9
ADAM_B2 = 0.999
ADAM_EPS = 1e-08
ADAM_WD = 0.01
ADAM_STEP = 10

HBM_SPEC = pl.BlockSpec(memory_space=pltpu.HBM)


def _params(*semantics):
    return pltpu.CompilerParams(dimension_semantics=semantics, vmem_limit_bytes=VMEM_LIMIT)


def _tile(n, cap):
    if n <= cap:
        return n
    t = cap - cap % LANES
    while n % t:
        t -= LANES
    return t


def _row_tile(n, cap):
    if n <= cap:
        return n
    t = cap - cap % 8
    while t >= 8:
        if n % t == 0:
            return t
        t -= 8
    return n


def _rows_for(n, bytes_per_row, budget=2 * 1024 * 1024):
    return _row_tile(n, max(8, budget // bytes_per_row))


def _dot(a, b):
    return lax.dot_general(a, b, (((1,), (0,)), ((), ())), preferred_element_type=F32)


def _dot_nt(a, b):
    return lax.dot_general(a, b, (((1,), (1,)), ((), ())), preferred_element_type=F32)


def _dot_tn(a, b):
    return lax.dot_general(a, b, (((0,), (0,)), ((), ())), preferred_element_type=F32)


def _split_bf16(x):
    hi = x.astype(BF16)
    lo = (x - hi.astype(F32)).astype(BF16)
    return hi, lo


def _softplus(x):
    return jnp.maximum(x, 0.0) + jnp.log1p(jnp.exp(-jnp.abs(x)))


def _gate_sigmoid(x):
    return 0.5 * jnp.tanh(0.5 * x) + 0.5


def _silu_and_grad(g):
    sg = _gate_sigmoid(g)
    return g * sg, sg * (1.0 + g * (1.0 - sg))


def _one_minus_square(a, log_a):
    x = 2.0 * log_a
    series = -x * (1.0 + x * (0.5 + x * (1.0 / 6.0)))
    return jnp.where(x > -0.03, series, 1.0 - a * a)


def _rms_fwd(x, g, name):
    T, D = x.shape
    tt = _row_tile(T, 512)

    def body(x_ref, g_ref, o_ref):
        xv = x_ref[...]
        r = lax.rsqrt(jnp.mean(xv * xv, axis=-1, keepdims=True) + EPS)
        o_ref[...] = ((xv * r) * g_ref[...]).astype(o_ref.dtype)

    return pl.pallas_call(
        body, name=name, grid=(T // tt,),
        in_specs=[pl.BlockSpec((tt, D), lambda i: (i, 0)), pl.BlockSpec((1, D), lambda i: (0, 0))],
        out_specs=pl.BlockSpec((tt, D), lambda i: (i, 0)),
        out_shape=jax.ShapeDtypeStruct((T, D), BF16),
        compiler_params=_params("parallel"),
    )(x, g.reshape(1, D))


def _rms_bwd(du, h, g, res, name):
    T, D = h.shape
    tt = _row_tile(T, 256)

    def body(du_ref, h_ref, g_ref, res_ref, dh_ref, dg_ref):
        @pl.when(pl.program_id(0) == 0)
        def _():
            dg_ref[...] = jnp.zeros_like(dg_ref)

        hv = h_ref[...]
        duv = du_ref[...]
        r = lax.rsqrt(jnp.mean(hv * hv, axis=-1, keepdims=True) + EPS)
        xhat = hv * r
        dxhat = duv * g_ref[...]
        dh_ref[...] = res_ref[...] + r * (dxhat - xhat * jnp.mean(dxhat * xhat, axis=-1, keepdims=True))
        dg_ref[...] += jnp.sum(duv * xhat, axis=0, keepdims=True)

    row = pl.BlockSpec((tt, D), lambda i: (i, 0))
    vec = pl.BlockSpec((1, D), lambda i: (0, 0))
    return pl.pallas_call(
        body, name=name, grid=(T // tt,),
        in_specs=[row, row, vec, row],
        out_specs=[row, vec],
        out_shape=[jax.ShapeDtypeStruct((T, D), F32), jax.ShapeDtypeStruct((1, D), F32)],
        compiler_params=_params("arbitrary"),
    )(du, h, g.reshape(1, D), res)


def _mm_nn(a, b3, res=None, out_dtype=F32, name=None, rider=None):
    M, K = a.shape
    S, _, ns = b3.shape
    tm = _row_tile(M, 1024 if a.dtype == BF16 else 512)
    tn = _tile(ns, 512)
    nj = ns // tn

    def body(*refs):
        if res is None:
            a_ref, b_ref, o_ref = refs
        else:
            a_ref, b_ref, r_ref, o_ref = refs
        acc = _dot(a_ref[...].astype(BF16), b_ref[...])
        if res is not None:
            acc = acc + r_ref[...]
        o_ref[...] = acc.astype(o_ref.dtype)

    in_specs = [pl.BlockSpec((tm, K), lambda i, j: (i, 0)),
                pl.BlockSpec((None, K, tn), lambda i, j: (j // nj, 0, j % nj))]
    args = [a, b3]
    if res is not None:
        in_specs.append(pl.BlockSpec((tm, tn), lambda i, j: (i, j)))
        args.append(res)
    (out,), landed = _call_with_rider(
        body, rider, *_grid_ends((M // tm, S * nj)), name=name, grid=(M // tm, S * nj),
        in_specs=in_specs,
        out_specs=[pl.BlockSpec((tm, tn), lambda i, j: (i, j))],
        out_shape=[jax.ShapeDtypeStruct((M, S * ns), out_dtype)],
        scratch_shapes=[], semantics=("parallel", "parallel"), args=args)
    return out if rider is None else (out, landed)


def _grid_ends(grid):
    def first():
        return functools.reduce(jnp.logical_and, [pl.program_id(d) == 0 for d in range(len(grid))])

    def last():
        return functools.reduce(jnp.logical_and, [pl.program_id(d) == n - 1 for d, n in enumerate(grid)])

    return first, last


def _mm_nt(a, b3, out_dtype=F32, name=None, rider=None):
    M = a.shape[0]
    S, N, ns = b3.shape
    tm = _row_tile(M, 1024 if a.dtype == BF16 else 512)
    tn = _tile(N, 512)

    def body(a_ref, b_ref, o_ref):
        acc = _dot_nt(a_ref[:, 0:ns].astype(BF16), b_ref[0])
        for s in range(1, S):
            acc = acc + _dot_nt(a_ref[:, s * ns:(s + 1) * ns].astype(BF16), b_ref[s])
        o_ref[...] = acc.astype(o_ref.dtype)

    (out,), landed = _call_with_rider(
        body, rider, *_grid_ends((M // tm, N // tn)), name=name, grid=(M // tm, N // tn),
        in_specs=[pl.BlockSpec((tm, S * ns), lambda i, j: (i, 0)),
                  pl.BlockSpec((S, tn, ns), lambda i, j: (0, j, 0))],
        out_specs=[pl.BlockSpec((tm, tn), lambda i, j: (i, j))],
        out_shape=[jax.ShapeDtypeStruct((M, N), out_dtype)],
        scratch_shapes=[], semantics=("parallel", "parallel"), args=(a, b3))
    return out if rider is None else (out, landed)


def _mm_tn(a, b, S, name=None):
    T, K = a.shape
    ns = b.shape[1] // S
    tk = _tile(K, 1024)
    tn = _tile(ns, 512)
    nj = ns // tn
    tt = _row_tile(T, 2048)
    nt = T // tt

    def body(a_ref, b_ref, o_ref, acc_ref):
        t = pl.program_id(2)
        p = _dot_tn(a_ref[...].astype(BF16), b_ref[...].astype(BF16))

        @pl.when(t == 0)
        def _():
            acc_ref[...] = p

        @pl.when(t > 0)
        def _():
            acc_ref[...] += p

        @pl.when(t == nt - 1)
        def _():
            o_ref[...] = acc_ref[...].astype(o_ref.dtype)

    return pl.pallas_call(
        body, name=name, grid=(K // tk, S * nj, nt),
        in_specs=[pl.BlockSpec((tt, tk), lambda i, j, t: (t, i)),
                  pl.BlockSpec((tt, tn), lambda i, j, t: (t, j))],
        out_specs=pl.BlockSpec((None, tk, tn), lambda i, j, t: (j // nj, i, j % nj)),
        out_shape=jax.ShapeDtypeStruct((S, K, ns), BF16),
        scratch_shapes=[pltpu.VMEM((tk, tn), F32)],
        compiler_params=_params("parallel", "parallel", "arbitrary"),
    )(a, b)


def _lru_coeffs(lam_row, r):
    cl = -LRU_C * _softplus(-lam_row)
    log_a = cl * r
    a = jnp.exp(log_a)
    em = _one_minus_square(a, log_a)
    return cl, a, em, jnp.sqrt(em)


def _rglru_fwd(proj, conv_w, conv_b, w_r, b_r, w_i, b_i, lam, name, rider=None):
    T = proj.shape[0]
    DR = conv_w.shape[1]
    NB = DR // HEAD
    tt = _row_tile(T, 256)
    PAD = 8

    def body(x_ref, g_ref, cw_ref, cb_ref, wr_ref, br_ref, wi_ref, bi_ref, lam_ref,
             y_ref, xc_ref, r_ref, i_ref, h_ref, xbuf, hcar):
        @pl.when(pl.program_id(0) == 0)
        def _():
            xbuf[0:PAD, :] = jnp.zeros((PAD, DR), F32)
            hcar[...] = jnp.zeros_like(hcar)

        xbuf[PAD:PAD + tt, :] = x_ref[...]
        xc = cb_ref[...] + cw_ref[0:1, :] * xbuf[pl.ds(PAD - 3, tt), :]
        for k in range(1, CONV_WIDTH):
            xc = xc + cw_ref[k:k + 1, :] * xbuf[pl.ds(PAD - 3 + k, tt), :]
        xbuf[0:PAD, :] = xbuf[tt:tt + PAD, :]
        xc_ref[...] = xc
        xcb = xc.astype(BF16)
        for n in range(NB):
            sl = slice(n * HEAD, (n + 1) * HEAD)
            r_ref[:, sl] = jax.nn.sigmoid(_dot(xcb[:, sl], wr_ref[n]) + br_ref[:, sl])
            i_ref[:, sl] = _gate_sigmoid(_dot(xcb[:, sl], wi_ref[n]) + bi_ref[:, sl])
        _, a, _, mult = _lru_coeffs(lam_ref[...], r_ref[...])
        hs = mult * (i_ref[...] * xc)
        groups = tt // SUB
        a = a.reshape(groups, SUB, DR)
        hs = hs.reshape(groups, SUB, DR)
        sub = lax.broadcasted_iota(jnp.int32, (groups, SUB, DR), 1)
        d = 1
        while d < SUB:
            keep = sub >= d
            a_sh = jnp.where(keep, pltpu.roll(a, d, 1), 1.0)
            h_sh = jnp.where(keep, pltpu.roll(hs, d, 1), 0.0)
            hs = a * h_sh + hs
            a = a * a_sh
            d *= 2
        carry = hcar[...]
        for n in range(groups):
            h_ref[n * SUB:(n + 1) * SUB, :] = hs[n] + a[n] * carry
            carry = h_ref[(n + 1) * SUB - 1:(n + 1) * SUB, :]
        hcar[...] = carry
        h = h_ref[...]
        g = g_ref[...]
        y_ref[...] = (h * (g * _gate_sigmoid(g))).astype(y_ref.dtype)

    col = lambda j: pl.BlockSpec((tt, DR), lambda c: (c, j))
    vec = pl.BlockSpec((1, DR), lambda c: (0, 0))
    gate = pl.BlockSpec((NB, HEAD, HEAD), lambda c: (0, 0, 0))
    f32_out = jax.ShapeDtypeStruct((T, DR), F32)
    nc = T // tt
    return _call_with_rider(
        body, rider, lambda: pl.program_id(0) == 0, lambda: pl.program_id(0) == nc - 1,
        name=name, grid=(nc,),
        in_specs=[col(0), col(1), pl.BlockSpec((CONV_WIDTH, DR), lambda c: (0, 0)), vec, gate, vec, gate, vec, vec],
        out_specs=[col(0)] * 5,
        out_shape=[jax.ShapeDtypeStruct((T, DR), BF16), f32_out, f32_out, f32_out, f32_out],
        scratch_shapes=[pltpu.VMEM((tt + PAD, DR), F32), pltpu.VMEM((1, DR), F32)],
        semantics=("arbitrary",),
        args=(proj, proj, conv_w, conv_b.reshape(1, DR), w_r, b_r.reshape(1, DR), w_i, b_i.reshape(1, DR),
              lam.reshape(1, DR)))


def _rglru_bwd(dy, proj, xc, r, i, h, conv_w, w_r, w_i, lam, name, rider=None):
    T = proj.shape[0]
    DR = conv_w.shape[1]
    NB = DR // HEAD
    tt = _row_tile(T, 128)
    nc = T // tt
    PAD = 8
    per = tt // PAD

    def body(dy_ref, x_ref, g_ref, xc_ref, r_ref, i_ref, h_ref, xprev_ref, hprev_ref,
             cw_ref, wr_ref, wi_ref, lam_ref,
             dxg_ref, dcw_ref, dcb_ref, dwr_ref, dbr_ref, dwi_ref, dbi_ref, dlam_ref,
             xbuf, dxcbuf, gcar, acar):
        step = pl.program_id(0)
        chunk = nc - 1 - step

        @pl.when(step == 0)
        def _():
            for ref in (dcw_ref, dcb_ref, dwr_ref, dbr_ref, dwi_ref, dbi_ref, dlam_ref, gcar, acar):
                ref[...] = jnp.zeros_like(ref)
            dxcbuf[tt:tt + PAD, :] = jnp.zeros((PAD, DR), F32)

        not_first = (chunk > 0).astype(F32)
        row = lax.broadcasted_iota(jnp.int32, (tt, DR), 0)
        silu, dsilu = _silu_and_grad(g_ref[...])
        dyv = dy_ref[...]
        hv = h_ref[...]
        dxg_ref[:, DR:2 * DR] = (dyv * hv * dsilu).astype(dxg_ref.dtype)
        dh = dyv * silu
        rv = r_ref[...]
        iv = i_ref[...]
        xcv = xc_ref[...]
        lam_row = lam_ref[...]
        cl, a, em, mult = _lru_coeffs(lam_row, rv)
        b = jnp.where(row == tt - 1, acar[...], pltpu.roll(a, tt - 1, 0))
        gs = dh
        d = 1
        while d < tt:
            keep = row < tt - d
            b_sh = jnp.where(keep, pltpu.roll(b, tt - d, 0), 1.0)
            g_sh = jnp.where(keep, pltpu.roll(gs, tt - d, 0), 0.0)
            gs = gs + b * g_sh
            b = b * b_sh
            d *= 2
        gt = gs + b * gcar[...]
        xbuf[0:tt, :] = gt
        gcar[...] = xbuf[0:1, :]
        acar[...] = _lru_coeffs(lam_row, r_ref[0:1, :])[1]
        h_before = hprev_ref[PAD - 1:PAD, :] * not_first
        hprev = jnp.where(row == 0, h_before, pltpu.roll(hv, 1, 0))
        da = gt * hprev
        dmult = gt * (iv * xcv)
        di = gt * mult * xcv
        dxc = gt * mult * iv
        dlog_a = da * a - dmult * (1.0 - em) / mult
        dr = dlog_a * cl
        dlam_ref[...] += jnp.sum(dlog_a * rv, axis=0, keepdims=True) * (LRU_C * jax.nn.sigmoid(-lam_row))
        drp = dr * rv * (1.0 - rv)
        dip = di * iv * (1.0 - iv)
        dbr_ref[...] += jnp.sum(drp, axis=0, keepdims=True)
        dbi_ref[...] += jnp.sum(dip, axis=0, keepdims=True)
        drpb = drp.astype(BF16)
        dipb = dip.astype(BF16)
        xcb = xcv.astype(BF16)
        for n in range(NB):
            sl = slice(n * HEAD, (n + 1) * HEAD)
            dxcbuf[0:tt, sl] = dxc[:, sl] + _dot_nt(drpb[:, sl], wr_ref[n]) + _dot_nt(dipb[:, sl], wi_ref[n])
            dwr_ref[n] += _dot_tn(xcb[:, sl], drpb[:, sl])
            dwi_ref[n] += _dot_tn(xcb[:, sl], dipb[:, sl])
        dxc_all = dxcbuf[0:tt, :]
        dcb_ref[...] += jnp.sum(dxc_all, axis=0, keepdims=True)
        xbuf[0:PAD, :] = xprev_ref[...] * not_first
        xbuf[PAD:PAD + tt, :] = x_ref[...]
        dx = cw_ref[0:1, :] * dxcbuf[pl.ds(3, tt), :]
        for k in range(1, CONV_WIDTH):
            dx = dx + cw_ref[k:k + 1, :] * dxcbuf[pl.ds(3 - k, tt), :]
        dxg_ref[:, 0:DR] = dx.astype(dxg_ref.dtype)
        for k in range(CONV_WIDTH):
            dcw_ref[k:k + 1, :] += jnp.sum(xbuf[pl.ds(PAD - 3 + k, tt), :] * dxc_all, axis=0, keepdims=True)
        dxcbuf[tt:tt + PAD, :] = dxcbuf[0:PAD, :]

    rev = lambda j: pl.BlockSpec((tt, DR), lambda s: (nc - 1 - s, j))
    prev = pl.BlockSpec((PAD, DR), lambda s: (jnp.maximum((nc - 1 - s) * per - 1, 0), 0))
    vec = pl.BlockSpec((1, DR), lambda s: (0, 0))
    gate = pl.BlockSpec((NB, HEAD, HEAD), lambda s: (0, 0, 0))
    taps = pl.BlockSpec((CONV_WIDTH, DR), lambda s: (0, 0))
    vec_out = jax.ShapeDtypeStruct((1, DR), F32)
    gate_out = jax.ShapeDtypeStruct((NB, HEAD, HEAD), F32)
    return _call_with_rider(
        body, rider, lambda: pl.program_id(0) == 0, lambda: pl.program_id(0) == nc - 1,
        name=name, grid=(nc,),
        in_specs=[rev(0), rev(0), rev(1), rev(0), rev(0), rev(0), rev(0), prev, prev, taps, gate, gate, vec],
        out_specs=[pl.BlockSpec((tt, 2 * DR), lambda s: (nc - 1 - s, 0)), taps, vec, gate, vec, gate, vec, vec],
        out_shape=[jax.ShapeDtypeStruct((T, 2 * DR), BF16), jax.ShapeDtypeStruct((CONV_WIDTH, DR), F32),
                   vec_out, gate_out, vec_out, gate_out, vec_out, vec_out],
        scratch_shapes=[pltpu.VMEM((tt + PAD, DR), F32), pltpu.VMEM((tt + PAD, DR), F32),
                        pltpu.VMEM((1, DR), F32), pltpu.VMEM((1, DR), F32)],
        semantics=("arbitrary",),
        args=(dy, proj, proj, xc, r, i, h, proj, h, conv_w, w_r, w_i, lam.reshape(1, DR)))


def _mem_probs(q, k, scale):
    s = _dot_nt(q, k) * scale
    p = jnp.exp(s - jnp.max(s, axis=-1, keepdims=True))
    return p * (1.0 / jnp.sum(p, axis=-1, keepdims=True))


def _memattn_fwd(proj, mkv, DR, DM, name):
    T = proj.shape[0]
    M = mkv.shape[0]
    NH = DM // HEAD
    tt = _row_tile(T, 512)
    qcol = 2 * DR // DM
    scale = HEAD ** -0.5

    def body(q_ref, g_ref, k_ref, v_ref, y_ref):
        for n in range(NH):
            sl = slice(n * HEAD, (n + 1) * HEAD)
            p = _mem_probs(q_ref[:, sl].astype(BF16), k_ref[:, sl], scale)
            o = _dot(p.astype(BF16), v_ref[:, sl])
            g = g_ref[:, sl]
            y_ref[:, sl] = (o * (g * _gate_sigmoid(g))).astype(y_ref.dtype)

    return pl.pallas_call(
        body, name=name, grid=(T // tt,),
        in_specs=[pl.BlockSpec((tt, DM), lambda t: (t, qcol)), pl.BlockSpec((tt, DM), lambda t: (t, qcol + 1)),
                  pl.BlockSpec((M, DM), lambda t: (0, 0)), pl.BlockSpec((M, DM), lambda t: (0, 1))],
        out_specs=pl.BlockSpec((tt, DM), lambda t: (t, 0)),
        out_shape=jax.ShapeDtypeStruct((T, DM), BF16),
        compiler_params=_params("parallel"),
    )(proj, proj, mkv, mkv)


def _memattn_bwd(dy, proj, mkv, DR, DM, name):
    T = proj.shape[0]
    M = mkv.shape[0]
    NH = DM // HEAD
    tt = _row_tile(T, 512)
    qcol = 2 * DR // DM
    scale = HEAD ** -0.5

    def body(dy_ref, q_ref, g_ref, k_ref, v_ref, dqg_ref, dkv_ref):
        @pl.when(pl.program_id(0) == 0)
        def _():
            dkv_ref[...] = jnp.zeros_like(dkv_ref)

        for n in range(NH):
            sl = slice(n * HEAD, (n + 1) * HEAD)
            qb = q_ref[:, sl].astype(BF16)
            kb = k_ref[:, sl]
            vb = v_ref[:, sl]
            p = _mem_probs(qb, kb, scale)
            pb = p.astype(BF16)
            o = _dot(pb, vb)
            silu, dsilu = _silu_and_grad(g_ref[:, sl])
            dyv = dy_ref[:, sl]
            dqg_ref[:, DM + n * HEAD:DM + (n + 1) * HEAD] = (dyv * o * dsilu).astype(dqg_ref.dtype)
            dob = (dyv * silu).astype(BF16)
            dp = _dot_nt(dob, vb)
            ds = (p * (dp - jnp.sum(dp * p, axis=-1, keepdims=True)) * scale).astype(BF16)
            dqg_ref[:, sl] = _dot(ds, kb).astype(dqg_ref.dtype)
            dkv_ref[:, sl] += _dot_tn(ds, qb)
            dkv_ref[:, DM + n * HEAD:DM + (n + 1) * HEAD] += _dot_tn(pb, dob)

    return pl.pallas_call(
        body, name=name, grid=(T // tt,),
        in_specs=[pl.BlockSpec((tt, DM), lambda t: (t, DR // DM)),
                  pl.BlockSpec((tt, DM), lambda t: (t, qcol)), pl.BlockSpec((tt, DM), lambda t: (t, qcol + 1)),
                  pl.BlockSpec((M, DM), lambda t: (0, 0)), pl.BlockSpec((M, DM), lambda t: (0, 1))],
        out_specs=[pl.BlockSpec((tt, 2 * DM), lambda t: (t, 0)), pl.BlockSpec((M, 2 * DM), lambda t: (0, 0))],
        out_shape=[jax.ShapeDtypeStruct((T, 2 * DM), BF16), jax.ShapeDtypeStruct((M, 2 * DM), F32)],
        compiler_params=_params("arbitrary"),
    )(dy, proj, proj, mkv, mkv)


def _sb_blocks(T):
    tk = _row_tile(T // 2, 256)
    tq = 2 * tk
    assert T % tq == 0
    return tq, tk


def _sb_upper(tk):
    row = lax.broadcasted_iota(jnp.int32, (tk, tk), 0)
    col = lax.broadcasted_iota(jnp.int32, (tk, tk), 1)
    return (row > col).astype(BF16)


def _sb_causal(tq, tk, d):
    row = lax.broadcasted_iota(jnp.int32, (tq, tk), 0)
    col = lax.broadcasted_iota(jnp.int32, (tq, tk), 1)
    return col + d * tk < row


def _sb_own_span(tile, carry, i, tq, tk):
    span = tq // tk
    for d in reversed(range(span)):
        lo = d * tk
        part = tile(span * i + d, tuple(c[lo:] for c in carry), _sb_causal(tq - lo, tk, 0), slice(lo, tq))
        carry = tuple(p if lo == 0 else jnp.concatenate([c[:lo], p], axis=0) for c, p in zip(carry, part))
    return carry


def _later_sum(x, upper):
    n = x.shape[0]
    hi, lo = _split_bf16(x)
    both = _dot(jnp.concatenate([hi, lo], axis=0), upper)
    return both[0:n] + both[n:2 * n]


def _sb_alive(gone_c):
    return (jnp.min(gone_c) < SB_EXHAUSTED).astype(jnp.int32)


def _sb_weights(q, kb, gone_c, causal, upper, scale):
    return _sb_weights_of(_dot_nt(q, kb) * scale, gone_c, causal, upper)


def _sb_weights_of(z, gone_c, causal, upper):
    sp = jnp.where(z > 20.0, z, jnp.log(1.0 + jnp.exp(z)))
    spm = sp if causal is None else jnp.where(causal, sp, 0.0)
    gone = _later_sum(spm, upper) + gone_c
    w = jnp.exp(z - sp - gone)
    if causal is not None:
        w = jnp.where(causal, w, 0.0)
    return z, sp, spm, w


def _sb_fwd(proj, kv, DR, name, rider=None):
    T = proj.shape[0]
    NH = DR // HEAD
    tq, tk = _sb_blocks(T)
    span = tq // tk
    scale = HEAD ** -0.5

    def body(q_ref, g_ref, k_ref, v_ref, y_ref, o_ref):
        i = pl.program_id(1)
        q = q_ref[...].astype(BF16)
        upper = _sb_upper(tk)

        def tile(jb, carry, causal, rows=slice(0, tq)):
            gone_c, acc = carry
            n = rows.stop - rows.start
            start = pl.multiple_of(jb * tk, tk)
            kb = k_ref[pl.ds(start, tk), :]
            vb = v_ref[pl.ds(start, tk), :]
            _, _, spm, w = _sb_weights(q[rows], kb, gone_c, causal, upper, scale)
            hi, lo = _split_bf16(w)
            pv = _dot(jnp.concatenate([hi, lo], axis=0), vb)
            return gone_c + jnp.sum(spm, axis=-1, keepdims=True), acc + pv[0:n] + pv[n:2 * n]

        carry = _sb_own_span(tile, (jnp.zeros((tq, 1), F32), jnp.zeros((tq, HEAD), F32)), i, tq, tk)

        def more(c):
            return jnp.logical_and(c[0] >= 0, c[1] > 0)

        def step(c):
            jb, _, gone_c, acc = c
            gone_c, acc = tile(jb, (gone_c, acc), None)
            return jb - 1, _sb_alive(gone_c), gone_c, acc

        _, _, _, acc = lax.while_loop(more, step, (span * i - 1, _sb_alive(carry[0]), *carry))
        o_ref[...] = acc
        g = g_ref[...]
        y_ref[...] = (acc * (g * _gate_sigmoid(g))).astype(y_ref.dtype)

    blk = lambda off: pl.BlockSpec((tq, HEAD), lambda h, i: (i, off + h))
    whole = lambda off: pl.BlockSpec((T, HEAD), lambda h, i: (0, off + h))
    nq = T // tq
    return _call_with_rider(
        body, rider, lambda: (pl.program_id(0) == 0) & (pl.program_id(1) == 0),
        lambda: (pl.program_id(0) == NH - 1) & (pl.program_id(1) == nq - 1),
        name=name, grid=(NH, nq),
        in_specs=[blk(0), blk(NH), whole(0), whole(NH)],
        out_specs=[blk(0), blk(0)],
        out_shape=[jax.ShapeDtypeStruct((T, DR), BF16), jax.ShapeDtypeStruct((T, DR), F32)],
        scratch_shapes=[], semantics=("parallel", "arbitrary"),
        args=(proj, proj, kv, kv))


def _sb_bwd(dy, proj, kv, o, DR, name, rider=None):
    T = proj.shape[0]
    NH = DR // HEAD
    tq, tk = _sb_blocks(T)
    span = tq // tk
    scale = HEAD ** -0.5

    def body(dy_ref, q_ref, g_ref, k_ref, v_ref, o_ref, dq_ref, dg_ref, dk_ref, dv_ref):
        i = pl.program_id(1)

        @pl.when(i == 0)
        def _():
            dk_ref[...] = jnp.zeros_like(dk_ref)
            dv_ref[...] = jnp.zeros_like(dv_ref)

        qf = q_ref[...]
        q = qf.astype(BF16)
        q_t = qf.T.astype(BF16)
        upper = _sb_upper(tk)
        silu, dsilu = _silu_and_grad(g_ref[...])
        dyv = dy_ref[...]
        ov = o_ref[...]
        dg_ref[...] = (dyv * ov * dsilu).astype(dg_ref.dtype)
        do = dyv * silu
        dob = do.astype(BF16)
        do_t = do.T.astype(BF16)
        total = jnp.sum(dob.astype(F32) * ov, axis=-1, keepdims=True)

        def tile(jb, carry, causal, rows=slice(0, tq)):
            gone_c, e_after, dq = carry
            start = pl.multiple_of(jb * tk, tk)
            kb = k_ref[pl.ds(start, tk), :]
            vb = v_ref[pl.ds(start, tk), :]
            z, sp, spm, w = _sb_weights(q[rows], kb, gone_c, causal, upper, scale)
            e = _dot_nt(dob[rows], vb) * w
            dz = e - (total[rows] - (_later_sum(e, upper) + e_after)) * jnp.exp(z - sp)
            if causal is not None:
                dz = jnp.where(causal, dz, 0.0)
            dzb = dz.astype(BF16)
            dk_ref[jb] += _dot(q_t[:, rows], dzb) * scale
            dv_ref[jb] += _dot(do_t[:, rows], w.astype(BF16))
            return (gone_c + jnp.sum(spm, axis=-1, keepdims=True), e_after + jnp.sum(e, axis=-1, keepdims=True),
                    dq + _dot(dzb, kb))

        carry = _sb_own_span(
            tile, (jnp.zeros((tq, 1), F32), jnp.zeros((tq, 1), F32), jnp.zeros((tq, HEAD), F32)), i, tq, tk)

        def more(c):
            return jnp.logical_and(c[0] >= 0, c[1] > 0)

        def step(c):
            jb, _, *rest = c
            rest = tile(jb, tuple(rest), None)
            return jb - 1, _sb_alive(rest[0]), *rest

        *_, dq = lax.while_loop(more, step, (span * i - 1, _sb_alive(carry[0]), *carry))
        dq_ref[...] = (dq * scale).astype(dq_ref.dtype)

    blk = lambda off: pl.BlockSpec((tq, HEAD), lambda h, i: (i, off + h))
    whole = lambda off: pl.BlockSpec((T, HEAD), lambda h, i: (0, off + h))
    keys_t = pl.BlockSpec((None, T // tk, HEAD, tk), lambda h, i: (h, 0, 0, 0))
    keys_t_shape = jax.ShapeDtypeStruct((NH, T // tk, HEAD, tk), F32)
    nq = T // tq
    return _call_with_rider(
        body, rider, lambda: (pl.program_id(0) == 0) & (pl.program_id(1) == 0),
        lambda: (pl.program_id(0) == NH - 1) & (pl.program_id(1) == nq - 1),
        name=name, grid=(NH, nq),
        in_specs=[blk(0), blk(0), blk(NH), whole(0), whole(NH), blk(0)],
        out_specs=[blk(0), blk(0), keys_t, keys_t],
        out_shape=[jax.ShapeDtypeStruct((T, DR), BF16), jax.ShapeDtypeStruct((T, DR), BF16),
                   keys_t_shape, keys_t_shape],
        scratch_shapes=[], semantics=("parallel", "arbitrary"),
        args=(dy, proj, proj, kv, kv, o))


def _merge_dkv(parts, name):
    NH, nblk, _, tk = parts[0][0].shape
    DR = NH * HEAD
    n = len(parts)

    def body(*refs):
        o_ref = refs[-1]
        for which in range(2):
            for h in range(NH):
                acc = refs[which][h]
                for p in range(1, n):
                    acc = acc + refs[2 * p + which][h]
                col = which * DR + h * HEAD
                o_ref[:, col:col + HEAD] = acc.T.astype(o_ref.dtype)

    blk = pl.BlockSpec((NH, None, HEAD, tk), lambda t: (0, t, 0, 0))
    return pl.pallas_call(
        body, name=name, grid=(nblk,),
        in_specs=[blk] * (2 * n),
        out_specs=pl.BlockSpec((tk, 2 * DR), lambda t: (t, 0)),
        out_shape=jax.ShapeDtypeStruct((nblk * tk, 2 * DR), BF16),
        compiler_params=_params("parallel"),
    )(*[a for pair in parts for a in pair])


def _final_loss(h, g, target, name):
    T, D = h.shape
    tt = _row_tile(T, 256)

    def body(h_ref, g_ref, t_ref, dh_ref, dg_ref, sq_ref):
        @pl.when(pl.program_id(0) == 0)
        def _():
            dg_ref[...] = jnp.zeros_like(dg_ref)
            sq_ref[...] = jnp.zeros_like(sq_ref)

        hv = h_ref[...]
        gv = g_ref[...]
        r = lax.rsqrt(jnp.mean(hv * hv, axis=-1, keepdims=True) + EPS)
        xhat = hv * r
        err = xhat * gv - t_ref[...]
        sq_ref[...] += jnp.sum(err * err, axis=0, keepdims=True)
        dy = err * (1.0 / D)
        dxhat = dy * gv
        dh_ref[...] = r * (dxhat - xhat * jnp.mean(dxhat * xhat, axis=-1, keepdims=True))
        dg_ref[...] += jnp.sum(dy * xhat, axis=0, keepdims=True)

    row = pl.BlockSpec((tt, D), lambda i: (i, 0))
    vec = pl.BlockSpec((1, D), lambda i: (0, 0))
    return pl.pallas_call(
        body, name=name, grid=(T // tt,),
        in_specs=[row, vec, row],
        out_specs=[row, vec, vec],
        out_shape=[jax.ShapeDtypeStruct((T, D), F32), jax.ShapeDtypeStruct((1, D), F32),
                   jax.ShapeDtypeStruct((1, D), F32)],
        compiler_params=_params("arbitrary"),
    )(h, g.reshape(1, D), target)


def _sum_parts(parts_ref):
    g = parts_ref[0].astype(F32)
    for s in range(1, parts_ref.shape[0]):
        g = g + parts_ref[s].astype(F32)
    return g


def _adamw(parts, w, m, v, name):
    P, R, C = parts.shape
    tr = _rows_for(R, P * C * 4)

    def body(p_ref, w_ref, m_ref, v_ref, g_ref, d_ref, nm_ref, nv_ref):
        g = _sum_parts(p_ref)
        nm = ADAM_B1 * m_ref[...] + (1.0 - ADAM_B1) * g
        nv = ADAM_B2 * v_ref[...] + (1.0 - ADAM_B2) * jnp.square(g)
        m_hat = nm / (1.0 - ADAM_B1 ** ADAM_STEP)
        v_hat = nv / (1.0 - ADAM_B2 ** ADAM_STEP)
        g_ref[...] = g
        d_ref[...] = -ADAM_LR * (m_hat / (jnp.sqrt(v_hat) + ADAM_EPS) + ADAM_WD * w_ref[...])
        nm_ref[...] = nm
        nv_ref[...] = nv

    row = pl.BlockSpec((tr, C), lambda i: (i, 0))
    out = jax.ShapeDtypeStruct((R, C), F32)
    return pl.pallas_call(
        body, name=name, grid=(R // tr,),
        in_specs=[pl.BlockSpec((P, tr, C), lambda i: (0, i, 0)), row, row, row],
        out_specs=[row] * 4,
        out_shape=[out] * 4,
        compiler_params=_params("parallel"),
    )(parts, w, m, v)


def _sum_devices(parts, name):
    P, R, C = parts.shape
    tr = _rows_for(R, P * C * 4)

    def body(p_ref, o_ref):
        o_ref[...] = _sum_parts(p_ref)

    return pl.pallas_call(
        body, name=name, grid=(R // tr,),
        in_specs=[pl.BlockSpec((P, tr, C), lambda i: (0, i, 0))],
        out_specs=pl.BlockSpec((tr, C), lambda i: (i, 0)),
        out_shape=jax.ShapeDtypeStruct((R, C), F32),
        compiler_params=_params("parallel"),
    )(parts)


def _mesh_position():
    return lax.axis_index("x"), lax.axis_index("y"), lax.axis_index("c")


def _device_index(p):
    return 4 * p[0] + 2 * p[1] + p[2]


def _all_gather(arrs, name):
    n = len(arrs)

    def body(*refs):
        ins, outs = refs[:n], refs[n:2 * n]
        send_sems, recv_sems, local_sems = refs[2 * n:]
        x, y, c = _mesh_position()
        me, sibling = (x, y, c), (x, y, 1 - c)
        chips = [(1 - x, y), (x, 1 - y), (1 - x, 1 - y)]

        def slot(a, p):
            return outs[a].at[_device_index(p)]

        def copy(a, k, block, to, src=None):
            return pltpu.make_async_remote_copy(
                src_ref=slot(a, block) if src is None else src, dst_ref=slot(a, block),
                send_sem=send_sems.at[a, k], recv_sem=recv_sems.at[a, k],
                device_id=to, device_id_type=pl.DeviceIdType.MESH)

        mine = [pltpu.make_async_copy(ins[a], slot(a, me), local_sems.at[a]) for a in range(n)]
        for cp in mine:
            cp.start()
        first = []
        for a in range(n):
            first.append(copy(a, 0, me, sibling, src=ins[a]))
            first += [copy(a, 1 + j, me, (*chip, c), src=ins[a]) for j, chip in enumerate(chips)]
        for cp in first:
            cp.start()
        passed = []
        for a in range(n):
            for j, chip in enumerate(chips):
                copy(a, 1 + j, (*chip, c), me).wait_recv()
                fwd = copy(a, 4 + j, (*chip, c), sibling)
                fwd.start()
                passed.append(fwd)
        for a in range(n):
            copy(a, 0, sibling, me).wait_recv()
            for j, chip in enumerate(chips):
                copy(a, 4 + j, (*chip, 1 - c), me).wait_recv()
        for cp in first + passed:
            cp.wait_send()
        for cp in mine:
            cp.wait()

    return pl.pallas_call(
        body, name=name,
        in_specs=[HBM_SPEC] * n, out_specs=[HBM_SPEC] * n,
        out_shape=[jax.ShapeDtypeStruct((N_DEV,) + a.shape, a.dtype) for a in arrs],
        scratch_shapes=[pltpu.SemaphoreType.DMA((n, 7)), pltpu.SemaphoreType.DMA((n, 7)),
                        pltpu.SemaphoreType.DMA((n,))],
    )(*arrs)


class _Exchange:
    def __init__(self, arrs, gather):
        self.arrs = list(arrs)
        self.gather = gather
        self.n = len(self.arrs)

    def out_shape(self):
        lead = (N_DEV,) if self.gather else ()
        return [jax.ShapeDtypeStruct(lead + a.shape, a.dtype) for a in self.arrs]

    def scratch(self):
        return [pltpu.SemaphoreType.DMA((self.n, 7)), pltpu.SemaphoreType.DMA((self.n, 7)),
                pltpu.SemaphoreType.DMA((self.n,))]

    def _copies(self, ins, outs, sems, arrivals):
        send_sems, recv_sems, local_sems = sems
        x, y, c = _mesh_position()
        me = _device_index((x, y, c))
        peers = [(1 - x if k & 4 else x, 1 - y if k & 2 else y, 1 - c if k & 1 else c) for k in range(1, N_DEV)]
        local, sends, recvs = [], [], []
        for a in range(self.n):
            mine = ins[a] if self.gather else ins[a].at[me]
            local.append(pltpu.make_async_copy(mine, outs[a].at[me], local_sems.at[a]))
            for k, peer in enumerate(peers):
                there = _device_index(peer)
                src = ins[a] if self.gather else ins[a].at[there]
                sem = dict(send_sem=send_sems.at[a, k], recv_sem=recv_sems.at[a, k],
                           device_id=peer, device_id_type=pl.DeviceIdType.MESH)
                sends.append(pltpu.make_async_remote_copy(src_ref=src, dst_ref=outs[a].at[me], **sem))
                if arrivals:
                    recvs.append(pltpu.make_async_remote_copy(src_ref=src, dst_ref=outs[a].at[there], **sem))
        return local, sends, recvs

    def start(self, ins, outs, sems):
        local, sends, _ = self._copies(ins, outs, sems, arrivals=False)
        for cp in local + sends:
            cp.start()

    def wait(self, ins, outs, sems):
        local, sends, recvs = self._copies(ins, outs, sems, arrivals=True)
        for cp in recvs:
            cp.wait_recv()
        for cp in sends:
            cp.wait_send()
        for cp in local:
            cp.wait()


def _all_to_all(arrs, name):
    ex = _Exchange(arrs, gather=False)
    n = ex.n

    def body(*refs):
        ins, outs, sems = refs[:n], refs[n:2 * n], refs[2 * n:]
        ex.start(ins, outs, sems)
        ex.wait(ins, outs, sems)

    return pl.pallas_call(
        body, name=name,
        in_specs=[HBM_SPEC] * n, out_specs=[HBM_SPEC] * n,
        out_shape=ex.out_shape(), scratch_shapes=ex.scratch(),
    )(*arrs)


def _call_with_rider(body, rider, first, last, *, name, grid, in_specs, out_specs, out_shape, scratch_shapes,
                     semantics, args):
    if rider is None:
        out = pl.pallas_call(body, name=name, grid=grid, in_specs=in_specs, out_specs=out_specs,
                             out_shape=out_shape, scratch_shapes=scratch_shapes,
                             compiler_params=_params(*semantics))(*args)
        return out, None
    n, n_in, n_out = rider.n, len(in_specs), len(out_specs)

    def riding(*refs):
        ins, r_in = refs[:n_in], refs[n_in:n_in + n]
        outs, r_out = refs[n_in + n:n_in + n + n_out], refs[n_in + n + n_out:n_in + 2 * n + n_out]
        scratch, sems = refs[n_in + 2 * n + n_out:-3], refs[-3:]

        @pl.when(first())
        def _():
            rider.start(r_in, r_out, sems)

        body(*ins, *outs, *scratch)

        @pl.when(last())
        def _():
            rider.wait(r_in, r_out, sems)

    out = pl.pallas_call(
        riding, name=name, grid=grid,
        in_specs=list(in_specs) + [HBM_SPEC] * n, out_specs=list(out_specs) + [HBM_SPEC] * n,
        out_shape=list(out_shape) + rider.out_shape(),
        scratch_shapes=list(scratch_shapes) + rider.scratch(),
        compiler_params=_params(*["arbitrary"] * len(grid)),
    )(*args, *rider.arrs)
    return out[:n_out], out[n_out:]


def _pack(arrs, row_multiple):
    parts = []
    rows = 0
    for a in arrs:
        flat = a.reshape(-1).astype(F32)
        r = -(-flat.shape[0] // (8 * LANES)) * 8
        parts.append(jnp.pad(flat, (0, r * LANES - flat.shape[0])).reshape(r, LANES))
        rows += r
    pad = -rows % row_multiple
    if pad:
        parts.append(jnp.zeros((pad, LANES), F32))
    return jnp.concatenate(parts, axis=0)


def _unpack(buf, shapes, lead=()):
    out = []
    r0 = 0
    for shape in shapes:
        size = 1
        for s in shape:
            size *= s
        r = -(-size // (8 * LANES)) * 8
        part = buf[..., r0:r0 + r, :].reshape(lead + (r * LANES,))[..., :size]
        out.append(part.reshape(lead + tuple(shape)))
        r0 += r
    return out


def _gathered_cols(g):
    g = jnp.moveaxis(g, 0, -2)
    return g.reshape(g.shape[:-2] + (g.shape[-2] * g.shape[-1],))


def kernel(x, mem, mem_norm, w_mem_kv, norm_a, w_in_a, conv_w, conv_b, w_rec_gate, b_rec_gate, w_in_gate, b_in_gate, lru_lambda, w_out_a, kv_norm, w_kv, norm_b, w_in_b, w_out_b, final_norm, loss_target, m_mem_norm, m_w_mem_kv, m_norm_a, m_w_in_a, m_conv_w, m_conv_b, m_w_rec_gate, m_b_rec_gate, m_w_in_gate, m_b_in_gate, m_lru_lambda, m_w_out_a, m_kv_norm, m_w_kv, m_norm_b, m_w_in_b, m_w_out_b, m_final_norm, v_mem_norm, v_w_mem_kv, v_norm_a, v_w_in_a, v_conv_w, v_conv_b, v_w_rec_gate, v_b_rec_gate, v_w_in_gate, v_b_in_gate, v_lru_lambda, v_w_out_a, v_kv_norm, v_w_kv, v_norm_b, v_w_in_b, v_w_out_b, v_final_norm):
    xs = x[0]
    T, D = xs.shape
    L = w_mem_kv.shape[0]
    NA = w_in_a.shape[0]
    NB = w_in_b.shape[0]
    DM2 = w_mem_kv.shape[2]
    DM = DM2 // 2
    DR = w_rec_gate.shape[1] * w_rec_gate.shape[2]
    me = _device_index(_mesh_position())

    small_sharded = [norm_a, conv_w, conv_b, b_rec_gate, b_in_gate, lru_lambda]
    shard = {f"mem_kv{l}": w_mem_kv[l] for l in range(L)}
    shard.update({f"in_a{l}": w_in_a[l] for l in range(NA)}, **{f"out_a{l}": w_out_a[l] for l in range(NA)})
    shard.update({f"in_b{j}": w_in_b[j] for j in range(NB)}, **{f"out_b{j}": w_out_b[j] for j in range(NB)})
    shard["kv"] = w_kv
    shard = {k: w.astype(BF16) for k, w in shard.items()}
    now = [f"mem_kv{l}" for l in range(L)] + ["in_a0", "out_a0"]
    gathered = _all_gather([shard[k] for k in now] + [_pack(small_sharded, 8)], "gather_params")
    full = dict(zip(now, gathered[:-1]))

    def gather_rider(keys):
        return _Exchange([shard[k] for k in keys], gather=True), keys

    def out_proj(key):
        return full[key].reshape(1, -1, D)

    g_mem = [full[f"mem_kv{l}"].reshape(1, D, DM2) for l in range(L)]
    norm_a_f, conv_w_f, conv_b_f, b_r_f, b_i_f, lam_f = [
        _gathered_cols(s) for s in _unpack(gathered[-1], [s.shape for s in small_sharded], lead=(N_DEV,))]
    w_r_bf = w_rec_gate.astype(BF16)
    w_i_bf = w_in_gate.astype(BF16)

    zeros_mem = jnp.zeros_like(mem[0])
    mem_n = _rms_fwd(mem[0], mem_norm, "rms_mem")
    mkv = [_mm_nn(mem_n, g_mem[l], out_dtype=BF16, name=f"mm_mem_kv{l}") for l in range(L)]

    h = xs
    saved_a = []
    for l in range(NA):
        u = _rms_fwd(h, norm_a_f[l], f"rms_a{l}")
        if l == 0:
            proj = _mm_nn(u, full[f"in_a{l}"], name=f"mm_in_a{l}")
        else:
            rider, keys = gather_rider([f"out_a{l}"])
            proj, landed = _mm_nn(u, full[f"in_a{l}"], name=f"mm_in_a{l}", rider=rider)
            full.update(zip(keys, landed))
        rider, keys = gather_rider([f"in_a{l + 1}"] if l + 1 < NA else ["kv"])
        (y_rnn, xc, r, i, hr), landed = _rglru_fwd(proj, conv_w_f[l], conv_b_f[l], w_r_bf[l], b_r_f[l], w_i_bf[l],
                                                   b_i_f[l], lam_f[l], f"rglru_fwd{l}", rider)
        full.update(zip(keys, landed))
        y_mem = _memattn_fwd(proj, mkv[l], DR, DM, f"memattn_fwd_a{l}")
        ycat = jnp.concatenate([y_rnn, y_mem], axis=-1)
        h_next = _mm_nn(ycat, out_proj(f"out_a{l}"), res=h, name=f"mm_out_a{l}")
        saved_a.append((h, u, proj, xc, r, i, hr, ycat))
        h = h_next
    h_kv = h
    u_kv = _rms_fwd(h_kv, kv_norm, "rms_kv")
    rider, keys = gather_rider(["in_b0"])
    kv, landed = _mm_nn(u_kv, full["kv"], out_dtype=BF16, name="mm_kv", rider=rider)
    full.update(zip(keys, landed))
    saved_b = []
    for j in range(NB):
        u = _rms_fwd(h, norm_b[j], f"rms_b{j}")
        proj = _mm_nn(u, full[f"in_b{j}"], name=f"mm_in_b{j}")
        rider, keys = gather_rider([f"out_b{j}"] + ([f"in_b{j + 1}"] if j + 1 < NB else []))
        (y_sb, o_sb), landed = _sb_fwd(proj, kv, DR, f"sb_fwd{j}", rider)
        full.update(zip(keys, landed))
        y_mem = _memattn_fwd(proj, mkv[NA + j], DR, DM, f"memattn_fwd_b{j}")
        ycat = jnp.concatenate([y_sb, y_mem], axis=-1)
        h_next = _mm_nn(ycat, out_proj(f"out_b{j}"), res=h, name=f"mm_out_b{j}")
        saved_b.append((h, u, proj, o_sb, ycat))
        h = h_next

    dh, d_final_norm, sq = _final_loss(h, final_norm, loss_target[0], "final_loss")
    loss = lax.psum(0.5 * jnp.sum(sq) / D, ("x", "y", "c"))

    big_grads = {}
    received = {}

    def scatter_rider(keys):
        return _Exchange([big_grads[k] for k in keys], gather=False), keys

    dmkv = [None] * L
    d_norm_b = [None] * NB
    dkv_parts = []
    for j in reversed(range(NB)):
        h_in, u, proj, o_sb, ycat = saved_b[j]
        dy = _mm_nt(dh, out_proj(f"out_b{j}"), name=f"mm_dy_b{j}")
        big_grads[f"out_b{j}"] = _mm_tn(ycat, dh, 1, name=f"mm_dw_out_b{j}").reshape(N_DEV, -1, D)
        rider, keys = scatter_rider(([f"in_b{j + 1}"] if j + 1 < NB else []) + [f"out_b{j}"])
        (dq, dg, dk, dv), landed = _sb_bwd(dy, proj, kv, o_sb, DR, f"sb_bwd{j}", rider)
        received.update(zip(keys, landed))
        dkv_parts.append((dk, dv))
        dqg_mem, dmkv[NA + j] = _memattn_bwd(dy, proj, mkv[NA + j], DR, DM, f"memattn_bwd_b{j}")
        dproj = jnp.concatenate([dq, dg, dqg_mem], axis=-1)
        du = _mm_nt(dproj, full[f"in_b{j}"], name=f"mm_du_b{j}")
        big_grads[f"in_b{j}"] = _mm_tn(u, dproj, N_DEV, name=f"mm_dw_in_b{j}")
        dh, d_norm_b[j] = _rms_bwd(du, h_in, norm_b[j], dh, f"rms_bwd_b{j}")

    dkv = _merge_dkv(dkv_parts, "merge_dkv")
    du_kv = _mm_nt(dkv, full["kv"], name="mm_du_kv")
    big_grads["kv"] = _mm_tn(u_kv, dkv, N_DEV, name="mm_dw_kv")
    dh, d_kv_norm = _rms_bwd(du_kv, h_kv, kv_norm, dh, "rms_bwd_kv")

    d_norm_a, d_conv_w, d_conv_b, d_w_r, d_b_r, d_w_i, d_b_i, d_lam = ([None] * NA for _ in range(8))
    for l in reversed(range(NA)):
        h_in, u, proj, xc, r, i, hr, ycat = saved_a[l]
        if l == NA - 1:
            rider, keys = scatter_rider(["kv"])
            dy, landed = _mm_nt(dh, out_proj(f"out_a{l}"), name=f"mm_dy_a{l}", rider=rider)
            received.update(zip(keys, landed))
        else:
            dy = _mm_nt(dh, out_proj(f"out_a{l}"), name=f"mm_dy_a{l}")
        big_grads[f"out_a{l}"] = _mm_tn(ycat, dh, 1, name=f"mm_dw_out_a{l}").reshape(N_DEV, -1, D)
        rider, keys = scatter_rider([f"in_a{l + 1}" if l + 1 < NA else "in_b0", f"out_a{l}"])
        (dxg, d_conv_w[l], d_conv_b[l], d_w_r[l], d_b_r[l], d_w_i[l], d_b_i[l], d_lam[l]), landed = _rglru_bwd(
            dy, proj, xc, r, i, hr, conv_w_f[l], w_r_bf[l], w_i_bf[l], lam_f[l], f"rglru_bwd{l}", rider)
        received.update(zip(keys, landed))
        dqg_mem, dmkv[l] = _memattn_bwd(dy, proj, mkv[l], DR, DM, f"memattn_bwd_a{l}")
        dproj = jnp.concatenate([dxg, dqg_mem], axis=-1)
        du = _mm_nt(dproj, full[f"in_a{l}"], name=f"mm_du_a{l}")
        big_grads[f"in_a{l}"] = _mm_tn(u, dproj, N_DEV, name=f"mm_dw_in_a{l}")
        dh, d_norm_a[l] = _rms_bwd(du, h_in, norm_a_f[l], dh, f"rms_bwd_a{l}")
    grad_x = dh.reshape(x.shape)

    dmkv_all = jnp.concatenate(dmkv, axis=-1)
    dmem_n = _mm_nt(dmkv_all, jnp.concatenate(g_mem, axis=0), name="mm_dmem")
    for l in range(L):
        big_grads[f"mem_kv{l}"] = _mm_tn(mem_n, dmkv[l], 1, name=f"mm_dw_mem_kv{l}").reshape(N_DEV, -1, DM2)
    _, d_mem_norm = _rms_bwd(dmem_n, mem[0], mem_norm, zeros_mem, "rms_bwd_mem")

    last = ["in_a0"] + [f"mem_kv{l}" for l in range(L)]
    received.update(zip(last, _all_to_all([big_grads[k] for k in last], "scatter_grads")))

    def update(key, w, m, v):
        shape = w.shape
        two_d = (-1, shape[-1])
        return [o.reshape(shape) for o in _adamw(received[key], w.reshape(two_d), m.reshape(two_d),
                                                 v.reshape(two_d), f"adamw_{key}")]

    def update_layers(prefix, w, m, v):
        per_layer = [update(f"{prefix}{l}", w[l], m[l], v[l]) for l in range(w.shape[0])]
        return [jnp.stack([per_layer[l][k] for l in range(w.shape[0])]) for k in range(4)]

    upd = {
        "w_mem_kv": update_layers("mem_kv", w_mem_kv, m_w_mem_kv, v_w_mem_kv),
        "w_in_a": update_layers("in_a", w_in_a, m_w_in_a, v_w_in_a),
        "w_out_a": update_layers("out_a", w_out_a, m_w_out_a, v_w_out_a),
        "w_kv": update("kv", w_kv, m_w_kv, v_w_kv),
        "w_in_b": update_layers("in_b", w_in_b, m_w_in_b, v_w_in_b),
        "w_out_b": update_layers("out_b", w_out_b, m_w_out_b, v_w_out_b),
    }

    small_full = {
        "mem_norm": d_mem_norm.reshape(-1),
        "norm_a": jnp.concatenate(d_norm_a, axis=0),
        "conv_w": jnp.stack(d_conv_w),
        "conv_b": jnp.concatenate(d_conv_b, axis=0),
        "w_rec_gate": jnp.stack(d_w_r),
        "b_rec_gate": jnp.concatenate(d_b_r, axis=0),
        "w_in_gate": jnp.stack(d_w_i),
        "b_in_gate": jnp.concatenate(d_b_i, axis=0),
        "lru_lambda": jnp.concatenate(d_lam, axis=0),
        "kv_norm": d_kv_norm.reshape(-1),
        "norm_b": jnp.concatenate(d_norm_b, axis=0),
        "final_norm": d_final_norm.reshape(-1),
    }
    small_names = list(small_full)
    (small_gathered,) = _all_gather([_pack([small_full[k] for k in small_names], 256)], "gather_small_grads")
    small_sum = _sum_devices(small_gathered, "sum_small_grads")
    small_grad = dict(zip(small_names, _unpack(small_sum, [small_full[k].shape for k in small_names])))
    small_w = {"mem_norm": (mem_norm, m_mem_norm, v_mem_norm), "norm_a": (norm_a, m_norm_a, v_norm_a),
               "conv_w": (conv_w, m_conv_w, v_conv_w), "conv_b": (conv_b, m_conv_b, v_conv_b),
               "w_rec_gate": (w_rec_gate, m_w_rec_gate, v_w_rec_gate),
               "b_rec_gate": (b_rec_gate, m_b_rec_gate, v_b_rec_gate),
               "w_in_gate": (w_in_gate, m_w_in_gate, v_w_in_gate), "b_in_gate": (b_in_gate, m_b_in_gate, v_b_in_gate),
               "lru_lambda": (lru_lambda, m_lru_lambda, v_lru_lambda), "kv_norm": (kv_norm, m_kv_norm, v_kv_norm),
               "norm_b": (norm_b, m_norm_b, v_norm_b), "final_norm": (final_norm, m_final_norm, v_final_norm)}
    for k in small_names:
        w = small_w[k][0]
        if small_grad[k].shape != w.shape:
            n = w.shape[-1]
            small_grad[k] = lax.dynamic_slice_in_dim(small_grad[k], me * n, n, axis=-1)
    small_shapes = [small_w[k][0].shape for k in small_names]
    packed = [_pack([small_grad[k] for k in small_names], 256)[None]]
    packed += [_pack([small_w[k][t] for k in small_names], 256) for t in range(3)]
    small_out = [_unpack(o, small_shapes) for o in _adamw(*packed, "adamw_small")]
    for idx, k in enumerate(small_names):
        upd[k] = [small_out[t][idx] for t in range(4)]

    order = ["mem_norm", "w_mem_kv", "norm_a", "w_in_a", "conv_w", "conv_b", "w_rec_gate", "b_rec_gate", "w_in_gate",
             "b_in_gate", "lru_lambda", "w_out_a", "kv_norm", "w_kv", "norm_b", "w_in_b", "w_out_b", "final_norm"]
    return (loss, grad_x, *[upd[k][0] for k in order], *[upd[k][1] for k in order],
            *[upd[k][2] for k in order], *[upd[k][3] for k in order])
```

```python
import functools

import jax
import jax.numpy as jnp
from jax import lax
from jax.experimental import pallas as pl
from jax.experimental.pallas import tpu as pltpu

F32 = jnp.float32
BF16 = jnp.bfloat16

N_DEV = 8
EPS = 1e-6
LRU_C = 8.0
HEAD = 128
CONV_WIDTH = 4
LANES = 128
SUB = 8
SB_EXHAUSTED = 110.0
VMEM_LIMIT = 56 * 1024 * 1024
RESIDENT_WEIGHT_BYTES = 8 * 1024 * 1024

ADAM_LR = 0.001
ADAM_B1 = 0.9
ADAM_B2 = 0.999
ADAM_EPS = 1e-08
ADAM_WD = 0.01
ADAM_STEP = 10

HBM_SPEC = pl.BlockSpec(memory_space=pltpu.HBM)


def _params(*semantics):
    return pltpu.CompilerParams(dimension_semantics=semantics, vmem_limit_bytes=VMEM_LIMIT)


def _tile(n, cap):
    if n <= cap:
        return n
    t = cap - cap % LANES
    while n % t:
        t -= LANES
    return t


def _row_tile(n, cap):
    if n <= cap:
        return n
    t = cap - cap % 8
    while t >= 8:
        if n % t == 0:
            return t
        t -= 8
    return n


def _rows_for(n, bytes_per_row, budget=2 * 1024 * 1024):
    return _row_tile(n, max(8, budget // bytes_per_row))


def _dot(a, b):
    return lax.dot_general(a, b, (((1,), (0,)), ((), ())), preferred_element_type=F32)


def _dot_nt(a, b):
    return lax.dot_general(a, b, (((1,), (1,)), ((), ())), preferred_element_type=F32)


def _dot_tn(a, b):
    return lax.dot_general(a, b, (((0,), (0,)), ((), ())), preferred_element_type=F32)


def _split_bf16(x):
    hi = x.astype(BF16)
    lo = (x - hi.astype(F32)).astype(BF16)
    return hi, lo


def _softplus(x):
    return jnp.maximum(x, 0.0) + jnp.log1p(jnp.exp(-jnp.abs(x)))


def _gate_sigmoid(x):
    return 0.5 * jnp.tanh(0.5 * x) + 0.5


def _silu_and_grad(g):
    sg = _gate_sigmoid(g)
    return g * sg, sg * (1.0 + g * (1.0 - sg))


def _one_minus_square(a, log_a):
    x = 2.0 * log_a
    series = -x * (1.0 + x * (0.5 + x * (1.0 / 6.0)))
    return jnp.where(x > -0.03, series, 1.0 - a * a)


def _rms_fwd(x, g, name):
    T, D = x.shape
    tt = _row_tile(T, 512)

    def body(x_ref, g_ref, o_ref):
        xv = x_ref[...]
        r = lax.rsqrt(jnp.mean(xv * xv, axis=-1, keepdims=True) + EPS)
        o_ref[...] = ((xv * r) * g_ref[...]).astype(o_ref.dtype)

    return pl.pallas_call(
        body, name=name, grid=(T // tt,),
        in_specs=[pl.BlockSpec((tt, D), lambda i: (i, 0)), pl.BlockSpec((1, D), lambda i: (0, 0))],
        out_specs=pl.BlockSpec((tt, D), lambda i: (i, 0)),
        out_shape=jax.ShapeDtypeStruct((T, D), BF16),
        compiler_params=_params("parallel"),
    )(x, g.reshape(1, D))


def _rms_bwd(du, h, g, res, name):
    T, D = h.shape
    tt = _row_tile(T, 256)

    def body(du_ref, h_ref, g_ref, res_ref, dh_ref, dg_ref):
        @pl.when(pl.program_id(0) == 0)
        def _():
            dg_ref[...] = jnp.zeros_like(dg_ref)

        hv = h_ref[...]
        duv = du_ref[...]
        r = lax.rsqrt(jnp.mean(hv * hv, axis=-1, keepdims=True) + EPS)
        xhat = hv * r
        dxhat = duv * g_ref[...]
        dh_ref[...] = res_ref[...] + r * (dxhat - xhat * jnp.mean(dxhat * xhat, axis=-1, keepdims=True))
        dg_ref[...] += jnp.sum(duv * xhat, axis=0, keepdims=True)

    row = pl.BlockSpec((tt, D), lambda i: (i, 0))
    vec = pl.BlockSpec((1, D), lambda i: (0, 0))
    return pl.pallas_call(
        body, name=name, grid=(T // tt,),
        in_specs=[row, row, vec, row],
        out_specs=[row, vec],
        out_shape=[jax.ShapeDtypeStruct((T, D), F32), jax.ShapeDtypeStruct((1, D), F32)],
        compiler_params=_params("arbitrary"),
    )(du, h, g.reshape(1, D), res)


def _mm_nn(a, b3, res=None, out_dtype=F32, name=None, rider=None):
    M, K = a.shape
    S, _, ns = b3.shape
    whole_b = S == 1 and K * ns * 2 <= RESIDENT_WEIGHT_BYTES
    tm = _row_tile(M, 512 if whole_b or a.dtype != BF16 else 1024)
    tn = ns if whole_b else _tile(ns, 512)
    nj = ns // tn

    def body(*refs):
        if res is None:
            a_ref, b_ref, o_ref = refs
        else:
            a_ref, b_ref, r_ref, o_ref = refs
        acc = _dot(a_ref[...].astype(BF16), b_ref[...])
        if res is not None:
            acc = acc + r_ref[...]
        o_ref[...] = acc.astype(o_ref.dtype)

    in_specs = [pl.BlockSpec((tm, K), lambda i, j: (i, 0)),
                pl.BlockSpec((None, K, tn), lambda i, j: (j // nj, 0, j % nj))]
    args = [a, b3]
    if res is not None:
        in_specs.append(pl.BlockSpec((tm, tn), lambda i, j: (i, j)))
        args.append(res)
    (out,), landed = _call_with_rider(
        body, rider, *_grid_ends((M // tm, S * nj)), name=name, grid=(M // tm, S * nj),
        in_specs=in_specs,
        out_specs=[pl.BlockSpec((tm, tn), lambda i, j: (i, j))],
        out_shape=[jax.ShapeDtypeStruct((M, S * ns), out_dtype)],
        scratch_shapes=[], semantics=("parallel", "parallel"), args=args)
    return out if rider is None else (out, landed)


def _grid_ends(grid):
    def first():
        return functools.reduce(jnp.logical_and, [pl.program_id(d) == 0 for d in range(len(grid))])

    def last():
        return functools.reduce(jnp.logical_and, [pl.program_id(d) == n - 1 for d, n in enumerate(grid)])

    return first, last


def _mm_nt(a, b3, out_dtype=F32, name=None, rider=None):
    M = a.shape[0]
    S, N, ns = b3.shape
    whole_b = S * N * ns * 2 <= RESIDENT_WEIGHT_BYTES
    tm = _row_tile(M, 512 if whole_b or a.dtype != BF16 else 1024)
    tn = N if whole_b else _tile(N, 512)

    def body(a_ref, b_ref, o_ref):
        acc = _dot_nt(a_ref[:, 0:ns].astype(BF16), b_ref[0])
        for s in range(1, S):
            acc = acc + _dot_nt(a_ref[:, s * ns:(s + 1) * ns].astype(BF16), b_ref[s])
        o_ref[...] = acc.astype(o_ref.dtype)

    (out,), landed = _call_with_rider(
        body, rider, *_grid_ends((M // tm, N // tn)), name=name, grid=(M // tm, N // tn),
        in_specs=[pl.BlockSpec((tm, S * ns), lambda i, j: (i, 0)),
                  pl.BlockSpec((S, tn, ns), lambda i, j: (0, j, 0))],
        out_specs=[pl.BlockSpec((tm, tn), lambda i, j: (i, j))],
        out_shape=[jax.ShapeDtypeStruct((M, N), out_dtype)],
        scratch_shapes=[], semantics=("parallel", "parallel"), args=(a, b3))
    return out if rider is None else (out, landed)


def _mm_tn(a, b, S, name=None, rider=None):
    T, K = a.shape
    ns = b.shape[1] // S
    tk = _tile(K, 1024)
    tn = _tile(ns, 512)
    nj = ns // tn
    tt = _row_tile(T, 2048)
    nt = T // tt

    def body(a_ref, b_ref, o_ref, acc_ref):
        t = pl.program_id(2)
        p = _dot_tn(a_ref[...].astype(BF16), b_ref[...].astype(BF16))

        @pl.when(t == 0)
        def _():
            acc_ref[...] = p

        @pl.when(t > 0)
        def _():
            acc_ref[...] += p

        @pl.when(t == nt - 1)
        def _():
            o_ref[...] = acc_ref[...].astype(o_ref.dtype)

    grid = (K // tk, S * nj, nt)
    (out,), landed = _call_with_rider(
        body, rider, *_grid_ends(grid), name=name, grid=grid,
        in_specs=[pl.BlockSpec((tt, tk), lambda i, j, t: (t, i)),
                  pl.BlockSpec((tt, tn), lambda i, j, t: (t, j))],
        out_specs=[pl.BlockSpec((None, tk, tn), lambda i, j, t: (j // nj, i, j % nj))],
        out_shape=[jax.ShapeDtypeStruct((S, K, ns), BF16)],
        scratch_shapes=[pltpu.VMEM((tk, tn), F32)],
        semantics=("parallel", "parallel", "arbitrary"), args=(a, b))
    return out if rider is None else (out, landed)


def _lru_coeffs(lam_row, r):
    cl = -LRU_C * _softplus(-lam_row)
    log_a = cl * r
    a = jnp.exp(log_a)
    em = _one_minus_square(a, log_a)
    return cl, a, em, jnp.sqrt(em)


def _rglru_fwd(proj, conv_w, conv_b, w_r, b_r, w_i, b_i, lam, name, rider=None):
    T = proj.shape[0]
    DR = conv_w.shape[1]
    NB = DR // HEAD
    tt = _row_tile(T, 256)
    PAD = 8

    def body(x_ref, g_ref, cw_ref, cb_ref, wr_ref, br_ref, wi_ref, bi_ref, lam_ref,
             y_ref, xc_ref, r_ref, i_ref, h_ref, xbuf, hcar):
        @pl.when(pl.program_id(0) == 0)
        def _():
            xbuf[0:PAD, :] = jnp.zeros((PAD, DR), F32)
            hcar[...] = jnp.zeros_like(hcar)

        xbuf[PAD:PAD + tt, :] = x_ref[...]
        xc = cb_ref[...] + cw_ref[0:1, :] * xbuf[pl.ds(PAD - 3, tt), :]
        for k in range(1, CONV_WIDTH):
            xc = xc + cw_ref[k:k + 1, :] * xbuf[pl.ds(PAD - 3 + k, tt), :]
        xbuf[0:PAD, :] = xbuf[tt:tt + PAD, :]
        xc_ref[...] = xc
        xcb = xc.astype(BF16)
        for n in range(NB):
            sl = slice(n * HEAD, (n + 1) * HEAD)
            r_ref[:, sl] = jax.nn.sigmoid(_dot(xcb[:, sl], wr_ref[n]) + br_ref[:, sl])
            i_ref[:, sl] = _gate_sigmoid(_dot(xcb[:, sl], wi_ref[n]) + bi_ref[:, sl])
        _, a, _, mult = _lru_coeffs(lam_ref[...], r_ref[...])
        hs = mult * (i_ref[...] * xc)
        groups = tt // SUB
        a = a.reshape(groups, SUB, DR)
        hs = hs.reshape(groups, SUB, DR)
        sub = lax.broadcasted_iota(jnp.int32, (groups, SUB, DR), 1)
        d = 1
        while d < SUB:
            keep = sub >= d
            a_sh = jnp.where(keep, pltpu.roll(a, d, 1), 1.0)
            h_sh = jnp.where(keep, pltpu.roll(hs, d, 1), 0.0)
            hs = a * h_sh + hs
            a = a * a_sh
            d *= 2
        carry = hcar[...]
        for n in range(groups):
            h_ref[n * SUB:(n + 1) * SUB, :] = hs[n] + a[n] * carry
            carry = h_ref[(n + 1) * SUB - 1:(n + 1) * SUB, :]
        hcar[...] = carry
        h = h_ref[...]
        g = g_ref[...]
        y_ref[...] = (h * (g * _gate_sigmoid(g))).astype(y_ref.dtype)

    col = lambda j: pl.BlockSpec((tt, DR), lambda c: (c, j))
    vec = pl.BlockSpec((1, DR), lambda c: (0, 0))
    gate = pl.BlockSpec((NB, HEAD, HEAD), lambda c: (0, 0, 0))
    f32_out = jax.ShapeDtypeStruct((T, DR), F32)
    nc = T // tt
    return _call_with_rider(
        body, rider, lambda: pl.program_id(0) == 0, lambda: pl.program_id(0) == nc - 1,
        name=name, grid=(nc,),
        in_specs=[col(0), col(1), pl.BlockSpec((CONV_WIDTH, DR), lambda c: (0, 0)), vec, gate, vec, gate, vec, vec],
        out_specs=[col(0)] * 5,
        out_shape=[jax.ShapeDtypeStruct((T, DR), BF16), f32_out, f32_out, f32_out, f32_out],
        scratch_shapes=[pltpu.VMEM((tt + PAD, DR), F32), pltpu.VMEM((1, DR), F32)],
        semantics=("arbitrary",),
        args=(proj, proj, conv_w, conv_b.reshape(1, DR), w_r, b_r.reshape(1, DR), w_i, b_i.reshape(1, DR),
              lam.reshape(1, DR)))


def _rglru_bwd(dy, proj, xc, r, i, h, conv_w, w_r, w_i, lam, name, rider=None):
    T = proj.shape[0]
    DR = conv_w.shape[1]
    NB = DR // HEAD
    tt = _row_tile(T, 128)
    nc = T // tt
    PAD = 8
    per = tt // PAD

    def body(dy_ref, x_ref, g_ref, xc_ref, r_ref, i_ref, h_ref, xprev_ref, hprev_ref,
             cw_ref, wr_ref, wi_ref, lam_ref,
             dxg_ref, dcw_ref, dcb_ref, dwr_ref, dbr_ref, dwi_ref, dbi_ref, dlam_ref,
             xbuf, dxcbuf, gcar, acar):
        step = pl.program_id(0)
        chunk = nc - 1 - step

        @pl.when(step == 0)
        def _():
            for ref in (dcw_ref, dcb_ref, dwr_ref, dbr_ref, dwi_ref, dbi_ref, dlam_ref, gcar, acar):
                ref[...] = jnp.zeros_like(ref)
            dxcbuf[tt:tt + PAD, :] = jnp.zeros((PAD, DR), F32)

        not_first = (chunk > 0).astype(F32)
        row = lax.broadcasted_iota(jnp.int32, (tt, DR), 0)
        silu, dsilu = _silu_and_grad(g_ref[...])
        dyv = dy_ref[...]
        hv = h_ref[...]
        dxg_ref[:, DR:2 * DR] = (dyv * hv * dsilu).astype(dxg_ref.dtype)
        dh = dyv * silu
        rv = r_ref[...]
        iv = i_ref[...]
        xcv = xc_ref[...]
        lam_row = lam_ref[...]
        cl, a, em, mult = _lru_coeffs(lam_row, rv)
        b = jnp.where(row == tt - 1, acar[...], pltpu.roll(a, tt - 1, 0))
        gs = dh
        d = 1
        while d < tt:
            keep = row < tt - d
            b_sh = jnp.where(keep, pltpu.roll(b, tt - d, 0), 1.0)
            g_sh = jnp.where(keep, pltpu.roll(gs, tt - d, 0), 0.0)
            gs = gs + b * g_sh
            b = b * b_sh
            d *= 2
        gt = gs + b * gcar[...]
        xbuf[0:tt, :] = gt
        gcar[...] = xbuf[0:1, :]
        acar[...] = _lru_coeffs(lam_row, r_ref[0:1, :])[1]
        h_before = hprev_ref[PAD - 1:PAD, :] * not_first
        hprev = jnp.where(row == 0, h_before, pltpu.roll(hv, 1, 0))
        da = gt * hprev
        dmult = gt * (iv * xcv)
        di = gt * mult * xcv
        dxc = gt * mult * iv
        dlog_a = da * a - dmult * (1.0 - em) / mult
        dr = dlog_a * cl
        dlam_ref[...] += jnp.sum(dlog_a * rv, axis=0, keepdims=True) * (LRU_C * jax.nn.sigmoid(-lam_row))
        drp = dr * rv * (1.0 - rv)
        dip = di * iv * (1.0 - iv)
        dbr_ref[...] += jnp.sum(drp, axis=0, keepdims=True)
        dbi_ref[...] += jnp.sum(dip, axis=0, keepdims=True)
        drpb = drp.astype(BF16)
        dipb = dip.astype(BF16)
        xcb = xcv.astype(BF16)
        for n in range(NB):
            sl = slice(n * HEAD, (n + 1) * HEAD)
            dxcbuf[0:tt, sl] = dxc[:, sl] + _dot_nt(drpb[:, sl], wr_ref[n]) + _dot_nt(dipb[:, sl], wi_ref[n])
            dwr_ref[n] += _dot_tn(xcb[:, sl], drpb[:, sl])
            dwi_ref[n] += _dot_tn(xcb[:, sl], dipb[:, sl])
        dxc_all = dxcbuf[0:tt, :]
        dcb_ref[...] += jnp.sum(dxc_all, axis=0, keepdims=True)
        xbuf[0:PAD, :] = xprev_ref[...] * not_first
        xbuf[PAD:PAD + tt, :] = x_ref[...]
        dx = cw_ref[0:1, :] * dxcbuf[pl.ds(3, tt), :]
        for k in range(1, CONV_WIDTH):
            dx = dx + cw_ref[k:k + 1, :] * dxcbuf[pl.ds(3 - k, tt), :]
        dxg_ref[:, 0:DR] = dx.astype(dxg_ref.dtype)
        for k in range(CONV_WIDTH):
            dcw_ref[k:k + 1, :] += jnp.sum(xbuf[pl.ds(PAD - 3 + k, tt), :] * dxc_all, axis=0, keepdims=True)
        dxcbuf[tt:tt + PAD, :] = dxcbuf[0:PAD, :]

    rev = lambda j: pl.BlockSpec((tt, DR), lambda s: (nc - 1 - s, j))
    prev = pl.BlockSpec((PAD, DR), lambda s: (jnp.maximum((nc - 1 - s) * per - 1, 0), 0))
    vec = pl.BlockSpec((1, DR), lambda s: (0, 0))
    gate = pl.BlockSpec((NB, HEAD, HEAD), lambda s: (0, 0, 0))
    taps = pl.BlockSpec((CONV_WIDTH, DR), lambda s: (0, 0))
    vec_out = jax.ShapeDtypeStruct((1, DR), F32)
    gate_out = jax.ShapeDtypeStruct((NB, HEAD, HEAD), F32)
    return _call_with_rider(
        body, rider, lambda: pl.program_id(0) == 0, lambda: pl.program_id(0) == nc - 1,
        name=name, grid=(nc,),
        in_specs=[rev(0), rev(0), rev(1), rev(0), rev(0), rev(0), rev(0), prev, prev, taps, gate, gate, vec],
        out_specs=[pl.BlockSpec((tt, 2 * DR), lambda s: (nc - 1 - s, 0)), taps, vec, gate, vec, gate, vec, vec],
        out_shape=[jax.ShapeDtypeStruct((T, 2 * DR), BF16), jax.ShapeDtypeStruct((CONV_WIDTH, DR), F32),
                   vec_out, gate_out, vec_out, gate_out, vec_out, vec_out],
        scratch_shapes=[pltpu.VMEM((tt + PAD, DR), F32), pltpu.VMEM((tt + PAD, DR), F32),
                        pltpu.VMEM((1, DR), F32), pltpu.VMEM((1, DR), F32)],
        semantics=("arbitrary",),
        args=(dy, proj, proj, xc, r, i, h, proj, h, conv_w, w_r, w_i, lam.reshape(1, DR)))


def _mem_probs(q, k, scale):
    s = _dot_nt(q, k) * scale
    p = jnp.exp(s - jnp.max(s, axis=-1, keepdims=True))
    return p * (1.0 / jnp.sum(p, axis=-1, keepdims=True))


def _memattn_fwd(proj, mkv, DR, DM, name):
    T = proj.shape[0]
    M = mkv.shape[0]
    NH = DM // HEAD
    tt = _row_tile(T, 512)
    qcol = 2 * DR // DM
    scale = HEAD ** -0.5

    def body(q_ref, g_ref, k_ref, v_ref, y_ref):
        for n in range(NH):
            sl = slice(n * HEAD, (n + 1) * HEAD)
            p = _mem_probs(q_ref[:, sl].astype(BF16), k_ref[:, sl], scale)
            o = _dot(p.astype(BF16), v_ref[:, sl])
            g = g_ref[:, sl]
            y_ref[:, sl] = (o * (g * _gate_sigmoid(g))).astype(y_ref.dtype)

    return pl.pallas_call(
        body, name=name, grid=(T // tt,),
        in_specs=[pl.BlockSpec((tt, DM), lambda t: (t, qcol)), pl.BlockSpec((tt, DM), lambda t: (t, qcol + 1)),
                  pl.BlockSpec((M, DM), lambda t: (0, 0)), pl.BlockSpec((M, DM), lambda t: (0, 1))],
        out_specs=pl.BlockSpec((tt, DM), lambda t: (t, 0)),
        out_shape=jax.ShapeDtypeStruct((T, DM), BF16),
        compiler_params=_params("parallel"),
    )(proj, proj, mkv, mkv)


def _memattn_bwd(dy, proj, mkv, DR, DM, name):
    T = proj.shape[0]
    M = mkv.shape[0]
    NH = DM // HEAD
    tt = _row_tile(T, 512)
    qcol = 2 * DR // DM
    scale = HEAD ** -0.5

    def body(dy_ref, q_ref, g_ref, k_ref, v_ref, dqg_ref, dkv_ref):
        @pl.when(pl.program_id(0) == 0)
        def _():
            dkv_ref[...] = jnp.zeros_like(dkv_ref)

        for n in range(NH):
            sl = slice(n * HEAD, (n + 1) * HEAD)
            qb = q_ref[:, sl].astype(BF16)
            kb = k_ref[:, sl]
            vb = v_ref[:, sl]
            p = _mem_probs(qb, kb, scale)
            pb = p.astype(BF16)
            o = _dot(pb, vb)
            silu, dsilu = _silu_and_grad(g_ref[:, sl])
            dyv = dy_ref[:, sl]
            dqg_ref[:, DM + n * HEAD:DM + (n + 1) * HEAD] = (dyv * o * dsilu).astype(dqg_ref.dtype)
            dob = (dyv * silu).astype(BF16)
            dp = _dot_nt(dob, vb)
            ds = (p * (dp - jnp.sum(dp * p, axis=-1, keepdims=True)) * scale).astype(BF16)
            dqg_ref[:, sl] = _dot(ds, kb).astype(dqg_ref.dtype)
            dkv_ref[:, sl] += _dot_tn(ds, qb)
            dkv_ref[:, DM + n * HEAD:DM + (n + 1) * HEAD] += _dot_tn(pb, dob)

    return pl.pallas_call(
        body, name=name, grid=(T // tt,),
        in_specs=[pl.BlockSpec((tt, DM), lambda t: (t, DR // DM)),
                  pl.BlockSpec((tt, DM), lambda t: (t, qcol)), pl.BlockSpec((tt, DM), lambda t: (t, qcol + 1)),
                  pl.BlockSpec((M, DM), lambda t: (0, 0)), pl.BlockSpec((M, DM), lambda t: (0, 1))],
        out_specs=[pl.BlockSpec((tt, 2 * DM), lambda t: (t, 0)), pl.BlockSpec((M, 2 * DM), lambda t: (0, 0))],
        out_shape=[jax.ShapeDtypeStruct((T, 2 * DM), BF16), jax.ShapeDtypeStruct((M, 2 * DM), F32)],
        compiler_params=_params("arbitrary"),
    )(dy, proj, proj, mkv, mkv)


def _sb_blocks(T):
    tk = _row_tile(T // 2, 256)
    tq = 2 * tk
    assert T % tq == 0
    return tq, tk


def _sb_upper(tk):
    row = lax.broadcasted_iota(jnp.int32, (tk, tk), 0)
    col = lax.broadcasted_iota(jnp.int32, (tk, tk), 1)
    return (row > col).astype(BF16)


def _sb_causal(tq, tk, d):
    row = lax.broadcasted_iota(jnp.int32, (tq, tk), 0)
    col = lax.broadcasted_iota(jnp.int32, (tq, tk), 1)
    return col + d * tk < row


def _sb_own_span(tile, carry, i, tq, tk):
    span = tq // tk
    for d in reversed(range(span)):
        lo = d * tk
        part = tile(span * i + d, tuple(c[lo:] for c in carry), _sb_causal(tq - lo, tk, 0), slice(lo, tq))
        carry = tuple(p if lo == 0 else jnp.concatenate([c[:lo], p], axis=0) for c, p in zip(carry, part))
    return carry


def _later_sum(x, upper):
    n = x.shape[0]
    hi, lo = _split_bf16(x)
    both = _dot(jnp.concatenate([hi, lo], axis=0), upper)
    return both[0:n] + both[n:2 * n]


def _sb_alive(gone_c):
    return (jnp.min(gone_c) < SB_EXHAUSTED).astype(jnp.int32)


def _sb_weights(q, kb, gone_c, causal, upper, scale):
    return _sb_weights_of(_dot_nt(q, kb) * scale, gone_c, causal, upper)


def _sb_weights_of(z, gone_c, causal, upper):
    sp = jnp.where(z > 20.0, z, jnp.log(1.0 + jnp.exp(z)))
    spm = sp if causal is None else jnp.where(causal, sp, 0.0)
    gone = _later_sum(spm, upper) + gone_c
    w = jnp.exp(z - sp - gone)
    if causal is not None:
        w = jnp.where(causal, w, 0.0)
    return z, sp, spm, w


def _sb_fwd(proj, kv, DR, name, rider=None):
    T = proj.shape[0]
    NH = DR // HEAD
    tq, tk = _sb_blocks(T)
    span = tq // tk
    scale = HEAD ** -0.5

    def body(q_ref, g_ref, k_ref, v_ref, y_ref, o_ref):
        i = pl.program_id(1)
        q = q_ref[...].astype(BF16)
        upper = _sb_upper(tk)

        def tile(jb, carry, causal, rows=slice(0, tq)):
            gone_c, acc = carry
            n = rows.stop - rows.start
            start = pl.multiple_of(jb * tk, tk)
            kb = k_ref[pl.ds(start, tk), :]
            vb = v_ref[pl.ds(start, tk), :]
            _, _, spm, w = _sb_weights(q[rows], kb, gone_c, causal, upper, scale)
            hi, lo = _split_bf16(w)
            pv = _dot(jnp.concatenate([hi, lo], axis=0), vb)
            return gone_c + jnp.sum(spm, axis=-1, keepdims=True), acc + pv[0:n] + pv[n:2 * n]

        carry = _sb_own_span(tile, (jnp.zeros((tq, 1), F32), jnp.zeros((tq, HEAD), F32)), i, tq, tk)

        def more(c):
            return jnp.logical_and(c[0] >= 0, c[1] > 0)

        def step(c):
            jb, _, gone_c, acc = c
            gone_c, acc = tile(jb, (gone_c, acc), None)
            return jb - 1, _sb_alive(gone_c), gone_c, acc

        _, _, _, acc = lax.while_loop(more, step, (span * i - 1, _sb_alive(carry[0]), *carry))
        o_ref[...] = acc
        g = g_ref[...]
        y_ref[...] = (acc * (g * _gate_sigmoid(g))).astype(y_ref.dtype)

    blk = lambda off: pl.BlockSpec((tq, HEAD), lambda h, i: (i, off + h))
    whole = lambda off: pl.BlockSpec((T, HEAD), lambda h, i: (0, off + h))
    nq = T // tq
    return _call_with_rider(
        body, rider, lambda: (pl.program_id(0) == 0) & (pl.program_id(1) == 0),
        lambda: (pl.program_id(0) == NH - 1) & (pl.program_id(1) == nq - 1),
        name=name, grid=(NH, nq),
        in_specs=[blk(0), blk(NH), whole(0), whole(NH)],
        out_specs=[blk(0), blk(0)],
        out_shape=[jax.ShapeDtypeStruct((T, DR), BF16), jax.ShapeDtypeStruct((T, DR), F32)],
        scratch_shapes=[], semantics=("parallel", "arbitrary"),
        args=(proj, proj, kv, kv))


def _sb_bwd(dy, proj, kv, o, DR, name, rider=None):
    T = proj.shape[0]
    NH = DR // HEAD
    tq, tk = _sb_blocks(T)
    span = tq // tk
    scale = HEAD ** -0.5

    def body(dy_ref, q_ref, g_ref, k_ref, v_ref, o_ref, dq_ref, dg_ref, dk_ref, dv_ref):
        i = pl.program_id(1)

        @pl.when(i == 0)
        def _():
            dk_ref[...] = jnp.zeros_like(dk_ref)
            dv_ref[...] = jnp.zeros_like(dv_ref)

        qf = q_ref[...]
        q = qf.astype(BF16)
        q_t = qf.T.astype(BF16)
        upper = _sb_upper(tk)
        silu, dsilu = _silu_and_grad(g_ref[...])
        dyv = dy_ref[...]
        ov = o_ref[...]
        dg_ref[...] = (dyv * ov * dsilu).astype(dg_ref.dtype)
        do = dyv * silu
        dob = do.astype(BF16)
        do_t = do.T.astype(BF16)
        total = jnp.sum(dob.astype(F32) * ov, axis=-1, keepdims=True)

        def tile(jb, carry, causal, rows=slice(0, tq)):
            gone_c, e_after, dq = carry
            start = pl.multiple_of(jb * tk, tk)
            kb = k_ref[pl.ds(start, tk), :]
            vb = v_ref[pl.ds(start, tk), :]
            z, sp, spm, w = _sb_weights(q[rows], kb, gone_c, causal, upper, scale)
            e = _dot_nt(dob[rows], vb) * w
            dz = e - (total[rows] - (_later_sum(e, upper) + e_after)) * jnp.exp(z - sp)
            if causal is not None:
                dz = jnp.where(causal, dz, 0.0)
            dzb = dz.astype(BF16)
            dk_ref[jb] += _dot(q_t[:, rows], dzb) * scale
            dv_ref[jb] += _dot(do_t[:, rows], w.astype(BF16))
            return (gone_c + jnp.sum(spm, axis=-1, keepdims=True), e_after + jnp.sum(e, axis=-1, keepdims=True),
                    dq + _dot(dzb, kb))

        carry = _sb_own_span(
            tile, (jnp.zeros((tq, 1), F32), jnp.zeros((tq, 1), F32), jnp.zeros((tq, HEAD), F32)), i, tq, tk)

        def more(c):
            return jnp.logical_and(c[0] >= 0, c[1] > 0)

        def step(c):
            jb, _, *rest = c
            rest = tile(jb, tuple(rest), None)
            return jb - 1, _sb_alive(rest[0]), *rest

        *_, dq = lax.while_loop(more, step, (span * i - 1, _sb_alive(carry[0]), *carry))
        dq_ref[...] = (dq * scale).astype(dq_ref.dtype)

    blk = lambda off: pl.BlockSpec((tq, HEAD), lambda h, i: (i, off + h))
    whole = lambda off: pl.BlockSpec((T, HEAD), lambda h, i: (0, off + h))
    keys_t = pl.BlockSpec((None, T // tk, HEAD, tk), lambda h, i: (h, 0, 0, 0))
    keys_t_shape = jax.ShapeDtypeStruct((NH, T // tk, HEAD, tk), F32)
    nq = T // tq
    return _call_with_rider(
        body, rider, lambda: (pl.program_id(0) == 0) & (pl.program_id(1) == 0),
        lambda: (pl.program_id(0) == NH - 1) & (pl.program_id(1) == nq - 1),
        name=name, grid=(NH, nq),
        in_specs=[blk(0), blk(0), blk(NH), whole(0), whole(NH), blk(0)],
        out_specs=[blk(0), blk(0), keys_t, keys_t],
        out_shape=[jax.ShapeDtypeStruct((T, DR), BF16), jax.ShapeDtypeStruct((T, DR), BF16),
                   keys_t_shape, keys_t_shape],
        scratch_shapes=[], semantics=("parallel", "arbitrary"),
        args=(dy, proj, proj, kv, kv, o))


def _merge_dkv(parts, name):
    NH, nblk, _, tk = parts[0][0].shape
    DR = NH * HEAD
    n = len(parts)

    def body(*refs):
        o_ref = refs[-1]
        for which in range(2):
            for h in range(NH):
                acc = refs[which][h]
                for p in range(1, n):
                    acc = acc + refs[2 * p + which][h]
                col = which * DR + h * HEAD
                o_ref[:, col:col + HEAD] = acc.T.astype(o_ref.dtype)

    blk = pl.BlockSpec((NH, None, HEAD, tk), lambda t: (0, t, 0, 0))
    return pl.pallas_call(
        body, name=name, grid=(nblk,),
        in_specs=[blk] * (2 * n),
        out_specs=pl.BlockSpec((tk, 2 * DR), lambda t: (t, 0)),
        out_shape=jax.ShapeDtypeStruct((nblk * tk, 2 * DR), BF16),
        compiler_params=_params("parallel"),
    )(*[a for pair in parts for a in pair])


def _final_loss(h, g, target, name):
    T, D = h.shape
    tt = _row_tile(T, 256)

    def body(h_ref, g_ref, t_ref, dh_ref, dg_ref, sq_ref):
        @pl.when(pl.program_id(0) == 0)
        def _():
            dg_ref[...] = jnp.zeros_like(dg_ref)
            sq_ref[...] = jnp.zeros_like(sq_ref)

        hv = h_ref[...]
        gv = g_ref[...]
        r = lax.rsqrt(jnp.mean(hv * hv, axis=-1, keepdims=True) + EPS)
        xhat = hv * r
        err = xhat * gv - t_ref[...]
        sq_ref[...] += jnp.sum(err * err, axis=0, keepdims=True)
        dy = err * (1.0 / D)
        dxhat = dy * gv
        dh_ref[...] = r * (dxhat - xhat * jnp.mean(dxhat * xhat, axis=-1, keepdims=True))
        dg_ref[...] += jnp.sum(dy * xhat, axis=0, keepdims=True)

    row = pl.BlockSpec((tt, D), lambda i: (i, 0))
    vec = pl.BlockSpec((1, D), lambda i: (0, 0))
    return pl.pallas_call(
        body, name=name, grid=(T // tt,),
        in_specs=[row, vec, row],
        out_specs=[row, vec, vec],
        out_shape=[jax.ShapeDtypeStruct((T, D), F32), jax.ShapeDtypeStruct((1, D), F32),
                   jax.ShapeDtypeStruct((1, D), F32)],
        compiler_params=_params("arbitrary"),
    )(h, g.reshape(1, D), target)


def _sum_parts(parts_ref):
    g = parts_ref[0].astype(F32)
    for s in range(1, parts_ref.shape[0]):
        g = g + parts_ref[s].astype(F32)
    return g


def _adamw(parts, w, m, v, name):
    P, R, C = parts.shape
    tr = _rows_for(R, P * C * 4)

    def body(p_ref, w_ref, m_ref, v_ref, g_ref, d_ref, nm_ref, nv_ref):
        g = _sum_parts(p_ref)
        nm = ADAM_B1 * m_ref[...] + (1.0 - ADAM_B1) * g
        nv = ADAM_B2 * v_ref[...] + (1.0 - ADAM_B2) * jnp.square(g)
        m_hat = nm / (1.0 - ADAM_B1 ** ADAM_STEP)
        v_hat = nv / (1.0 - ADAM_B2 ** ADAM_STEP)
        g_ref[...] = g
        d_ref[...] = -ADAM_LR * (m_hat / (jnp.sqrt(v_hat) + ADAM_EPS) + ADAM_WD * w_ref[...])
        nm_ref[...] = nm
        nv_ref[...] = nv

    row = pl.BlockSpec((tr, C), lambda i: (i, 0))
    out = jax.ShapeDtypeStruct((R, C), F32)
    return pl.pallas_call(
        body, name=name, grid=(R // tr,),
        in_specs=[pl.BlockSpec((P, tr, C), lambda i: (0, i, 0)), row, row, row],
        out_specs=[row] * 4,
        out_shape=[out] * 4,
        compiler_params=_params("parallel"),
    )(parts, w, m, v)


def _sum_devices(parts, name):
    P, R, C = parts.shape
    tr = _rows_for(R, P * C * 4)

    def body(p_ref, o_ref):
        o_ref[...] = _sum_parts(p_ref)

    return pl.pallas_call(
        body, name=name, grid=(R // tr,),
        in_specs=[pl.BlockSpec((P, tr, C), lambda i: (0, i, 0))],
        out_specs=pl.BlockSpec((tr, C), lambda i: (i, 0)),
        out_shape=jax.ShapeDtypeStruct((R, C), F32),
        compiler_params=_params("parallel"),
    )(parts)


def _mesh_position():
    return lax.axis_index("x"), lax.axis_index("y"), lax.axis_index("c")


def _device_index(p):
    return 4 * p[0] + 2 * p[1] + p[2]


def _all_gather(arrs, name):
    n = len(arrs)

    def body(*refs):
        ins, outs = refs[:n], refs[n:2 * n]
        send_sems, recv_sems, local_sems = refs[2 * n:]
        x, y, c = _mesh_position()
        me, sibling = (x, y, c), (x, y, 1 - c)
        chips = [(1 - x, y), (x, 1 - y), (1 - x, 1 - y)]

        def slot(a, p):
            return outs[a].at[_device_index(p)]

        def copy(a, k, block, to, src=None):
            return pltpu.make_async_remote_copy(
                src_ref=slot(a, block) if src is None else src, dst_ref=slot(a, block),
                send_sem=send_sems.at[a, k], recv_sem=recv_sems.at[a, k],
                device_id=to, device_id_type=pl.DeviceIdType.MESH)

        mine = [pltpu.make_async_copy(ins[a], slot(a, me), local_sems.at[a]) for a in range(n)]
        for cp in mine:
            cp.start()
        first = []
        for a in range(n):
            first.append(copy(a, 0, me, sibling, src=ins[a]))
            first += [copy(a, 1 + j, me, (*chip, c), src=ins[a]) for j, chip in enumerate(chips)]
        for cp in first:
            cp.start()
        passed = []
        for a in range(n):
            for j, chip in enumerate(chips):
                copy(a, 1 + j, (*chip, c), me).wait_recv()
                fwd = copy(a, 4 + j, (*chip, c), sibling)
                fwd.start()
                passed.append(fwd)
        for a in range(n):
            copy(a, 0, sibling, me).wait_recv()
            for j, chip in enumerate(chips):
                copy(a, 4 + j, (*chip, 1 - c), me).wait_recv()
        for cp in first + passed:
            cp.wait_send()
        for cp in mine:
            cp.wait()

    return pl.pallas_call(
        body, name=name,
        in_specs=[HBM_SPEC] * n, out_specs=[HBM_SPEC] * n,
        out_shape=[jax.ShapeDtypeStruct((N_DEV,) + a.shape, a.dtype) for a in arrs],
        scratch_shapes=[pltpu.SemaphoreType.DMA((n, 7)), pltpu.SemaphoreType.DMA((n, 7)),
                        pltpu.SemaphoreType.DMA((n,))],
    )(*arrs)


class _Exchange:
    def __init__(self, arrs, gather):
        self.arrs = list(arrs)
        self.gather = gather
        self.n = len(self.arrs)

    def out_shape(self):
        lead = (N_DEV,) if self.gather else ()
        return [jax.ShapeDtypeStruct(lead + a.shape, a.dtype) for a in self.arrs]

    def scratch(self):
        return [pltpu.SemaphoreType.DMA((self.n, 7)), pltpu.SemaphoreType.DMA((self.n, 7)),
                pltpu.SemaphoreType.DMA((self.n,))]

    def _copies(self, ins, outs, sems, arrivals):
        send_sems, recv_sems, local_sems = sems
        x, y, c = _mesh_position()
        me = _device_index((x, y, c))
        peers = [(1 - x if k & 4 else x, 1 - y if k & 2 else y, 1 - c if k & 1 else c) for k in range(1, N_DEV)]
        local, sends, recvs = [], [], []
        for a in range(self.n):
            mine = ins[a] if self.gather else ins[a].at[me]
            local.append(pltpu.make_async_copy(mine, outs[a].at[me], local_sems.at[a]))
            for k, peer in enumerate(peers):
                there = _device_index(peer)
                src = ins[a] if self.gather else ins[a].at[there]
                sem = dict(send_sem=send_sems.at[a, k], recv_sem=recv_sems.at[a, k],
                           device_id=peer, device_id_type=pl.DeviceIdType.MESH)
                sends.append(pltpu.make_async_remote_copy(src_ref=src, dst_ref=outs[a].at[me], **sem))
                if arrivals:
                    recvs.append(pltpu.make_async_remote_copy(src_ref=src, dst_ref=outs[a].at[there], **sem))
        return local, sends, recvs

    def start(self, ins, outs, sems):
        local, sends, _ = self._copies(ins, outs, sems, arrivals=False)
        for cp in local + sends:
            cp.start()

    def wait(self, ins, outs, sems):
        local, sends, recvs = self._copies(ins, outs, sems, arrivals=True)
        for cp in recvs:
            cp.wait_recv()
        for cp in sends:
            cp.wait_send()
        for cp in local:
            cp.wait()


def _all_to_all(arrs, name):
    ex = _Exchange(arrs, gather=False)
    n = ex.n

    def body(*refs):
        ins, outs, sems = refs[:n], refs[n:2 * n], refs[2 * n:]
        ex.start(ins, outs, sems)
        ex.wait(ins, outs, sems)

    return pl.pallas_call(
        body, name=name,
        in_specs=[HBM_SPEC] * n, out_specs=[HBM_SPEC] * n,
        out_shape=ex.out_shape(), scratch_shapes=ex.scratch(),
    )(*arrs)


def _call_with_rider(body, rider, first, last, *, name, grid, in_specs, out_specs, out_shape, scratch_shapes,
                     semantics, args):
    if rider is None:
        out = pl.pallas_call(body, name=name, grid=grid, in_specs=in_specs, out_specs=out_specs,
                             out_shape=out_shape, scratch_shapes=scratch_shapes,
                             compiler_params=_params(*semantics))(*args)
        return out, None
    n, n_in, n_out = rider.n, len(in_specs), len(out_specs)

    def riding(*refs):
        ins, r_in = refs[:n_in], refs[n_in:n_in + n]
        outs, r_out = refs[n_in + n:n_in + n + n_out], refs[n_in + n + n_out:n_in + 2 * n + n_out]
        scratch, sems = refs[n_in + 2 * n + n_out:-3], refs[-3:]

        @pl.when(first())
        def _():
            rider.start(r_in, r_out, sems)

        body(*ins, *outs, *scratch)

        @pl.when(last())
        def _():
            rider.wait(r_in, r_out, sems)

    out = pl.pallas_call(
        riding, name=name, grid=grid,
        in_specs=list(in_specs) + [HBM_SPEC] * n, out_specs=list(out_specs) + [HBM_SPEC] * n,
        out_shape=list(out_shape) + rider.out_shape(),
        scratch_shapes=list(scratch_shapes) + rider.scratch(),
        compiler_params=_params(*["arbitrary"] * len(grid)),
    )(*args, *rider.arrs)
    return out[:n_out], out[n_out:]


def _pack(arrs, row_multiple):
    parts = []
    rows = 0
    for a in arrs:
        flat = a.reshape(-1).astype(F32)
        r = -(-flat.shape[0] // (8 * LANES)) * 8
        parts.append(jnp.pad(flat, (0, r * LANES - flat.shape[0])).reshape(r, LANES))
        rows += r
    pad = -rows % row_multiple
    if pad:
        parts.append(jnp.zeros((pad, LANES), F32))
    return jnp.concatenate(parts, axis=0)


def _unpack(buf, shapes, lead=()):
    out = []
    r0 = 0
    for shape in shapes:
        size = 1
        for s in shape:
            size *= s
        r = -(-size // (8 * LANES)) * 8
        part = buf[..., r0:r0 + r, :].reshape(lead + (r * LANES,))[..., :size]
        out.append(part.reshape(lead + tuple(shape)))
        r0 += r
    return out


def _gathered_cols(g):
    g = jnp.moveaxis(g, 0, -2)
    return g.reshape(g.shape[:-2] + (g.shape[-2] * g.shape[-1],))


def kernel(x, mem, mem_norm, w_mem_kv, norm_a, w_in_a, conv_w, conv_b, w_rec_gate, b_rec_gate, w_in_gate, b_in_gate, lru_lambda, w_out_a, kv_norm, w_kv, norm_b, w_in_b, w_out_b, final_norm, loss_target, m_mem_norm, m_w_mem_kv, m_norm_a, m_w_in_a, m_conv_w, m_conv_b, m_w_rec_gate, m_b_rec_gate, m_w_in_gate, m_b_in_gate, m_lru_lambda, m_w_out_a, m_kv_norm, m_w_kv, m_norm_b, m_w_in_b, m_w_out_b, m_final_norm, v_mem_norm, v_w_mem_kv, v_norm_a, v_w_in_a, v_conv_w, v_conv_b, v_w_rec_gate, v_b_rec_gate, v_w_in_gate, v_b_in_gate, v_lru_lambda, v_w_out_a, v_kv_norm, v_w_kv, v_norm_b, v_w_in_b, v_w_out_b, v_final_norm):
    xs = x[0]
    T, D = xs.shape
    L = w_mem_kv.shape[0]
    NA = w_in_a.shape[0]
    NB = w_in_b.shape[0]
    DM2 = w_mem_kv.shape[2]
    DM = DM2 // 2
    DR = w_rec_gate.shape[1] * w_rec_gate.shape[2]
    me = _device_index(_mesh_position())

    small_sharded = [norm_a, conv_w, conv_b, b_rec_gate, b_in_gate, lru_lambda]
    shard = {f"mem_kv{l}": w_mem_kv[l] for l in range(L)}
    shard.update({f"in_a{l}": w_in_a[l] for l in range(NA)}, **{f"out_a{l}": w_out_a[l] for l in range(NA)})
    shard.update({f"in_b{j}": w_in_b[j] for j in range(NB)}, **{f"out_b{j}": w_out_b[j] for j in range(NB)})
    shard["kv"] = w_kv
    shard = {k: w.astype(BF16) for k, w in shard.items()}
    now = ["in_a0", "mem_kv0"]
    gathered = _all_gather([shard[k] for k in now] + [_pack(small_sharded, 8)], "gather_params")
    full = dict(zip(now, gathered[:-1]))

    def gather_rider(keys):
        return _Exchange([shard[k] for k in keys], gather=True), keys

    def out_proj(key):
        return full[key].reshape(1, -1, D)

    def mem_proj(l):
        return full[f"mem_kv{l}"].reshape(1, D, DM2)

    norm_a_f, conv_w_f, conv_b_f, b_r_f, b_i_f, lam_f = [
        _gathered_cols(s) for s in _unpack(gathered[-1], [s.shape for s in small_sharded], lead=(N_DEV,))]
    w_r_bf = w_rec_gate.astype(BF16)
    w_i_bf = w_in_gate.astype(BF16)

    zeros_mem = jnp.zeros_like(mem[0])
    mem_n = _rms_fwd(mem[0], mem_norm, "rms_mem")
    mkv = [_mm_nn(mem_n, mem_proj(0), out_dtype=BF16, name="mm_mem_kv0")]

    h = xs
    saved_a = []
    for l in range(NA):
        u = _rms_fwd(h, norm_a_f[l], f"rms_a{l}")
        rider, keys = gather_rider([f"out_a{l}"])
        proj, landed = _mm_nn(u, full[f"in_a{l}"], name=f"mm_in_a{l}", rider=rider)
        full.update(zip(keys, landed))
        later_mem = [f"mem_kv{m}" for m in range(1, L)] if l == 0 else []
        rider, keys = gather_rider([f"in_a{l + 1}" if l + 1 < NA else "kv"] + later_mem)
        (y_rnn, xc, r, i, hr), landed = _rglru_fwd(proj, conv_w_f[l], conv_b_f[l], w_r_bf[l], b_r_f[l], w_i_bf[l],
                                                   b_i_f[l], lam_f[l], f"rglru_fwd{l}", rider)
        full.update(zip(keys, landed))
        if l == 0:
            mkv += [_mm_nn(mem_n, mem_proj(m), out_dtype=BF16, name=f"mm_mem_kv{m}") for m in range(1, L)]
        y_mem = _memattn_fwd(proj, mkv[l], DR, DM, f"memattn_fwd_a{l}")
        ycat = jnp.concatenate([y_rnn, y_mem], axis=-1)
        h_next = _mm_nn(ycat, out_proj(f"out_a{l}"), res=h, name=f"mm_out_a{l}")
        saved_a.append((h, u, proj, xc, r, i, hr, ycat))
        h = h_next
    h_kv = h
    u_kv = _rms_fwd(h_kv, kv_norm, "rms_kv")
    rider, keys = gather_rider(["in_b0"])
    kv, landed = _mm_nn(u_kv, full["kv"], out_dtype=BF16, name="mm_kv", rider=rider)
    full.update(zip(keys, landed))
    saved_b = []
    for j in range(NB):
        u = _rms_fwd(h, norm_b[j], f"rms_b{j}")
        proj = _mm_nn(u, full[f"in_b{j}"], name=f"mm_in_b{j}")
        rider, keys = gather_rider([f"out_b{j}"] + ([f"in_b{j + 1}"] if j + 1 < NB else []))
        (y_sb, o_sb), landed = _sb_fwd(proj, kv, DR, f"sb_fwd{j}", rider)
        full.update(zip(keys, landed))
        y_mem = _memattn_fwd(proj, mkv[NA + j], DR, DM, f"memattn_fwd_b{j}")
        ycat = jnp.concatenate([y_sb, y_mem], axis=-1)
        h_next = _mm_nn(ycat, out_proj(f"out_b{j}"), res=h, name=f"mm_out_b{j}")
        saved_b.append((h, u, proj, o_sb, ycat))
        h = h_next

    dh, d_final_norm, sq = _final_loss(h, final_norm, loss_target[0], "final_loss")
    loss = lax.psum(0.5 * jnp.sum(sq) / D, ("x", "y", "c"))

    big_grads = {}
    received = {}

    def scatter_rider(keys):
        return _Exchange([big_grads[k] for k in keys], gather=False), keys

    dmkv = [None] * L
    d_norm_b = [None] * NB
    dkv_parts = []
    for j in reversed(range(NB)):
        h_in, u, proj, o_sb, ycat = saved_b[j]
        dy = _mm_nt(dh, out_proj(f"out_b{j}"), name=f"mm_dy_b{j}")
        big_grads[f"out_b{j}"] = _mm_tn(ycat, dh, 1, name=f"mm_dw_out_b{j}").reshape(N_DEV, -1, D)
        rider, keys = scatter_rider(([f"in_b{j + 1}"] if j + 1 < NB else []) + [f"out_b{j}"])
        (dq, dg, dk, dv), landed = _sb_bwd(dy, proj, kv, o_sb, DR, f"sb_bwd{j}", rider)
        received.update(zip(keys, landed))
        dkv_parts.append((dk, dv))
        dqg_mem, dmkv[NA + j] = _memattn_bwd(dy, proj, mkv[NA + j], DR, DM, f"memattn_bwd_b{j}")
        dproj = jnp.concatenate([dq, dg, dqg_mem], axis=-1)
        du = _mm_nt(dproj, full[f"in_b{j}"], name=f"mm_du_b{j}")
        big_grads[f"in_b{j}"] = _mm_tn(u, dproj, N_DEV, name=f"mm_dw_in_b{j}")
        dh, d_norm_b[j] = _rms_bwd(du, h_in, norm_b[j], dh, f"rms_bwd_b{j}")

    dkv = _merge_dkv(dkv_parts, "merge_dkv")
    du_kv = _mm_nt(dkv, full["kv"], name="mm_du_kv")
    big_grads["kv"] = _mm_tn(u_kv, dkv, N_DEV, name="mm_dw_kv")
    dh, d_kv_norm = _rms_bwd(du_kv, h_kv, kv_norm, dh, "rms_bwd_kv")

    d_norm_a, d_conv_w, d_conv_b, d_w_r, d_b_r, d_w_i, d_b_i, d_lam = ([None] * NA for _ in range(8))
    for l in reversed(range(NA)):
        h_in, u, proj, xc, r, i, hr, ycat = saved_a[l]
        if l == NA - 1:
            rider, keys = scatter_rider(["kv"])
            dy, landed = _mm_nt(dh, out_proj(f"out_a{l}"), name=f"mm_dy_a{l}", rider=rider)
            received.update(zip(keys, landed))
        else:
            dy = _mm_nt(dh, out_proj(f"out_a{l}"), name=f"mm_dy_a{l}")
        big_grads[f"out_a{l}"] = _mm_tn(ycat, dh, 1, name=f"mm_dw_out_a{l}").reshape(N_DEV, -1, D)
        rider, keys = scatter_rider([f"in_a{l + 1}" if l + 1 < NA else "in_b0", f"out_a{l}"])
        (dxg, d_conv_w[l], d_conv_b[l], d_w_r[l], d_b_r[l], d_w_i[l], d_b_i[l], d_lam[l]), landed = _rglru_bwd(
            dy, proj, xc, r, i, hr, conv_w_f[l], w_r_bf[l], w_i_bf[l], lam_f[l], f"rglru_bwd{l}", rider)
        received.update(zip(keys, landed))
        dqg_mem, dmkv[l] = _memattn_bwd(dy, proj, mkv[l], DR, DM, f"memattn_bwd_a{l}")
        dproj = jnp.concatenate([dxg, dqg_mem], axis=-1)
        if l > 0:
            du = _mm_nt(dproj, full[f"in_a{l}"], name=f"mm_du_a{l}")
            big_grads[f"in_a{l}"] = _mm_tn(u, dproj, N_DEV, name=f"mm_dw_in_a{l}")
        else:
            dmkv_all = jnp.concatenate(dmkv, axis=-1)
            dmem_n = _mm_nt(dmkv_all, jnp.concatenate([mem_proj(m) for m in range(L)], axis=0), name="mm_dmem")
            for m in range(L):
                big_grads[f"mem_kv{m}"] = _mm_tn(mem_n, dmkv[m], 1, name=f"mm_dw_mem_kv{m}").reshape(N_DEV, -1, DM2)
            _, d_mem_norm = _rms_bwd(dmem_n, mem[0], mem_norm, zeros_mem, "rms_bwd_mem")
            rider, keys = scatter_rider([f"mem_kv{m}" for m in range(L)])
            du, landed = _mm_nt(dproj, full["in_a0"], name="mm_du_a0", rider=rider)
            received.update(zip(keys, landed))
            small_early = {
                "mem_norm": d_mem_norm.reshape(-1),
                "conv_w": jnp.stack(d_conv_w),
                "conv_b": jnp.concatenate(d_conv_b, axis=0),
                "w_rec_gate": jnp.stack(d_w_r),
                "b_rec_gate": jnp.concatenate(d_b_r, axis=0),
                "w_in_gate": jnp.stack(d_w_i),
                "b_in_gate": jnp.concatenate(d_b_i, axis=0),
                "lru_lambda": jnp.concatenate(d_lam, axis=0),
                "kv_norm": d_kv_norm.reshape(-1),
                "norm_b": jnp.concatenate(d_norm_b, axis=0),
                "final_norm": d_final_norm.reshape(-1),
            }
            rider = _Exchange([_pack(list(small_early.values()), 256)], gather=True)
            big_grads["in_a0"], (early_gathered,) = _mm_tn(u, dproj, N_DEV, name="mm_dw_in_a0", rider=rider)
        dh, d_norm_a[l] = _rms_bwd(du, h_in, norm_a_f[l], dh, f"rms_bwd_a{l}")
    grad_x = dh.reshape(x.shape)

    received.update(zip(["in_a0"], _all_to_all([big_grads["in_a0"]], "scatter_grads")))
    small_late = {"norm_a": jnp.concatenate(d_norm_a, axis=0)}
    (late_gathered,) = _all_gather([_pack(list(small_late.values()), 256)], "gather_small_grads")

    def update(key, w, m, v):
        shape = w.shape
        two_d = (-1, shape[-1])
        return [o.reshape(shape) for o in _adamw(received[key], w.reshape(two_d), m.reshape(two_d),
                                                 v.reshape(two_d), f"adamw_{key}")]

    def update_layers(prefix, w, m, v):
        per_layer = [update(f"{prefix}{l}", w[l], m[l], v[l]) for l in range(w.shape[0])]
        return [jnp.stack([per_layer[l][k] for l in range(w.shape[0])]) for k in range(4)]

    upd = {
        "w_mem_kv": update_layers("mem_kv", w_mem_kv, m_w_mem_kv, v_w_mem_kv),
        "w_in_a": update_layers("in_a", w_in_a, m_w_in_a, v_w_in_a),
        "w_out_a": update_layers("out_a", w_out_a, m_w_out_a, v_w_out_a),
        "w_kv": update("kv", w_kv, m_w_kv, v_w_kv),
        "w_in_b": update_layers("in_b", w_in_b, m_w_in_b, v_w_in_b),
        "w_out_b": update_layers("out_b", w_out_b, m_w_out_b, v_w_out_b),
    }

    small_grad = {}
    for part, got, name in ((small_early, early_gathered, "sum_small_early"), (small_late, late_gathered, "sum_small_late")):
        summed = _unpack(_sum_devices(got, name), [g.shape for g in part.values()])
        small_grad.update(zip(part, summed))
    small_names = list(small_grad)
    small_w = {"mem_norm": (mem_norm, m_mem_norm, v_mem_norm), "norm_a": (norm_a, m_norm_a, v_norm_a),
               "conv_w": (conv_w, m_conv_w, v_conv_w), "conv_b": (conv_b, m_conv_b, v_conv_b),
               "w_rec_gate": (w_rec_gate, m_w_rec_gate, v_w_rec_gate),
               "b_rec_gate": (b_rec_gate, m_b_rec_gate, v_b_rec_gate),
               "w_in_gate": (w_in_gate, m_w_in_gate, v_w_in_gate), "b_in_gate": (b_in_gate, m_b_in_gate, v_b_in_gate),
               "lru_lambda": (lru_lambda, m_lru_lambda, v_lru_lambda), "kv_norm": (kv_norm, m_kv_norm, v_kv_norm),
               "norm_b": (norm_b, m_norm_b, v_norm_b), "final_norm": (final_norm, m_final_norm, v_final_norm)}
    for k in small_names:
        w = small_w[k][0]
        if small_grad[k].shape != w.shape:
            n = w.shape[-1]
            small_grad[k] = lax.dynamic_slice_in_dim(small_grad[k], me * n, n, axis=-1)
    small_shapes = [small_w[k][0].shape for k in small_names]
    packed = [_pack([small_grad[k] for k in small_names], 256)[None]]
    packed += [_pack([small_w[k][t] for k in small_names], 256) for t in range(3)]
    small_out = [_unpack(o, small_shapes) for o in _adamw(*packed, "adamw_small")]
    for idx, k in enumerate(small_names):
        upd[k] = [small_out[t][idx] for t in range(4)]

    order = ["mem_norm", "w_mem_kv", "norm_a", "w_in_a", "conv_w", "conv_b", "w_rec_gate", "b_rec_gate", "w_in_gate",
             "b_in_gate", "lru_lambda", "w_out_a", "kv_norm", "w_kv", "norm_b", "w_in_b", "w_out_b", "final_norm"]
    return (loss, grad_x, *[upd[k][0] for k in order], *[upd[k][1] for k in order],
            *[upd[k][2] for k in order], *[upd[k][3] for k in order])
```

```python
import functools

import jax
import jax.numpy as jnp
from jax import lax
from jax.experimental import pallas as pl
from jax.experimental.pallas import tpu as pltpu

F32 = jnp.float32
BF16 = jnp.bfloat16

N_DEV = 8
EPS = 1e-6
LRU_C = 8.0
HEAD = 128
CONV_WIDTH = 4
LANES = 128
SUB = 8
SB_EXHAUSTED = 110.0
VMEM_LIMIT = 56 * 1024 * 1024
RESIDENT_WEIGHT_BYTES = 8 * 1024 * 1024

ADAM_LR = 0.001
ADAM_B1 = 0.9
ADAM_B2 = 0.999
ADAM_EPS = 1e-08
ADAM_WD = 0.01
ADAM_STEP = 10

HBM_SPEC = pl.BlockSpec(memory_space=pltpu.HBM)


def _params(*semantics):
    return pltpu.CompilerParams(dimension_semantics=semantics, vmem_limit_bytes=VMEM_LIMIT)


def _tile(n, cap):
    if n <= cap:
        return n
    t = cap - cap % LANES
    while n % t:
        t -= LANES
    return t


def _row_tile(n, cap):
    if n <= cap:
        return n
    t = cap - cap % 8
    while t >= 8:
        if n % t == 0:
            return t
        t -= 8
    return n


def _rows_for(n, bytes_per_row, budget=2 * 1024 * 1024):
    return _row_tile(n, max(8, budget // bytes_per_row))


def _dot(a, b):
    return lax.dot_general(a, b, (((1,), (0,)), ((), ())), preferred_element_type=F32)


def _dot_nt(a, b):
    return lax.dot_general(a, b, (((1,), (1,)), ((), ())), preferred_element_type=F32)


def _dot_tn(a, b):
    return lax.dot_general(a, b, (((0,), (0,)), ((), ())), preferred_element_type=F32)


def _split_bf16(x):
    hi = x.astype(BF16)
    lo = (x - hi.astype(F32)).astype(BF16)
    return hi, lo


def _softplus(x):
    return jnp.maximum(x, 0.0) + jnp.log1p(jnp.exp(-jnp.abs(x)))


def _gate_sigmoid(x):
    return 0.5 * jnp.tanh(0.5 * x) + 0.5


def _silu_and_grad(g):
    sg = _gate_sigmoid(g)
    return g * sg, sg * (1.0 + g * (1.0 - sg))


def _one_minus_square(a, log_a):
    x = 2.0 * log_a
    series = -x * (1.0 + x * (0.5 + x * (1.0 / 6.0)))
    return jnp.where(x > -0.03, series, 1.0 - a * a)


def _rms_fwd(x, g, name):
    T, D = x.shape
    tt = _row_tile(T, 512)

    def body(x_ref, g_ref, o_ref):
        xv = x_ref[...]
        r = lax.rsqrt(jnp.mean(xv * xv, axis=-1, keepdims=True) + EPS)
        o_ref[...] = ((xv * r) * g_ref[...]).astype(o_ref.dtype)

    return pl.pallas_call(
        body, name=name, grid=(T // tt,),
        in_specs=[pl.BlockSpec((tt, D), lambda i: (i, 0)), pl.BlockSpec((1, D), lambda i: (0, 0))],
        out_specs=pl.BlockSpec((tt, D), lambda i: (i, 0)),
        out_shape=jax.ShapeDtypeStruct((T, D), BF16),
        compiler_params=_params("parallel"),
    )(x, g.reshape(1, D))


def _rms_bwd(du, h, g, res, name):
    T, D = h.shape
    tt = _row_tile(T, 256)

    def body(du_ref, h_ref, g_ref, res_ref, dh_ref, dg_ref):
        @pl.when(pl.program_id(0) == 0)
        def _():
            dg_ref[...] = jnp.zeros_like(dg_ref)

        hv = h_ref[...]
        duv = du_ref[...]
        r = lax.rsqrt(jnp.mean(hv * hv, axis=-1, keepdims=True) + EPS)
        xhat = hv * r
        dxhat = duv * g_ref[...]
        dh_ref[...] = res_ref[...] + r * (dxhat - xhat * jnp.mean(dxhat * xhat, axis=-1, keepdims=True))
        dg_ref[...] += jnp.sum(duv * xhat, axis=0, keepdims=True)

    row = pl.BlockSpec((tt, D), lambda i: (i, 0))
    vec = pl.BlockSpec((1, D), lambda i: (0, 0))
    return pl.pallas_call(
        body, name=name, grid=(T // tt,),
        in_specs=[row, row, vec, row],
        out_specs=[row, vec],
        out_shape=[jax.ShapeDtypeStruct((T, D), F32), jax.ShapeDtypeStruct((1, D), F32)],
        compiler_params=_params("arbitrary"),
    )(du, h, g.reshape(1, D), res)


def _mm_nn(a, b3, res=None, out_dtype=F32, name=None, rider=None):
    M, K = a.shape
    S, _, ns = b3.shape
    whole_b = S == 1 and K * ns * 2 <= RESIDENT_WEIGHT_BYTES
    tm = _row_tile(M, 512 if whole_b or a.dtype != BF16 else 1024)
    tn = ns if whole_b else _tile(ns, 512)
    nj = ns // tn

    def body(*refs):
        if res is None:
            a_ref, b_ref, o_ref = refs
        else:
            a_ref, b_ref, r_ref, o_ref = refs
        acc = _dot(a_ref[...].astype(BF16), b_ref[...])
        if res is not None:
            acc = acc + r_ref[...]
        o_ref[...] = acc.astype(o_ref.dtype)

    in_specs = [pl.BlockSpec((tm, K), lambda i, j: (i, 0)),
                pl.BlockSpec((None, K, tn), lambda i, j: (j // nj, 0, j % nj))]
    args = [a, b3]
    if res is not None:
        in_specs.append(pl.BlockSpec((tm, tn), lambda i, j: (i, j)))
        args.append(res)
    (out,), landed = _call_with_rider(
        body, rider, *_grid_ends((M // tm, S * nj)), name=name, grid=(M // tm, S * nj),
        in_specs=in_specs,
        out_specs=[pl.BlockSpec((tm, tn), lambda i, j: (i, j))],
        out_shape=[jax.ShapeDtypeStruct((M, S * ns), out_dtype)],
        scratch_shapes=[], semantics=("parallel", "parallel"), args=args)
    return out if rider is None else (out, landed)


def _grid_ends(grid):
    def first():
        return functools.reduce(jnp.logical_and, [pl.program_id(d) == 0 for d in range(len(grid))])

    def last():
        return functools.reduce(jnp.logical_and, [pl.program_id(d) == n - 1 for d, n in enumerate(grid)])

    return first, last


def _mm_nt(a, b3, out_dtype=F32, name=None, rider=None):
    M = a.shape[0]
    S, N, ns = b3.shape
    whole_b = S * N * ns * 2 <= RESIDENT_WEIGHT_BYTES
    tm = _row_tile(M, 512 if whole_b or a.dtype != BF16 else 1024)
    tn = N if whole_b else _tile(N, 512)

    def body(a_ref, b_ref, o_ref):
        acc = _dot_nt(a_ref[:, 0:ns].astype(BF16), b_ref[0])
        for s in range(1, S):
            acc = acc + _dot_nt(a_ref[:, s * ns:(s + 1) * ns].astype(BF16), b_ref[s])
        o_ref[...] = acc.astype(o_ref.dtype)

    (out,), landed = _call_with_rider(
        body, rider, *_grid_ends((M // tm, N // tn)), name=name, grid=(M // tm, N // tn),
        in_specs=[pl.BlockSpec((tm, S * ns), lambda i, j: (i, 0)),
                  pl.BlockSpec((S, tn, ns), lambda i, j: (0, j, 0))],
        out_specs=[pl.BlockSpec((tm, tn), lambda i, j: (i, j))],
        out_shape=[jax.ShapeDtypeStruct((M, N), out_dtype)],
        scratch_shapes=[], semantics=("parallel", "parallel"), args=(a, b3))
    return out if rider is None else (out, landed)


def _mm_tn(a, b, S, name=None, rider=None):
    T, K = a.shape
    ns = b.shape[1] // S
    tk = _tile(K, 1024)
    tn = _tile(ns, 512)
    nj = ns // tn
    tt = _row_tile(T, 2048)
    nt = T // tt

    def body(a_ref, b_ref, o_ref, acc_ref):
        t = pl.program_id(2)
        p = _dot_tn(a_ref[...].astype(BF16), b_ref[...].astype(BF16))

        @pl.when(t == 0)
        def _():
            acc_ref[...] = p

        @pl.when(t > 0)
        def _():
            acc_ref[...] += p

        @pl.when(t == nt - 1)
        def _():
            o_ref[...] = acc_ref[...].astype(o_ref.dtype)

    grid = (K // tk, S * nj, nt)
    (out,), landed = _call_with_rider(
        body, rider, *_grid_ends(grid), name=name, grid=grid,
        in_specs=[pl.BlockSpec((tt, tk), lambda i, j, t: (t, i)),
                  pl.BlockSpec((tt, tn), lambda i, j, t: (t, j))],
        out_specs=[pl.BlockSpec((None, tk, tn), lambda i, j, t: (j // nj, i, j % nj))],
        out_shape=[jax.ShapeDtypeStruct((S, K, ns), BF16)],
        scratch_shapes=[pltpu.VMEM((tk, tn), F32)],
        semantics=("parallel", "parallel", "arbitrary"), args=(a, b))
    return out if rider is None else (out, landed)


def _lru_coeffs(lam_row, r):
    cl = -LRU_C * _softplus(-lam_row)
    log_a = cl * r
    a = jnp.exp(log_a)
    em = _one_minus_square(a, log_a)
    return cl, a, em, jnp.sqrt(em)


def _rglru_fwd(proj, conv_w, conv_b, w_r, b_r, w_i, b_i, lam, name, rider=None):
    T = proj.shape[0]
    DR = conv_w.shape[1]
    NB = DR // HEAD
    tt = _row_tile(T, 256)
    PAD = 8

    def body(x_ref, g_ref, cw_ref, cb_ref, wr_ref, br_ref, wi_ref, bi_ref, lam_ref,
             y_ref, xc_ref, r_ref, i_ref, h_ref, xbuf, hcar):
        @pl.when(pl.program_id(0) == 0)
        def _():
            xbuf[0:PAD, :] = jnp.zeros((PAD, DR), F32)
            hcar[...] = jnp.zeros_like(hcar)

        xbuf[PAD:PAD + tt, :] = x_ref[...]
        xc = cb_ref[...] + cw_ref[0:1, :] * xbuf[pl.ds(PAD - 3, tt), :]
        for k in range(1, CONV_WIDTH):
            xc = xc + cw_ref[k:k + 1, :] * xbuf[pl.ds(PAD - 3 + k, tt), :]
        xbuf[0:PAD, :] = xbuf[tt:tt + PAD, :]
        xc_ref[...] = xc
        xcb = xc.astype(BF16)
        for n in range(NB):
            sl = slice(n * HEAD, (n + 1) * HEAD)
            r_ref[:, sl] = jax.nn.sigmoid(_dot(xcb[:, sl], wr_ref[n]) + br_ref[:, sl])
            i_ref[:, sl] = _gate_sigmoid(_dot(xcb[:, sl], wi_ref[n]) + bi_ref[:, sl])
        _, a, _, mult = _lru_coeffs(lam_ref[...], r_ref[...])
        hs = mult * (i_ref[...] * xc)
        groups = tt // SUB
        a = a.reshape(groups, SUB, DR)
        hs = hs.reshape(groups, SUB, DR)
        sub = lax.broadcasted_iota(jnp.int32, (groups, SUB, DR), 1)
        d = 1
        while d < SUB:
            keep = sub >= d
            a_sh = jnp.where(keep, pltpu.roll(a, d, 1), 1.0)
            h_sh = jnp.where(keep, pltpu.roll(hs, d, 1), 0.0)
            hs = a * h_sh + hs
            a = a * a_sh
            d *= 2
        carry = hcar[...]
        for n in range(groups):
            h_ref[n * SUB:(n + 1) * SUB, :] = hs[n] + a[n] * carry
            carry = h_ref[(n + 1) * SUB - 1:(n + 1) * SUB, :]
        hcar[...] = carry
        h = h_ref[...]
        g = g_ref[...]
        y_ref[...] = (h * (g * _gate_sigmoid(g))).astype(y_ref.dtype)

    col = lambda j: pl.BlockSpec((tt, DR), lambda c: (c, j))
    vec = pl.BlockSpec((1, DR), lambda c: (0, 0))
    gate = pl.BlockSpec((NB, HEAD, HEAD), lambda c: (0, 0, 0))
    f32_out = jax.ShapeDtypeStruct((T, DR), F32)
    nc = T // tt
    return _call_with_rider(
        body, rider, lambda: pl.program_id(0) == 0, lambda: pl.program_id(0) == nc - 1,
        name=name, grid=(nc,),
        in_specs=[col(0), col(1), pl.BlockSpec((CONV_WIDTH, DR), lambda c: (0, 0)), vec, gate, vec, gate, vec, vec],
        out_specs=[col(0)] * 5,
        out_shape=[jax.ShapeDtypeStruct((T, DR), BF16), f32_out, f32_out, f32_out, f32_out],
        scratch_shapes=[pltpu.VMEM((tt + PAD, DR), F32), pltpu.VMEM((1, DR), F32)],
        semantics=("arbitrary",),
        args=(proj, proj, conv_w, conv_b.reshape(1, DR), w_r, b_r.reshape(1, DR), w_i, b_i.reshape(1, DR),
              lam.reshape(1, DR)))


def _rglru_bwd(dy, proj, xc, r, i, h, conv_w, w_r, w_i, lam, name, rider=None):
    T = proj.shape[0]
    DR = conv_w.shape[1]
    NB = DR // HEAD
    tt = _row_tile(T, 128)
    nc = T // tt
    PAD = 8
    per = tt // PAD

    def body(dy_ref, x_ref, g_ref, xc_ref, r_ref, i_ref, h_ref, xprev_ref, hprev_ref,
             cw_ref, wr_ref, wi_ref, lam_ref,
             dxg_ref, dcw_ref, dcb_ref, dwr_ref, dbr_ref, dwi_ref, dbi_ref, dlam_ref,
             xbuf, dxcbuf, gcar, acar):
        step = pl.program_id(0)
        chunk = nc - 1 - step

        @pl.when(step == 0)
        def _():
            for ref in (dcw_ref, dcb_ref, dwr_ref, dbr_ref, dwi_ref, dbi_ref, dlam_ref, gcar, acar):
                ref[...] = jnp.zeros_like(ref)
            dxcbuf[tt:tt + PAD, :] = jnp.zeros((PAD, DR), F32)

        not_first = (chunk > 0).astype(F32)
        row = lax.broadcasted_iota(jnp.int32, (tt, DR), 0)
        silu, dsilu = _silu_and_grad(g_ref[...])
        dyv = dy_ref[...]
        hv = h_ref[...]
        dxg_ref[:, DR:2 * DR] = (dyv * hv * dsilu).astype(dxg_ref.dtype)
        dh = dyv * silu
        rv = r_ref[...]
        iv = i_ref[...]
        xcv = xc_ref[...]
        lam_row = lam_ref[...]
        cl, a, em, mult = _lru_coeffs(lam_row, rv)
        b = jnp.where(row == tt - 1, acar[...], pltpu.roll(a, tt - 1, 0))
        gs = dh
        d = 1
        while d < tt:
            keep = row < tt - d
            b_sh = jnp.where(keep, pltpu.roll(b, tt - d, 0), 1.0)
            g_sh = jnp.where(keep, pltpu.roll(gs, tt - d, 0), 0.0)
            gs = gs + b * g_sh
            b = b * b_sh
            d *= 2
        gt = gs + b * gcar[...]
        xbuf[0:tt, :] = gt
        gcar[...] = xbuf[0:1, :]
        acar[...] = _lru_coeffs(lam_row, r_ref[0:1, :])[1]
        h_before = hprev_ref[PAD - 1:PAD, :] * not_first
        hprev = jnp.where(row == 0, h_before, pltpu.roll(hv, 1, 0))
        da = gt * hprev
        dmult = gt * (iv * xcv)
        di = gt * mult * xcv
        dxc = gt * mult * iv
        dlog_a = da * a - dmult * (1.0 - em) / mult
        dr = dlog_a * cl
        dlam_ref[...] += jnp.sum(dlog_a * rv, axis=0, keepdims=True) * (LRU_C * jax.nn.sigmoid(-lam_row))
        drp = dr * rv * (1.0 - rv)
        dip = di * iv * (1.0 - iv)
        dbr_ref[...] += jnp.sum(drp, axis=0, keepdims=True)
        dbi_ref[...] += jnp.sum(dip, axis=0, keepdims=True)
        drpb = drp.astype(BF16)
        dipb = dip.astype(BF16)
        xcb = xcv.astype(BF16)
        for n in range(NB):
            sl = slice(n * HEAD, (n + 1) * HEAD)
            dxcbuf[0:tt, sl] = dxc[:, sl] + _dot_nt(drpb[:, sl], wr_ref[n]) + _dot_nt(dipb[:, sl], wi_ref[n])
            dwr_ref[n] += _dot_tn(xcb[:, sl], drpb[:, sl])
            dwi_ref[n] += _dot_tn(xcb[:, sl], dipb[:, sl])
        dxc_all = dxcbuf[0:tt, :]
        dcb_ref[...] += jnp.sum(dxc_all, axis=0, keepdims=True)
        xbuf[0:PAD, :] = xprev_ref[...] * not_first
        xbuf[PAD:PAD + tt, :] = x_ref[...]
        dx = cw_ref[0:1, :] * dxcbuf[pl.ds(3, tt), :]
        for k in range(1, CONV_WIDTH):
            dx = dx + cw_ref[k:k + 1, :] * dxcbuf[pl.ds(3 - k, tt), :]
        dxg_ref[:, 0:DR] = dx.astype(dxg_ref.dtype)
        for k in range(CONV_WIDTH):
            dcw_ref[k:k + 1, :] += jnp.sum(xbuf[pl.ds(PAD - 3 + k, tt), :] * dxc_all, axis=0, keepdims=True)
        dxcbuf[tt:tt + PAD, :] = dxcbuf[0:PAD, :]

    rev = lambda j: pl.BlockSpec((tt, DR), lambda s: (nc - 1 - s, j))
    prev = pl.BlockSpec((PAD, DR), lambda s: (jnp.maximum((nc - 1 - s) * per - 1, 0), 0))
    vec = pl.BlockSpec((1, DR), lambda s: (0, 0))
    gate = pl.BlockSpec((NB, HEAD, HEAD), lambda s: (0, 0, 0))
    taps = pl.BlockSpec((CONV_WIDTH, DR), lambda s: (0, 0))
    vec_out = jax.ShapeDtypeStruct((1, DR), F32)
    gate_out = jax.ShapeDtypeStruct((NB, HEAD, HEAD), F32)
    return _call_with_rider(
        body, rider, lambda: pl.program_id(0) == 0, lambda: pl.program_id(0) == nc - 1,
        name=name, grid=(nc,),
        in_specs=[rev(0), rev(0), rev(1), rev(0), rev(0), rev(0), rev(0), prev, prev, taps, gate, gate, vec],
        out_specs=[pl.BlockSpec((tt, 2 * DR), lambda s: (nc - 1 - s, 0)), taps, vec, gate, vec, gate, vec, vec],
        out_shape=[jax.ShapeDtypeStruct((T, 2 * DR), BF16), jax.ShapeDtypeStruct((CONV_WIDTH, DR), F32),
                   vec_out, gate_out, vec_out, gate_out, vec_out, vec_out],
        scratch_shapes=[pltpu.VMEM((tt + PAD, DR), F32), pltpu.VMEM((tt + PAD, DR), F32),
                        pltpu.VMEM((1, DR), F32), pltpu.VMEM((1, DR), F32)],
        semantics=("arbitrary",),
        args=(dy, proj, proj, xc, r, i, h, proj, h, conv_w, w_r, w_i, lam.reshape(1, DR)))


def _mem_probs(q, k, scale):
    s = _dot_nt(q, k) * scale
    p = jnp.exp(s - jnp.max(s, axis=-1, keepdims=True))
    return p * (1.0 / jnp.sum(p, axis=-1, keepdims=True))


def _memattn_fwd(proj, mkv, DR, DM, name):
    T = proj.shape[0]
    M = mkv.shape[0]
    NH = DM // HEAD
    tt = _row_tile(T, 512)
    qcol = 2 * DR // DM
    scale = HEAD ** -0.5

    def body(q_ref, g_ref, k_ref, v_ref, y_ref):
        for n in range(NH):
            sl = slice(n * HEAD, (n + 1) * HEAD)
            p = _mem_probs(q_ref[:, sl].astype(BF16), k_ref[:, sl], scale)
            o = _dot(p.astype(BF16), v_ref[:, sl])
            g = g_ref[:, sl]
            y_ref[:, sl] = (o * (g * _gate_sigmoid(g))).astype(y_ref.dtype)

    return pl.pallas_call(
        body, name=name, grid=(T // tt,),
        in_specs=[pl.BlockSpec((tt, DM), lambda t: (t, qcol)), pl.BlockSpec((tt, DM), lambda t: (t, qcol + 1)),
                  pl.BlockSpec((M, DM), lambda t: (0, 0)), pl.BlockSpec((M, DM), lambda t: (0, 1))],
        out_specs=pl.BlockSpec((tt, DM), lambda t: (t, 0)),
        out_shape=jax.ShapeDtypeStruct((T, DM), BF16),
        compiler_params=_params("parallel"),
    )(proj, proj, mkv, mkv)


def _memattn_bwd(dy, proj, mkv, DR, DM, name):
    T = proj.shape[0]
    M = mkv.shape[0]
    NH = DM // HEAD
    tt = _row_tile(T, 512)
    qcol = 2 * DR // DM
    scale = HEAD ** -0.5

    def body(dy_ref, q_ref, g_ref, k_ref, v_ref, dqg_ref, dkv_ref):
        @pl.when(pl.program_id(0) == 0)
        def _():
            dkv_ref[...] = jnp.zeros_like(dkv_ref)

        for n in range(NH):
            sl = slice(n * HEAD, (n + 1) * HEAD)
            qb = q_ref[:, sl].astype(BF16)
            kb = k_ref[:, sl]
            vb = v_ref[:, sl]
            p = _mem_probs(qb, kb, scale)
            pb = p.astype(BF16)
            o = _dot(pb, vb)
            silu, dsilu = _silu_and_grad(g_ref[:, sl])
            dyv = dy_ref[:, sl]
            dqg_ref[:, DM + n * HEAD:DM + (n + 1) * HEAD] = (dyv * o * dsilu).astype(dqg_ref.dtype)
            dob = (dyv * silu).astype(BF16)
            dp = _dot_nt(dob, vb)
            ds = (p * (dp - jnp.sum(dp * p, axis=-1, keepdims=True)) * scale).astype(BF16)
            dqg_ref[:, sl] = _dot(ds, kb).astype(dqg_ref.dtype)
            dkv_ref[:, sl] += _dot_tn(ds, qb)
            dkv_ref[:, DM + n * HEAD:DM + (n + 1) * HEAD] += _dot_tn(pb, dob)

    return pl.pallas_call(
        body, name=name, grid=(T // tt,),
        in_specs=[pl.BlockSpec((tt, DM), lambda t: (t, DR // DM)),
                  pl.BlockSpec((tt, DM), lambda t: (t, qcol)), pl.BlockSpec((tt, DM), lambda t: (t, qcol + 1)),
                  pl.BlockSpec((M, DM), lambda t: (0, 0)), pl.BlockSpec((M, DM), lambda t: (0, 1))],
        out_specs=[pl.BlockSpec((tt, 2 * DM), lambda t: (t, 0)), pl.BlockSpec((M, 2 * DM), lambda t: (0, 0))],
        out_shape=[jax.ShapeDtypeStruct((T, 2 * DM), BF16), jax.ShapeDtypeStruct((M, 2 * DM), F32)],
        compiler_params=_params("arbitrary"),
    )(dy, proj, proj, mkv, mkv)


def _sb_blocks(T):
    tk = _row_tile(T // 2, 256)
    tq = 2 * tk
    assert T % tq == 0
    return tq, tk


def _sb_upper(tk):
    row = lax.broadcasted_iota(jnp.int32, (tk, tk), 0)
    col = lax.broadcasted_iota(jnp.int32, (tk, tk), 1)
    return (row > col).astype(BF16)


def _sb_causal(tq, tk, d):
    row = lax.broadcasted_iota(jnp.int32, (tq, tk), 0)
    col = lax.broadcasted_iota(jnp.int32, (tq, tk), 1)
    return col + d * tk < row


def _sb_own_span(tile, carry, i, tq, tk):
    span = tq // tk
    for d in reversed(range(span)):
        lo = d * tk
        part = tile(span * i + d, tuple(c[lo:] for c in carry), _sb_causal(tq - lo, tk, 0), slice(lo, tq))
        carry = tuple(p if lo == 0 else jnp.concatenate([c[:lo], p], axis=0) for c, p in zip(carry, part))
    return carry


def _sb_earlier(tile, carry, jb_first, tq):
    half = tq // 2

    def walk(carry, jb, rows, watched):
        def more(c):
            return jnp.logical_and(c[0] >= 0, c[1] > 0)

        def step(c):
            jb, _, *rest = c
            rest = tile(jb, tuple(rest), None, rows)
            return jb - 1, _sb_alive(watched(rest[0])), *rest

        jb, _, *carry = lax.while_loop(more, step, (jb, _sb_alive(watched(carry[0])), *carry))
        return jb, tuple(carry)

    jb, carry = walk(carry, jb_first, slice(0, tq), lambda gone: gone[half:])
    _, low = walk(tuple(c[:half] for c in carry), jb, slice(0, half), lambda gone: gone)
    return tuple(jnp.concatenate([lo, c[half:]], axis=0) for lo, c in zip(low, carry))


def _later_sum(x, upper):
    n = x.shape[0]
    hi, lo = _split_bf16(x)
    both = _dot(jnp.concatenate([hi, lo], axis=0), upper)
    return both[0:n] + both[n:2 * n]


def _sb_alive(gone_c):
    return (jnp.min(gone_c) < SB_EXHAUSTED).astype(jnp.int32)


def _sb_weights(q, kb, gone_c, causal, upper, scale):
    return _sb_weights_of(_dot_nt(q, kb) * scale, gone_c, causal, upper)


def _sb_weights_of(z, gone_c, causal, upper):
    sp = jnp.where(z > 20.0, z, jnp.log(1.0 + jnp.exp(z)))
    spm = sp if causal is None else jnp.where(causal, sp, 0.0)
    gone = _later_sum(spm, upper) + gone_c
    w = jnp.exp(z - sp - gone)
    if causal is not None:
        w = jnp.where(causal, w, 0.0)
    return z, sp, spm, w


def _sb_fwd(proj, kv, DR, name, rider=None):
    T = proj.shape[0]
    NH = DR // HEAD
    tq, tk = _sb_blocks(T)
    span = tq // tk
    scale = HEAD ** -0.5

    def body(q_ref, g_ref, k_ref, v_ref, y_ref, o_ref):
        i = pl.program_id(1)
        q = q_ref[...].astype(BF16)
        upper = _sb_upper(tk)

        def tile(jb, carry, causal, rows=slice(0, tq)):
            gone_c, acc = carry
            n = rows.stop - rows.start
            start = pl.multiple_of(jb * tk, tk)
            kb = k_ref[pl.ds(start, tk), :]
            vb = v_ref[pl.ds(start, tk), :]
            _, _, spm, w = _sb_weights(q[rows], kb, gone_c, causal, upper, scale)
            hi, lo = _split_bf16(w)
            pv = _dot(jnp.concatenate([hi, lo], axis=0), vb)
            return gone_c + jnp.sum(spm, axis=-1, keepdims=True), acc + pv[0:n] + pv[n:2 * n]

        carry = _sb_own_span(tile, (jnp.zeros((tq, 1), F32), jnp.zeros((tq, HEAD), F32)), i, tq, tk)

        _, acc = _sb_earlier(tile, carry, span * i - 1, tq)
        o_ref[...] = acc
        g = g_ref[...]
        y_ref[...] = (acc * (g * _gate_sigmoid(g))).astype(y_ref.dtype)

    blk = lambda off: pl.BlockSpec((tq, HEAD), lambda h, i: (i, off + h))
    whole = lambda off: pl.BlockSpec((T, HEAD), lambda h, i: (0, off + h))
    nq = T // tq
    return _call_with_rider(
        body, rider, lambda: (pl.program_id(0) == 0) & (pl.program_id(1) == 0),
        lambda: (pl.program_id(0) == NH - 1) & (pl.program_id(1) == nq - 1),
        name=name, grid=(NH, nq),
        in_specs=[blk(0), blk(NH), whole(0), whole(NH)],
        out_specs=[blk(0), blk(0)],
        out_shape=[jax.ShapeDtypeStruct((T, DR), BF16), jax.ShapeDtypeStruct((T, DR), F32)],
        scratch_shapes=[], semantics=("parallel", "arbitrary"),
        args=(proj, proj, kv, kv))


def _sb_bwd(dy, proj, kv, o, DR, name, rider=None):
    T = proj.shape[0]
    NH = DR // HEAD
    tq, tk = _sb_blocks(T)
    span = tq // tk
    scale = HEAD ** -0.5

    def body(dy_ref, q_ref, g_ref, k_ref, v_ref, o_ref, dq_ref, dg_ref, dk_ref, dv_ref):
        i = pl.program_id(1)

        @pl.when(i == 0)
        def _():
            dk_ref[...] = jnp.zeros_like(dk_ref)
            dv_ref[...] = jnp.zeros_like(dv_ref)

        qf = q_ref[...]
        q = qf.astype(BF16)
        q_t = qf.T.astype(BF16)
        upper = _sb_upper(tk)
        silu, dsilu = _silu_and_grad(g_ref[...])
        dyv = dy_ref[...]
        ov = o_ref[...]
        dg_ref[...] = (dyv * ov * dsilu).astype(dg_ref.dtype)
        do = dyv * silu
        dob = do.astype(BF16)
        do_t = do.T.astype(BF16)
        total = jnp.sum(dob.astype(F32) * ov, axis=-1, keepdims=True)

        def tile(jb, carry, causal, rows=slice(0, tq)):
            gone_c, e_after, dq = carry
            start = pl.multiple_of(jb * tk, tk)
            kb = k_ref[pl.ds(start, tk), :]
            vb = v_ref[pl.ds(start, tk), :]
            z, sp, spm, w = _sb_weights(q[rows], kb, gone_c, causal, upper, scale)
            e = _dot_nt(dob[rows], vb) * w
            dz = e - (total[rows] - (_later_sum(e, upper) + e_after)) * jnp.exp(z - sp)
            if causal is not None:
                dz = jnp.where(causal, dz, 0.0)
            dzb = dz.astype(BF16)
            dk_ref[jb] += _dot(q_t[:, rows], dzb) * scale
            dv_ref[jb] += _dot(do_t[:, rows], w.astype(BF16))
            return (gone_c + jnp.sum(spm, axis=-1, keepdims=True), e_after + jnp.sum(e, axis=-1, keepdims=True),
                    dq + _dot(dzb, kb))

        carry = _sb_own_span(
            tile, (jnp.zeros((tq, 1), F32), jnp.zeros((tq, 1), F32), jnp.zeros((tq, HEAD), F32)), i, tq, tk)

        *_, dq = _sb_earlier(tile, carry, span * i - 1, tq)
        dq_ref[...] = (dq * scale).astype(dq_ref.dtype)

    blk = lambda off: pl.BlockSpec((tq, HEAD), lambda h, i: (i, off + h))
    whole = lambda off: pl.BlockSpec((T, HEAD), lambda h, i: (0, off + h))
    keys_t = pl.BlockSpec((None, T // tk, HEAD, tk), lambda h, i: (h, 0, 0, 0))
    keys_t_shape = jax.ShapeDtypeStruct((NH, T // tk, HEAD, tk), F32)
    nq = T // tq
    return _call_with_rider(
        body, rider, lambda: (pl.program_id(0) == 0) & (pl.program_id(1) == 0),
        lambda: (pl.program_id(0) == NH - 1) & (pl.program_id(1) == nq - 1),
        name=name, grid=(NH, nq),
        in_specs=[blk(0), blk(0), blk(NH), whole(0), whole(NH), blk(0)],
        out_specs=[blk(0), blk(0), keys_t, keys_t],
        out_shape=[jax.ShapeDtypeStruct((T, DR), BF16), jax.ShapeDtypeStruct((T, DR), BF16),
                   keys_t_shape, keys_t_shape],
        scratch_shapes=[], semantics=("parallel", "arbitrary"),
        args=(dy, proj, proj, kv, kv, o))


def _merge_dkv(parts, name):
    NH, nblk, _, tk = parts[0][0].shape
    DR = NH * HEAD
    n = len(parts)

    def body(*refs):
        o_ref = refs[-1]
        for which in range(2):
            for h in range(NH):
                acc = refs[which][h]
                for p in range(1, n):
                    acc = acc + refs[2 * p + which][h]
                col = which * DR + h * HEAD
                o_ref[:, col:col + HEAD] = acc.T.astype(o_ref.dtype)

    blk = pl.BlockSpec((NH, None, HEAD, tk), lambda t: (0, t, 0, 0))
    return pl.pallas_call(
        body, name=name, grid=(nblk,),
        in_specs=[blk] * (2 * n),
        out_specs=pl.BlockSpec((tk, 2 * DR), lambda t: (t, 0)),
        out_shape=jax.ShapeDtypeStruct((nblk * tk, 2 * DR), BF16),
        compiler_params=_params("parallel"),
    )(*[a for pair in parts for a in pair])


def _final_loss(h, g, target, name):
    T, D = h.shape
    tt = _row_tile(T, 256)

    def body(h_ref, g_ref, t_ref, dh_ref, dg_ref, sq_ref):
        @pl.when(pl.program_id(0) == 0)
        def _():
            dg_ref[...] = jnp.zeros_like(dg_ref)
            sq_ref[...] = jnp.zeros_like(sq_ref)

        hv = h_ref[...]
        gv = g_ref[...]
        r = lax.rsqrt(jnp.mean(hv * hv, axis=-1, keepdims=True) + EPS)
        xhat = hv * r
        err = xhat * gv - t_ref[...]
        sq_ref[...] += jnp.sum(err * err, axis=0, keepdims=True)
        dy = err * (1.0 / D)
        dxhat = dy * gv
        dh_ref[...] = r * (dxhat - xhat * jnp.mean(dxhat * xhat, axis=-1, keepdims=True))
        dg_ref[...] += jnp.sum(dy * xhat, axis=0, keepdims=True)

    row = pl.BlockSpec((tt, D), lambda i: (i, 0))
    vec = pl.BlockSpec((1, D), lambda i: (0, 0))
    return pl.pallas_call(
        body, name=name, grid=(T // tt,),
        in_specs=[row, vec, row],
        out_specs=[row, vec, vec],
        out_shape=[jax.ShapeDtypeStruct((T, D), F32), jax.ShapeDtypeStruct((1, D), F32),
                   jax.ShapeDtypeStruct((1, D), F32)],
        compiler_params=_params("arbitrary"),
    )(h, g.reshape(1, D), target)


def _sum_parts(parts_ref):
    g = parts_ref[0].astype(F32)
    for s in range(1, parts_ref.shape[0]):
        g = g + parts_ref[s].astype(F32)
    return g


def _adamw(parts, w, m, v, name):
    P, R, C = parts.shape
    tr = _rows_for(R, P * C * 4)

    def body(p_ref, w_ref, m_ref, v_ref, g_ref, d_ref, nm_ref, nv_ref):
        g = _sum_parts(p_ref)
        nm = ADAM_B1 * m_ref[...] + (1.0 - ADAM_B1) * g
        nv = ADAM_B2 * v_ref[...] + (1.0 - ADAM_B2) * jnp.square(g)
        m_hat = nm / (1.0 - ADAM_B1 ** ADAM_STEP)
        v_hat = nv / (1.0 - ADAM_B2 ** ADAM_STEP)
        g_ref[...] = g
        d_ref[...] = -ADAM_LR * (m_hat / (jnp.sqrt(v_hat) + ADAM_EPS) + ADAM_WD * w_ref[...])
        nm_ref[...] = nm
        nv_ref[...] = nv

    row = pl.BlockSpec((tr, C), lambda i: (i, 0))
    out = jax.ShapeDtypeStruct((R, C), F32)
    return pl.pallas_call(
        body, name=name, grid=(R // tr,),
        in_specs=[pl.BlockSpec((P, tr, C), lambda i: (0, i, 0)), row, row, row],
        out_specs=[row] * 4,
        out_shape=[out] * 4,
        compiler_params=_params("parallel"),
    )(parts, w, m, v)


def _sum_devices(parts, name):
    P, R, C = parts.shape
    tr = _rows_for(R, P * C * 4)

    def body(p_ref, o_ref):
        o_ref[...] = _sum_parts(p_ref)

    return pl.pallas_call(
        body, name=name, grid=(R // tr,),
        in_specs=[pl.BlockSpec((P, tr, C), lambda i: (0, i, 0))],
        out_specs=pl.BlockSpec((tr, C), lambda i: (i, 0)),
        out_shape=jax.ShapeDtypeStruct((R, C), F32),
        compiler_params=_params("parallel"),
    )(parts)


def _mesh_position():
    return lax.axis_index("x"), lax.axis_index("y"), lax.axis_index("c")


def _device_index(p):
    return 4 * p[0] + 2 * p[1] + p[2]


class _Gather:
    def __init__(self, arrs):
        self.arrs = list(arrs)
        self.n = len(self.arrs)

    def out_shape(self):
        return [jax.ShapeDtypeStruct((N_DEV,) + a.shape, a.dtype) for a in self.arrs]

    def scratch(self):
        return [pltpu.SemaphoreType.DMA((self.n, 7)), pltpu.SemaphoreType.DMA((self.n, 7)),
                pltpu.SemaphoreType.DMA((self.n,))]

    def _plan(self, ins, outs, sems):
        send_sems, recv_sems, local_sems = sems
        x, y, c = _mesh_position()
        me, sibling = (x, y, c), (x, y, 1 - c)
        chips = [(1 - x, y), (x, 1 - y), (1 - x, 1 - y)]

        def slot(a, p):
            return outs[a].at[_device_index(p)]

        def copy(a, k, block, to, src=None):
            return pltpu.make_async_remote_copy(
                src_ref=slot(a, block) if src is None else src, dst_ref=slot(a, block),
                send_sem=send_sems.at[a, k], recv_sem=recv_sems.at[a, k],
                device_id=to, device_id_type=pl.DeviceIdType.MESH)

        mine = [pltpu.make_async_copy(ins[a], slot(a, me), local_sems.at[a]) for a in range(self.n)]
        first = []
        for a in range(self.n):
            first.append(copy(a, 0, me, sibling, src=ins[a]))
            first += [copy(a, 1 + j, me, (*chip, c), src=ins[a]) for j, chip in enumerate(chips)]
        return me, sibling, c, chips, copy, mine, first

    def start(self, ins, outs, sems):
        *_, mine, first = self._plan(ins, outs, sems)
        for cp in mine + first:
            cp.start()

    def wait(self, ins, outs, sems):
        me, sibling, c, chips, copy, mine, first = self._plan(ins, outs, sems)
        passed = []
        for a in range(self.n):
            for j, chip in enumerate(chips):
                copy(a, 1 + j, (*chip, c), me).wait_recv()
                fwd = copy(a, 4 + j, (*chip, c), sibling)
                fwd.start()
                passed.append(fwd)
        for a in range(self.n):
            copy(a, 0, sibling, me).wait_recv()
            for j, chip in enumerate(chips):
                copy(a, 4 + j, (*chip, 1 - c), me).wait_recv()
        for cp in first + passed:
            cp.wait_send()
        for cp in mine:
            cp.wait()


def _exchange_now(ex, name):
    n = ex.n

    def body(*refs):
        ins, outs, sems = refs[:n], refs[n:2 * n], refs[2 * n:]
        ex.start(ins, outs, sems)
        ex.wait(ins, outs, sems)

    return pl.pallas_call(
        body, name=name,
        in_specs=[HBM_SPEC] * n, out_specs=[HBM_SPEC] * n,
        out_shape=ex.out_shape(), scratch_shapes=ex.scratch(),
    )(*ex.arrs)


def _all_gather(arrs, name):
    return _exchange_now(_Gather(arrs), name)


class _Scatter:
    def __init__(self, arrs):
        self.arrs = list(arrs)
        self.n = len(self.arrs)

    def out_shape(self):
        return [jax.ShapeDtypeStruct(a.shape, a.dtype) for a in self.arrs]

    def scratch(self):
        return [pltpu.SemaphoreType.DMA((self.n, 7)), pltpu.SemaphoreType.DMA((self.n, 7)),
                pltpu.SemaphoreType.DMA((self.n,))]

    def _copies(self, ins, outs, sems, arrivals):
        send_sems, recv_sems, local_sems = sems
        x, y, c = _mesh_position()
        me = _device_index((x, y, c))
        peers = [(1 - x if k & 4 else x, 1 - y if k & 2 else y, 1 - c if k & 1 else c) for k in range(1, N_DEV)]
        local, sends, recvs = [], [], []
        for a in range(self.n):
            local.append(pltpu.make_async_copy(ins[a].at[me], outs[a].at[me], local_sems.at[a]))
            for k, peer in enumerate(peers):
                there = _device_index(peer)
                src = ins[a].at[there]
                sem = dict(send_sem=send_sems.at[a, k], recv_sem=recv_sems.at[a, k],
                           device_id=peer, device_id_type=pl.DeviceIdType.MESH)
                sends.append(pltpu.make_async_remote_copy(src_ref=src, dst_ref=outs[a].at[me], **sem))
                if arrivals:
                    recvs.append(pltpu.make_async_remote_copy(src_ref=src, dst_ref=outs[a].at[there], **sem))
        return local, sends, recvs

    def start(self, ins, outs, sems):
        local, sends, _ = self._copies(ins, outs, sems, arrivals=False)
        for cp in local + sends:
            cp.start()

    def wait(self, ins, outs, sems):
        local, sends, recvs = self._copies(ins, outs, sems, arrivals=True)
        for cp in recvs:
            cp.wait_recv()
        for cp in sends:
            cp.wait_send()
        for cp in local:
            cp.wait()


def _all_to_all(arrs, name):
    return _exchange_now(_Scatter(arrs), name)


def _call_with_rider(body, rider, first, last, *, name, grid, in_specs, out_specs, out_shape, scratch_shapes,
                     semantics, args):
    if rider is None:
        out = pl.pallas_call(body, name=name, grid=grid, in_specs=in_specs, out_specs=out_specs,
                             out_shape=out_shape, scratch_shapes=scratch_shapes,
                             compiler_params=_params(*semantics))(*args)
        return out, None
    n, n_in, n_out = rider.n, len(in_specs), len(out_specs)

    def riding(*refs):
        ins, r_in = refs[:n_in], refs[n_in:n_in + n]
        outs, r_out = refs[n_in + n:n_in + n + n_out], refs[n_in + n + n_out:n_in + 2 * n + n_out]
        scratch, sems = refs[n_in + 2 * n + n_out:-3], refs[-3:]

        @pl.when(first())
        def _():
            rider.start(r_in, r_out, sems)

        body(*ins, *outs, *scratch)

        @pl.when(last())
        def _():
            rider.wait(r_in, r_out, sems)

    out = pl.pallas_call(
        riding, name=name, grid=grid,
        in_specs=list(in_specs) + [HBM_SPEC] * n, out_specs=list(out_specs) + [HBM_SPEC] * n,
        out_shape=list(out_shape) + rider.out_shape(),
        scratch_shapes=list(scratch_shapes) + rider.scratch(),
        compiler_params=_params(*["arbitrary"] * len(grid)),
    )(*args, *rider.arrs)
    return out[:n_out], out[n_out:]


def _pack(arrs, row_multiple):
    parts = []
    rows = 0
    for a in arrs:
        flat = a.reshape(-1).astype(F32)
        r = -(-flat.shape[0] // (8 * LANES)) * 8
        parts.append(jnp.pad(flat, (0, r * LANES - flat.shape[0])).reshape(r, LANES))
        rows += r
    pad = -rows % row_multiple
    if pad:
        parts.append(jnp.zeros((pad, LANES), F32))
    return jnp.concatenate(parts, axis=0)


def _unpack(buf, shapes, lead=()):
    out = []
    r0 = 0
    for shape in shapes:
        size = 1
        for s in shape:
            size *= s
        r = -(-size // (8 * LANES)) * 8
        part = buf[..., r0:r0 + r, :].reshape(lead + (r * LANES,))[..., :size]
        out.append(part.reshape(lead + tuple(shape)))
        r0 += r
    return out


def _gathered_cols(g):
    g = jnp.moveaxis(g, 0, -2)
    return g.reshape(g.shape[:-2] + (g.shape[-2] * g.shape[-1],))


def kernel(x, mem, mem_norm, w_mem_kv, norm_a, w_in_a, conv_w, conv_b, w_rec_gate, b_rec_gate, w_in_gate, b_in_gate, lru_lambda, w_out_a, kv_norm, w_kv, norm_b, w_in_b, w_out_b, final_norm, loss_target, m_mem_norm, m_w_mem_kv, m_norm_a, m_w_in_a, m_conv_w, m_conv_b, m_w_rec_gate, m_b_rec_gate, m_w_in_gate, m_b_in_gate, m_lru_lambda, m_w_out_a, m_kv_norm, m_w_kv, m_norm_b, m_w_in_b, m_w_out_b, m_final_norm, v_mem_norm, v_w_mem_kv, v_norm_a, v_w_in_a, v_conv_w, v_conv_b, v_w_rec_gate, v_b_rec_gate, v_w_in_gate, v_b_in_gate, v_lru_lambda, v_w_out_a, v_kv_norm, v_w_kv, v_norm_b, v_w_in_b, v_w_out_b, v_final_norm):
    xs = x[0]
    T, D = xs.shape
    L = w_mem_kv.shape[0]
    NA = w_in_a.shape[0]
    NB = w_in_b.shape[0]
    DM2 = w_mem_kv.shape[2]
    DM = DM2 // 2
    DR = w_rec_gate.shape[1] * w_rec_gate.shape[2]
    me = _device_index(_mesh_position())

    small_sharded = [norm_a, conv_w, conv_b, b_rec_gate, b_in_gate, lru_lambda]
    shard = {f"mem_kv{l}": w_mem_kv[l] for l in range(L)}
    shard.update({f"in_a{l}": w_in_a[l] for l in range(NA)}, **{f"out_a{l}": w_out_a[l] for l in range(NA)})
    shard.update({f"in_b{j}": w_in_b[j] for j in range(NB)}, **{f"out_b{j}": w_out_b[j] for j in range(NB)})
    shard["kv"] = w_kv
    shard = {k: w.astype(BF16) for k, w in shard.items()}
    now = ["in_a0", "mem_kv0"]
    gathered = _all_gather([shard[k] for k in now] + [_pack(small_sharded, 8)], "gather_params")
    full = dict(zip(now, gathered[:-1]))

    def gather_rider(keys):
        return _Gather([shard[k] for k in keys]), keys

    def out_proj(key):
        return full[key].reshape(1, -1, D)

    def mem_proj(l):
        return full[f"mem_kv{l}"].reshape(1, D, DM2)

    norm_a_f, conv_w_f, conv_b_f, b_r_f, b_i_f, lam_f = [
        _gathered_cols(s) for s in _unpack(gathered[-1], [s.shape for s in small_sharded], lead=(N_DEV,))]
    w_r_bf = w_rec_gate.astype(BF16)
    w_i_bf = w_in_gate.astype(BF16)

    zeros_mem = jnp.zeros_like(mem[0])
    mem_n = _rms_fwd(mem[0], mem_norm, "rms_mem")
    mkv = [_mm_nn(mem_n, mem_proj(0), out_dtype=BF16, name="mm_mem_kv0")]

    h = xs
    saved_a = []
    for l in range(NA):
        u = _rms_fwd(h, norm_a_f[l], f"rms_a{l}")
        rider, keys = gather_rider([f"out_a{l}"])
        proj, landed = _mm_nn(u, full[f"in_a{l}"], name=f"mm_in_a{l}", rider=rider)
        full.update(zip(keys, landed))
        later_mem = [f"mem_kv{m}" for m in range(1, L)] if l == 0 else []
        rider, keys = gather_rider([f"in_a{l + 1}" if l + 1 < NA else "kv"] + later_mem)
        (y_rnn, xc, r, i, hr), landed = _rglru_fwd(proj, conv_w_f[l], conv_b_f[l], w_r_bf[l], b_r_f[l], w_i_bf[l],
                                                   b_i_f[l], lam_f[l], f"rglru_fwd{l}", rider)
        full.update(zip(keys, landed))
        if l == 0:
            mkv += [_mm_nn(mem_n, mem_proj(m), out_dtype=BF16, name=f"mm_mem_kv{m}") for m in range(1, L)]
        y_mem = _memattn_fwd(proj, mkv[l], DR, DM, f"memattn_fwd_a{l}")
        ycat = jnp.concatenate([y_rnn, y_mem], axis=-1)
        h_next = _mm_nn(ycat, out_proj(f"out_a{l}"), res=h, name=f"mm_out_a{l}")
        saved_a.append((h, u, proj, xc, r, i, hr, ycat))
        h = h_next
    h_kv = h
    u_kv = _rms_fwd(h_kv, kv_norm, "rms_kv")
    rider, keys = gather_rider(["in_b0"])
    kv, landed = _mm_nn(u_kv, full["kv"], out_dtype=BF16, name="mm_kv", rider=rider)
    full.update(zip(keys, landed))
    saved_b = []
    for j in range(NB):
        u = _rms_fwd(h, norm_b[j], f"rms_b{j}")
        proj = _mm_nn(u, full[f"in_b{j}"], name=f"mm_in_b{j}")
        rider, keys = gather_rider([f"out_b{j}"] + ([f"in_b{j + 1}"] if j + 1 < NB else []))
        (y_sb, o_sb), landed = _sb_fwd(proj, kv, DR, f"sb_fwd{j}", rider)
        full.update(zip(keys, landed))
        y_mem = _memattn_fwd(proj, mkv[NA + j], DR, DM, f"memattn_fwd_b{j}")
        ycat = jnp.concatenate([y_sb, y_mem], axis=-1)
        h_next = _mm_nn(ycat, out_proj(f"out_b{j}"), res=h, name=f"mm_out_b{j}")
        saved_b.append((h, u, proj, o_sb, ycat))
        h = h_next

    dh, d_final_norm, sq = _final_loss(h, final_norm, loss_target[0], "final_loss")
    loss = lax.psum(0.5 * jnp.sum(sq) / D, ("x", "y", "c"))

    big_grads = {}
    received = {}

    def scatter_rider(keys):
        return _Scatter([big_grads[k] for k in keys]), keys

    dmkv = [None] * L
    d_norm_b = [None] * NB
    dkv_parts = []
    for j in reversed(range(NB)):
        h_in, u, proj, o_sb, ycat = saved_b[j]
        dy = _mm_nt(dh, out_proj(f"out_b{j}"), name=f"mm_dy_b{j}")
        big_grads[f"out_b{j}"] = _mm_tn(ycat, dh, 1, name=f"mm_dw_out_b{j}").reshape(N_DEV, -1, D)
        rider, keys = scatter_rider(([f"in_b{j + 1}"] if j + 1 < NB else []) + [f"out_b{j}"])
        (dq, dg, dk, dv), landed = _sb_bwd(dy, proj, kv, o_sb, DR, f"sb_bwd{j}", rider)
        received.update(zip(keys, landed))
        dkv_parts.append((dk, dv))
        dqg_mem, dmkv[NA + j] = _memattn_bwd(dy, proj, mkv[NA + j], DR, DM, f"memattn_bwd_b{j}")
        dproj = jnp.concatenate([dq, dg, dqg_mem], axis=-1)
        du = _mm_nt(dproj, full[f"in_b{j}"], name=f"mm_du_b{j}")
        big_grads[f"in_b{j}"] = _mm_tn(u, dproj, N_DEV, name=f"mm_dw_in_b{j}")
        dh, d_norm_b[j] = _rms_bwd(du, h_in, norm_b[j], dh, f"rms_bwd_b{j}")

    dkv = _merge_dkv(dkv_parts, "merge_dkv")
    du_kv = _mm_nt(dkv, full["kv"], name="mm_du_kv")
    big_grads["kv"] = _mm_tn(u_kv, dkv, N_DEV, name="mm_dw_kv")
    dh, d_kv_norm = _rms_bwd(du_kv, h_kv, kv_norm, dh, "rms_bwd_kv")

    d_norm_a, d_conv_w, d_conv_b, d_w_r, d_b_r, d_w_i, d_b_i, d_lam = ([None] * NA for _ in range(8))
    for l in reversed(range(NA)):
        h_in, u, proj, xc, r, i, hr, ycat = saved_a[l]
        if l == NA - 1:
            rider, keys = scatter_rider(["kv"])
            dy, landed = _mm_nt(dh, out_proj(f"out_a{l}"), name=f"mm_dy_a{l}", rider=rider)
            received.update(zip(keys, landed))
        else:
            dy = _mm_nt(dh, out_proj(f"out_a{l}"), name=f"mm_dy_a{l}")
        big_grads[f"out_a{l}"] = _mm_tn(ycat, dh, 1, name=f"mm_dw_out_a{l}").reshape(N_DEV, -1, D)
        rider, keys = scatter_rider([f"in_a{l + 1}" if l + 1 < NA else "in_b0", f"out_a{l}"])
        (dxg, d_conv_w[l], d_conv_b[l], d_w_r[l], d_b_r[l], d_w_i[l], d_b_i[l], d_lam[l]), landed = _rglru_bwd(
            dy, proj, xc, r, i, hr, conv_w_f[l], w_r_bf[l], w_i_bf[l], lam_f[l], f"rglru_bwd{l}", rider)
        received.update(zip(keys, landed))
        dqg_mem, dmkv[l] = _memattn_bwd(dy, proj, mkv[l], DR, DM, f"memattn_bwd_a{l}")
        dproj = jnp.concatenate([dxg, dqg_mem], axis=-1)
        if l > 0:
            du = _mm_nt(dproj, full[f"in_a{l}"], name=f"mm_du_a{l}")
            big_grads[f"in_a{l}"] = _mm_tn(u, dproj, N_DEV, name=f"mm_dw_in_a{l}")
        else:
            dmkv_all = jnp.concatenate(dmkv, axis=-1)
            dmem_n = _mm_nt(dmkv_all, jnp.concatenate([mem_proj(m) for m in range(L)], axis=0), name="mm_dmem")
            for m in range(L):
                big_grads[f"mem_kv{m}"] = _mm_tn(mem_n, dmkv[m], 1, name=f"mm_dw_mem_kv{m}").reshape(N_DEV, -1, DM2)
            _, d_mem_norm = _rms_bwd(dmem_n, mem[0], mem_norm, zeros_mem, "rms_bwd_mem")
            rider, keys = scatter_rider([f"mem_kv{m}" for m in range(L)])
            du, landed = _mm_nt(dproj, full["in_a0"], name="mm_du_a0", rider=rider)
            received.update(zip(keys, landed))
            small_early = {
                "mem_norm": d_mem_norm.reshape(-1),
                "conv_w": jnp.stack(d_conv_w),
                "conv_b": jnp.concatenate(d_conv_b, axis=0),
                "w_rec_gate": jnp.stack(d_w_r),
                "b_rec_gate": jnp.concatenate(d_b_r, axis=0),
                "w_in_gate": jnp.stack(d_w_i),
                "b_in_gate": jnp.concatenate(d_b_i, axis=0),
                "lru_lambda": jnp.concatenate(d_lam, axis=0),
                "kv_norm": d_kv_norm.reshape(-1),
                "norm_b": jnp.concatenate(d_norm_b, axis=0),
                "final_norm": d_final_norm.reshape(-1),
            }
            rider = _Gather([_pack(list(small_early.values()), 256)])
            big_grads["in_a0"], (early_gathered,) = _mm_tn(u, dproj, N_DEV, name="mm_dw_in_a0", rider=rider)
        dh, d_norm_a[l] = _rms_bwd(du, h_in, norm_a_f[l], dh, f"rms_bwd_a{l}")
    grad_x = dh.reshape(x.shape)

    received.update(zip(["in_a0"], _all_to_all([big_grads["in_a0"]], "scatter_grads")))
    small_late = {"norm_a": jnp.concatenate(d_norm_a, axis=0)}
    (late_gathered,) = _all_gather([_pack(list(small_late.values()), 256)], "gather_small_grads")

    def update(key, w, m, v):
        shape = w.shape
        two_d = (-1, shape[-1])
        return [o.reshape(shape) for o in _adamw(received[key], w.reshape(two_d), m.reshape(two_d),
                                                 v.reshape(two_d), f"adamw_{key}")]

    def update_layers(prefix, w, m, v):
        per_layer = [update(f"{prefix}{l}", w[l], m[l], v[l]) for l in range(w.shape[0])]
        return [jnp.stack([per_layer[l][k] for l in range(w.shape[0])]) for k in range(4)]

    upd = {
        "w_mem_kv": update_layers("mem_kv", w_mem_kv, m_w_mem_kv, v_w_mem_kv),
        "w_in_a": update_layers("in_a", w_in_a, m_w_in_a, v_w_in_a),
        "w_out_a": update_layers("out_a", w_out_a, m_w_out_a, v_w_out_a),
        "w_kv": update("kv", w_kv, m_w_kv, v_w_kv),
        "w_in_b": update_layers("in_b", w_in_b, m_w_in_b, v_w_in_b),
        "w_out_b": update_layers("out_b", w_out_b, m_w_out_b, v_w_out_b),
    }

    small_grad = {}
    for part, got, name in ((small_early, early_gathered, "sum_small_early"), (small_late, late_gathered, "sum_small_late")):
        summed = _unpack(_sum_devices(got, name), [g.shape for g in part.values()])
        small_grad.update(zip(part, summed))
    small_names = list(small_grad)
    small_w = {"mem_norm": (mem_norm, m_mem_norm, v_mem_norm), "norm_a": (norm_a, m_norm_a, v_norm_a),
               "conv_w": (conv_w, m_conv_w, v_conv_w), "conv_b": (conv_b, m_conv_b, v_conv_b),
               "w_rec_gate": (w_rec_gate, m_w_rec_gate, v_w_rec_gate),
               "b_rec_gate": (b_rec_gate, m_b_rec_gate, v_b_rec_gate),
               "w_in_gate": (w_in_gate, m_w_in_gate, v_w_in_gate), "b_in_gate": (b_in_gate, m_b_in_gate, v_b_in_gate),
               "lru_lambda": (lru_lambda, m_lru_lambda, v_lru_lambda), "kv_norm": (kv_norm, m_kv_norm, v_kv_norm),
               "norm_b": (norm_b, m_norm_b, v_norm_b), "final_norm": (final_norm, m_final_norm, v_final_norm)}
    for k in small_names:
        w = small_w[k][0]
        if small_grad[k].shape != w.shape:
            n = w.shape[-1]
            small_grad[k] = lax.dynamic_slice_in_dim(small_grad[k], me * n, n, axis=-1)
    small_shapes = [small_w[k][0].shape for k in small_names]
    packed = [_pack([small_grad[k] for k in small_names], 256)[None]]
    packed += [_pack([small_w[k][t] for k in small_names], 256) for t in range(3)]
    small_out = [_unpack(o, small_shapes) for o in _adamw(*packed, "adamw_small")]
    for idx, k in enumerate(small_names):
        upd[k] = [small_out[t][idx] for t in range(4)]

    order = ["mem_norm", "w_mem_kv", "norm_a", "w_in_a", "conv_w", "conv_b", "w_rec_gate", "b_rec_gate", "w_in_gate",
             "b_in_gate", "lru_lambda", "w_out_a", "kv_norm", "w_kv", "norm_b", "w_in_b", "w_out_b", "final_norm"]
    return (loss, grad_x, *[upd[k][0] for k in order], *[upd[k][1] for k in order],
            *[upd[k][2] for k in order], *[upd[k][3] for k in order])
```

```python
import functools

import jax
import jax.numpy as jnp
from jax import lax
from jax.experimental import pallas as pl
from jax.experimental.pallas import tpu as pltpu

F32 = jnp.float32
BF16 = jnp.bfloat16

N_DEV = 8
EPS = 1e-6
LRU_C = 8.0
HEAD = 128
CONV_WIDTH = 4
LANES = 128
SUB = 8
SB_EXHAUSTED = 110.0
VMEM_LIMIT = 56 * 1024 * 1024
RESIDENT_WEIGHT_BYTES = 8 * 1024 * 1024

ADAM_LR = 0.001
ADAM_B1 = 0.9
ADAM_B2 = 0.999
ADAM_EPS = 1e-08
ADAM_WD = 0.01
ADAM_STEP = 10

HBM_SPEC = pl.BlockSpec(memory_space=pltpu.HBM)


def _params(*semantics):
    return pltpu.CompilerParams(dimension_semantics=semantics, vmem_limit_bytes=VMEM_LIMIT)


def _tile(n, cap):
    if n <= cap:
        return n
    t = cap - cap % LANES
    while n % t:
        t -= LANES
    return t


def _row_tile(n, cap):
    if n <= cap:
        return n
    t = cap - cap % 8
    while t >= 8:
        if n % t == 0:
            return t
        t -= 8
    return n


def _rows_for(n, bytes_per_row, budget=2 * 1024 * 1024):
    return _row_tile(n, max(8, budget // bytes_per_row))


def _dot(a, b):
    return lax.dot_general(a, b, (((1,), (0,)), ((), ())), preferred_element_type=F32)


def _dot_nt(a, b):
    return lax.dot_general(a, b, (((1,), (1,)), ((), ())), preferred_element_type=F32)


def _dot_tn(a, b):
    return lax.dot_general(a, b, (((0,), (0,)), ((), ())), preferred_element_type=F32)


def _split_bf16(x):
    hi = x.astype(BF16)
    lo = (x - hi.astype(F32)).astype(BF16)
    return hi, lo


def _softplus(x):
    return jnp.maximum(x, 0.0) + jnp.log1p(jnp.exp(-jnp.abs(x)))


def _gate_sigmoid(x):
    return 0.5 * jnp.tanh(0.5 * x) + 0.5


def _silu_and_grad(g):
    sg = _gate_sigmoid(g)
    return g * sg, sg * (1.0 + g * (1.0 - sg))


def _one_minus_square(a, log_a):
    x = 2.0 * log_a
    series = -x * (1.0 + x * (0.5 + x * (1.0 / 6.0)))
    return jnp.where(x > -0.03, series, 1.0 - a * a)


def _rms_fwd(x, g, name):
    T, D = x.shape
    tt = _row_tile(T, 512)

    def body(x_ref, g_ref, o_ref):
        xv = x_ref[...]
        r = lax.rsqrt(jnp.mean(xv * xv, axis=-1, keepdims=True) + EPS)
        o_ref[...] = ((xv * r) * g_ref[...]).astype(o_ref.dtype)

    return pl.pallas_call(
        body, name=name, grid=(T // tt,),
        in_specs=[pl.BlockSpec((tt, D), lambda i: (i, 0)), pl.BlockSpec((1, D), lambda i: (0, 0))],
        out_specs=pl.BlockSpec((tt, D), lambda i: (i, 0)),
        out_shape=jax.ShapeDtypeStruct((T, D), BF16),
        compiler_params=_params("parallel"),
    )(x, g.reshape(1, D))


def _rms_bwd(du, h, g, res, name):
    T, D = h.shape
    tt = _row_tile(T, 256)

    def body(du_ref, h_ref, g_ref, res_ref, dh_ref, dg_ref):
        @pl.when(pl.program_id(0) == 0)
        def _():
            dg_ref[...] = jnp.zeros_like(dg_ref)

        hv = h_ref[...]
        duv = du_ref[...]
        r = lax.rsqrt(jnp.mean(hv * hv, axis=-1, keepdims=True) + EPS)
        xhat = hv * r
        dxhat = duv * g_ref[...]
        dh_ref[...] = res_ref[...] + r * (dxhat - xhat * jnp.mean(dxhat * xhat, axis=-1, keepdims=True))
        dg_ref[...] += jnp.sum(duv * xhat, axis=0, keepdims=True)

    row = pl.BlockSpec((tt, D), lambda i: (i, 0))
    vec = pl.BlockSpec((1, D), lambda i: (0, 0))
    return pl.pallas_call(
        body, name=name, grid=(T // tt,),
        in_specs=[row, row, vec, row],
        out_specs=[row, vec],
        out_shape=[jax.ShapeDtypeStruct((T, D), F32), jax.ShapeDtypeStruct((1, D), F32)],
        compiler_params=_params("arbitrary"),
    )(du, h, g.reshape(1, D), res)


def _mm_nn(a, b3, res=None, out_dtype=F32, name=None, rider=None, norms=()):
    M, K = a.shape
    S, _, ns = b3.shape
    whole_b = S == 1 and K * ns * 2 <= RESIDENT_WEIGHT_BYTES
    tm = _row_tile(M, 512 if whole_b or a.dtype != BF16 else 1024)
    tn = ns if whole_b else _tile(ns, 512)
    nj = ns // tn

    assert not norms or (whole_b and rider is None)
    n_in = 2 + (res is not None) + len(norms)

    def body(*refs):
        a_ref, b_ref = refs[:2]
        o_ref = refs[n_in]
        acc = _dot(a_ref[...].astype(BF16), b_ref[...])
        if res is not None:
            acc = acc + refs[2][...]
        o_ref[...] = acc.astype(o_ref.dtype)
        if norms:
            r = lax.rsqrt(jnp.mean(acc * acc, axis=-1, keepdims=True) + EPS)
            for g_ref, u_ref in zip(refs[n_in - len(norms):n_in], refs[n_in + 1:]):
                u_ref[...] = ((acc * r) * g_ref[...]).astype(u_ref.dtype)

    tile = pl.BlockSpec((tm, tn), lambda i, j: (i, j))
    in_specs = [pl.BlockSpec((tm, K), lambda i, j: (i, 0)),
                pl.BlockSpec((None, K, tn), lambda i, j: (j // nj, 0, j % nj))]
    args = [a, b3]
    if res is not None:
        in_specs.append(tile)
        args.append(res)
    in_specs += [pl.BlockSpec((1, tn), lambda i, j: (0, j))] * len(norms)
    args += [g.reshape(1, -1) for g in norms]
    outs, landed = _call_with_rider(
        body, rider, *_grid_ends((M // tm, S * nj)), name=name, grid=(M // tm, S * nj),
        in_specs=in_specs,
        out_specs=[tile] * (1 + len(norms)),
        out_shape=[jax.ShapeDtypeStruct((M, S * ns), out_dtype)] + [jax.ShapeDtypeStruct((M, S * ns), BF16)] * len(norms),
        scratch_shapes=[], semantics=("parallel", "parallel"), args=args)
    if norms:
        return tuple(outs)
    return outs[0] if rider is None else (outs[0], landed)


def _grid_ends(grid):
    def first():
        return functools.reduce(jnp.logical_and, [pl.program_id(d) == 0 for d in range(len(grid))])

    def last():
        return functools.reduce(jnp.logical_and, [pl.program_id(d) == n - 1 for d, n in enumerate(grid)])

    return first, last


def _mm_nt(a, b3, out_dtype=F32, name=None, rider=None):
    M = a.shape[0]
    S, N, ns = b3.shape
    whole_b = S * N * ns * 2 <= RESIDENT_WEIGHT_BYTES
    tm = _row_tile(M, 512 if whole_b or a.dtype != BF16 else 1024)
    tn = N if whole_b else _tile(N, 512)

    def body(a_ref, b_ref, o_ref):
        acc = _dot_nt(a_ref[:, 0:ns].astype(BF16), b_ref[0])
        for s in range(1, S):
            acc = acc + _dot_nt(a_ref[:, s * ns:(s + 1) * ns].astype(BF16), b_ref[s])
        o_ref[...] = acc.astype(o_ref.dtype)

    (out,), landed = _call_with_rider(
        body, rider, *_grid_ends((M // tm, N // tn)), name=name, grid=(M // tm, N // tn),
        in_specs=[pl.BlockSpec((tm, S * ns), lambda i, j: (i, 0)),
                  pl.BlockSpec((S, tn, ns), lambda i, j: (0, j, 0))],
        out_specs=[pl.BlockSpec((tm, tn), lambda i, j: (i, j))],
        out_shape=[jax.ShapeDtypeStruct((M, N), out_dtype)],
        scratch_shapes=[], semantics=("parallel", "parallel"), args=(a, b3))
    return out if rider is None else (out, landed)


def _mm_tn(a, b, S, name=None, rider=None):
    T, K = a.shape
    ns = b.shape[1] // S
    tk = _tile(K, 1024)
    tn = _tile(ns, 512)
    nj = ns // tn
    tt = _row_tile(T, 2048)
    nt = T // tt

    def body(a_ref, b_ref, o_ref, acc_ref):
        t = pl.program_id(2)
        p = _dot_tn(a_ref[...].astype(BF16), b_ref[...].astype(BF16))

        @pl.when(t == 0)
        def _():
            acc_ref[...] = p

        @pl.when(t > 0)
        def _():
            acc_ref[...] += p

        @pl.when(t == nt - 1)
        def _():
            o_ref[...] = acc_ref[...].astype(o_ref.dtype)

    grid = (K // tk, S * nj, nt)
    (out,), landed = _call_with_rider(
        body, rider, *_grid_ends(grid), name=name, grid=grid,
        in_specs=[pl.BlockSpec((tt, tk), lambda i, j, t: (t, i)),
                  pl.BlockSpec((tt, tn), lambda i, j, t: (t, j))],
        out_specs=[pl.BlockSpec((None, tk, tn), lambda i, j, t: (j // nj, i, j % nj))],
        out_shape=[jax.ShapeDtypeStruct((S, K, ns), BF16)],
        scratch_shapes=[pltpu.VMEM((tk, tn), F32)],
        semantics=("parallel", "parallel", "arbitrary"), args=(a, b))
    return out if rider is None else (out, landed)


def _lru_coeffs(lam_row, r):
    cl = -LRU_C * _softplus(-lam_row)
    log_a = cl * r
    a = jnp.exp(log_a)
    em = _one_minus_square(a, log_a)
    return cl, a, em, jnp.sqrt(em)


def _rglru_fwd(proj, conv_w, conv_b, w_r, b_r, w_i, b_i, lam, name, rider=None, y_spare=0):
    T = proj.shape[0]
    DR = conv_w.shape[1]
    NB = DR // HEAD
    tt = _row_tile(T, 256)
    PAD = 8

    def body(x_ref, g_ref, cw_ref, cb_ref, wr_ref, br_ref, wi_ref, bi_ref, lam_ref,
             y_ref, xc_ref, r_ref, i_ref, h_ref, xbuf, hcar):
        @pl.when(pl.program_id(0) == 0)
        def _():
            xbuf[0:PAD, :] = jnp.zeros((PAD, DR), F32)
            hcar[...] = jnp.zeros_like(hcar)

        xbuf[PAD:PAD + tt, :] = x_ref[...]
        xc = cb_ref[...] + cw_ref[0:1, :] * xbuf[pl.ds(PAD - 3, tt), :]
        for k in range(1, CONV_WIDTH):
            xc = xc + cw_ref[k:k + 1, :] * xbuf[pl.ds(PAD - 3 + k, tt), :]
        xbuf[0:PAD, :] = xbuf[tt:tt + PAD, :]
        xc_ref[...] = xc
        xcb = xc.astype(BF16)
        for n in range(NB):
            sl = slice(n * HEAD, (n + 1) * HEAD)
            r_ref[:, sl] = jax.nn.sigmoid(_dot(xcb[:, sl], wr_ref[n]) + br_ref[:, sl])
            i_ref[:, sl] = _gate_sigmoid(_dot(xcb[:, sl], wi_ref[n]) + bi_ref[:, sl])
        _, a, _, mult = _lru_coeffs(lam_ref[...], r_ref[...])
        hs = mult * (i_ref[...] * xc)
        groups = tt // SUB
        a = a.reshape(groups, SUB, DR)
        hs = hs.reshape(groups, SUB, DR)
        sub = lax.broadcasted_iota(jnp.int32, (groups, SUB, DR), 1)
        d = 1
        while d < SUB:
            keep = sub >= d
            a_sh = jnp.where(keep, pltpu.roll(a, d, 1), 1.0)
            h_sh = jnp.where(keep, pltpu.roll(hs, d, 1), 0.0)
            hs = a * h_sh + hs
            a = a * a_sh
            d *= 2
        carry = hcar[...]
        for n in range(groups):
            h_ref[n * SUB:(n + 1) * SUB, :] = hs[n] + a[n] * carry
            carry = h_ref[(n + 1) * SUB - 1:(n + 1) * SUB, :]
        hcar[...] = carry
        h = h_ref[...]
        g = g_ref[...]
        y_ref[...] = (h * (g * _gate_sigmoid(g))).astype(y_ref.dtype)

    col = lambda j: pl.BlockSpec((tt, DR), lambda c: (c, j))
    vec = pl.BlockSpec((1, DR), lambda c: (0, 0))
    gate = pl.BlockSpec((NB, HEAD, HEAD), lambda c: (0, 0, 0))
    f32_out = jax.ShapeDtypeStruct((T, DR), F32)
    nc = T // tt
    return _call_with_rider(
        body, rider, lambda: pl.program_id(0) == 0, lambda: pl.program_id(0) == nc - 1,
        name=name, grid=(nc,),
        in_specs=[col(0), col(1), pl.BlockSpec((CONV_WIDTH, DR), lambda c: (0, 0)), vec, gate, vec, gate, vec, vec],
        out_specs=[col(0)] * 5,
        out_shape=[jax.ShapeDtypeStruct((T, DR + y_spare), BF16), f32_out, f32_out, f32_out, f32_out],
        scratch_shapes=[pltpu.VMEM((tt + PAD, DR), F32), pltpu.VMEM((1, DR), F32)],
        semantics=("arbitrary",),
        args=(proj, proj, conv_w, conv_b.reshape(1, DR), w_r, b_r.reshape(1, DR), w_i, b_i.reshape(1, DR),
              lam.reshape(1, DR)))


def _rglru_bwd(dy, proj, xc, r, i, h, conv_w, w_r, w_i, lam, name, rider=None):
    T = proj.shape[0]
    DR = conv_w.shape[1]
    NB = DR // HEAD
    tt = _row_tile(T, 128)
    nc = T // tt
    PAD = 8
    per = tt // PAD

    def body(dy_ref, x_ref, g_ref, xc_ref, r_ref, i_ref, h_ref, xprev_ref, hprev_ref,
             cw_ref, wr_ref, wi_ref, lam_ref,
             dxg_ref, dcw_ref, dcb_ref, dwr_ref, dbr_ref, dwi_ref, dbi_ref, dlam_ref,
             xbuf, dxcbuf, gcar, acar):
        step = pl.program_id(0)
        chunk = nc - 1 - step

        @pl.when(step == 0)
        def _():
            for ref in (dcw_ref, dcb_ref, dwr_ref, dbr_ref, dwi_ref, dbi_ref, dlam_ref, gcar, acar):
                ref[...] = jnp.zeros_like(ref)
            dxcbuf[tt:tt + PAD, :] = jnp.zeros((PAD, DR), F32)

        not_first = (chunk > 0).astype(F32)
        row = lax.broadcasted_iota(jnp.int32, (tt, DR), 0)
        silu, dsilu = _silu_and_grad(g_ref[...])
        dyv = dy_ref[...]
        hv = h_ref[...]
        dxg_ref[:, DR:2 * DR] = (dyv * hv * dsilu).astype(dxg_ref.dtype)
        dh = dyv * silu
        rv = r_ref[...]
        iv = i_ref[...]
        xcv = xc_ref[...]
        lam_row = lam_ref[...]
        cl, a, em, mult = _lru_coeffs(lam_row, rv)
        b = jnp.where(row == tt - 1, acar[...], pltpu.roll(a, tt - 1, 0))
        gs = dh
        d = 1
        while d < tt:
            keep = row < tt - d
            b_sh = jnp.where(keep, pltpu.roll(b, tt - d, 0), 1.0)
            g_sh = jnp.where(keep, pltpu.roll(gs, tt - d, 0), 0.0)
            gs = gs + b * g_sh
            b = b * b_sh
            d *= 2
        gt = gs + b * gcar[...]
        xbuf[0:tt, :] = gt
        gcar[...] = xbuf[0:1, :]
        acar[...] = _lru_coeffs(lam_row, r_ref[0:1, :])[1]
        h_before = hprev_ref[PAD - 1:PAD, :] * not_first
        hprev = jnp.where(row == 0, h_before, pltpu.roll(hv, 1, 0))
        da = gt * hprev
        dmult = gt * (iv * xcv)
        di = gt * mult * xcv
        dxc = gt * mult * iv
        dlog_a = da * a - dmult * (1.0 - em) / mult
        dr = dlog_a * cl
        dlam_ref[...] += jnp.sum(dlog_a * rv, axis=0, keepdims=True) * (LRU_C * jax.nn.sigmoid(-lam_row))
        drp = dr * rv * (1.0 - rv)
        dip = di * iv * (1.0 - iv)
        dbr_ref[...] += jnp.sum(drp, axis=0, keepdims=True)
        dbi_ref[...] += jnp.sum(dip, axis=0, keepdims=True)
        drpb = drp.astype(BF16)
        dipb = dip.astype(BF16)
        xcb = xcv.astype(BF16)
        for n in range(NB):
            sl = slice(n * HEAD, (n + 1) * HEAD)
            dxcbuf[0:tt, sl] = dxc[:, sl] + _dot_nt(drpb[:, sl], wr_ref[n]) + _dot_nt(dipb[:, sl], wi_ref[n])
            dwr_ref[n] += _dot_tn(xcb[:, sl], drpb[:, sl])
            dwi_ref[n] += _dot_tn(xcb[:, sl], dipb[:, sl])
        dxc_all = dxcbuf[0:tt, :]
        dcb_ref[...] += jnp.sum(dxc_all, axis=0, keepdims=True)
        xbuf[0:PAD, :] = xprev_ref[...] * not_first
        xbuf[PAD:PAD + tt, :] = x_ref[...]
        dx = cw_ref[0:1, :] * dxcbuf[pl.ds(3, tt), :]
        for k in range(1, CONV_WIDTH):
            dx = dx + cw_ref[k:k + 1, :] * dxcbuf[pl.ds(3 - k, tt), :]
        dxg_ref[:, 0:DR] = dx.astype(dxg_ref.dtype)
        for k in range(CONV_WIDTH):
            dcw_ref[k:k + 1, :] += jnp.sum(xbuf[pl.ds(PAD - 3 + k, tt), :] * dxc_all, axis=0, keepdims=True)
        dxcbuf[tt:tt + PAD, :] = dxcbuf[0:PAD, :]

    rev = lambda j: pl.BlockSpec((tt, DR), lambda s: (nc - 1 - s, j))
    prev = pl.BlockSpec((PAD, DR), lambda s: (jnp.maximum((nc - 1 - s) * per - 1, 0), 0))
    vec = pl.BlockSpec((1, DR), lambda s: (0, 0))
    gate = pl.BlockSpec((NB, HEAD, HEAD), lambda s: (0, 0, 0))
    taps = pl.BlockSpec((CONV_WIDTH, DR), lambda s: (0, 0))
    vec_out = jax.ShapeDtypeStruct((1, DR), F32)
    gate_out = jax.ShapeDtypeStruct((NB, HEAD, HEAD), F32)
    return _call_with_rider(
        body, rider, lambda: pl.program_id(0) == 0, lambda: pl.program_id(0) == nc - 1,
        name=name, grid=(nc,),
        in_specs=[rev(0), rev(0), rev(1), rev(0), rev(0), rev(0), rev(0), prev, prev, taps, gate, gate, vec],
        out_specs=[pl.BlockSpec((tt, 2 * DR), lambda s: (nc - 1 - s, 0)), taps, vec, gate, vec, gate, vec, vec],
        out_shape=[jax.ShapeDtypeStruct((T, 2 * DR), BF16), jax.ShapeDtypeStruct((CONV_WIDTH, DR), F32),
                   vec_out, gate_out, vec_out, gate_out, vec_out, vec_out],
        scratch_shapes=[pltpu.VMEM((tt + PAD, DR), F32), pltpu.VMEM((tt + PAD, DR), F32),
                        pltpu.VMEM((1, DR), F32), pltpu.VMEM((1, DR), F32)],
        semantics=("arbitrary",),
        args=(dy, proj, proj, xc, r, i, h, proj, h, conv_w, w_r, w_i, lam.reshape(1, DR)))


def _mem_probs(q, k, scale):
    s = _dot_nt(q, k) * scale
    p = jnp.exp(s - jnp.max(s, axis=-1, keepdims=True))
    return p * (1.0 / jnp.sum(p, axis=-1, keepdims=True))


def _memattn_fwd(proj, mkv, y_mix, DR, DM, name):
    T = proj.shape[0]
    M = mkv.shape[0]
    NH = DM // HEAD
    tt = _row_tile(T, 512)
    qcol = 2 * DR // DM
    scale = HEAD ** -0.5

    def body(q_ref, g_ref, k_ref, v_ref, _, y_ref):
        for n in range(NH):
            sl = slice(n * HEAD, (n + 1) * HEAD)
            p = _mem_probs(q_ref[:, sl].astype(BF16), k_ref[:, sl], scale)
            o = _dot(p.astype(BF16), v_ref[:, sl])
            g = g_ref[:, sl]
            y_ref[:, sl] = (o * (g * _gate_sigmoid(g))).astype(y_ref.dtype)

    return pl.pallas_call(
        body, name=name, grid=(T // tt,),
        in_specs=[pl.BlockSpec((tt, DM), lambda t: (t, qcol)), pl.BlockSpec((tt, DM), lambda t: (t, qcol + 1)),
                  pl.BlockSpec((M, DM), lambda t: (0, 0)), pl.BlockSpec((M, DM), lambda t: (0, 1)), HBM_SPEC],
        out_specs=pl.BlockSpec((tt, DM), lambda t: (t, DR // DM)),
        out_shape=jax.ShapeDtypeStruct(y_mix.shape, y_mix.dtype),
        input_output_aliases={4: 0},
        compiler_params=_params("parallel"),
    )(proj, proj, mkv, mkv, y_mix)


def _memattn_bwd(dy, proj, mkv, DR, DM, name):
    T = proj.shape[0]
    M = mkv.shape[0]
    NH = DM // HEAD
    tt = _row_tile(T, 512)
    qcol = 2 * DR // DM
    scale = HEAD ** -0.5

    def body(dy_ref, q_ref, g_ref, k_ref, v_ref, dqg_ref, dkv_ref):
        @pl.when(pl.program_id(0) == 0)
        def _():
            dkv_ref[...] = jnp.zeros_like(dkv_ref)

        for n in range(NH):
            sl = slice(n * HEAD, (n + 1) * HEAD)
            qb = q_ref[:, sl].astype(BF16)
            kb = k_ref[:, sl]
            vb = v_ref[:, sl]
            p = _mem_probs(qb, kb, scale)
            pb = p.astype(BF16)
            o = _dot(pb, vb)
            silu, dsilu = _silu_and_grad(g_ref[:, sl])
            dyv = dy_ref[:, sl]
            dqg_ref[:, DM + n * HEAD:DM + (n + 1) * HEAD] = (dyv * o * dsilu).astype(dqg_ref.dtype)
            dob = (dyv * silu).astype(BF16)
            dp = _dot_nt(dob, vb)
            ds = (p * (dp - jnp.sum(dp * p, axis=-1, keepdims=True)) * scale).astype(BF16)
            dqg_ref[:, sl] = _dot(ds, kb).astype(dqg_ref.dtype)
            dkv_ref[:, sl] += _dot_tn(ds, qb)
            dkv_ref[:, DM + n * HEAD:DM + (n + 1) * HEAD] += _dot_tn(pb, dob)

    return pl.pallas_call(
        body, name=name, grid=(T // tt,),
        in_specs=[pl.BlockSpec((tt, DM), lambda t: (t, DR // DM)),
                  pl.BlockSpec((tt, DM), lambda t: (t, qcol)), pl.BlockSpec((tt, DM), lambda t: (t, qcol + 1)),
                  pl.BlockSpec((M, DM), lambda t: (0, 0)), pl.BlockSpec((M, DM), lambda t: (0, 1))],
        out_specs=[pl.BlockSpec((tt, 2 * DM), lambda t: (t, 0)), pl.BlockSpec((M, 2 * DM), lambda t: (0, 0))],
        out_shape=[jax.ShapeDtypeStruct((T, 2 * DM), BF16), jax.ShapeDtypeStruct((M, 2 * DM), F32)],
        compiler_params=_params("arbitrary"),
    )(dy, proj, proj, mkv, mkv)


def _sb_blocks(T):
    tk = _row_tile(T // 2, 256)
    tq = 2 * tk
    assert T % tq == 0
    return tq, tk


def _sb_upper(tk):
    row = lax.broadcasted_iota(jnp.int32, (tk, tk), 0)
    col = lax.broadcasted_iota(jnp.int32, (tk, tk), 1)
    return (row > col).astype(BF16)


def _sb_causal(tq, tk, d):
    row = lax.broadcasted_iota(jnp.int32, (tq, tk), 0)
    col = lax.broadcasted_iota(jnp.int32, (tq, tk), 1)
    return col + d * tk < row


def _sb_own_span(tile, carry, i, tq, tk):
    span = tq // tk
    for d in reversed(range(span)):
        lo = d * tk
        part = tile(span * i + d, tuple(c[lo:] for c in carry), _sb_causal(tq - lo, tk, 0), slice(lo, tq))
        carry = tuple(p if lo == 0 else jnp.concatenate([c[:lo], p], axis=0) for c, p in zip(carry, part))
    return carry


def _sb_earlier(tile, carry, jb_first, tq):
    half = tq // 2

    def walk(carry, jb, rows, watched):
        def more(c):
            return jnp.logical_and(c[0] >= 0, c[1] > 0)

        def step(c):
            jb, _, *rest = c
            rest = tile(jb, tuple(rest), None, rows)
            return jb - 1, _sb_alive(watched(rest[0])), *rest

        jb, _, *carry = lax.while_loop(more, step, (jb, _sb_alive(watched(carry[0])), *carry))
        return jb, tuple(carry)

    jb, carry = walk(carry, jb_first, slice(0, tq), lambda gone: gone[half:])
    _, low = walk(tuple(c[:half] for c in carry), jb, slice(0, half), lambda gone: gone)
    return tuple(jnp.concatenate([lo, c[half:]], axis=0) for lo, c in zip(low, carry))


def _later_sum(x, upper):
    n = x.shape[0]
    hi, lo = _split_bf16(x)
    both = _dot(jnp.concatenate([hi, lo], axis=0), upper)
    return both[0:n] + both[n:2 * n]


def _sb_alive(gone_c):
    return (jnp.min(gone_c) < SB_EXHAUSTED).astype(jnp.int32)


def _sb_weights(q, kb, gone_c, causal, upper, scale):
    return _sb_weights_of(_dot_nt(q, kb) * scale, gone_c, causal, upper)


def _sb_weights_of(z, gone_c, causal, upper):
    sp = jnp.where(z > 20.0, z, jnp.log(1.0 + jnp.exp(z)))
    spm = sp if causal is None else jnp.where(causal, sp, 0.0)
    gone = _later_sum(spm, upper) + gone_c
    w = jnp.exp(z - sp - gone)
    if causal is not None:
        w = jnp.where(causal, w, 0.0)
    return z, sp, spm, w


def _sb_fwd(proj, kv, DR, name, rider=None, y_spare=0):
    T = proj.shape[0]
    NH = DR // HEAD
    tq, tk = _sb_blocks(T)
    span = tq // tk
    scale = HEAD ** -0.5

    def body(q_ref, g_ref, k_ref, v_ref, y_ref, o_ref):
        i = pl.program_id(1)
        q = q_ref[...].astype(BF16)
        upper = _sb_upper(tk)

        def tile(jb, carry, causal, rows=slice(0, tq)):
            gone_c, acc = carry
            n = rows.stop - rows.start
            start = pl.multiple_of(jb * tk, tk)
            kb = k_ref[pl.ds(start, tk), :]
            vb = v_ref[pl.ds(start, tk), :]
            _, _, spm, w = _sb_weights(q[rows], kb, gone_c, causal, upper, scale)
            hi, lo = _split_bf16(w)
            pv = _dot(jnp.concatenate([hi, lo], axis=0), vb)
            return gone_c + jnp.sum(spm, axis=-1, keepdims=True), acc + pv[0:n] + pv[n:2 * n]

        carry = _sb_own_span(tile, (jnp.zeros((tq, 1), F32), jnp.zeros((tq, HEAD), F32)), i, tq, tk)

        _, acc = _sb_earlier(tile, carry, span * i - 1, tq)
        o_ref[...] = acc
        g = g_ref[...]
        y_ref[...] = (acc * (g * _gate_sigmoid(g))).astype(y_ref.dtype)

    blk = lambda off: pl.BlockSpec((tq, HEAD), lambda h, i: (i, off + h))
    whole = lambda off: pl.BlockSpec((T, HEAD), lambda h, i: (0, off + h))
    nq = T // tq
    return _call_with_rider(
        body, rider, lambda: (pl.program_id(0) == 0) & (pl.program_id(1) == 0),
        lambda: (pl.program_id(0) == NH - 1) & (pl.program_id(1) == nq - 1),
        name=name, grid=(NH, nq),
        in_specs=[blk(0), blk(NH), whole(0), whole(NH)],
        out_specs=[blk(0), blk(0)],
        out_shape=[jax.ShapeDtypeStruct((T, DR + y_spare), BF16), jax.ShapeDtypeStruct((T, DR), F32)],
        scratch_shapes=[], semantics=("parallel", "arbitrary"),
        args=(proj, proj, kv, kv))


def _sb_bwd(dy, proj, kv, o, DR, name, rider=None):
    T = proj.shape[0]
    NH = DR // HEAD
    tq, tk = _sb_blocks(T)
    span = tq // tk
    scale = HEAD ** -0.5

    def body(dy_ref, q_ref, g_ref, k_ref, v_ref, o_ref, dq_ref, dg_ref, dk_ref, dv_ref):
        i = pl.program_id(1)

        @pl.when(i == 0)
        def _():
            dk_ref[...] = jnp.zeros_like(dk_ref)
            dv_ref[...] = jnp.zeros_like(dv_ref)

        qf = q_ref[...]
        q = qf.astype(BF16)
        q_t = qf.T.astype(BF16)
        upper = _sb_upper(tk)
        silu, dsilu = _silu_and_grad(g_ref[...])
        dyv = dy_ref[...]
        ov = o_ref[...]
        dg_ref[...] = (dyv * ov * dsilu).astype(dg_ref.dtype)
        do = dyv * silu
        dob = do.astype(BF16)
        do_t = do.T.astype(BF16)
        total = jnp.sum(dob.astype(F32) * ov, axis=-1, keepdims=True)

        def tile(jb, carry, causal, rows=slice(0, tq)):
            gone_c, e_after, dq = carry
            start = pl.multiple_of(jb * tk, tk)
            kb = k_ref[pl.ds(start, tk), :]
            vb = v_ref[pl.ds(start, tk), :]
            z, sp, spm, w = _sb_weights(q[rows], kb, gone_c, causal, upper, scale)
            e = _dot_nt(dob[rows], vb) * w
            dz = e - (total[rows] - (_later_sum(e, upper) + e_after)) * jnp.exp(z - sp)
            if causal is not None:
                dz = jnp.where(causal, dz, 0.0)
            dzb = dz.astype(BF16)
            dk_ref[jb] += _dot(q_t[:, rows], dzb) * scale
            dv_ref[jb] += _dot(do_t[:, rows], w.astype(BF16))
            return (gone_c + jnp.sum(spm, axis=-1, keepdims=True), e_after + jnp.sum(e, axis=-1, keepdims=True),
                    dq + _dot(dzb, kb))

        carry = _sb_own_span(
            tile, (jnp.zeros((tq, 1), F32), jnp.zeros((tq, 1), F32), jnp.zeros((tq, HEAD), F32)), i, tq, tk)

        *_, dq = _sb_earlier(tile, carry, span * i - 1, tq)
        dq_ref[...] = (dq * scale).astype(dq_ref.dtype)

    blk = lambda off: pl.BlockSpec((tq, HEAD), lambda h, i: (i, off + h))
    whole = lambda off: pl.BlockSpec((T, HEAD), lambda h, i: (0, off + h))
    keys_t = pl.BlockSpec((None, T // tk, HEAD, tk), lambda h, i: (h, 0, 0, 0))
    keys_t_shape = jax.ShapeDtypeStruct((NH, T // tk, HEAD, tk), F32)
    nq = T // tq
    return _call_with_rider(
        body, rider, lambda: (pl.program_id(0) == 0) & (pl.program_id(1) == 0),
        lambda: (pl.program_id(0) == NH - 1) & (pl.program_id(1) == nq - 1),
        name=name, grid=(NH, nq),
        in_specs=[blk(0), blk(0), blk(NH), whole(0), whole(NH), blk(0)],
        out_specs=[blk(0), blk(0), keys_t, keys_t],
        out_shape=[jax.ShapeDtypeStruct((T, DR), BF16), jax.ShapeDtypeStruct((T, DR), BF16),
                   keys_t_shape, keys_t_shape],
        scratch_shapes=[], semantics=("parallel", "arbitrary"),
        args=(dy, proj, proj, kv, kv, o))


def _merge_dkv(parts, name):
    NH, nblk, _, tk = parts[0][0].shape
    DR = NH * HEAD
    n = len(parts)

    def body(*refs):
        o_ref = refs[-1]
        for which in range(2):
            for h in range(NH):
                acc = refs[which][h]
                for p in range(1, n):
                    acc = acc + refs[2 * p + which][h]
                col = which * DR + h * HEAD
                o_ref[:, col:col + HEAD] = acc.T.astype(o_ref.dtype)

    blk = pl.BlockSpec((NH, None, HEAD, tk), lambda t: (0, t, 0, 0))
    return pl.pallas_call(
        body, name=name, grid=(nblk,),
        in_specs=[blk] * (2 * n),
        out_specs=pl.BlockSpec((tk, 2 * DR), lambda t: (t, 0)),
        out_shape=jax.ShapeDtypeStruct((nblk * tk, 2 * DR), BF16),
        compiler_params=_params("parallel"),
    )(*[a for pair in parts for a in pair])


def _final_loss(h, g, target, name):
    T, D = h.shape
    tt = _row_tile(T, 256)

    def body(h_ref, g_ref, t_ref, dh_ref, dg_ref, sq_ref):
        @pl.when(pl.program_id(0) == 0)
        def _():
            dg_ref[...] = jnp.zeros_like(dg_ref)
            sq_ref[...] = jnp.zeros_like(sq_ref)

        hv = h_ref[...]
        gv = g_ref[...]
        r = lax.rsqrt(jnp.mean(hv * hv, axis=-1, keepdims=True) + EPS)
        xhat = hv * r
        err = xhat * gv - t_ref[...]
        sq_ref[...] += jnp.sum(err * err, axis=0, keepdims=True)
        dy = err * (1.0 / D)
        dxhat = dy * gv
        dh_ref[...] = r * (dxhat - xhat * jnp.mean(dxhat * xhat, axis=-1, keepdims=True))
        dg_ref[...] += jnp.sum(dy * xhat, axis=0, keepdims=True)

    row = pl.BlockSpec((tt, D), lambda i: (i, 0))
    vec = pl.BlockSpec((1, D), lambda i: (0, 0))
    return pl.pallas_call(
        body, name=name, grid=(T // tt,),
        in_specs=[row, vec, row],
        out_specs=[row, vec, vec],
        out_shape=[jax.ShapeDtypeStruct((T, D), F32), jax.ShapeDtypeStruct((1, D), F32),
                   jax.ShapeDtypeStruct((1, D), F32)],
        compiler_params=_params("arbitrary"),
    )(h, g.reshape(1, D), target)


def _sum_parts(parts_ref):
    g = parts_ref[0].astype(F32)
    for s in range(1, parts_ref.shape[0]):
        g = g + parts_ref[s].astype(F32)
    return g


def _adamw(parts, w, m, v, name):
    P, R, C = parts.shape
    tr = _rows_for(R, P * C * 4)

    def body(p_ref, w_ref, m_ref, v_ref, g_ref, d_ref, nm_ref, nv_ref):
        g = _sum_parts(p_ref)
        nm = ADAM_B1 * m_ref[...] + (1.0 - ADAM_B1) * g
        nv = ADAM_B2 * v_ref[...] + (1.0 - ADAM_B2) * jnp.square(g)
        m_hat = nm / (1.0 - ADAM_B1 ** ADAM_STEP)
        v_hat = nv / (1.0 - ADAM_B2 ** ADAM_STEP)
        g_ref[...] = g
        d_ref[...] = -ADAM_LR * (m_hat / (jnp.sqrt(v_hat) + ADAM_EPS) + ADAM_WD * w_ref[...])
        nm_ref[...] = nm
        nv_ref[...] = nv

    row = pl.BlockSpec((tr, C), lambda i: (i, 0))
    out = jax.ShapeDtypeStruct((R, C), F32)
    return pl.pallas_call(
        body, name=name, grid=(R // tr,),
        in_specs=[pl.BlockSpec((P, tr, C), lambda i: (0, i, 0)), row, row, row],
        out_specs=[row] * 4,
        out_shape=[out] * 4,
        compiler_params=_params("parallel"),
    )(parts, w, m, v)


def _sum_devices(parts, name):
    P, R, C = parts.shape
    tr = _rows_for(R, P * C * 4)

    def body(p_ref, o_ref):
        o_ref[...] = _sum_parts(p_ref)

    return pl.pallas_call(
        body, name=name, grid=(R // tr,),
        in_specs=[pl.BlockSpec((P, tr, C), lambda i: (0, i, 0))],
        out_specs=pl.BlockSpec((tr, C), lambda i: (i, 0)),
        out_shape=jax.ShapeDtypeStruct((R, C), F32),
        compiler_params=_params("parallel"),
    )(parts)


def _mesh_position():
    return lax.axis_index("x"), lax.axis_index("y"), lax.axis_index("c")


def _device_index(p):
    return 4 * p[0] + 2 * p[1] + p[2]


class _Gather:
    def __init__(self, arrs):
        self.arrs = list(arrs)
        self.n = len(self.arrs)

    def out_shape(self):
        return [jax.ShapeDtypeStruct((N_DEV,) + a.shape, a.dtype) for a in self.arrs]

    def scratch(self):
        return [pltpu.SemaphoreType.DMA((self.n, 7)), pltpu.SemaphoreType.DMA((self.n, 7)),
                pltpu.SemaphoreType.DMA((self.n,))]

    def _plan(self, ins, outs, sems):
        send_sems, recv_sems, local_sems = sems
        x, y, c = _mesh_position()
        me, sibling = (x, y, c), (x, y, 1 - c)
        chips = [(1 - x, y), (x, 1 - y), (1 - x, 1 - y)]

        def slot(a, p):
            return outs[a].at[_device_index(p)]

        def copy(a, k, block, to, src=None):
            return pltpu.make_async_remote_copy(
                src_ref=slot(a, block) if src is None else src, dst_ref=slot(a, block),
                send_sem=send_sems.at[a, k], recv_sem=recv_sems.at[a, k],
                device_id=to, device_id_type=pl.DeviceIdType.MESH)

        mine = [pltpu.make_async_copy(ins[a], slot(a, me), local_sems.at[a]) for a in range(self.n)]
        first = []
        for a in range(self.n):
            first.append(copy(a, 0, me, sibling, src=ins[a]))
            first += [copy(a, 1 + j, me, (*chip, c), src=ins[a]) for j, chip in enumerate(chips)]
        return me, sibling, c, chips, copy, mine, first

    def start(self, ins, outs, sems):
        *_, mine, first = self._plan(ins, outs, sems)
        for cp in mine + first:
            cp.start()

    def wait(self, ins, outs, sems):
        me, sibling, c, chips, copy, mine, first = self._plan(ins, outs, sems)
        passed = []
        for a in range(self.n):
            for j, chip in enumerate(chips):
                copy(a, 1 + j, (*chip, c), me).wait_recv()
                fwd = copy(a, 4 + j, (*chip, c), sibling)
                fwd.start()
                passed.append(fwd)
        for a in range(self.n):
            copy(a, 0, sibling, me).wait_recv()
            for j, chip in enumerate(chips):
                copy(a, 4 + j, (*chip, 1 - c), me).wait_recv()
        for cp in first + passed:
            cp.wait_send()
        for cp in mine:
            cp.wait()


def _exchange_now(ex, name):
    n = ex.n

    def body(*refs):
        ins, outs, sems = refs[:n], refs[n:2 * n], refs[2 * n:]
        ex.start(ins, outs, sems)
        ex.wait(ins, outs, sems)

    return pl.pallas_call(
        body, name=name,
        in_specs=[HBM_SPEC] * n, out_specs=[HBM_SPEC] * n,
        out_shape=ex.out_shape(), scratch_shapes=ex.scratch(),
    )(*ex.arrs)


def _all_gather(arrs, name):
    return _exchange_now(_Gather(arrs), name)


class _Scatter:
    def __init__(self, arrs):
        self.arrs = list(arrs)
        self.n = len(self.arrs)

    def out_shape(self):
        return [jax.ShapeDtypeStruct(a.shape, a.dtype) for a in self.arrs]

    def scratch(self):
        return [pltpu.SemaphoreType.DMA((self.n, 7)), pltpu.SemaphoreType.DMA((self.n, 7)),
                pltpu.SemaphoreType.DMA((self.n,))]

    def _copies(self, ins, outs, sems, arrivals):
        send_sems, recv_sems, local_sems = sems
        x, y, c = _mesh_position()
        me = _device_index((x, y, c))
        peers = [(1 - x if k & 4 else x, 1 - y if k & 2 else y, 1 - c if k & 1 else c) for k in range(1, N_DEV)]
        local, sends, recvs = [], [], []
        for a in range(self.n):
            local.append(pltpu.make_async_copy(ins[a].at[me], outs[a].at[me], local_sems.at[a]))
            for k, peer in enumerate(peers):
                there = _device_index(peer)
                src = ins[a].at[there]
                sem = dict(send_sem=send_sems.at[a, k], recv_sem=recv_sems.at[a, k],
                           device_id=peer, device_id_type=pl.DeviceIdType.MESH)
                sends.append(pltpu.make_async_remote_copy(src_ref=src, dst_ref=outs[a].at[me], **sem))
                if arrivals:
                    recvs.append(pltpu.make_async_remote_copy(src_ref=src, dst_ref=outs[a].at[there], **sem))
        return local, sends, recvs

    def start(self, ins, outs, sems):
        local, sends, _ = self._copies(ins, outs, sems, arrivals=False)
        for cp in local + sends:
            cp.start()

    def wait(self, ins, outs, sems):
        local, sends, recvs = self._copies(ins, outs, sems, arrivals=True)
        for cp in recvs:
            cp.wait_recv()
        for cp in sends:
            cp.wait_send()
        for cp in local:
            cp.wait()


def _all_to_all(arrs, name):
    return _exchange_now(_Scatter(arrs), name)


def _call_with_rider(body, rider, first, last, *, name, grid, in_specs, out_specs, out_shape, scratch_shapes,
                     semantics, args):
    if rider is None:
        out = pl.pallas_call(body, name=name, grid=grid, in_specs=in_specs, out_specs=out_specs,
                             out_shape=out_shape, scratch_shapes=scratch_shapes,
                             compiler_params=_params(*semantics))(*args)
        return out, None
    n, n_in, n_out = rider.n, len(in_specs), len(out_specs)

    def riding(*refs):
        ins, r_in = refs[:n_in], refs[n_in:n_in + n]
        outs, r_out = refs[n_in + n:n_in + n + n_out], refs[n_in + n + n_out:n_in + 2 * n + n_out]
        scratch, sems = refs[n_in + 2 * n + n_out:-3], refs[-3:]

        @pl.when(first())
        def _():
            rider.start(r_in, r_out, sems)

        body(*ins, *outs, *scratch)

        @pl.when(last())
        def _():
            rider.wait(r_in, r_out, sems)

    out = pl.pallas_call(
        riding, name=name, grid=grid,
        in_specs=list(in_specs) + [HBM_SPEC] * n, out_specs=list(out_specs) + [HBM_SPEC] * n,
        out_shape=list(out_shape) + rider.out_shape(),
        scratch_shapes=list(scratch_shapes) + rider.scratch(),
        compiler_params=_params(*["arbitrary"] * len(grid)),
    )(*args, *rider.arrs)
    return out[:n_out], out[n_out:]


def _pack(arrs, row_multiple):
    parts = []
    rows = 0
    for a in arrs:
        flat = a.reshape(-1).astype(F32)
        r = -(-flat.shape[0] // (8 * LANES)) * 8
        parts.append(jnp.pad(flat, (0, r * LANES - flat.shape[0])).reshape(r, LANES))
        rows += r
    pad = -rows % row_multiple
    if pad:
        parts.append(jnp.zeros((pad, LANES), F32))
    return jnp.concatenate(parts, axis=0)


def _unpack(buf, shapes, lead=()):
    out = []
    r0 = 0
    for shape in shapes:
        size = 1
        for s in shape:
            size *= s
        r = -(-size // (8 * LANES)) * 8
        part = buf[..., r0:r0 + r, :].reshape(lead + (r * LANES,))[..., :size]
        out.append(part.reshape(lead + tuple(shape)))
        r0 += r
    return out


def _gathered_cols(g):
    g = jnp.moveaxis(g, 0, -2)
    return g.reshape(g.shape[:-2] + (g.shape[-2] * g.shape[-1],))


def kernel(x, mem, mem_norm, w_mem_kv, norm_a, w_in_a, conv_w, conv_b, w_rec_gate, b_rec_gate, w_in_gate, b_in_gate, lru_lambda, w_out_a, kv_norm, w_kv, norm_b, w_in_b, w_out_b, final_norm, loss_target, m_mem_norm, m_w_mem_kv, m_norm_a, m_w_in_a, m_conv_w, m_conv_b, m_w_rec_gate, m_b_rec_gate, m_w_in_gate, m_b_in_gate, m_lru_lambda, m_w_out_a, m_kv_norm, m_w_kv, m_norm_b, m_w_in_b, m_w_out_b, m_final_norm, v_mem_norm, v_w_mem_kv, v_norm_a, v_w_in_a, v_conv_w, v_conv_b, v_w_rec_gate, v_b_rec_gate, v_w_in_gate, v_b_in_gate, v_lru_lambda, v_w_out_a, v_kv_norm, v_w_kv, v_norm_b, v_w_in_b, v_w_out_b, v_final_norm):
    xs = x[0]
    T, D = xs.shape
    L = w_mem_kv.shape[0]
    NA = w_in_a.shape[0]
    NB = w_in_b.shape[0]
    DM2 = w_mem_kv.shape[2]
    DM = DM2 // 2
    DR = w_rec_gate.shape[1] * w_rec_gate.shape[2]
    me = _device_index(_mesh_position())

    small_sharded = [norm_a, conv_w, conv_b, b_rec_gate, b_in_gate, lru_lambda]
    shard = {f"mem_kv{l}": w_mem_kv[l] for l in range(L)}
    shard.update({f"in_a{l}": w_in_a[l] for l in range(NA)}, **{f"out_a{l}": w_out_a[l] for l in range(NA)})
    shard.update({f"in_b{j}": w_in_b[j] for j in range(NB)}, **{f"out_b{j}": w_out_b[j] for j in range(NB)})
    shard["kv"] = w_kv
    shard = {k: w.astype(BF16) for k, w in shard.items()}
    now = ["in_a0", "mem_kv0"]
    gathered = _all_gather([shard[k] for k in now] + [_pack(small_sharded, 8)], "gather_params")
    full = dict(zip(now, gathered[:-1]))

    def gather_rider(keys):
        return _Gather([shard[k] for k in keys]), keys

    def out_proj(key):
        return full[key].reshape(1, -1, D)

    def mem_proj(l):
        return full[f"mem_kv{l}"].reshape(1, D, DM2)

    norm_a_f, conv_w_f, conv_b_f, b_r_f, b_i_f, lam_f = [
        _gathered_cols(s) for s in _unpack(gathered[-1], [s.shape for s in small_sharded], lead=(N_DEV,))]
    w_r_bf = w_rec_gate.astype(BF16)
    w_i_bf = w_in_gate.astype(BF16)

    zeros_mem = jnp.zeros_like(mem[0])
    mem_n = _rms_fwd(mem[0], mem_norm, "rms_mem")
    mkv = [_mm_nn(mem_n, mem_proj(0), out_dtype=BF16, name="mm_mem_kv0")]

    h = xs
    u = _rms_fwd(h, norm_a_f[0], "rms_a0")
    saved_a = []
    for l in range(NA):
        rider, keys = gather_rider([f"out_a{l}"])
        proj, landed = _mm_nn(u, full[f"in_a{l}"], name=f"mm_in_a{l}", rider=rider)
        full.update(zip(keys, landed))
        later_mem = [f"mem_kv{m}" for m in range(1, L)] if l == 0 else []
        rider, keys = gather_rider([f"in_a{l + 1}" if l + 1 < NA else "kv"] + later_mem)
        (y_rnn, xc, r, i, hr), landed = _rglru_fwd(proj, conv_w_f[l], conv_b_f[l], w_r_bf[l], b_r_f[l], w_i_bf[l],
                                                   b_i_f[l], lam_f[l], f"rglru_fwd{l}", rider, y_spare=DM)
        full.update(zip(keys, landed))
        if l == 0:
            mkv += [_mm_nn(mem_n, mem_proj(m), out_dtype=BF16, name=f"mm_mem_kv{m}") for m in range(1, L)]
        ycat = _memattn_fwd(proj, mkv[l], y_rnn, DR, DM, f"memattn_fwd_a{l}")
        gains = [norm_a_f[l + 1]] if l + 1 < NA else [kv_norm, norm_b[0]]
        h_next, *normed = _mm_nn(ycat, out_proj(f"out_a{l}"), res=h, name=f"mm_out_a{l}", norms=gains)
        saved_a.append((h, u, proj, xc, r, i, hr, ycat))
        h, u = h_next, normed[-1]
    h_kv, u_kv = h, normed[0]
    rider, keys = gather_rider(["in_b0"])
    kv, landed = _mm_nn(u_kv, full["kv"], out_dtype=BF16, name="mm_kv", rider=rider)
    full.update(zip(keys, landed))
    saved_b = []
    for j in range(NB):
        proj = _mm_nn(u, full[f"in_b{j}"], name=f"mm_in_b{j}")
        rider, keys = gather_rider([f"out_b{j}"] + ([f"in_b{j + 1}"] if j + 1 < NB else []))
        (y_sb, o_sb), landed = _sb_fwd(proj, kv, DR, f"sb_fwd{j}", rider, y_spare=DM)
        full.update(zip(keys, landed))
        ycat = _memattn_fwd(proj, mkv[NA + j], y_sb, DR, DM, f"memattn_fwd_b{j}")
        if j + 1 < NB:
            h_next, u_next = _mm_nn(ycat, out_proj(f"out_b{j}"), res=h, name=f"mm_out_b{j}", norms=[norm_b[j + 1]])
        else:
            h_next, u_next = _mm_nn(ycat, out_proj(f"out_b{j}"), res=h, name=f"mm_out_b{j}"), None
        saved_b.append((h, u, proj, o_sb, ycat))
        h, u = h_next, u_next

    dh, d_final_norm, sq = _final_loss(h, final_norm, loss_target[0], "final_loss")
    loss = lax.psum(0.5 * jnp.sum(sq) / D, ("x", "y", "c"))

    big_grads = {}
    received = {}

    def scatter_rider(keys):
        return _Scatter([big_grads[k] for k in keys]), keys

    dmkv = [None] * L
    d_norm_b = [None] * NB
    dkv_parts = []
    for j in reversed(range(NB)):
        h_in, u, proj, o_sb, ycat = saved_b[j]
        dy = _mm_nt(dh, out_proj(f"out_b{j}"), name=f"mm_dy_b{j}")
        big_grads[f"out_b{j}"] = _mm_tn(ycat, dh, 1, name=f"mm_dw_out_b{j}").reshape(N_DEV, -1, D)
        rider, keys = scatter_rider(([f"in_b{j + 1}"] if j + 1 < NB else []) + [f"out_b{j}"])
        (dq, dg, dk, dv), landed = _sb_bwd(dy, proj, kv, o_sb, DR, f"sb_bwd{j}", rider)
        received.update(zip(keys, landed))
        dkv_parts.append((dk, dv))
        dqg_mem, dmkv[NA + j] = _memattn_bwd(dy, proj, mkv[NA + j], DR, DM, f"memattn_bwd_b{j}")
        dproj = jnp.concatenate([dq, dg, dqg_mem], axis=-1)
        du = _mm_nt(dproj, full[f"in_b{j}"], name=f"mm_du_b{j}")
        big_grads[f"in_b{j}"] = _mm_tn(u, dproj, N_DEV, name=f"mm_dw_in_b{j}")
        dh, d_norm_b[j] = _rms_bwd(du, h_in, norm_b[j], dh, f"rms_bwd_b{j}")

    dkv = _merge_dkv(dkv_parts, "merge_dkv")
    du_kv = _mm_nt(dkv, full["kv"], name="mm_du_kv")
    big_grads["kv"] = _mm_tn(u_kv, dkv, N_DEV, name="mm_dw_kv")
    dh, d_kv_norm = _rms_bwd(du_kv, h_kv, kv_norm, dh, "rms_bwd_kv")

    d_norm_a, d_conv_w, d_conv_b, d_w_r, d_b_r, d_w_i, d_b_i, d_lam = ([None] * NA for _ in range(8))
    for l in reversed(range(NA)):
        h_in, u, proj, xc, r, i, hr, ycat = saved_a[l]
        if l == NA - 1:
            rider, keys = scatter_rider(["kv"])
            dy, landed = _mm_nt(dh, out_proj(f"out_a{l}"), name=f"mm_dy_a{l}", rider=rider)
            received.update(zip(keys, landed))
        else:
            dy = _mm_nt(dh, out_proj(f"out_a{l}"), name=f"mm_dy_a{l}")
        big_grads[f"out_a{l}"] = _mm_tn(ycat, dh, 1, name=f"mm_dw_out_a{l}").reshape(N_DEV, -1, D)
        rider, keys = scatter_rider([f"in_a{l + 1}" if l + 1 < NA else "in_b0", f"out_a{l}"])
        (dxg, d_conv_w[l], d_conv_b[l], d_w_r[l], d_b_r[l], d_w_i[l], d_b_i[l], d_lam[l]), landed = _rglru_bwd(
            dy, proj, xc, r, i, hr, conv_w_f[l], w_r_bf[l], w_i_bf[l], lam_f[l], f"rglru_bwd{l}", rider)
        received.update(zip(keys, landed))
        dqg_mem, dmkv[l] = _memattn_bwd(dy, proj, mkv[l], DR, DM, f"memattn_bwd_a{l}")
        dproj = jnp.concatenate([dxg, dqg_mem], axis=-1)
        if l > 0:
            du = _mm_nt(dproj, full[f"in_a{l}"], name=f"mm_du_a{l}")
            big_grads[f"in_a{l}"] = _mm_tn(u, dproj, N_DEV, name=f"mm_dw_in_a{l}")
        else:
            dmkv_all = jnp.concatenate(dmkv, axis=-1)
            dmem_n = _mm_nt(dmkv_all, jnp.concatenate([mem_proj(m) for m in range(L)], axis=0), name="mm_dmem")
            for m in range(L):
                big_grads[f"mem_kv{m}"] = _mm_tn(mem_n, dmkv[m], 1, name=f"mm_dw_mem_kv{m}").reshape(N_DEV, -1, DM2)
            _, d_mem_norm = _rms_bwd(dmem_n, mem[0], mem_norm, zeros_mem, "rms_bwd_mem")
            rider, keys = scatter_rider([f"mem_kv{m}" for m in range(L)])
            du, landed = _mm_nt(dproj, full["in_a0"], name="mm_du_a0", rider=rider)
            received.update(zip(keys, landed))
            small_early = {
                "mem_norm": d_mem_norm.reshape(-1),
                "conv_w": jnp.stack(d_conv_w),
                "conv_b": jnp.concatenate(d_conv_b, axis=0),
                "w_rec_gate": jnp.stack(d_w_r),
                "b_rec_gate": jnp.concatenate(d_b_r, axis=0),
                "w_in_gate": jnp.stack(d_w_i),
                "b_in_gate": jnp.concatenate(d_b_i, axis=0),
                "lru_lambda": jnp.concatenate(d_lam, axis=0),
                "kv_norm": d_kv_norm.reshape(-1),
                "norm_b": jnp.concatenate(d_norm_b, axis=0),
                "final_norm": d_final_norm.reshape(-1),
            }
            rider = _Gather([_pack(list(small_early.values()), 256)])
            big_grads["in_a0"], (early_gathered,) = _mm_tn(u, dproj, N_DEV, name="mm_dw_in_a0", rider=rider)
        dh, d_norm_a[l] = _rms_bwd(du, h_in, norm_a_f[l], dh, f"rms_bwd_a{l}")
    grad_x = dh.reshape(x.shape)

    received.update(zip(["in_a0"], _all_to_all([big_grads["in_a0"]], "scatter_grads")))
    small_late = {"norm_a": jnp.concatenate(d_norm_a, axis=0)}
    (late_gathered,) = _all_gather([_pack(list(small_late.values()), 256)], "gather_small_grads")

    def update(key, w, m, v):
        shape = w.shape
        two_d = (-1, shape[-1])
        return [o.reshape(shape) for o in _adamw(received[key], w.reshape(two_d), m.reshape(two_d),
                                                 v.reshape(two_d), f"adamw_{key}")]

    def update_layers(prefix, w, m, v):
        per_layer = [update(f"{prefix}{l}", w[l], m[l], v[l]) for l in range(w.shape[0])]
        return [jnp.stack([per_layer[l][k] for l in range(w.shape[0])]) for k in range(4)]

    upd = {
        "w_mem_kv": update_layers("mem_kv", w_mem_kv, m_w_mem_kv, v_w_mem_kv),
        "w_in_a": update_layers("in_a", w_in_a, m_w_in_a, v_w_in_a),
        "w_out_a": update_layers("out_a", w_out_a, m_w_out_a, v_w_out_a),
        "w_kv": update("kv", w_kv, m_w_kv, v_w_kv),
        "w_in_b": update_layers("in_b", w_in_b, m_w_in_b, v_w_in_b),
        "w_out_b": update_layers("out_b", w_out_b, m_w_out_b, v_w_out_b),
    }

    small_grad = {}
    for part, got, name in ((small_early, early_gathered, "sum_small_early"), (small_late, late_gathered, "sum_small_late")):
        summed = _unpack(_sum_devices(got, name), [g.shape for g in part.values()])
        small_grad.update(zip(part, summed))
    small_names = list(small_grad)
    small_w = {"mem_norm": (mem_norm, m_mem_norm, v_mem_norm), "norm_a": (norm_a, m_norm_a, v_norm_a),
               "conv_w": (conv_w, m_conv_w, v_conv_w), "conv_b": (conv_b, m_conv_b, v_conv_b),
               "w_rec_gate": (w_rec_gate, m_w_rec_gate, v_w_rec_gate),
               "b_rec_gate": (b_rec_gate, m_b_rec_gate, v_b_rec_gate),
               "w_in_gate": (w_in_gate, m_w_in_gate, v_w_in_gate), "b_in_gate": (b_in_gate, m_b_in_gate, v_b_in_gate),
               "lru_lambda": (lru_lambda, m_lru_lambda, v_lru_lambda), "kv_norm": (kv_norm, m_kv_norm, v_kv_norm),
               "norm_b": (norm_b, m_norm_b, v_norm_b), "final_norm": (final_norm, m_final_norm, v_final_norm)}
    for k in small_names:
        w = small_w[k][0]
        if small_grad[k].shape != w.shape:
            n = w.shape[-1]
            small_grad[k] = lax.dynamic_slice_in_dim(small_grad[k], me * n, n, axis=-1)
    small_shapes = [small_w[k][0].shape for k in small_names]
    packed = [_pack([small_grad[k] for k in small_names], 256)[None]]
    packed += [_pack([small_w[k][t] for k in small_names], 256) for t in range(3)]
    small_out = [_unpack(o, small_shapes) for o in _adamw(*packed, "adamw_small")]
    for idx, k in enumerate(small_names):
        upd[k] = [small_out[t][idx] for t in range(4)]

    order = ["mem_norm", "w_mem_kv", "norm_a", "w_in_a", "conv_w", "conv_b", "w_rec_gate", "b_rec_gate", "w_in_gate",
             "b_in_gate", "lru_lambda", "w_out_a", "kv_norm", "w_kv", "norm_b", "w_in_b", "w_out_b", "final_norm"]
    return (loss, grad_x, *[upd[k][0] for k in order], *[upd[k][1] for k in order],
            *[upd[k][2] for k in order], *[upd[k][3] for k in order])
```

```python
import functools

import jax
import jax.numpy as jnp
from jax import lax
from jax.experimental import pallas as pl
from jax.experimental.pallas import tpu as pltpu

F32 = jnp.float32
BF16 = jnp.bfloat16

N_DEV = 8
EPS = 1e-6
LRU_C = 8.0
HEAD = 128
CONV_WIDTH = 4
LANES = 128
SUB = 8
SB_EXHAUSTED = 110.0
VMEM_LIMIT = 56 * 1024 * 1024
RESIDENT_WEIGHT_BYTES = 8 * 1024 * 1024

ADAM_LR = 0.001
ADAM_B1 = 0.9
ADAM_B2 = 0.999
ADAM_EPS = 1e-08
ADAM_WD = 0.01
ADAM_STEP = 10

HBM_SPEC = pl.BlockSpec(memory_space=pltpu.HBM)


def _params(*semantics):
    return pltpu.CompilerParams(dimension_semantics=semantics, vmem_limit_bytes=VMEM_LIMIT)


def _tile(n, cap):
    if n <= cap:
        return n
    t = cap - cap % LANES
    while n % t:
        t -= LANES
    return t


def _row_tile(n, cap):
    if n <= cap:
        return n
    t = cap - cap % 8
    while t >= 8:
        if n % t == 0:
            return t
        t -= 8
    return n


def _rows_for(n, bytes_per_row, budget=2 * 1024 * 1024):
    return _row_tile(n, max(8, budget // bytes_per_row))


def _dot(a, b):
    return lax.dot_general(a, b, (((1,), (0,)), ((), ())), preferred_element_type=F32)


def _dot_nt(a, b):
    return lax.dot_general(a, b, (((1,), (1,)), ((), ())), preferred_element_type=F32)


def _dot_tn(a, b):
    return lax.dot_general(a, b, (((0,), (0,)), ((), ())), preferred_element_type=F32)


def _split_bf16(x):
    hi = x.astype(BF16)
    lo = (x - hi.astype(F32)).astype(BF16)
    return hi, lo


def _softplus(x):
    return jnp.maximum(x, 0.0) + jnp.log1p(jnp.exp(-jnp.abs(x)))


def _gate_sigmoid(x):
    return 0.5 * jnp.tanh(0.5 * x) + 0.5


def _silu_and_grad(g):
    sg = _gate_sigmoid(g)
    return g * sg, sg * (1.0 + g * (1.0 - sg))


def _one_minus_square(a, log_a):
    x = 2.0 * log_a
    series = -x * (1.0 + x * (0.5 + x * (1.0 / 6.0)))
    return jnp.where(x > -0.03, series, 1.0 - a * a)


def _rms_fwd(x, g, name):
    T, D = x.shape
    tt = _row_tile(T, 512)

    def body(x_ref, g_ref, o_ref):
        xv = x_ref[...]
        r = lax.rsqrt(jnp.mean(xv * xv, axis=-1, keepdims=True) + EPS)
        o_ref[...] = ((xv * r) * g_ref[...]).astype(o_ref.dtype)

    return pl.pallas_call(
        body, name=name, grid=(T // tt,),
        in_specs=[pl.BlockSpec((tt, D), lambda i: (i, 0)), pl.BlockSpec((1, D), lambda i: (0, 0))],
        out_specs=pl.BlockSpec((tt, D), lambda i: (i, 0)),
        out_shape=jax.ShapeDtypeStruct((T, D), BF16),
        compiler_params=_params("parallel"),
    )(x, g.reshape(1, D))


def _rms_bwd(du, h, g, res, name):
    T, D = h.shape
    tt = _row_tile(T, 256)

    def body(du_ref, h_ref, g_ref, res_ref, dh_ref, dg_ref):
        @pl.when(pl.program_id(0) == 0)
        def _():
            dg_ref[...] = jnp.zeros_like(dg_ref)

        hv = h_ref[...]
        duv = du_ref[...]
        r = lax.rsqrt(jnp.mean(hv * hv, axis=-1, keepdims=True) + EPS)
        xhat = hv * r
        dxhat = duv * g_ref[...]
        dh_ref[...] = res_ref[...] + r * (dxhat - xhat * jnp.mean(dxhat * xhat, axis=-1, keepdims=True))
        dg_ref[...] += jnp.sum(duv * xhat, axis=0, keepdims=True)

    row = pl.BlockSpec((tt, D), lambda i: (i, 0))
    vec = pl.BlockSpec((1, D), lambda i: (0, 0))
    return pl.pallas_call(
        body, name=name, grid=(T // tt,),
        in_specs=[row, row, vec, row],
        out_specs=[row, vec],
        out_shape=[jax.ShapeDtypeStruct((T, D), F32), jax.ShapeDtypeStruct((1, D), F32)],
        compiler_params=_params("arbitrary"),
    )(du, h, g.reshape(1, D), res)


def _mm_nn(a, b3, res=None, out_dtype=F32, name=None, rider=None, norms=()):
    M, K = a.shape
    S, _, ns = b3.shape
    whole_b = S == 1 and K * ns * 2 <= RESIDENT_WEIGHT_BYTES
    tm = _row_tile(M, 512 if whole_b or a.dtype != BF16 else 1024)
    tn = ns if whole_b else _tile(ns, 512)
    nj = ns // tn

    assert not norms or (whole_b and rider is None)
    n_in = 2 + (res is not None) + len(norms)

    def body(*refs):
        a_ref, b_ref = refs[:2]
        o_ref = refs[n_in]
        acc = _dot(a_ref[...].astype(BF16), b_ref[...])
        if res is not None:
            acc = acc + refs[2][...]
        o_ref[...] = acc.astype(o_ref.dtype)
        if norms:
            r = lax.rsqrt(jnp.mean(acc * acc, axis=-1, keepdims=True) + EPS)
            for g_ref, u_ref in zip(refs[n_in - len(norms):n_in], refs[n_in + 1:]):
                u_ref[...] = ((acc * r) * g_ref[...]).astype(u_ref.dtype)

    tile = pl.BlockSpec((tm, tn), lambda i, j: (i, j))
    in_specs = [pl.BlockSpec((tm, K), lambda i, j: (i, 0)),
                pl.BlockSpec((None, K, tn), lambda i, j: (j // nj, 0, j % nj))]
    args = [a, b3]
    if res is not None:
        in_specs.append(tile)
        args.append(res)
    in_specs += [pl.BlockSpec((1, tn), lambda i, j: (0, j))] * len(norms)
    args += [g.reshape(1, -1) for g in norms]
    outs, landed = _call_with_rider(
        body, rider, *_grid_ends((M // tm, S * nj)), name=name, grid=(M // tm, S * nj),
        in_specs=in_specs,
        out_specs=[tile] * (1 + len(norms)),
        out_shape=[jax.ShapeDtypeStruct((M, S * ns), out_dtype)] + [jax.ShapeDtypeStruct((M, S * ns), BF16)] * len(norms),
        scratch_shapes=[], semantics=("parallel", "parallel"), args=args)
    if norms:
        return tuple(outs)
    return outs[0] if rider is None else (outs[0], landed)


def _grid_ends(grid):
    def first():
        return functools.reduce(jnp.logical_and, [pl.program_id(d) == 0 for d in range(len(grid))])

    def last():
        return functools.reduce(jnp.logical_and, [pl.program_id(d) == n - 1 for d, n in enumerate(grid)])

    return first, last


def _mm_nt(a, b3, out_dtype=F32, name=None, rider=None):
    M = a.shape[0]
    S, N, ns = b3.shape
    whole_b = S * N * ns * 2 <= RESIDENT_WEIGHT_BYTES
    tm = _row_tile(M, 512 if whole_b or a.dtype != BF16 else 1024)
    tn = N if whole_b else _tile(N, 512)

    def body(a_ref, b_ref, o_ref):
        acc = _dot_nt(a_ref[:, 0:ns].astype(BF16), b_ref[0])
        for s in range(1, S):
            acc = acc + _dot_nt(a_ref[:, s * ns:(s + 1) * ns].astype(BF16), b_ref[s])
        o_ref[...] = acc.astype(o_ref.dtype)

    (out,), landed = _call_with_rider(
        body, rider, *_grid_ends((M // tm, N // tn)), name=name, grid=(M // tm, N // tn),
        in_specs=[pl.BlockSpec((tm, S * ns), lambda i, j: (i, 0)),
                  pl.BlockSpec((S, tn, ns), lambda i, j: (0, j, 0))],
        out_specs=[pl.BlockSpec((tm, tn), lambda i, j: (i, j))],
        out_shape=[jax.ShapeDtypeStruct((M, N), out_dtype)],
        scratch_shapes=[], semantics=("parallel", "parallel"), args=(a, b3))
    return out if rider is None else (out, landed)


def _mm_tn(a, b, S, name=None, rider=None):
    T, K = a.shape
    ns = b.shape[1] // S
    tk = _tile(K, 1024)
    tn = _tile(ns, 512)
    nj = ns // tn
    tt = _row_tile(T, 2048)
    nt = T // tt

    def body(a_ref, b_ref, o_ref, acc_ref):
        t = pl.program_id(2)
        p = _dot_tn(a_ref[...].astype(BF16), b_ref[...].astype(BF16))

        @pl.when(t == 0)
        def _():
            acc_ref[...] = p

        @pl.when(t > 0)
        def _():
            acc_ref[...] += p

        @pl.when(t == nt - 1)
        def _():
            o_ref[...] = acc_ref[...].astype(o_ref.dtype)

    grid = (K // tk, S * nj, nt)
    (out,), landed = _call_with_rider(
        body, rider, *_grid_ends(grid), name=name, grid=grid,
        in_specs=[pl.BlockSpec((tt, tk), lambda i, j, t: (t, i)),
                  pl.BlockSpec((tt, tn), lambda i, j, t: (t, j))],
        out_specs=[pl.BlockSpec((None, tk, tn), lambda i, j, t: (j // nj, i, j % nj))],
        out_shape=[jax.ShapeDtypeStruct((S, K, ns), BF16)],
        scratch_shapes=[pltpu.VMEM((tk, tn), F32)],
        semantics=("parallel", "parallel", "arbitrary"), args=(a, b))
    return out if rider is None else (out, landed)


def _lru_coeffs(lam_row, r):
    cl = -LRU_C * _softplus(-lam_row)
    log_a = cl * r
    a = jnp.exp(log_a)
    em = _one_minus_square(a, log_a)
    return cl, a, em, jnp.sqrt(em)


def _rglru_fwd(proj, conv_w, conv_b, w_r, b_r, w_i, b_i, lam, name, rider=None, y_spare=0):
    T = proj.shape[0]
    DR = conv_w.shape[1]
    NB = DR // HEAD
    tt = _row_tile(T, 256)
    PAD = 8

    def body(x_ref, g_ref, cw_ref, cb_ref, wr_ref, br_ref, wi_ref, bi_ref, lam_ref,
             y_ref, xc_ref, r_ref, i_ref, h_ref, xbuf, hcar):
        @pl.when(pl.program_id(0) == 0)
        def _():
            xbuf[0:PAD, :] = jnp.zeros((PAD, DR), F32)
            hcar[...] = jnp.zeros_like(hcar)

        xbuf[PAD:PAD + tt, :] = x_ref[...]
        xc = cb_ref[...] + cw_ref[0:1, :] * xbuf[pl.ds(PAD - 3, tt), :]
        for k in range(1, CONV_WIDTH):
            xc = xc + cw_ref[k:k + 1, :] * xbuf[pl.ds(PAD - 3 + k, tt), :]
        xbuf[0:PAD, :] = xbuf[tt:tt + PAD, :]
        xc_ref[...] = xc
        xcb = xc.astype(BF16)
        for n in range(NB):
            sl = slice(n * HEAD, (n + 1) * HEAD)
            r_ref[:, sl] = jax.nn.sigmoid(_dot(xcb[:, sl], wr_ref[n]) + br_ref[:, sl])
            i_ref[:, sl] = _gate_sigmoid(_dot(xcb[:, sl], wi_ref[n]) + bi_ref[:, sl])
        _, a, _, mult = _lru_coeffs(lam_ref[...], r_ref[...])
        hs = mult * (i_ref[...] * xc)
        groups = tt // SUB
        a = a.reshape(groups, SUB, DR)
        hs = hs.reshape(groups, SUB, DR)
        sub = lax.broadcasted_iota(jnp.int32, (groups, SUB, DR), 1)
        d = 1
        while d < SUB:
            keep = sub >= d
            a_sh = jnp.where(keep, pltpu.roll(a, d, 1), 1.0)
            h_sh = jnp.where(keep, pltpu.roll(hs, d, 1), 0.0)
            hs = a * h_sh + hs
            a = a * a_sh
            d *= 2
        carry = hcar[...]
        for n in range(groups):
            h_ref[n * SUB:(n + 1) * SUB, :] = hs[n] + a[n] * carry
            carry = h_ref[(n + 1) * SUB - 1:(n + 1) * SUB, :]
        hcar[...] = carry
        h = h_ref[...]
        g = g_ref[...]
        y_ref[...] = (h * (g * _gate_sigmoid(g))).astype(y_ref.dtype)

    col = lambda j: pl.BlockSpec((tt, DR), lambda c: (c, j))
    vec = pl.BlockSpec((1, DR), lambda c: (0, 0))
    gate = pl.BlockSpec((NB, HEAD, HEAD), lambda c: (0, 0, 0))
    f32_out = jax.ShapeDtypeStruct((T, DR), F32)
    nc = T // tt
    return _call_with_rider(
        body, rider, lambda: pl.program_id(0) == 0, lambda: pl.program_id(0) == nc - 1,
        name=name, grid=(nc,),
        in_specs=[col(0), col(1), pl.BlockSpec((CONV_WIDTH, DR), lambda c: (0, 0)), vec, gate, vec, gate, vec, vec],
        out_specs=[col(0)] * 5,
        out_shape=[jax.ShapeDtypeStruct((T, DR + y_spare), BF16), f32_out, f32_out, f32_out, f32_out],
        scratch_shapes=[pltpu.VMEM((tt + PAD, DR), F32), pltpu.VMEM((1, DR), F32)],
        semantics=("arbitrary",),
        args=(proj, proj, conv_w, conv_b.reshape(1, DR), w_r, b_r.reshape(1, DR), w_i, b_i.reshape(1, DR),
              lam.reshape(1, DR)))


def _rglru_bwd(dy, proj, xc, r, i, h, conv_w, w_r, w_i, lam, name, rider=None, dx_spare=0):
    T = proj.shape[0]
    DR = conv_w.shape[1]
    NB = DR // HEAD
    tt = _row_tile(T, 128)
    nc = T // tt
    PAD = 8
    per = tt // PAD

    def body(dy_ref, x_ref, g_ref, xc_ref, r_ref, i_ref, h_ref, xprev_ref, hprev_ref,
             cw_ref, wr_ref, wi_ref, lam_ref,
             dxg_ref, dcw_ref, dcb_ref, dwr_ref, dbr_ref, dwi_ref, dbi_ref, dlam_ref,
             xbuf, dxcbuf, gcar, acar):
        step = pl.program_id(0)
        chunk = nc - 1 - step

        @pl.when(step == 0)
        def _():
            for ref in (dcw_ref, dcb_ref, dwr_ref, dbr_ref, dwi_ref, dbi_ref, dlam_ref, gcar, acar):
                ref[...] = jnp.zeros_like(ref)
            dxcbuf[tt:tt + PAD, :] = jnp.zeros((PAD, DR), F32)

        not_first = (chunk > 0).astype(F32)
        row = lax.broadcasted_iota(jnp.int32, (tt, DR), 0)
        silu, dsilu = _silu_and_grad(g_ref[...])
        dyv = dy_ref[...]
        hv = h_ref[...]
        dxg_ref[:, DR:2 * DR] = (dyv * hv * dsilu).astype(dxg_ref.dtype)
        dh = dyv * silu
        rv = r_ref[...]
        iv = i_ref[...]
        xcv = xc_ref[...]
        lam_row = lam_ref[...]
        cl, a, em, mult = _lru_coeffs(lam_row, rv)
        b = jnp.where(row == tt - 1, acar[...], pltpu.roll(a, tt - 1, 0))
        gs = dh
        d = 1
        while d < tt:
            keep = row < tt - d
            b_sh = jnp.where(keep, pltpu.roll(b, tt - d, 0), 1.0)
            g_sh = jnp.where(keep, pltpu.roll(gs, tt - d, 0), 0.0)
            gs = gs + b * g_sh
            b = b * b_sh
            d *= 2
        gt = gs + b * gcar[...]
        xbuf[0:tt, :] = gt
        gcar[...] = xbuf[0:1, :]
        acar[...] = _lru_coeffs(lam_row, r_ref[0:1, :])[1]
        h_before = hprev_ref[PAD - 1:PAD, :] * not_first
        hprev = jnp.where(row == 0, h_before, pltpu.roll(hv, 1, 0))
        da = gt * hprev
        dmult = gt * (iv * xcv)
        di = gt * mult * xcv
        dxc = gt * mult * iv
        dlog_a = da * a - dmult * (1.0 - em) / mult
        dr = dlog_a * cl
        dlam_ref[...] += jnp.sum(dlog_a * rv, axis=0, keepdims=True) * (LRU_C * jax.nn.sigmoid(-lam_row))
        drp = dr * rv * (1.0 - rv)
        dip = di * iv * (1.0 - iv)
        dbr_ref[...] += jnp.sum(drp, axis=0, keepdims=True)
        dbi_ref[...] += jnp.sum(dip, axis=0, keepdims=True)
        drpb = drp.astype(BF16)
        dipb = dip.astype(BF16)
        xcb = xcv.astype(BF16)
        for n in range(NB):
            sl = slice(n * HEAD, (n + 1) * HEAD)
            dxcbuf[0:tt, sl] = dxc[:, sl] + _dot_nt(drpb[:, sl], wr_ref[n]) + _dot_nt(dipb[:, sl], wi_ref[n])
            dwr_ref[n] += _dot_tn(xcb[:, sl], drpb[:, sl])
            dwi_ref[n] += _dot_tn(xcb[:, sl], dipb[:, sl])
        dxc_all = dxcbuf[0:tt, :]
        dcb_ref[...] += jnp.sum(dxc_all, axis=0, keepdims=True)
        xbuf[0:PAD, :] = xprev_ref[...] * not_first
        xbuf[PAD:PAD + tt, :] = x_ref[...]
        dx = cw_ref[0:1, :] * dxcbuf[pl.ds(3, tt), :]
        for k in range(1, CONV_WIDTH):
            dx = dx + cw_ref[k:k + 1, :] * dxcbuf[pl.ds(3 - k, tt), :]
        dxg_ref[:, 0:DR] = dx.astype(dxg_ref.dtype)
        for k in range(CONV_WIDTH):
            dcw_ref[k:k + 1, :] += jnp.sum(xbuf[pl.ds(PAD - 3 + k, tt), :] * dxc_all, axis=0, keepdims=True)
        dxcbuf[tt:tt + PAD, :] = dxcbuf[0:PAD, :]

    rev = lambda j: pl.BlockSpec((tt, DR), lambda s: (nc - 1 - s, j))
    prev = pl.BlockSpec((PAD, DR), lambda s: (jnp.maximum((nc - 1 - s) * per - 1, 0), 0))
    vec = pl.BlockSpec((1, DR), lambda s: (0, 0))
    gate = pl.BlockSpec((NB, HEAD, HEAD), lambda s: (0, 0, 0))
    taps = pl.BlockSpec((CONV_WIDTH, DR), lambda s: (0, 0))
    vec_out = jax.ShapeDtypeStruct((1, DR), F32)
    gate_out = jax.ShapeDtypeStruct((NB, HEAD, HEAD), F32)
    return _call_with_rider(
        body, rider, lambda: pl.program_id(0) == 0, lambda: pl.program_id(0) == nc - 1,
        name=name, grid=(nc,),
        in_specs=[rev(0), rev(0), rev(1), rev(0), rev(0), rev(0), rev(0), prev, prev, taps, gate, gate, vec],
        out_specs=[pl.BlockSpec((tt, 2 * DR), lambda s: (nc - 1 - s, 0)), taps, vec, gate, vec, gate, vec, vec],
        out_shape=[jax.ShapeDtypeStruct((T, 2 * DR + dx_spare), BF16), jax.ShapeDtypeStruct((CONV_WIDTH, DR), F32),
                   vec_out, gate_out, vec_out, gate_out, vec_out, vec_out],
        scratch_shapes=[pltpu.VMEM((tt + PAD, DR), F32), pltpu.VMEM((tt + PAD, DR), F32),
                        pltpu.VMEM((1, DR), F32), pltpu.VMEM((1, DR), F32)],
        semantics=("arbitrary",),
        args=(dy, proj, proj, xc, r, i, h, proj, h, conv_w, w_r, w_i, lam.reshape(1, DR)))


def _mem_probs(q, k, scale):
    s = _dot_nt(q, k) * scale
    p = jnp.exp(s - jnp.max(s, axis=-1, keepdims=True))
    return p * (1.0 / jnp.sum(p, axis=-1, keepdims=True))


def _memattn_fwd(proj, mkv, y_mix, DR, DM, name):
    T = proj.shape[0]
    M = mkv.shape[0]
    NH = DM // HEAD
    tt = _row_tile(T, 512)
    qcol = 2 * DR // DM
    scale = HEAD ** -0.5

    def body(q_ref, g_ref, k_ref, v_ref, _, y_ref):
        for n in range(NH):
            sl = slice(n * HEAD, (n + 1) * HEAD)
            p = _mem_probs(q_ref[:, sl].astype(BF16), k_ref[:, sl], scale)
            o = _dot(p.astype(BF16), v_ref[:, sl])
            g = g_ref[:, sl]
            y_ref[:, sl] = (o * (g * _gate_sigmoid(g))).astype(y_ref.dtype)

    return pl.pallas_call(
        body, name=name, grid=(T // tt,),
        in_specs=[pl.BlockSpec((tt, DM), lambda t: (t, qcol)), pl.BlockSpec((tt, DM), lambda t: (t, qcol + 1)),
                  pl.BlockSpec((M, DM), lambda t: (0, 0)), pl.BlockSpec((M, DM), lambda t: (0, 1)), HBM_SPEC],
        out_specs=pl.BlockSpec((tt, DM), lambda t: (t, DR // DM)),
        out_shape=jax.ShapeDtypeStruct(y_mix.shape, y_mix.dtype),
        input_output_aliases={4: 0},
        compiler_params=_params("parallel"),
    )(proj, proj, mkv, mkv, y_mix)


def _memattn_bwd(dy, proj, mkv, DR, DM, name, dproj=None):
    T = proj.shape[0]
    M = mkv.shape[0]
    NH = DM // HEAD
    tt = _row_tile(T, 512)
    qcol = 2 * DR // DM
    scale = HEAD ** -0.5

    def body(dy_ref, q_ref, g_ref, k_ref, v_ref, *rest):
        dqg_ref, dkv_ref = rest[-2:]

        @pl.when(pl.program_id(0) == 0)
        def _():
            dkv_ref[...] = jnp.zeros_like(dkv_ref)

        for n in range(NH):
            sl = slice(n * HEAD, (n + 1) * HEAD)
            qb = q_ref[:, sl].astype(BF16)
            kb = k_ref[:, sl]
            vb = v_ref[:, sl]
            p = _mem_probs(qb, kb, scale)
            pb = p.astype(BF16)
            o = _dot(pb, vb)
            silu, dsilu = _silu_and_grad(g_ref[:, sl])
            dyv = dy_ref[:, sl]
            dqg_ref[:, DM + n * HEAD:DM + (n + 1) * HEAD] = (dyv * o * dsilu).astype(dqg_ref.dtype)
            dob = (dyv * silu).astype(BF16)
            dp = _dot_nt(dob, vb)
            ds = (p * (dp - jnp.sum(dp * p, axis=-1, keepdims=True)) * scale).astype(BF16)
            dqg_ref[:, sl] = _dot(ds, kb).astype(dqg_ref.dtype)
            dkv_ref[:, sl] += _dot_tn(ds, qb)
            dkv_ref[:, DM + n * HEAD:DM + (n + 1) * HEAD] += _dot_tn(pb, dob)

    in_place = dproj is not None
    return pl.pallas_call(
        body, name=name, grid=(T // tt,),
        in_specs=[pl.BlockSpec((tt, DM), lambda t: (t, DR // DM)),
                  pl.BlockSpec((tt, DM), lambda t: (t, qcol)), pl.BlockSpec((tt, DM), lambda t: (t, qcol + 1)),
                  pl.BlockSpec((M, DM), lambda t: (0, 0)), pl.BlockSpec((M, DM), lambda t: (0, 1))]
                 + [HBM_SPEC] * in_place,
        out_specs=[pl.BlockSpec((tt, 2 * DM), lambda t: (t, DR // DM if in_place else 0)),
                   pl.BlockSpec((M, 2 * DM), lambda t: (0, 0))],
        out_shape=[jax.ShapeDtypeStruct(dproj.shape if in_place else (T, 2 * DM), BF16),
                   jax.ShapeDtypeStruct((M, 2 * DM), F32)],
        input_output_aliases={5: 0} if in_place else {},
        compiler_params=_params("arbitrary"),
    )(dy, proj, proj, mkv, mkv, *([dproj] if in_place else []))


def _sb_blocks(T):
    tk = _row_tile(T // 2, 256)
    tq = 2 * tk
    assert T % tq == 0
    return tq, tk


def _sb_upper(tk):
    row = lax.broadcasted_iota(jnp.int32, (tk, tk), 0)
    col = lax.broadcasted_iota(jnp.int32, (tk, tk), 1)
    return (row > col).astype(BF16)


def _sb_causal(tq, tk, d):
    row = lax.broadcasted_iota(jnp.int32, (tq, tk), 0)
    col = lax.broadcasted_iota(jnp.int32, (tq, tk), 1)
    return col + d * tk < row


def _sb_own_span(tile, carry, i, tq, tk):
    span = tq // tk
    for d in reversed(range(span)):
        lo = d * tk
        part = tile(span * i + d, tuple(c[lo:] for c in carry), _sb_causal(tq - lo, tk, 0), slice(lo, tq))
        carry = tuple(p if lo == 0 else jnp.concatenate([c[:lo], p], axis=0) for c, p in zip(carry, part))
    return carry


def _sb_earlier(tile, carry, jb_first, tq):
    half = tq // 2

    def walk(carry, jb, rows, watched):
        def more(c):
            return jnp.logical_and(c[0] >= 0, c[1] > 0)

        def step(c):
            jb, _, *rest = c
            rest = tile(jb, tuple(rest), None, rows)
            return jb - 1, _sb_alive(watched(rest[0])), *rest

        jb, _, *carry = lax.while_loop(more, step, (jb, _sb_alive(watched(carry[0])), *carry))
        return jb, tuple(carry)

    jb, carry = walk(carry, jb_first, slice(0, tq), lambda gone: gone[half:])
    _, low = walk(tuple(c[:half] for c in carry), jb, slice(0, half), lambda gone: gone)
    return tuple(jnp.concatenate([lo, c[half:]], axis=0) for lo, c in zip(low, carry))


def _later_sum(x, upper):
    n = x.shape[0]
    hi, lo = _split_bf16(x)
    both = _dot(jnp.concatenate([hi, lo], axis=0), upper)
    return both[0:n] + both[n:2 * n]


def _sb_alive(gone_c):
    return (jnp.min(gone_c) < SB_EXHAUSTED).astype(jnp.int32)


def _sb_weights(q, kb, gone_c, causal, upper, scale):
    return _sb_weights_of(_dot_nt(q, kb) * scale, gone_c, causal, upper)


def _sb_weights_of(z, gone_c, causal, upper):
    sp = jnp.where(z > 20.0, z, jnp.log(1.0 + jnp.exp(z)))
    spm = sp if causal is None else jnp.where(causal, sp, 0.0)
    gone = _later_sum(spm, upper) + gone_c
    w = jnp.exp(z - sp - gone)
    if causal is not None:
        w = jnp.where(causal, w, 0.0)
    return z, sp, spm, w


def _sb_fwd(proj, kv, DR, name, rider=None, y_spare=0):
    T = proj.shape[0]
    NH = DR // HEAD
    tq, tk = _sb_blocks(T)
    span = tq // tk
    scale = HEAD ** -0.5

    def body(q_ref, g_ref, k_ref, v_ref, y_ref, o_ref):
        i = pl.program_id(1)
        q = q_ref[...].astype(BF16)
        upper = _sb_upper(tk)

        def tile(jb, carry, causal, rows=slice(0, tq)):
            gone_c, acc = carry
            n = rows.stop - rows.start
            start = pl.multiple_of(jb * tk, tk)
            kb = k_ref[pl.ds(start, tk), :]
            vb = v_ref[pl.ds(start, tk), :]
            _, _, spm, w = _sb_weights(q[rows], kb, gone_c, causal, upper, scale)
            hi, lo = _split_bf16(w)
            pv = _dot(jnp.concatenate([hi, lo], axis=0), vb)
            return gone_c + jnp.sum(spm, axis=-1, keepdims=True), acc + pv[0:n] + pv[n:2 * n]

        carry = _sb_own_span(tile, (jnp.zeros((tq, 1), F32), jnp.zeros((tq, HEAD), F32)), i, tq, tk)

        _, acc = _sb_earlier(tile, carry, span * i - 1, tq)
        o_ref[...] = acc
        g = g_ref[...]
        y_ref[...] = (acc * (g * _gate_sigmoid(g))).astype(y_ref.dtype)

    blk = lambda off: pl.BlockSpec((tq, HEAD), lambda h, i: (i, off + h))
    whole = lambda off: pl.BlockSpec((T, HEAD), lambda h, i: (0, off + h))
    nq = T // tq
    return _call_with_rider(
        body, rider, lambda: (pl.program_id(0) == 0) & (pl.program_id(1) == 0),
        lambda: (pl.program_id(0) == NH - 1) & (pl.program_id(1) == nq - 1),
        name=name, grid=(NH, nq),
        in_specs=[blk(0), blk(NH), whole(0), whole(NH)],
        out_specs=[blk(0), blk(0)],
        out_shape=[jax.ShapeDtypeStruct((T, DR + y_spare), BF16), jax.ShapeDtypeStruct((T, DR), F32)],
        scratch_shapes=[], semantics=("parallel", "arbitrary"),
        args=(proj, proj, kv, kv))


def _sb_bwd(dy, proj, kv, o, DR, name, rider=None):
    T = proj.shape[0]
    NH = DR // HEAD
    tq, tk = _sb_blocks(T)
    span = tq // tk
    scale = HEAD ** -0.5

    def body(dy_ref, q_ref, g_ref, k_ref, v_ref, o_ref, dq_ref, dg_ref, dk_ref, dv_ref):
        i = pl.program_id(1)

        @pl.when(i == 0)
        def _():
            dk_ref[...] = jnp.zeros_like(dk_ref)
            dv_ref[...] = jnp.zeros_like(dv_ref)

        qf = q_ref[...]
        q = qf.astype(BF16)
        q_t = qf.T.astype(BF16)
        upper = _sb_upper(tk)
        silu, dsilu = _silu_and_grad(g_ref[...])
        dyv = dy_ref[...]
        ov = o_ref[...]
        dg_ref[...] = (dyv * ov * dsilu).astype(dg_ref.dtype)
        do = dyv * silu
        dob = do.astype(BF16)
        do_t = do.T.astype(BF16)
        total = jnp.sum(dob.astype(F32) * ov, axis=-1, keepdims=True)

        def tile(jb, carry, causal, rows=slice(0, tq)):
            gone_c, e_after, dq = carry
            start = pl.multiple_of(jb * tk, tk)
            kb = k_ref[pl.ds(start, tk), :]
            vb = v_ref[pl.ds(start, tk), :]
            z, sp, spm, w = _sb_weights(q[rows], kb, gone_c, causal, upper, scale)
            e = _dot_nt(dob[rows], vb) * w
            dz = e - (total[rows] - (_later_sum(e, upper) + e_after)) * jnp.exp(z - sp)
            if causal is not None:
                dz = jnp.where(causal, dz, 0.0)
            dzb = dz.astype(BF16)
            dk_ref[jb] += _dot(q_t[:, rows], dzb) * scale
            dv_ref[jb] += _dot(do_t[:, rows], w.astype(BF16))
            return (gone_c + jnp.sum(spm, axis=-1, keepdims=True), e_after + jnp.sum(e, axis=-1, keepdims=True),
                    dq + _dot(dzb, kb))

        carry = _sb_own_span(
            tile, (jnp.zeros((tq, 1), F32), jnp.zeros((tq, 1), F32), jnp.zeros((tq, HEAD), F32)), i, tq, tk)

        *_, dq = _sb_earlier(tile, carry, span * i - 1, tq)
        dq_ref[...] = (dq * scale).astype(dq_ref.dtype)

    blk = lambda off: pl.BlockSpec((tq, HEAD), lambda h, i: (i, off + h))
    whole = lambda off: pl.BlockSpec((T, HEAD), lambda h, i: (0, off + h))
    keys_t = pl.BlockSpec((None, T // tk, HEAD, tk), lambda h, i: (h, 0, 0, 0))
    keys_t_shape = jax.ShapeDtypeStruct((NH, T // tk, HEAD, tk), F32)
    nq = T // tq
    return _call_with_rider(
        body, rider, lambda: (pl.program_id(0) == 0) & (pl.program_id(1) == 0),
        lambda: (pl.program_id(0) == NH - 1) & (pl.program_id(1) == nq - 1),
        name=name, grid=(NH, nq),
        in_specs=[blk(0), blk(0), blk(NH), whole(0), whole(NH), blk(0)],
        out_specs=[blk(0), blk(0), keys_t, keys_t],
        out_shape=[jax.ShapeDtypeStruct((T, DR), BF16), jax.ShapeDtypeStruct((T, DR), BF16),
                   keys_t_shape, keys_t_shape],
        scratch_shapes=[], semantics=("parallel", "arbitrary"),
        args=(dy, proj, proj, kv, kv, o))


def _merge_dkv(parts, name):
    NH, nblk, _, tk = parts[0][0].shape
    DR = NH * HEAD
    n = len(parts)

    def body(*refs):
        o_ref = refs[-1]
        for which in range(2):
            for h in range(NH):
                acc = refs[which][h]
                for p in range(1, n):
                    acc = acc + refs[2 * p + which][h]
                col = which * DR + h * HEAD
                o_ref[:, col:col + HEAD] = acc.T.astype(o_ref.dtype)

    blk = pl.BlockSpec((NH, None, HEAD, tk), lambda t: (0, t, 0, 0))
    return pl.pallas_call(
        body, name=name, grid=(nblk,),
        in_specs=[blk] * (2 * n),
        out_specs=pl.BlockSpec((tk, 2 * DR), lambda t: (t, 0)),
        out_shape=jax.ShapeDtypeStruct((nblk * tk, 2 * DR), BF16),
        compiler_params=_params("parallel"),
    )(*[a for pair in parts for a in pair])


def _final_loss(h, g, target, name):
    T, D = h.shape
    tt = _row_tile(T, 256)

    def body(h_ref, g_ref, t_ref, dh_ref, dg_ref, sq_ref):
        @pl.when(pl.program_id(0) == 0)
        def _():
            dg_ref[...] = jnp.zeros_like(dg_ref)
            sq_ref[...] = jnp.zeros_like(sq_ref)

        hv = h_ref[...]
        gv = g_ref[...]
        r = lax.rsqrt(jnp.mean(hv * hv, axis=-1, keepdims=True) + EPS)
        xhat = hv * r
        err = xhat * gv - t_ref[...]
        sq_ref[...] += jnp.sum(err * err, axis=0, keepdims=True)
        dy = err * (1.0 / D)
        dxhat = dy * gv
        dh_ref[...] = r * (dxhat - xhat * jnp.mean(dxhat * xhat, axis=-1, keepdims=True))
        dg_ref[...] += jnp.sum(dy * xhat, axis=0, keepdims=True)

    row = pl.BlockSpec((tt, D), lambda i: (i, 0))
    vec = pl.BlockSpec((1, D), lambda i: (0, 0))
    return pl.pallas_call(
        body, name=name, grid=(T // tt,),
        in_specs=[row, vec, row],
        out_specs=[row, vec, vec],
        out_shape=[jax.ShapeDtypeStruct((T, D), F32), jax.ShapeDtypeStruct((1, D), F32),
                   jax.ShapeDtypeStruct((1, D), F32)],
        compiler_params=_params("arbitrary"),
    )(h, g.reshape(1, D), target)


def _sum_parts(parts_ref):
    g = parts_ref[0].astype(F32)
    for s in range(1, parts_ref.shape[0]):
        g = g + parts_ref[s].astype(F32)
    return g


def _adamw(parts, w, m, v, name):
    P, R, C = parts.shape
    tr = _rows_for(R, P * C * 4)

    def body(p_ref, w_ref, m_ref, v_ref, g_ref, d_ref, nm_ref, nv_ref):
        g = _sum_parts(p_ref)
        nm = ADAM_B1 * m_ref[...] + (1.0 - ADAM_B1) * g
        nv = ADAM_B2 * v_ref[...] + (1.0 - ADAM_B2) * jnp.square(g)
        m_hat = nm / (1.0 - ADAM_B1 ** ADAM_STEP)
        v_hat = nv / (1.0 - ADAM_B2 ** ADAM_STEP)
        g_ref[...] = g
        d_ref[...] = -ADAM_LR * (m_hat / (jnp.sqrt(v_hat) + ADAM_EPS) + ADAM_WD * w_ref[...])
        nm_ref[...] = nm
        nv_ref[...] = nv

    row = pl.BlockSpec((tr, C), lambda i: (i, 0))
    out = jax.ShapeDtypeStruct((R, C), F32)
    return pl.pallas_call(
        body, name=name, grid=(R // tr,),
        in_specs=[pl.BlockSpec((P, tr, C), lambda i: (0, i, 0)), row, row, row],
        out_specs=[row] * 4,
        out_shape=[out] * 4,
        compiler_params=_params("parallel"),
    )(parts, w, m, v)


def _sum_devices(parts, name):
    P, R, C = parts.shape
    tr = _rows_for(R, P * C * 4)

    def body(p_ref, o_ref):
        o_ref[...] = _sum_parts(p_ref)

    return pl.pallas_call(
        body, name=name, grid=(R // tr,),
        in_specs=[pl.BlockSpec((P, tr, C), lambda i: (0, i, 0))],
        out_specs=pl.BlockSpec((tr, C), lambda i: (i, 0)),
        out_shape=jax.ShapeDtypeStruct((R, C), F32),
        compiler_params=_params("parallel"),
    )(parts)


def _mesh_position():
    return lax.axis_index("x"), lax.axis_index("y"), lax.axis_index("c")


def _device_index(p):
    return 4 * p[0] + 2 * p[1] + p[2]


class _Gather:
    def __init__(self, arrs):
        self.arrs = list(arrs)
        self.n = len(self.arrs)

    def out_shape(self):
        return [jax.ShapeDtypeStruct((N_DEV,) + a.shape, a.dtype) for a in self.arrs]

    def scratch(self):
        return [pltpu.SemaphoreType.DMA((self.n, 7)), pltpu.SemaphoreType.DMA((self.n, 7)),
                pltpu.SemaphoreType.DMA((self.n,))]

    def _plan(self, ins, outs, sems):
        send_sems, recv_sems, local_sems = sems
        x, y, c = _mesh_position()
        me, sibling = (x, y, c), (x, y, 1 - c)
        chips = [(1 - x, y), (x, 1 - y), (1 - x, 1 - y)]

        def slot(a, p):
            return outs[a].at[_device_index(p)]

        def copy(a, k, block, to, src=None):
            return pltpu.make_async_remote_copy(
                src_ref=slot(a, block) if src is None else src, dst_ref=slot(a, block),
                send_sem=send_sems.at[a, k], recv_sem=recv_sems.at[a, k],
                device_id=to, device_id_type=pl.DeviceIdType.MESH)

        mine = [pltpu.make_async_copy(ins[a], slot(a, me), local_sems.at[a]) for a in range(self.n)]
        first = []
        for a in range(self.n):
            first.append(copy(a, 0, me, sibling, src=ins[a]))
            first += [copy(a, 1 + j, me, (*chip, c), src=ins[a]) for j, chip in enumerate(chips)]
        return me, sibling, c, chips, copy, mine, first

    def start(self, ins, outs, sems):
        *_, mine, first = self._plan(ins, outs, sems)
        for cp in mine + first:
            cp.start()

    def wait(self, ins, outs, sems):
        me, sibling, c, chips, copy, mine, first = self._plan(ins, outs, sems)
        passed = []
        for a in range(self.n):
            for j, chip in enumerate(chips):
                copy(a, 1 + j, (*chip, c), me).wait_recv()
                fwd = copy(a, 4 + j, (*chip, c), sibling)
                fwd.start()
                passed.append(fwd)
        for a in range(self.n):
            copy(a, 0, sibling, me).wait_recv()
            for j, chip in enumerate(chips):
                copy(a, 4 + j, (*chip, 1 - c), me).wait_recv()
        for cp in first + passed:
            cp.wait_send()
        for cp in mine:
            cp.wait()


def _exchange_now(ex, name):
    n = ex.n

    def body(*refs):
        ins, outs, sems = refs[:n], refs[n:2 * n], refs[2 * n:]
        ex.start(ins, outs, sems)
        ex.wait(ins, outs, sems)

    return pl.pallas_call(
        body, name=name,
        in_specs=[HBM_SPEC] * n, out_specs=[HBM_SPEC] * n,
        out_shape=ex.out_shape(), scratch_shapes=ex.scratch(),
    )(*ex.arrs)


def _all_gather(arrs, name):
    return _exchange_now(_Gather(arrs), name)


class _Scatter:
    def __init__(self, arrs):
        self.arrs = list(arrs)
        self.n = len(self.arrs)

    def out_shape(self):
        return [jax.ShapeDtypeStruct(a.shape, a.dtype) for a in self.arrs]

    def scratch(self):
        return [pltpu.SemaphoreType.DMA((self.n, 7)), pltpu.SemaphoreType.DMA((self.n, 7)),
                pltpu.SemaphoreType.DMA((self.n,))]

    def _copies(self, ins, outs, sems, arrivals):
        send_sems, recv_sems, local_sems = sems
        x, y, c = _mesh_position()
        me = _device_index((x, y, c))
        peers = [(1 - x if k & 4 else x, 1 - y if k & 2 else y, 1 - c if k & 1 else c) for k in range(1, N_DEV)]
        local, sends, recvs = [], [], []
        for a in range(self.n):
            local.append(pltpu.make_async_copy(ins[a].at[me], outs[a].at[me], local_sems.at[a]))
            for k, peer in enumerate(peers):
                there = _device_index(peer)
                src = ins[a].at[there]
                sem = dict(send_sem=send_sems.at[a, k], recv_sem=recv_sems.at[a, k],
                           device_id=peer, device_id_type=pl.DeviceIdType.MESH)
                sends.append(pltpu.make_async_remote_copy(src_ref=src, dst_ref=outs[a].at[me], **sem))
                if arrivals:
                    recvs.append(pltpu.make_async_remote_copy(src_ref=src, dst_ref=outs[a].at[there], **sem))
        return local, sends, recvs

    def start(self, ins, outs, sems):
        local, sends, _ = self._copies(ins, outs, sems, arrivals=False)
        for cp in local + sends:
            cp.start()

    def wait(self, ins, outs, sems):
        local, sends, recvs = self._copies(ins, outs, sems, arrivals=True)
        for cp in recvs:
            cp.wait_recv()
        for cp in sends:
            cp.wait_send()
        for cp in local:
            cp.wait()


def _all_to_all(arrs, name):
    return _exchange_now(_Scatter(arrs), name)


def _call_with_rider(body, rider, first, last, *, name, grid, in_specs, out_specs, out_shape, scratch_shapes,
                     semantics, args):
    if rider is None:
        out = pl.pallas_call(body, name=name, grid=grid, in_specs=in_specs, out_specs=out_specs,
                             out_shape=out_shape, scratch_shapes=scratch_shapes,
                             compiler_params=_params(*semantics))(*args)
        return out, None
    n, n_in, n_out = rider.n, len(in_specs), len(out_specs)

    def riding(*refs):
        ins, r_in = refs[:n_in], refs[n_in:n_in + n]
        outs, r_out = refs[n_in + n:n_in + n + n_out], refs[n_in + n + n_out:n_in + 2 * n + n_out]
        scratch, sems = refs[n_in + 2 * n + n_out:-3], refs[-3:]

        @pl.when(first())
        def _():
            rider.start(r_in, r_out, sems)

        body(*ins, *outs, *scratch)

        @pl.when(last())
        def _():
            rider.wait(r_in, r_out, sems)

    out = pl.pallas_call(
        riding, name=name, grid=grid,
        in_specs=list(in_specs) + [HBM_SPEC] * n, out_specs=list(out_specs) + [HBM_SPEC] * n,
        out_shape=list(out_shape) + rider.out_shape(),
        scratch_shapes=list(scratch_shapes) + rider.scratch(),
        compiler_params=_params(*["arbitrary"] * len(grid)),
    )(*args, *rider.arrs)
    return out[:n_out], out[n_out:]


def _pack(arrs, row_multiple):
    parts = []
    rows = 0
    for a in arrs:
        flat = a.reshape(-1).astype(F32)
        r = -(-flat.shape[0] // (8 * LANES)) * 8
        parts.append(jnp.pad(flat, (0, r * LANES - flat.shape[0])).reshape(r, LANES))
        rows += r
    pad = -rows % row_multiple
    if pad:
        parts.append(jnp.zeros((pad, LANES), F32))
    return jnp.concatenate(parts, axis=0)


def _unpack(buf, shapes, lead=()):
    out = []
    r0 = 0
    for shape in shapes:
        size = 1
        for s in shape:
            size *= s
        r = -(-size // (8 * LANES)) * 8
        part = buf[..., r0:r0 + r, :].reshape(lead + (r * LANES,))[..., :size]
        out.append(part.reshape(lead + tuple(shape)))
        r0 += r
    return out


def _gathered_cols(g):
    g = jnp.moveaxis(g, 0, -2)
    return g.reshape(g.shape[:-2] + (g.shape[-2] * g.shape[-1],))


def kernel(x, mem, mem_norm, w_mem_kv, norm_a, w_in_a, conv_w, conv_b, w_rec_gate, b_rec_gate, w_in_gate, b_in_gate, lru_lambda, w_out_a, kv_norm, w_kv, norm_b, w_in_b, w_out_b, final_norm, loss_target, m_mem_norm, m_w_mem_kv, m_norm_a, m_w_in_a, m_conv_w, m_conv_b, m_w_rec_gate, m_b_rec_gate, m_w_in_gate, m_b_in_gate, m_lru_lambda, m_w_out_a, m_kv_norm, m_w_kv, m_norm_b, m_w_in_b, m_w_out_b, m_final_norm, v_mem_norm, v_w_mem_kv, v_norm_a, v_w_in_a, v_conv_w, v_conv_b, v_w_rec_gate, v_b_rec_gate, v_w_in_gate, v_b_in_gate, v_lru_lambda, v_w_out_a, v_kv_norm, v_w_kv, v_norm_b, v_w_in_b, v_w_out_b, v_final_norm):
    xs = x[0]
    T, D = xs.shape
    L = w_mem_kv.shape[0]
    NA = w_in_a.shape[0]
    NB = w_in_b.shape[0]
    DM2 = w_mem_kv.shape[2]
    DM = DM2 // 2
    DR = w_rec_gate.shape[1] * w_rec_gate.shape[2]
    me = _device_index(_mesh_position())

    small_sharded = [norm_a, conv_w, conv_b, b_rec_gate, b_in_gate, lru_lambda]
    shard = {f"mem_kv{l}": w_mem_kv[l] for l in range(L)}
    shard.update({f"in_a{l}": w_in_a[l] for l in range(NA)}, **{f"out_a{l}": w_out_a[l] for l in range(NA)})
    shard.update({f"in_b{j}": w_in_b[j] for j in range(NB)}, **{f"out_b{j}": w_out_b[j] for j in range(NB)})
    shard["kv"] = w_kv
    shard = {k: w.astype(BF16) for k, w in shard.items()}
    now = ["in_a0", "mem_kv0"]
    gathered = _all_gather([shard[k] for k in now] + [_pack(small_sharded, 8)], "gather_params")
    full = dict(zip(now, gathered[:-1]))

    def gather_rider(keys):
        return _Gather([shard[k] for k in keys]), keys

    def out_proj(key):
        return full[key].reshape(1, -1, D)

    def mem_proj(l):
        return full[f"mem_kv{l}"].reshape(1, D, DM2)

    norm_a_f, conv_w_f, conv_b_f, b_r_f, b_i_f, lam_f = [
        _gathered_cols(s) for s in _unpack(gathered[-1], [s.shape for s in small_sharded], lead=(N_DEV,))]
    w_r_bf = w_rec_gate.astype(BF16)
    w_i_bf = w_in_gate.astype(BF16)

    zeros_mem = jnp.zeros_like(mem[0])
    mem_n = _rms_fwd(mem[0], mem_norm, "rms_mem")
    mkv = [_mm_nn(mem_n, mem_proj(0), out_dtype=BF16, name="mm_mem_kv0")]

    h = xs
    u = _rms_fwd(h, norm_a_f[0], "rms_a0")
    saved_a = []
    for l in range(NA):
        rider, keys = gather_rider([f"out_a{l}"])
        proj, landed = _mm_nn(u, full[f"in_a{l}"], name=f"mm_in_a{l}", rider=rider)
        full.update(zip(keys, landed))
        later_mem = [f"mem_kv{m}" for m in range(1, L)] if l == 0 else []
        rider, keys = gather_rider([f"in_a{l + 1}" if l + 1 < NA else "kv"] + later_mem)
        (y_rnn, xc, r, i, hr), landed = _rglru_fwd(proj, conv_w_f[l], conv_b_f[l], w_r_bf[l], b_r_f[l], w_i_bf[l],
                                                   b_i_f[l], lam_f[l], f"rglru_fwd{l}", rider, y_spare=DM)
        full.update(zip(keys, landed))
        if l == 0:
            mkv += [_mm_nn(mem_n, mem_proj(m), out_dtype=BF16, name=f"mm_mem_kv{m}") for m in range(1, L)]
        ycat = _memattn_fwd(proj, mkv[l], y_rnn, DR, DM, f"memattn_fwd_a{l}")
        gains = [norm_a_f[l + 1]] if l + 1 < NA else [kv_norm, norm_b[0]]
        h_next, *normed = _mm_nn(ycat, out_proj(f"out_a{l}"), res=h, name=f"mm_out_a{l}", norms=gains)
        saved_a.append((h, u, proj, xc, r, i, hr, ycat))
        h, u = h_next, normed[-1]
    h_kv, u_kv = h, normed[0]
    rider, keys = gather_rider(["in_b0"])
    kv, landed = _mm_nn(u_kv, full["kv"], out_dtype=BF16, name="mm_kv", rider=rider)
    full.update(zip(keys, landed))
    saved_b = []
    for j in range(NB):
        proj = _mm_nn(u, full[f"in_b{j}"], name=f"mm_in_b{j}")
        rider, keys = gather_rider([f"out_b{j}"] + ([f"in_b{j + 1}"] if j + 1 < NB else []))
        (y_sb, o_sb), landed = _sb_fwd(proj, kv, DR, f"sb_fwd{j}", rider, y_spare=DM)
        full.update(zip(keys, landed))
        ycat = _memattn_fwd(proj, mkv[NA + j], y_sb, DR, DM, f"memattn_fwd_b{j}")
        if j + 1 < NB:
            h_next, u_next = _mm_nn(ycat, out_proj(f"out_b{j}"), res=h, name=f"mm_out_b{j}", norms=[norm_b[j + 1]])
        else:
            h_next, u_next = _mm_nn(ycat, out_proj(f"out_b{j}"), res=h, name=f"mm_out_b{j}"), None
        saved_b.append((h, u, proj, o_sb, ycat))
        h, u = h_next, u_next

    dh, d_final_norm, sq = _final_loss(h, final_norm, loss_target[0], "final_loss")
    loss = lax.psum(0.5 * jnp.sum(sq) / D, ("x", "y", "c"))

    big_grads = {}
    received = {}

    def scatter_rider(keys):
        return _Scatter([big_grads[k] for k in keys]), keys

    dmkv = [None] * L
    d_norm_b = [None] * NB
    dkv_parts = []
    for j in reversed(range(NB)):
        h_in, u, proj, o_sb, ycat = saved_b[j]
        dy = _mm_nt(dh, out_proj(f"out_b{j}"), name=f"mm_dy_b{j}")
        big_grads[f"out_b{j}"] = _mm_tn(ycat, dh, 1, name=f"mm_dw_out_b{j}").reshape(N_DEV, -1, D)
        rider, keys = scatter_rider(([f"in_b{j + 1}"] if j + 1 < NB else []) + [f"out_b{j}"])
        (dq, dg, dk, dv), landed = _sb_bwd(dy, proj, kv, o_sb, DR, f"sb_bwd{j}", rider)
        received.update(zip(keys, landed))
        dkv_parts.append((dk, dv))
        dqg_mem, dmkv[NA + j] = _memattn_bwd(dy, proj, mkv[NA + j], DR, DM, f"memattn_bwd_b{j}")
        dproj = jnp.concatenate([dq, dg, dqg_mem], axis=-1)
        du = _mm_nt(dproj, full[f"in_b{j}"], name=f"mm_du_b{j}")
        big_grads[f"in_b{j}"] = _mm_tn(u, dproj, N_DEV, name=f"mm_dw_in_b{j}")
        dh, d_norm_b[j] = _rms_bwd(du, h_in, norm_b[j], dh, f"rms_bwd_b{j}")

    dkv = _merge_dkv(dkv_parts, "merge_dkv")
    du_kv = _mm_nt(dkv, full["kv"], name="mm_du_kv")
    big_grads["kv"] = _mm_tn(u_kv, dkv, N_DEV, name="mm_dw_kv")
    dh, d_kv_norm = _rms_bwd(du_kv, h_kv, kv_norm, dh, "rms_bwd_kv")

    d_norm_a, d_conv_w, d_conv_b, d_w_r, d_b_r, d_w_i, d_b_i, d_lam = ([None] * NA for _ in range(8))
    for l in reversed(range(NA)):
        h_in, u, proj, xc, r, i, hr, ycat = saved_a[l]
        if l == NA - 1:
            rider, keys = scatter_rider(["kv"])
            dy, landed = _mm_nt(dh, out_proj(f"out_a{l}"), name=f"mm_dy_a{l}", rider=rider)
            received.update(zip(keys, landed))
        else:
            dy = _mm_nt(dh, out_proj(f"out_a{l}"), name=f"mm_dy_a{l}")
        big_grads[f"out_a{l}"] = _mm_tn(ycat, dh, 1, name=f"mm_dw_out_a{l}").reshape(N_DEV, -1, D)
        rider, keys = scatter_rider([f"in_a{l + 1}" if l + 1 < NA else "in_b0", f"out_a{l}"])
        (dxg, d_conv_w[l], d_conv_b[l], d_w_r[l], d_b_r[l], d_w_i[l], d_b_i[l], d_lam[l]), landed = _rglru_bwd(
            dy, proj, xc, r, i, hr, conv_w_f[l], w_r_bf[l], w_i_bf[l], lam_f[l], f"rglru_bwd{l}", rider,
            dx_spare=DM2)
        received.update(zip(keys, landed))
        dproj, dmkv[l] = _memattn_bwd(dy, proj, mkv[l], DR, DM, f"memattn_bwd_a{l}", dproj=dxg)
        if l > 0:
            du = _mm_nt(dproj, full[f"in_a{l}"], name=f"mm_du_a{l}")
            big_grads[f"in_a{l}"] = _mm_tn(u, dproj, N_DEV, name=f"mm_dw_in_a{l}")
        else:
            dmkv_all = jnp.concatenate(dmkv, axis=-1)
            dmem_n = _mm_nt(dmkv_all, jnp.concatenate([mem_proj(m) for m in range(L)], axis=0), name="mm_dmem")
            for m in range(L):
                big_grads[f"mem_kv{m}"] = _mm_tn(mem_n, dmkv[m], 1, name=f"mm_dw_mem_kv{m}").reshape(N_DEV, -1, DM2)
            _, d_mem_norm = _rms_bwd(dmem_n, mem[0], mem_norm, zeros_mem, "rms_bwd_mem")
            rider, keys = scatter_rider([f"mem_kv{m}" for m in range(L)])
            du, landed = _mm_nt(dproj, full["in_a0"], name="mm_du_a0", rider=rider)
            received.update(zip(keys, landed))
            small_early = {
                "mem_norm": d_mem_norm.reshape(-1),
                "conv_w": jnp.stack(d_conv_w),
                "conv_b": jnp.concatenate(d_conv_b, axis=0),
                "w_rec_gate": jnp.stack(d_w_r),
                "b_rec_gate": jnp.concatenate(d_b_r, axis=0),
                "w_in_gate": jnp.stack(d_w_i),
                "b_in_gate": jnp.concatenate(d_b_i, axis=0),
                "lru_lambda": jnp.concatenate(d_lam, axis=0),
                "kv_norm": d_kv_norm.reshape(-1),
                "norm_b": jnp.concatenate(d_norm_b, axis=0),
                "final_norm": d_final_norm.reshape(-1),
            }
            rider = _Gather([_pack(list(small_early.values()), 256)])
            big_grads["in_a0"], (early_gathered,) = _mm_tn(u, dproj, N_DEV, name="mm_dw_in_a0", rider=rider)
        dh, d_norm_a[l] = _rms_bwd(du, h_in, norm_a_f[l], dh, f"rms_bwd_a{l}")
    grad_x = dh.reshape(x.shape)

    received.update(zip(["in_a0"], _all_to_all([big_grads["in_a0"]], "scatter_grads")))
    small_late = {"norm_a": jnp.concatenate(d_norm_a, axis=0)}
    (late_gathered,) = _all_gather([_pack(list(small_late.values()), 256)], "gather_small_grads")

    def update(key, w, m, v):
        shape = w.shape
        two_d = (-1, shape[-1])
        return [o.reshape(shape) for o in _adamw(received[key], w.reshape(two_d), m.reshape(two_d),
                                                 v.reshape(two_d), f"adamw_{key}")]

    def update_layers(prefix, w, m, v):
        per_layer = [update(f"{prefix}{l}", w[l], m[l], v[l]) for l in range(w.shape[0])]
        return [jnp.stack([per_layer[l][k] for l in range(w.shape[0])]) for k in range(4)]

    upd = {
        "w_mem_kv": update_layers("mem_kv", w_mem_kv, m_w_mem_kv, v_w_mem_kv),
        "w_in_a": update_layers("in_a", w_in_a, m_w_in_a, v_w_in_a),
        "w_out_a": update_layers("out_a", w_out_a, m_w_out_a, v_w_out_a),
        "w_kv": update("kv", w_kv, m_w_kv, v_w_kv),
        "w_in_b": update_layers("in_b", w_in_b, m_w_in_b, v_w_in_b),
        "w_out_b": update_layers("out_b", w_out_b, m_w_out_b, v_w_out_b),
    }

    small_grad = {}
    for part, got, name in ((small_early, early_gathered, "sum_small_early"), (small_late, late_gathered, "sum_small_late")):
        summed = _unpack(_sum_devices(got, name), [g.shape for g in part.values()])
        small_grad.update(zip(part, summed))
    small_names = list(small_grad)
    small_w = {"mem_norm": (mem_norm, m_mem_norm, v_mem_norm), "norm_a": (norm_a, m_norm_a, v_norm_a),
               "conv_w": (conv_w, m_conv_w, v_conv_w), "conv_b": (conv_b, m_conv_b, v_conv_b),
               "w_rec_gate": (w_rec_gate, m_w_rec_gate, v_w_rec_gate),
               "b_rec_gate": (b_rec_gate, m_b_rec_gate, v_b_rec_gate),
               "w_in_gate": (w_in_gate, m_w_in_gate, v_w_in_gate), "b_in_gate": (b_in_gate, m_b_in_gate, v_b_in_gate),
               "lru_lambda": (lru_lambda, m_lru_lambda, v_lru_lambda), "kv_norm": (kv_norm, m_kv_norm, v_kv_norm),
               "norm_b": (norm_b, m_norm_b, v_norm_b), "final_norm": (final_norm, m_final_norm, v_final_norm)}
    for k in small_names:
        w = small_w[k][0]
        if small_grad[k].shape != w.shape:
            n = w.shape[-1]
            small_grad[k] = lax.dynamic_slice_in_dim(small_grad[k], me * n, n, axis=-1)
    small_shapes = [small_w[k][0].shape for k in small_names]
    packed = [_pack([small_grad[k] for k in small_names], 256)[None]]
    packed += [_pack([small_w[k][t] for k in small_names], 256) for t in range(3)]
    small_out = [_unpack(o, small_shapes) for o in _adamw(*packed, "adamw_small")]
    for idx, k in enumerate(small_names):
        upd[k] = [small_out[t][idx] for t in range(4)]

    order = ["mem_norm", "w_mem_kv", "norm_a", "w_in_a", "conv_w", "conv_b", "w_rec_gate", "b_rec_gate", "w_in_gate",
             "b_in_gate", "lru_lambda", "w_out_a", "kv_norm", "w_kv", "norm_b", "w_in_b", "w_out_b", "final_norm"]
    return (loss, grad_x, *[upd[k][0] for k in order], *[upd[k][1] for k in order],
            *[upd[k][2] for k in order], *[upd[k][3] for k in order])
```

```python
import functools

import jax
import jax.numpy as jnp
from jax import lax
from jax.experimental import pallas as pl
from jax.experimental.pallas import tpu as pltpu

F32 = jnp.float32
BF16 = jnp.bfloat16

N_DEV = 8
EPS = 1e-6
LRU_C = 8.0
HEAD = 128
CONV_WIDTH = 4
LANES = 128
SUB = 8
SB_EXHAUSTED = 110.0
VMEM_LIMIT = 56 * 1024 * 1024
RESIDENT_WEIGHT_BYTES = 8 * 1024 * 1024

ADAM_LR = 0.001
ADAM_B1 = 0.9
ADAM_B2 = 0.999
ADAM_EPS = 1e-08
ADAM_WD = 0.01
ADAM_STEP = 10

HBM_SPEC = pl.BlockSpec(memory_space=pltpu.HBM)


def _params(*semantics):
    return pltpu.CompilerParams(dimension_semantics=semantics, vmem_limit_bytes=VMEM_LIMIT)


def _tile(n, cap):
    if n <= cap:
        return n
    t = cap - cap % LANES
    while n % t:
        t -= LANES
    return t


def _row_tile(n, cap):
    if n <= cap:
        return n
    t = cap - cap % 8
    while t >= 8:
        if n % t == 0:
            return t
        t -= 8
    return n


def _rows_for(n, bytes_per_row, budget=2 * 1024 * 1024):
    return _row_tile(n, max(8, budget // bytes_per_row))


def _dot(a, b):
    return lax.dot_general(a, b, (((1,), (0,)), ((), ())), preferred_element_type=F32)


def _dot_nt(a, b):
    return lax.dot_general(a, b, (((1,), (1,)), ((), ())), preferred_element_type=F32)


def _dot_tn(a, b):
    return lax.dot_general(a, b, (((0,), (0,)), ((), ())), preferred_element_type=F32)


def _split_bf16(x):
    hi = x.astype(BF16)
    lo = (x - hi.astype(F32)).astype(BF16)
    return hi, lo


def _softplus(x):
    return jnp.maximum(x, 0.0) + jnp.log1p(jnp.exp(-jnp.abs(x)))


def _gate_sigmoid(x):
    return 0.5 * jnp.tanh(0.5 * x) + 0.5


def _silu_and_grad(g):
    sg = _gate_sigmoid(g)
    return g * sg, sg * (1.0 + g * (1.0 - sg))


def _one_minus_square(a, log_a):
    x = 2.0 * log_a
    series = -x * (1.0 + x * (0.5 + x * (1.0 / 6.0)))
    return jnp.where(x > -0.03, series, 1.0 - a * a)


def _rms_fwd(x, g, name):
    T, D = x.shape
    tt = _row_tile(T, 512)

    def body(x_ref, g_ref, o_ref):
        xv = x_ref[...]
        r = lax.rsqrt(jnp.mean(xv * xv, axis=-1, keepdims=True) + EPS)
        o_ref[...] = ((xv * r) * g_ref[...]).astype(o_ref.dtype)

    return pl.pallas_call(
        body, name=name, grid=(T // tt,),
        in_specs=[pl.BlockSpec((tt, D), lambda i: (i, 0)), pl.BlockSpec((1, D), lambda i: (0, 0))],
        out_specs=pl.BlockSpec((tt, D), lambda i: (i, 0)),
        out_shape=jax.ShapeDtypeStruct((T, D), BF16),
        compiler_params=_params("parallel"),
    )(x, g.reshape(1, D))


def _rms_bwd(du, h, g, res, name):
    T, D = h.shape
    tt = _row_tile(T, 256)

    def body(du_ref, h_ref, g_ref, res_ref, dh_ref, dg_ref):
        @pl.when(pl.program_id(0) == 0)
        def _():
            dg_ref[...] = jnp.zeros_like(dg_ref)

        hv = h_ref[...]
        duv = du_ref[...]
        r = lax.rsqrt(jnp.mean(hv * hv, axis=-1, keepdims=True) + EPS)
        xhat = hv * r
        dxhat = duv * g_ref[...]
        dh_ref[...] = res_ref[...] + r * (dxhat - xhat * jnp.mean(dxhat * xhat, axis=-1, keepdims=True))
        dg_ref[...] += jnp.sum(duv * xhat, axis=0, keepdims=True)

    row = pl.BlockSpec((tt, D), lambda i: (i, 0))
    vec = pl.BlockSpec((1, D), lambda i: (0, 0))
    return pl.pallas_call(
        body, name=name, grid=(T // tt,),
        in_specs=[row, row, vec, row],
        out_specs=[row, vec],
        out_shape=[jax.ShapeDtypeStruct((T, D), F32), jax.ShapeDtypeStruct((1, D), F32)],
        compiler_params=_params("arbitrary"),
    )(du, h, g.reshape(1, D), res)


def _mm_nn(a, b3, res=None, out_dtype=F32, name=None, rider=None, norms=()):
    M, K = a.shape
    S, _, ns = b3.shape
    whole_b = S == 1 and K * ns * 2 <= RESIDENT_WEIGHT_BYTES
    tm = _row_tile(M, 512 if whole_b or a.dtype != BF16 else 1024)
    tn = ns if whole_b else _tile(ns, 512)
    nj = ns // tn

    assert not norms or (whole_b and rider is None)
    n_in = 2 + (res is not None) + len(norms)

    def body(*refs):
        a_ref, b_ref = refs[:2]
        o_ref = refs[n_in]
        acc = _dot(a_ref[...].astype(BF16), b_ref[...])
        if res is not None:
            acc = acc + refs[2][...]
        o_ref[...] = acc.astype(o_ref.dtype)
        if norms:
            r = lax.rsqrt(jnp.mean(acc * acc, axis=-1, keepdims=True) + EPS)
            for g_ref, u_ref in zip(refs[n_in - len(norms):n_in], refs[n_in + 1:]):
                u_ref[...] = ((acc * r) * g_ref[...]).astype(u_ref.dtype)

    tile = pl.BlockSpec((tm, tn), lambda i, j: (i, j))
    in_specs = [pl.BlockSpec((tm, K), lambda i, j: (i, 0)),
                pl.BlockSpec((None, K, tn), lambda i, j: (j // nj, 0, j % nj))]
    args = [a, b3]
    if res is not None:
        in_specs.append(tile)
        args.append(res)
    in_specs += [pl.BlockSpec((1, tn), lambda i, j: (0, j))] * len(norms)
    args += [g.reshape(1, -1) for g in norms]
    outs, landed = _call_with_rider(
        body, rider, *_grid_ends((M // tm, S * nj)), name=name, grid=(M // tm, S * nj),
        in_specs=in_specs,
        out_specs=[tile] * (1 + len(norms)),
        out_shape=[jax.ShapeDtypeStruct((M, S * ns), out_dtype)] + [jax.ShapeDtypeStruct((M, S * ns), BF16)] * len(norms),
        scratch_shapes=[], semantics=("parallel", "parallel"), args=args)
    if norms:
        return tuple(outs)
    return outs[0] if rider is None else (outs[0], landed)


def _grid_ends(grid):
    def first():
        return functools.reduce(jnp.logical_and, [pl.program_id(d) == 0 for d in range(len(grid))])

    def last():
        return functools.reduce(jnp.logical_and, [pl.program_id(d) == n - 1 for d, n in enumerate(grid)])

    return first, last


def _mm_nt(a, b3, out_dtype=F32, name=None, rider=None):
    M = a.shape[0]
    S, N, ns = b3.shape
    whole_b = S * N * ns * 2 <= RESIDENT_WEIGHT_BYTES
    tm = _row_tile(M, 512 if whole_b or a.dtype != BF16 else 1024)
    tn = N if whole_b else _tile(N, 512)

    def body(a_ref, b_ref, o_ref):
        acc = _dot_nt(a_ref[:, 0:ns].astype(BF16), b_ref[0])
        for s in range(1, S):
            acc = acc + _dot_nt(a_ref[:, s * ns:(s + 1) * ns].astype(BF16), b_ref[s])
        o_ref[...] = acc.astype(o_ref.dtype)

    (out,), landed = _call_with_rider(
        body, rider, *_grid_ends((M // tm, N // tn)), name=name, grid=(M // tm, N // tn),
        in_specs=[pl.BlockSpec((tm, S * ns), lambda i, j: (i, 0)),
                  pl.BlockSpec((S, tn, ns), lambda i, j: (0, j, 0))],
        out_specs=[pl.BlockSpec((tm, tn), lambda i, j: (i, j))],
        out_shape=[jax.ShapeDtypeStruct((M, N), out_dtype)],
        scratch_shapes=[], semantics=("parallel", "parallel"), args=(a, b3))
    return out if rider is None else (out, landed)


def _mm_tn(a, b, S, name=None, rider=None):
    T, K = a.shape
    ns = b.shape[1] // S
    tk = _tile(K, 1024)
    tn = _tile(ns, 512)
    nj = ns // tn
    tt = _row_tile(T, 2048)
    nt = T // tt

    def body(a_ref, b_ref, o_ref, acc_ref):
        t = pl.program_id(2)
        p = _dot_tn(a_ref[...].astype(BF16), b_ref[...].astype(BF16))

        @pl.when(t == 0)
        def _():
            acc_ref[...] = p

        @pl.when(t > 0)
        def _():
            acc_ref[...] += p

        @pl.when(t == nt - 1)
        def _():
            o_ref[...] = acc_ref[...].astype(o_ref.dtype)

    grid = (K // tk, S * nj, nt)
    (out,), landed = _call_with_rider(
        body, rider, *_grid_ends(grid), name=name, grid=grid,
        in_specs=[pl.BlockSpec((tt, tk), lambda i, j, t: (t, i)),
                  pl.BlockSpec((tt, tn), lambda i, j, t: (t, j))],
        out_specs=[pl.BlockSpec((None, tk, tn), lambda i, j, t: (j // nj, i, j % nj))],
        out_shape=[jax.ShapeDtypeStruct((S, K, ns), BF16)],
        scratch_shapes=[pltpu.VMEM((tk, tn), F32)],
        semantics=("parallel", "parallel", "arbitrary"), args=(a, b))
    return out if rider is None else (out, landed)


def _lru_coeffs(lam_row, r):
    cl = -LRU_C * _softplus(-lam_row)
    log_a = cl * r
    a = jnp.exp(log_a)
    em = _one_minus_square(a, log_a)
    return cl, a, em, jnp.sqrt(em)


def _rglru_fwd(proj, conv_w, conv_b, w_r, b_r, w_i, b_i, lam, name, rider=None, y_spare=0):
    T = proj.shape[0]
    DR = conv_w.shape[1]
    NB = DR // HEAD
    tt = _row_tile(T, 256)
    PAD = 8

    def body(x_ref, g_ref, cw_ref, cb_ref, wr_ref, br_ref, wi_ref, bi_ref, lam_ref,
             y_ref, xc_ref, r_ref, i_ref, h_ref, xbuf, hcar):
        @pl.when(pl.program_id(0) == 0)
        def _():
            xbuf[0:PAD, :] = jnp.zeros((PAD, DR), F32)
            hcar[...] = jnp.zeros_like(hcar)

        xbuf[PAD:PAD + tt, :] = x_ref[...]
        xc = cb_ref[...] + cw_ref[0:1, :] * xbuf[pl.ds(PAD - 3, tt), :]
        for k in range(1, CONV_WIDTH):
            xc = xc + cw_ref[k:k + 1, :] * xbuf[pl.ds(PAD - 3 + k, tt), :]
        xbuf[0:PAD, :] = xbuf[tt:tt + PAD, :]
        xc_ref[...] = xc
        xcb = xc.astype(BF16)
        for n in range(NB):
            sl = slice(n * HEAD, (n + 1) * HEAD)
            r_ref[:, sl] = jax.nn.sigmoid(_dot(xcb[:, sl], wr_ref[n]) + br_ref[:, sl])
            i_ref[:, sl] = _gate_sigmoid(_dot(xcb[:, sl], wi_ref[n]) + bi_ref[:, sl])
        _, a, _, mult = _lru_coeffs(lam_ref[...], r_ref[...])
        hs = mult * (i_ref[...] * xc)
        groups = tt // SUB
        a = a.reshape(groups, SUB, DR)
        hs = hs.reshape(groups, SUB, DR)
        sub = lax.broadcasted_iota(jnp.int32, (groups, SUB, DR), 1)
        d = 1
        while d < SUB:
            keep = sub >= d
            a_sh = jnp.where(keep, pltpu.roll(a, d, 1), 1.0)
            h_sh = jnp.where(keep, pltpu.roll(hs, d, 1), 0.0)
            hs = a * h_sh + hs
            a = a * a_sh
            d *= 2
        carry = hcar[...]
        for n in range(groups):
            h_ref[n * SUB:(n + 1) * SUB, :] = hs[n] + a[n] * carry
            carry = h_ref[(n + 1) * SUB - 1:(n + 1) * SUB, :]
        hcar[...] = carry
        h = h_ref[...]
        g = g_ref[...]
        y_ref[...] = (h * (g * _gate_sigmoid(g))).astype(y_ref.dtype)

    col = lambda j: pl.BlockSpec((tt, DR), lambda c: (c, j))
    vec = pl.BlockSpec((1, DR), lambda c: (0, 0))
    gate = pl.BlockSpec((NB, HEAD, HEAD), lambda c: (0, 0, 0))
    f32_out = jax.ShapeDtypeStruct((T, DR), F32)
    nc = T // tt
    return _call_with_rider(
        body, rider, lambda: pl.program_id(0) == 0, lambda: pl.program_id(0) == nc - 1,
        name=name, grid=(nc,),
        in_specs=[col(0), col(1), pl.BlockSpec((CONV_WIDTH, DR), lambda c: (0, 0)), vec, gate, vec, gate, vec, vec],
        out_specs=[col(0)] * 5,
        out_shape=[jax.ShapeDtypeStruct((T, DR + y_spare), BF16), f32_out, f32_out, f32_out, f32_out],
        scratch_shapes=[pltpu.VMEM((tt + PAD, DR), F32), pltpu.VMEM((1, DR), F32)],
        semantics=("arbitrary",),
        args=(proj, proj, conv_w, conv_b.reshape(1, DR), w_r, b_r.reshape(1, DR), w_i, b_i.reshape(1, DR),
              lam.reshape(1, DR)))


def _rglru_bwd(dy, proj, xc, r, i, h, conv_w, w_r, w_i, lam, name, rider=None, dx_spare=0):
    T = proj.shape[0]
    DR = conv_w.shape[1]
    NB = DR // HEAD
    tt = _row_tile(T, 128)
    nc = T // tt
    PAD = 8
    per = tt // PAD

    def body(dy_ref, x_ref, g_ref, xc_ref, r_ref, i_ref, h_ref, xprev_ref, hprev_ref,
             cw_ref, wr_ref, wi_ref, lam_ref,
             dxg_ref, dcw_ref, dcb_ref, dwr_ref, dbr_ref, dwi_ref, dbi_ref, dlam_ref,
             xbuf, dxcbuf, gcar, acar):
        step = pl.program_id(0)
        chunk = nc - 1 - step

        @pl.when(step == 0)
        def _():
            for ref in (dcw_ref, dcb_ref, dwr_ref, dbr_ref, dwi_ref, dbi_ref, dlam_ref, gcar, acar):
                ref[...] = jnp.zeros_like(ref)
            dxcbuf[tt:tt + PAD, :] = jnp.zeros((PAD, DR), F32)

        not_first = (chunk > 0).astype(F32)
        row = lax.broadcasted_iota(jnp.int32, (tt, DR), 0)
        silu, dsilu = _silu_and_grad(g_ref[...])
        dyv = dy_ref[...]
        hv = h_ref[...]
        dxg_ref[:, DR:2 * DR] = (dyv * hv * dsilu).astype(dxg_ref.dtype)
        dh = dyv * silu
        rv = r_ref[...]
        iv = i_ref[...]
        xcv = xc_ref[...]
        lam_row = lam_ref[...]
        cl, a, em, mult = _lru_coeffs(lam_row, rv)
        b = jnp.where(row == tt - 1, acar[...], pltpu.roll(a, tt - 1, 0))
        gs = dh
        d = 1
        while d < tt:
            keep = row < tt - d
            b_sh = jnp.where(keep, pltpu.roll(b, tt - d, 0), 1.0)
            g_sh = jnp.where(keep, pltpu.roll(gs, tt - d, 0), 0.0)
            gs = gs + b * g_sh
            b = b * b_sh
            d *= 2
        gt = gs + b * gcar[...]
        xbuf[0:tt, :] = gt
        gcar[...] = xbuf[0:1, :]
        acar[...] = _lru_coeffs(lam_row, r_ref[0:1, :])[1]
        h_before = hprev_ref[PAD - 1:PAD, :] * not_first
        hprev = jnp.where(row == 0, h_before, pltpu.roll(hv, 1, 0))
        da = gt * hprev
        dmult = gt * (iv * xcv)
        di = gt * mult * xcv
        dxc = gt * mult * iv
        dlog_a = da * a - dmult * (1.0 - em) / mult
        dr = dlog_a * cl
        dlam_ref[...] += jnp.sum(dlog_a * rv, axis=0, keepdims=True) * (LRU_C * jax.nn.sigmoid(-lam_row))
        drp = dr * rv * (1.0 - rv)
        dip = di * iv * (1.0 - iv)
        dbr_ref[...] += jnp.sum(drp, axis=0, keepdims=True)
        dbi_ref[...] += jnp.sum(dip, axis=0, keepdims=True)
        drpb = drp.astype(BF16)
        dipb = dip.astype(BF16)
        xcb = xcv.astype(BF16)
        for n in range(NB):
            sl = slice(n * HEAD, (n + 1) * HEAD)
            dxcbuf[0:tt, sl] = dxc[:, sl] + _dot_nt(drpb[:, sl], wr_ref[n]) + _dot_nt(dipb[:, sl], wi_ref[n])
            dwr_ref[n] += _dot_tn(xcb[:, sl], drpb[:, sl])
            dwi_ref[n] += _dot_tn(xcb[:, sl], dipb[:, sl])
        dxc_all = dxcbuf[0:tt, :]
        dcb_ref[...] += jnp.sum(dxc_all, axis=0, keepdims=True)
        xbuf[0:PAD, :] = xprev_ref[...] * not_first
        xbuf[PAD:PAD + tt, :] = x_ref[...]
        dx = cw_ref[0:1, :] * dxcbuf[pl.ds(3, tt), :]
        for k in range(1, CONV_WIDTH):
            dx = dx + cw_ref[k:k + 1, :] * dxcbuf[pl.ds(3 - k, tt), :]
        dxg_ref[:, 0:DR] = dx.astype(dxg_ref.dtype)
        for k in range(CONV_WIDTH):
            dcw_ref[k:k + 1, :] += jnp.sum(xbuf[pl.ds(PAD - 3 + k, tt), :] * dxc_all, axis=0, keepdims=True)
        dxcbuf[tt:tt + PAD, :] = dxcbuf[0:PAD, :]

    rev = lambda j: pl.BlockSpec((tt, DR), lambda s: (nc - 1 - s, j))
    prev = pl.BlockSpec((PAD, DR), lambda s: (jnp.maximum((nc - 1 - s) * per - 1, 0), 0))
    vec = pl.BlockSpec((1, DR), lambda s: (0, 0))
    gate = pl.BlockSpec((NB, HEAD, HEAD), lambda s: (0, 0, 0))
    taps = pl.BlockSpec((CONV_WIDTH, DR), lambda s: (0, 0))
    vec_out = jax.ShapeDtypeStruct((1, DR), F32)
    gate_out = jax.ShapeDtypeStruct((NB, HEAD, HEAD), F32)
    return _call_with_rider(
        body, rider, lambda: pl.program_id(0) == 0, lambda: pl.program_id(0) == nc - 1,
        name=name, grid=(nc,),
        in_specs=[rev(0), rev(0), rev(1), rev(0), rev(0), rev(0), rev(0), prev, prev, taps, gate, gate, vec],
        out_specs=[pl.BlockSpec((tt, 2 * DR), lambda s: (nc - 1 - s, 0)), taps, vec, gate, vec, gate, vec, vec],
        out_shape=[jax.ShapeDtypeStruct((T, 2 * DR + dx_spare), BF16), jax.ShapeDtypeStruct((CONV_WIDTH, DR), F32),
                   vec_out, gate_out, vec_out, gate_out, vec_out, vec_out],
        scratch_shapes=[pltpu.VMEM((tt + PAD, DR), F32), pltpu.VMEM((tt + PAD, DR), F32),
                        pltpu.VMEM((1, DR), F32), pltpu.VMEM((1, DR), F32)],
        semantics=("arbitrary",),
        args=(dy, proj, proj, xc, r, i, h, proj, h, conv_w, w_r, w_i, lam.reshape(1, DR)))


def _mem_probs(q, k, scale):
    s = _dot_nt(q, k) * scale
    p = jnp.exp(s - jnp.max(s, axis=-1, keepdims=True))
    return p * (1.0 / jnp.sum(p, axis=-1, keepdims=True))


def _memattn_fwd(proj, mkv, y_mix, DR, DM, name):
    T = proj.shape[0]
    M = mkv.shape[0]
    NH = DM // HEAD
    tt = _row_tile(T, 512)
    qcol = 2 * DR // DM
    scale = HEAD ** -0.5

    def body(q_ref, g_ref, k_ref, v_ref, _, y_ref):
        for n in range(NH):
            sl = slice(n * HEAD, (n + 1) * HEAD)
            p = _mem_probs(q_ref[:, sl].astype(BF16), k_ref[:, sl], scale)
            o = _dot(p.astype(BF16), v_ref[:, sl])
            g = g_ref[:, sl]
            y_ref[:, sl] = (o * (g * _gate_sigmoid(g))).astype(y_ref.dtype)

    return pl.pallas_call(
        body, name=name, grid=(T // tt,),
        in_specs=[pl.BlockSpec((tt, DM), lambda t: (t, qcol)), pl.BlockSpec((tt, DM), lambda t: (t, qcol + 1)),
                  pl.BlockSpec((M, DM), lambda t: (0, 0)), pl.BlockSpec((M, DM), lambda t: (0, 1)), HBM_SPEC],
        out_specs=pl.BlockSpec((tt, DM), lambda t: (t, DR // DM)),
        out_shape=jax.ShapeDtypeStruct(y_mix.shape, y_mix.dtype),
        input_output_aliases={4: 0},
        compiler_params=_params("parallel"),
    )(proj, proj, mkv, mkv, y_mix)


def _memattn_bwd(dy, proj, mkv, DR, DM, name, dproj=None):
    T = proj.shape[0]
    M = mkv.shape[0]
    NH = DM // HEAD
    tt = _row_tile(T, 512)
    qcol = 2 * DR // DM
    scale = HEAD ** -0.5

    def body(dy_ref, q_ref, g_ref, k_ref, v_ref, *rest):
        dqg_ref, dkv_ref = rest[-2:]

        @pl.when(pl.program_id(0) == 0)
        def _():
            dkv_ref[...] = jnp.zeros_like(dkv_ref)

        for n in range(NH):
            sl = slice(n * HEAD, (n + 1) * HEAD)
            qb = q_ref[:, sl].astype(BF16)
            kb = k_ref[:, sl]
            vb = v_ref[:, sl]
            p = _mem_probs(qb, kb, scale)
            pb = p.astype(BF16)
            o = _dot(pb, vb)
            silu, dsilu = _silu_and_grad(g_ref[:, sl])
            dyv = dy_ref[:, sl]
            dqg_ref[:, DM + n * HEAD:DM + (n + 1) * HEAD] = (dyv * o * dsilu).astype(dqg_ref.dtype)
            dob = (dyv * silu).astype(BF16)
            dp = _dot_nt(dob, vb)
            ds = (p * (dp - jnp.sum(dp * p, axis=-1, keepdims=True)) * scale).astype(BF16)
            dqg_ref[:, sl] = _dot(ds, kb).astype(dqg_ref.dtype)
            dkv_ref[:, sl] += _dot_tn(ds, qb)
            dkv_ref[:, DM + n * HEAD:DM + (n + 1) * HEAD] += _dot_tn(pb, dob)

    in_place = dproj is not None
    return pl.pallas_call(
        body, name=name, grid=(T // tt,),
        in_specs=[pl.BlockSpec((tt, DM), lambda t: (t, DR // DM)),
                  pl.BlockSpec((tt, DM), lambda t: (t, qcol)), pl.BlockSpec((tt, DM), lambda t: (t, qcol + 1)),
                  pl.BlockSpec((M, DM), lambda t: (0, 0)), pl.BlockSpec((M, DM), lambda t: (0, 1))]
                 + [HBM_SPEC] * in_place,
        out_specs=[pl.BlockSpec((tt, 2 * DM), lambda t: (t, DR // DM if in_place else 0)),
                   pl.BlockSpec((M, 2 * DM), lambda t: (0, 0))],
        out_shape=[jax.ShapeDtypeStruct(dproj.shape if in_place else (T, 2 * DM), BF16),
                   jax.ShapeDtypeStruct((M, 2 * DM), F32)],
        input_output_aliases={5: 0} if in_place else {},
        compiler_params=_params("arbitrary"),
    )(dy, proj, proj, mkv, mkv, *([dproj] if in_place else []))


def _sb_blocks(T):
    tk = _row_tile(T // 2, 256)
    tq = 2 * tk
    assert T % tq == 0
    return tq, tk


def _sb_upper(tk):
    row = lax.broadcasted_iota(jnp.int32, (tk, tk), 0)
    col = lax.broadcasted_iota(jnp.int32, (tk, tk), 1)
    return (row > col).astype(BF16)


def _sb_causal(tq, tk, d):
    row = lax.broadcasted_iota(jnp.int32, (tq, tk), 0)
    col = lax.broadcasted_iota(jnp.int32, (tq, tk), 1)
    return col + d * tk < row


def _sb_own_span(tile, carry, i, tq, tk):
    span = tq // tk
    for d in reversed(range(span)):
        lo = d * tk
        part = tile(span * i + d, tuple(c[lo:] for c in carry), _sb_causal(tq - lo, tk, 0), slice(lo, tq))
        carry = tuple(p if lo == 0 else jnp.concatenate([c[:lo], p], axis=0) for c, p in zip(carry, part))
    return carry


def _sb_earlier(tile, carry, jb_first, tq):
    half = tq // 2

    def walk(carry, jb, rows, watched):
        def more(c):
            return jnp.logical_and(c[0] >= 0, c[1] > 0)

        def step(c):
            jb, _, *rest = c
            rest = tile(jb, tuple(rest), None, rows)
            return jb - 1, _sb_alive(watched(rest[0])), *rest

        jb, _, *carry = lax.while_loop(more, step, (jb, _sb_alive(watched(carry[0])), *carry))
        return jb, tuple(carry)

    jb, carry = walk(carry, jb_first, slice(0, tq), lambda gone: gone[half:])
    _, low = walk(tuple(c[:half] for c in carry), jb, slice(0, half), lambda gone: gone)
    return tuple(jnp.concatenate([lo, c[half:]], axis=0) for lo, c in zip(low, carry))


def _later_sum(x, upper):
    n = x.shape[0]
    hi, lo = _split_bf16(x)
    both = _dot(jnp.concatenate([hi, lo], axis=0), upper)
    return both[0:n] + both[n:2 * n]


def _sb_alive(gone_c):
    return (jnp.min(gone_c) < SB_EXHAUSTED).astype(jnp.int32)


def _sb_weights(q, kb, gone_c, causal, upper, scale):
    return _sb_weights_of(_dot_nt(q, kb) * scale, gone_c, causal, upper)


def _sb_weights_of(z, gone_c, causal, upper):
    sp = jnp.where(z > 20.0, z, jnp.log(1.0 + jnp.exp(z)))
    spm = sp if causal is None else jnp.where(causal, sp, 0.0)
    gone = _later_sum(spm, upper) + gone_c
    w = jnp.exp(z - sp - gone)
    if causal is not None:
        w = jnp.where(causal, w, 0.0)
    return z, sp, spm, w


def _sb_fwd(proj, kv, DR, name, rider=None, y_spare=0):
    T = proj.shape[0]
    NH = DR // HEAD
    tq, tk = _sb_blocks(T)
    span = tq // tk
    scale = HEAD ** -0.5

    def body(q_ref, g_ref, k_ref, v_ref, y_ref, o_ref):
        i = pl.program_id(1)
        q = q_ref[...].astype(BF16)
        upper = _sb_upper(tk)

        def tile(jb, carry, causal, rows=slice(0, tq)):
            gone_c, acc = carry
            n = rows.stop - rows.start
            start = pl.multiple_of(jb * tk, tk)
            kb = k_ref[pl.ds(start, tk), :]
            vb = v_ref[pl.ds(start, tk), :]
            _, _, spm, w = _sb_weights(q[rows], kb, gone_c, causal, upper, scale)
            hi, lo = _split_bf16(w)
            pv = _dot(jnp.concatenate([hi, lo], axis=0), vb)
            return gone_c + jnp.sum(spm, axis=-1, keepdims=True), acc + pv[0:n] + pv[n:2 * n]

        carry = _sb_own_span(tile, (jnp.zeros((tq, 1), F32), jnp.zeros((tq, HEAD), F32)), i, tq, tk)

        _, acc = _sb_earlier(tile, carry, span * i - 1, tq)
        o_ref[...] = acc
        g = g_ref[...]
        y_ref[...] = (acc * (g * _gate_sigmoid(g))).astype(y_ref.dtype)

    blk = lambda off: pl.BlockSpec((tq, HEAD), lambda h, i: (i, off + h))
    whole = lambda off: pl.BlockSpec((T, HEAD), lambda h, i: (0, off + h))
    nq = T // tq
    return _call_with_rider(
        body, rider, lambda: (pl.program_id(0) == 0) & (pl.program_id(1) == 0),
        lambda: (pl.program_id(0) == NH - 1) & (pl.program_id(1) == nq - 1),
        name=name, grid=(NH, nq),
        in_specs=[blk(0), blk(NH), whole(0), whole(NH)],
        out_specs=[blk(0), blk(0)],
        out_shape=[jax.ShapeDtypeStruct((T, DR + y_spare), BF16), jax.ShapeDtypeStruct((T, DR), F32)],
        scratch_shapes=[], semantics=("parallel", "arbitrary"),
        args=(proj, proj, kv, kv))


def _sb_bwd(dy, proj, kv, o, DR, name, rider=None):
    T = proj.shape[0]
    NH = DR // HEAD
    tq, tk = _sb_blocks(T)
    span = tq // tk
    scale = HEAD ** -0.5

    def body(dy_ref, q_ref, g_ref, k_ref, v_ref, o_ref, dq_ref, dg_ref, dk_ref, dv_ref):
        i = pl.program_id(1)

        @pl.when(i == 0)
        def _():
            dk_ref[...] = jnp.zeros_like(dk_ref)
            dv_ref[...] = jnp.zeros_like(dv_ref)

        qf = q_ref[...]
        q = qf.astype(BF16)
        q_t = qf.T.astype(BF16)
        upper = _sb_upper(tk)
        silu, dsilu = _silu_and_grad(g_ref[...])
        dyv = dy_ref[...]
        ov = o_ref[...]
        dg_ref[...] = (dyv * ov * dsilu).astype(dg_ref.dtype)
        do = dyv * silu
        dob = do.astype(BF16)
        do_t = do.T.astype(BF16)
        total = jnp.sum(dob.astype(F32) * ov, axis=-1, keepdims=True)

        def tile(jb, carry, causal, rows=slice(0, tq)):
            gone_c, e_after, dq = carry
            start = pl.multiple_of(jb * tk, tk)
            kb = k_ref[pl.ds(start, tk), :]
            vb = v_ref[pl.ds(start, tk), :]
            z, sp, spm, w = _sb_weights(q[rows], kb, gone_c, causal, upper, scale)
            e = _dot_nt(dob[rows], vb) * w
            dz = e - (total[rows] - (_later_sum(e, upper) + e_after)) * jnp.exp(z - sp)
            if causal is not None:
                dz = jnp.where(causal, dz, 0.0)
            dzb = dz.astype(BF16)
            dk_ref[jb] += _dot(q_t[:, rows], dzb) * scale
            dv_ref[jb] += _dot(do_t[:, rows], w.astype(BF16))
            return (gone_c + jnp.sum(spm, axis=-1, keepdims=True), e_after + jnp.sum(e, axis=-1, keepdims=True),
                    dq + _dot(dzb, kb))

        carry = _sb_own_span(
            tile, (jnp.zeros((tq, 1), F32), jnp.zeros((tq, 1), F32), jnp.zeros((tq, HEAD), F32)), i, tq, tk)

        *_, dq = _sb_earlier(tile, carry, span * i - 1, tq)
        dq_ref[...] = (dq * scale).astype(dq_ref.dtype)

    blk = lambda off: pl.BlockSpec((tq, HEAD), lambda h, i: (i, off + h))
    whole = lambda off: pl.BlockSpec((T, HEAD), lambda h, i: (0, off + h))
    keys_t = pl.BlockSpec((None, T // tk, HEAD, tk), lambda h, i: (h, 0, 0, 0))
    keys_t_shape = jax.ShapeDtypeStruct((NH, T // tk, HEAD, tk), F32)
    nq = T // tq
    return _call_with_rider(
        body, rider, lambda: (pl.program_id(0) == 0) & (pl.program_id(1) == 0),
        lambda: (pl.program_id(0) == NH - 1) & (pl.program_id(1) == nq - 1),
        name=name, grid=(NH, nq),
        in_specs=[blk(0), blk(0), blk(NH), whole(0), whole(NH), blk(0)],
        out_specs=[blk(0), blk(0), keys_t, keys_t],
        out_shape=[jax.ShapeDtypeStruct((T, DR), BF16), jax.ShapeDtypeStruct((T, DR), BF16),
                   keys_t_shape, keys_t_shape],
        scratch_shapes=[], semantics=("parallel", "arbitrary"),
        args=(dy, proj, proj, kv, kv, o))


def _merge_dkv(parts, name):
    NH, nblk, _, tk = parts[0][0].shape
    DR = NH * HEAD
    n = len(parts)

    def body(*refs):
        o_ref = refs[-1]
        for which in range(2):
            for h in range(NH):
                acc = refs[which][h]
                for p in range(1, n):
                    acc = acc + refs[2 * p + which][h]
                col = which * DR + h * HEAD
                o_ref[:, col:col + HEAD] = acc.T.astype(o_ref.dtype)

    blk = pl.BlockSpec((NH, None, HEAD, tk), lambda t: (0, t, 0, 0))
    return pl.pallas_call(
        body, name=name, grid=(nblk,),
        in_specs=[blk] * (2 * n),
        out_specs=pl.BlockSpec((tk, 2 * DR), lambda t: (t, 0)),
        out_shape=jax.ShapeDtypeStruct((nblk * tk, 2 * DR), BF16),
        compiler_params=_params("parallel"),
    )(*[a for pair in parts for a in pair])


def _final_loss(h, g, target, name):
    T, D = h.shape
    tt = _row_tile(T, 256)

    def body(h_ref, g_ref, t_ref, dh_ref, dg_ref, sq_ref):
        @pl.when(pl.program_id(0) == 0)
        def _():
            dg_ref[...] = jnp.zeros_like(dg_ref)
            sq_ref[...] = jnp.zeros_like(sq_ref)

        hv = h_ref[...]
        gv = g_ref[...]
        r = lax.rsqrt(jnp.mean(hv * hv, axis=-1, keepdims=True) + EPS)
        xhat = hv * r
        err = xhat * gv - t_ref[...]
        sq_ref[...] += jnp.sum(err * err, axis=0, keepdims=True)
        dy = err * (1.0 / D)
        dxhat = dy * gv
        dh_ref[...] = r * (dxhat - xhat * jnp.mean(dxhat * xhat, axis=-1, keepdims=True))
        dg_ref[...] += jnp.sum(dy * xhat, axis=0, keepdims=True)

    row = pl.BlockSpec((tt, D), lambda i: (i, 0))
    vec = pl.BlockSpec((1, D), lambda i: (0, 0))
    return pl.pallas_call(
        body, name=name, grid=(T // tt,),
        in_specs=[row, vec, row],
        out_specs=[row, vec, vec],
        out_shape=[jax.ShapeDtypeStruct((T, D), F32), jax.ShapeDtypeStruct((1, D), F32),
                   jax.ShapeDtypeStruct((1, D), F32)],
        compiler_params=_params("arbitrary"),
    )(h, g.reshape(1, D), target)


def _sum_parts(parts_ref):
    g = parts_ref[0].astype(F32)
    for s in range(1, parts_ref.shape[0]):
        g = g + parts_ref[s].astype(F32)
    return g


def _adamw(parts, w, m, v, name):
    P, R, C = parts.shape
    tr = _rows_for(R, P * C * 4)

    def body(p_ref, w_ref, m_ref, v_ref, g_ref, d_ref, nm_ref, nv_ref):
        g = _sum_parts(p_ref)
        nm = ADAM_B1 * m_ref[...] + (1.0 - ADAM_B1) * g
        nv = ADAM_B2 * v_ref[...] + (1.0 - ADAM_B2) * jnp.square(g)
        m_hat = nm / (1.0 - ADAM_B1 ** ADAM_STEP)
        v_hat = nv / (1.0 - ADAM_B2 ** ADAM_STEP)
        g_ref[...] = g
        d_ref[...] = -ADAM_LR * (m_hat / (jnp.sqrt(v_hat) + ADAM_EPS) + ADAM_WD * w_ref[...])
        nm_ref[...] = nm
        nv_ref[...] = nv

    row = pl.BlockSpec((tr, C), lambda i: (i, 0))
    out = jax.ShapeDtypeStruct((R, C), F32)
    return pl.pallas_call(
        body, name=name, grid=(R // tr,),
        in_specs=[pl.BlockSpec((P, tr, C), lambda i: (0, i, 0)), row, row, row],
        out_specs=[row] * 4,
        out_shape=[out] * 4,
        compiler_params=_params("parallel"),
    )(parts, w, m, v)


def _sum_devices(parts, name):
    P, R, C = parts.shape
    tr = _rows_for(R, P * C * 4)

    def body(p_ref, o_ref):
        o_ref[...] = _sum_parts(p_ref)

    return pl.pallas_call(
        body, name=name, grid=(R // tr,),
        in_specs=[pl.BlockSpec((P, tr, C), lambda i: (0, i, 0))],
        out_specs=pl.BlockSpec((tr, C), lambda i: (i, 0)),
        out_shape=jax.ShapeDtypeStruct((R, C), F32),
        compiler_params=_params("parallel"),
    )(parts)


def _mesh_position():
    return lax.axis_index("x"), lax.axis_index("y"), lax.axis_index("c")


def _device_index(p):
    return 4 * p[0] + 2 * p[1] + p[2]


class _Gather:
    def __init__(self, arrs):
        self.arrs = list(arrs)
        self.n = len(self.arrs)

    def out_shape(self):
        return [jax.ShapeDtypeStruct((N_DEV,) + a.shape, a.dtype) for a in self.arrs]

    def scratch(self):
        return [pltpu.SemaphoreType.DMA((self.n, 7)), pltpu.SemaphoreType.DMA((self.n, 7)),
                pltpu.SemaphoreType.DMA((self.n,))]

    def _plan(self, ins, outs, sems):
        send_sems, recv_sems, local_sems = sems
        x, y, c = _mesh_position()
        me, sibling = (x, y, c), (x, y, 1 - c)
        chips = [(1 - x, y), (x, 1 - y), (1 - x, 1 - y)]

        def slot(a, p):
            return outs[a].at[_device_index(p)]

        def copy(a, k, block, to, src=None):
            return pltpu.make_async_remote_copy(
                src_ref=slot(a, block) if src is None else src, dst_ref=slot(a, block),
                send_sem=send_sems.at[a, k], recv_sem=recv_sems.at[a, k],
                device_id=to, device_id_type=pl.DeviceIdType.MESH)

        mine = [pltpu.make_async_copy(ins[a], slot(a, me), local_sems.at[a]) for a in range(self.n)]
        first = []
        for a in range(self.n):
            first.append(copy(a, 0, me, sibling, src=ins[a]))
            first += [copy(a, 1 + j, me, (*chip, c), src=ins[a]) for j, chip in enumerate(chips)]
        return me, sibling, c, chips, copy, mine, first

    def start(self, ins, outs, sems):
        *_, mine, first = self._plan(ins, outs, sems)
        for cp in mine + first:
            cp.start()

    def wait(self, ins, outs, sems):
        me, sibling, c, chips, copy, mine, first = self._plan(ins, outs, sems)
        passed = []
        for a in range(self.n):
            for j, chip in enumerate(chips):
                copy(a, 1 + j, (*chip, c), me).wait_recv()
                fwd = copy(a, 4 + j, (*chip, c), sibling)
                fwd.start()
                passed.append(fwd)
        for a in range(self.n):
            copy(a, 0, sibling, me).wait_recv()
            for j, chip in enumerate(chips):
                copy(a, 4 + j, (*chip, 1 - c), me).wait_recv()
        for cp in first + passed:
            cp.wait_send()
        for cp in mine:
            cp.wait()


def _exchange_now(ex, name):
    n = ex.n

    def body(*refs):
        ins, outs, sems = refs[:n], refs[n:2 * n], refs[2 * n:]
        ex.start(ins, outs, sems)
        ex.wait(ins, outs, sems)

    return pl.pallas_call(
        body, name=name,
        in_specs=[HBM_SPEC] * n, out_specs=[HBM_SPEC] * n,
        out_shape=ex.out_shape(), scratch_shapes=ex.scratch(),
    )(*ex.arrs)


def _all_gather(arrs, name):
    return _exchange_now(_Gather(arrs), name)


class _Scatter:
    def __init__(self, arrs):
        self.arrs = list(arrs)
        self.n = len(self.arrs)

    def out_shape(self):
        return [jax.ShapeDtypeStruct(a.shape, a.dtype) for a in self.arrs]

    def scratch(self):
        return [pltpu.SemaphoreType.DMA((self.n, 7)), pltpu.SemaphoreType.DMA((self.n, 7)),
                pltpu.SemaphoreType.DMA((self.n,))]

    def _copies(self, ins, outs, sems, arrivals):
        send_sems, recv_sems, local_sems = sems
        x, y, c = _mesh_position()
        me = _device_index((x, y, c))
        peers = [(1 - x if k & 4 else x, 1 - y if k & 2 else y, 1 - c if k & 1 else c) for k in range(1, N_DEV)]
        local, sends, recvs = [], [], []
        for a in range(self.n):
            local.append(pltpu.make_async_copy(ins[a].at[me], outs[a].at[me], local_sems.at[a]))
            for k, peer in enumerate(peers):
                there = _device_index(peer)
                src = ins[a].at[there]
                sem = dict(send_sem=send_sems.at[a, k], recv_sem=recv_sems.at[a, k],
                           device_id=peer, device_id_type=pl.DeviceIdType.MESH)
                sends.append(pltpu.make_async_remote_copy(src_ref=src, dst_ref=outs[a].at[me], **sem))
                if arrivals:
                    recvs.append(pltpu.make_async_remote_copy(src_ref=src, dst_ref=outs[a].at[there], **sem))
        return local, sends, recvs

    def start(self, ins, outs, sems):
        local, sends, _ = self._copies(ins, outs, sems, arrivals=False)
        for cp in local + sends:
            cp.start()

    def wait(self, ins, outs, sems):
        local, sends, recvs = self._copies(ins, outs, sems, arrivals=True)
        for cp in recvs:
            cp.wait_recv()
        for cp in sends:
            cp.wait_send()
        for cp in local:
            cp.wait()


def _all_to_all(arrs, name):
    return _exchange_now(_Scatter(arrs), name)


def _call_with_rider(body, rider, first, last, *, name, grid, in_specs, out_specs, out_shape, scratch_shapes,
                     semantics, args):
    if rider is None:
        out = pl.pallas_call(body, name=name, grid=grid, in_specs=in_specs, out_specs=out_specs,
                             out_shape=out_shape, scratch_shapes=scratch_shapes,
                             compiler_params=_params(*semantics))(*args)
        return out, None
    n, n_in, n_out = rider.n, len(in_specs), len(out_specs)

    def riding(*refs):
        ins, r_in = refs[:n_in], refs[n_in:n_in + n]
        outs, r_out = refs[n_in + n:n_in + n + n_out], refs[n_in + n + n_out:n_in + 2 * n + n_out]
        scratch, sems = refs[n_in + 2 * n + n_out:-3], refs[-3:]

        @pl.when(first())
        def _():
            rider.start(r_in, r_out, sems)

        body(*ins, *outs, *scratch)

        @pl.when(last())
        def _():
            rider.wait(r_in, r_out, sems)

    out = pl.pallas_call(
        riding, name=name, grid=grid,
        in_specs=list(in_specs) + [HBM_SPEC] * n, out_specs=list(out_specs) + [HBM_SPEC] * n,
        out_shape=list(out_shape) + rider.out_shape(),
        scratch_shapes=list(scratch_shapes) + rider.scratch(),
        compiler_params=_params(*["arbitrary"] * len(grid)),
    )(*args, *rider.arrs)
    return out[:n_out], out[n_out:]


def _pack(arrs, row_multiple):
    parts = []
    rows = 0
    for a in arrs:
        flat = a.reshape(-1).astype(F32)
        r = -(-flat.shape[0] // (8 * LANES)) * 8
        parts.append(jnp.pad(flat, (0, r * LANES - flat.shape[0])).reshape(r, LANES))
        rows += r
    pad = -rows % row_multiple
    if pad:
        parts.append(jnp.zeros((pad, LANES), F32))
    return jnp.concatenate(parts, axis=0)


def _unpack(buf, shapes, lead=()):
    out = []
    r0 = 0
    for shape in shapes:
        size = 1
        for s in shape:
            size *= s
        r = -(-size // (8 * LANES)) * 8
        part = buf[..., r0:r0 + r, :].reshape(lead + (r * LANES,))[..., :size]
        out.append(part.reshape(lead + tuple(shape)))
        r0 += r
    return out


def _gathered_cols(g):
    g = jnp.moveaxis(g, 0, -2)
    return g.reshape(g.shape[:-2] + (g.shape[-2] * g.shape[-1],))


def kernel(x, mem, mem_norm, w_mem_kv, norm_a, w_in_a, conv_w, conv_b, w_rec_gate, b_rec_gate, w_in_gate, b_in_gate, lru_lambda, w_out_a, kv_norm, w_kv, norm_b, w_in_b, w_out_b, final_norm, loss_target, m_mem_norm, m_w_mem_kv, m_norm_a, m_w_in_a, m_conv_w, m_conv_b, m_w_rec_gate, m_b_rec_gate, m_w_in_gate, m_b_in_gate, m_lru_lambda, m_w_out_a, m_kv_norm, m_w_kv, m_norm_b, m_w_in_b, m_w_out_b, m_final_norm, v_mem_norm, v_w_mem_kv, v_norm_a, v_w_in_a, v_conv_w, v_conv_b, v_w_rec_gate, v_b_rec_gate, v_w_in_gate, v_b_in_gate, v_lru_lambda, v_w_out_a, v_kv_norm, v_w_kv, v_norm_b, v_w_in_b, v_w_out_b, v_final_norm):
    xs = x[0]
    T, D = xs.shape
    L = w_mem_kv.shape[0]
    NA = w_in_a.shape[0]
    NB = w_in_b.shape[0]
    DM2 = w_mem_kv.shape[2]
    DM = DM2 // 2
    DR = w_rec_gate.shape[1] * w_rec_gate.shape[2]
    me = _device_index(_mesh_position())

    small_sharded = [norm_a, conv_w, conv_b, b_rec_gate, b_in_gate, lru_lambda]
    shard = {f"mem_kv{l}": w_mem_kv[l] for l in range(L)}
    shard.update({f"in_a{l}": w_in_a[l] for l in range(NA)}, **{f"out_a{l}": w_out_a[l] for l in range(NA)})
    shard.update({f"in_b{j}": w_in_b[j] for j in range(NB)}, **{f"out_b{j}": w_out_b[j] for j in range(NB)})
    shard["kv"] = w_kv
    shard = {k: w.astype(BF16) for k, w in shard.items()}
    now = ["in_a0", "mem_kv0"]
    gathered = _all_gather([shard[k] for k in now] + [_pack(small_sharded, 8)], "gather_params")
    full = dict(zip(now, gathered[:-1]))

    def gather_rider(keys):
        return _Gather([shard[k] for k in keys]), keys

    def out_proj(key):
        return full[key].reshape(1, -1, D)

    def mem_proj(l):
        return full[f"mem_kv{l}"].reshape(1, D, DM2)

    norm_a_f, conv_w_f, conv_b_f, b_r_f, b_i_f, lam_f = [
        _gathered_cols(s) for s in _unpack(gathered[-1], [s.shape for s in small_sharded], lead=(N_DEV,))]
    w_r_bf = w_rec_gate.astype(BF16)
    w_i_bf = w_in_gate.astype(BF16)

    zeros_mem = jnp.zeros_like(mem[0])
    mem_n = _rms_fwd(mem[0], mem_norm, "rms_mem")
    mkv = [_mm_nn(mem_n, mem_proj(0), out_dtype=BF16, name="mm_mem_kv0")]

    h = xs
    u = _rms_fwd(h, norm_a_f[0], "rms_a0")
    saved_a = []
    for l in range(NA):
        rider, keys = gather_rider([f"out_a{l}"])
        proj, landed = _mm_nn(u, full[f"in_a{l}"], name=f"mm_in_a{l}", rider=rider)
        full.update(zip(keys, landed))
        later_mem = [f"mem_kv{m}" for m in range(1, L)] if l == 0 else []
        rider, keys = gather_rider([f"in_a{l + 1}" if l + 1 < NA else "kv"] + later_mem)
        (y_rnn, xc, r, i, hr), landed = _rglru_fwd(proj, conv_w_f[l], conv_b_f[l], w_r_bf[l], b_r_f[l], w_i_bf[l],
                                                   b_i_f[l], lam_f[l], f"rglru_fwd{l}", rider, y_spare=DM)
        full.update(zip(keys, landed))
        if l == 0:
            mkv += [_mm_nn(mem_n, mem_proj(m), out_dtype=BF16, name=f"mm_mem_kv{m}") for m in range(1, L)]
        ycat = _memattn_fwd(proj, mkv[l], y_rnn, DR, DM, f"memattn_fwd_a{l}")
        gains = [norm_a_f[l + 1]] if l + 1 < NA else [kv_norm, norm_b[0]]
        h_next, *normed = _mm_nn(ycat, out_proj(f"out_a{l}"), res=h, name=f"mm_out_a{l}", norms=gains)
        saved_a.append((h, u, proj, xc, r, i, hr, ycat))
        h, u = h_next, normed[-1]
    h_kv, u_kv = h, normed[0]
    rider, keys = gather_rider(["in_b0"])
    kv, landed = _mm_nn(u_kv, full["kv"], out_dtype=BF16, name="mm_kv", rider=rider)
    full.update(zip(keys, landed))
    saved_b = []
    for j in range(NB):
        proj = _mm_nn(u, full[f"in_b{j}"], name=f"mm_in_b{j}")
        rider, keys = gather_rider([f"out_b{j}"] + ([f"in_b{j + 1}"] if j + 1 < NB else []))
        (y_sb, o_sb), landed = _sb_fwd(proj, kv, DR, f"sb_fwd{j}", rider, y_spare=DM)
        full.update(zip(keys, landed))
        ycat = _memattn_fwd(proj, mkv[NA + j], y_sb, DR, DM, f"memattn_fwd_b{j}")
        if j + 1 < NB:
            h_next, u_next = _mm_nn(ycat, out_proj(f"out_b{j}"), res=h, name=f"mm_out_b{j}", norms=[norm_b[j + 1]])
        else:
            h_next, u_next = _mm_nn(ycat, out_proj(f"out_b{j}"), res=h, name=f"mm_out_b{j}"), None
        saved_b.append((h, u, proj, o_sb, ycat))
        h, u = h_next, u_next

    dh, d_final_norm, sq = _final_loss(h, final_norm, loss_target[0], "final_loss")
    loss = lax.psum(0.5 * jnp.sum(sq) / D, ("x", "y", "c"))

    big_grads = {}
    received = {}

    def scatter_rider(keys):
        return _Scatter([big_grads[k] for k in keys]), keys

    dmkv = [None] * L
    d_norm_b = [None] * NB
    dkv_parts = []
    for j in reversed(range(NB)):
        h_in, u, proj, o_sb, ycat = saved_b[j]
        dy = _mm_nt(dh, out_proj(f"out_b{j}"), name=f"mm_dy_b{j}")
        big_grads[f"out_b{j}"] = _mm_tn(ycat, dh, 1, name=f"mm_dw_out_b{j}").reshape(N_DEV, -1, D)
        rider, keys = scatter_rider(([f"in_b{j + 1}"] if j + 1 < NB else []) + [f"out_b{j}"])
        (dq, dg, dk, dv), landed = _sb_bwd(dy, proj, kv, o_sb, DR, f"sb_bwd{j}", rider)
        received.update(zip(keys, landed))
        dkv_parts.append((dk, dv))
        dqg_mem, dmkv[NA + j] = _memattn_bwd(dy, proj, mkv[NA + j], DR, DM, f"memattn_bwd_b{j}")
        dproj = jnp.concatenate([dq, dg, dqg_mem], axis=-1)
        du = _mm_nt(dproj, full[f"in_b{j}"], name=f"mm_du_b{j}")
        big_grads[f"in_b{j}"] = _mm_tn(u, dproj, N_DEV, name=f"mm_dw_in_b{j}")
        dh, d_norm_b[j] = _rms_bwd(du, h_in, norm_b[j], dh, f"rms_bwd_b{j}")

    dkv = _merge_dkv(dkv_parts, "merge_dkv")
    rider, keys = scatter_rider(["in_b0"])
    du_kv, landed = _mm_nt(dkv, full["kv"], name="mm_du_kv", rider=rider)
    received.update(zip(keys, landed))
    big_grads["kv"] = _mm_tn(u_kv, dkv, N_DEV, name="mm_dw_kv")
    dh, d_kv_norm = _rms_bwd(du_kv, h_kv, kv_norm, dh, "rms_bwd_kv")

    d_norm_a, d_conv_w, d_conv_b, d_w_r, d_b_r, d_w_i, d_b_i, d_lam = ([None] * NA for _ in range(8))
    for l in reversed(range(NA)):
        h_in, u, proj, xc, r, i, hr, ycat = saved_a[l]
        dy = _mm_nt(dh, out_proj(f"out_a{l}"), name=f"mm_dy_a{l}")
        big_grads[f"out_a{l}"] = _mm_tn(ycat, dh, 1, name=f"mm_dw_out_a{l}").reshape(N_DEV, -1, D)
        rider, keys = scatter_rider([f"in_a{l + 1}" if l + 1 < NA else "kv", f"out_a{l}"])
        (dxg, d_conv_w[l], d_conv_b[l], d_w_r[l], d_b_r[l], d_w_i[l], d_b_i[l], d_lam[l]), landed = _rglru_bwd(
            dy, proj, xc, r, i, hr, conv_w_f[l], w_r_bf[l], w_i_bf[l], lam_f[l], f"rglru_bwd{l}", rider,
            dx_spare=DM2)
        received.update(zip(keys, landed))
        dproj, dmkv[l] = _memattn_bwd(dy, proj, mkv[l], DR, DM, f"memattn_bwd_a{l}", dproj=dxg)
        if l > 0:
            du = _mm_nt(dproj, full[f"in_a{l}"], name=f"mm_du_a{l}")
            big_grads[f"in_a{l}"] = _mm_tn(u, dproj, N_DEV, name=f"mm_dw_in_a{l}")
        else:
            dmkv_all = jnp.concatenate(dmkv, axis=-1)
            dmem_n = _mm_nt(dmkv_all, jnp.concatenate([mem_proj(m) for m in range(L)], axis=0), name="mm_dmem")
            for m in range(L):
                big_grads[f"mem_kv{m}"] = _mm_tn(mem_n, dmkv[m], 1, name=f"mm_dw_mem_kv{m}").reshape(N_DEV, -1, DM2)
            _, d_mem_norm = _rms_bwd(dmem_n, mem[0], mem_norm, zeros_mem, "rms_bwd_mem")
            rider, keys = scatter_rider([f"mem_kv{m}" for m in range(L)])
            du, landed = _mm_nt(dproj, full["in_a0"], name="mm_du_a0", rider=rider)
            received.update(zip(keys, landed))
            small_early = {
                "mem_norm": d_mem_norm.reshape(-1),
                "conv_w": jnp.stack(d_conv_w),
                "conv_b": jnp.concatenate(d_conv_b, axis=0),
                "w_rec_gate": jnp.stack(d_w_r),
                "b_rec_gate": jnp.concatenate(d_b_r, axis=0),
                "w_in_gate": jnp.stack(d_w_i),
                "b_in_gate": jnp.concatenate(d_b_i, axis=0),
                "lru_lambda": jnp.concatenate(d_lam, axis=0),
                "kv_norm": d_kv_norm.reshape(-1),
                "norm_b": jnp.concatenate(d_norm_b, axis=0),
                "final_norm": d_final_norm.reshape(-1),
            }
            rider = _Gather([_pack(list(small_early.values()), 256)])
            big_grads["in_a0"], (early_gathered,) = _mm_tn(u, dproj, N_DEV, name="mm_dw_in_a0", rider=rider)
        dh, d_norm_a[l] = _rms_bwd(du, h_in, norm_a_f[l], dh, f"rms_bwd_a{l}")
    grad_x = dh.reshape(x.shape)

    received.update(zip(["in_a0"], _all_to_all([big_grads["in_a0"]], "scatter_grads")))
    small_late = {"norm_a": jnp.concatenate(d_norm_a, axis=0)}
    (late_gathered,) = _all_gather([_pack(list(small_late.values()), 256)], "gather_small_grads")

    def update(key, w, m, v):
        shape = w.shape
        two_d = (-1, shape[-1])
        return [o.reshape(shape) for o in _adamw(received[key], w.reshape(two_d), m.reshape(two_d),
                                                 v.reshape(two_d), f"adamw_{key}")]

    def update_layers(prefix, w, m, v):
        per_layer = [update(f"{prefix}{l}", w[l], m[l], v[l]) for l in range(w.shape[0])]
        return [jnp.stack([per_layer[l][k] for l in range(w.shape[0])]) for k in range(4)]

    upd = {
        "w_mem_kv": update_layers("mem_kv", w_mem_kv, m_w_mem_kv, v_w_mem_kv),
        "w_in_a": update_layers("in_a", w_in_a, m_w_in_a, v_w_in_a),
        "w_out_a": update_layers("out_a", w_out_a, m_w_out_a, v_w_out_a),
        "w_kv": update("kv", w_kv, m_w_kv, v_w_kv),
        "w_in_b": update_layers("in_b", w_in_b, m_w_in_b, v_w_in_b),
        "w_out_b": update_layers("out_b", w_out_b, m_w_out_b, v_w_out_b),
    }

    small_grad = {}
    for part, got, name in ((small_early, early_gathered, "sum_small_early"), (small_late, late_gathered, "sum_small_late")):
        summed = _unpack(_sum_devices(got, name), [g.shape for g in part.values()])
        small_grad.update(zip(part, summed))
    small_names = list(small_grad)
    small_w = {"mem_norm": (mem_norm, m_mem_norm, v_mem_norm), "norm_a": (norm_a, m_norm_a, v_norm_a),
               "conv_w": (conv_w, m_conv_w, v_conv_w), "conv_b": (conv_b, m_conv_b, v_conv_b),
               "w_rec_gate": (w_rec_gate, m_w_rec_gate, v_w_rec_gate),
               "b_rec_gate": (b_rec_gate, m_b_rec_gate, v_b_rec_gate),
               "w_in_gate": (w_in_gate, m_w_in_gate, v_w_in_gate), "b_in_gate": (b_in_gate, m_b_in_gate, v_b_in_gate),
               "lru_lambda": (lru_lambda, m_lru_lambda, v_lru_lambda), "kv_norm": (kv_norm, m_kv_norm, v_kv_norm),
               "norm_b": (norm_b, m_norm_b, v_norm_b), "final_norm": (final_norm, m_final_norm, v_final_norm)}
    for k in small_names:
        w = small_w[k][0]
        if small_grad[k].shape != w.shape:
            n = w.shape[-1]
            small_grad[k] = lax.dynamic_slice_in_dim(small_grad[k], me * n, n, axis=-1)
    small_shapes = [small_w[k][0].shape for k in small_names]
    packed = [_pack([small_grad[k] for k in small_names], 256)[None]]
    packed += [_pack([small_w[k][t] for k in small_names], 256) for t in range(3)]
    small_out = [_unpack(o, small_shapes) for o in _adamw(*packed, "adamw_small")]
    for idx, k in enumerate(small_names):
        upd[k] = [small_out[t][idx] for t in range(4)]

    order = ["mem_norm", "w_mem_kv", "norm_a", "w_in_a", "conv_w", "conv_b", "w_rec_gate", "b_rec_gate", "w_in_gate",
             "b_in_gate", "lru_lambda", "w_out_a", "kv_norm", "w_kv", "norm_b", "w_in_b", "w_out_b", "final_norm"]
    return (loss, grad_x, *[upd[k][0] for k in order], *[upd[k][1] for k in order],
            *[upd[k][2] for k in order], *[upd[k][3] for k in order])
```

```python
import functools

import jax
import jax.numpy as jnp
from jax import lax
from jax.experimental import pallas as pl
from jax.experimental.pallas import tpu as pltpu

F32 = jnp.float32
BF16 = jnp.bfloat16

N_DEV = 8
EPS = 1e-6
LRU_C = 8.0
HEAD = 128
CONV_WIDTH = 4
LANES = 128
SUB = 8
SB_EXHAUSTED = 110.0
VMEM_LIMIT = 56 * 1024 * 1024
RESIDENT_WEIGHT_BYTES = 8 * 1024 * 1024

ADAM_LR = 0.001
ADAM_B1 = 0.9
ADAM_B2 = 0.999
ADAM_EPS = 1e-08
ADAM_WD = 0.01
ADAM_STEP = 10

HBM_SPEC = pl.BlockSpec(memory_space=pltpu.HBM)


def _params(*semantics):
    return pltpu.CompilerParams(dimension_semantics=semantics, vmem_limit_bytes=VMEM_LIMIT)


def _tile(n, cap):
    if n <= cap:
        return n
    t = cap - cap % LANES
    while n % t:
        t -= LANES
    return t


def _row_tile(n, cap):
    if n <= cap:
        return n
    t = cap - cap % 8
    while t >= 8:
        if n % t == 0:
            return t
        t -= 8
    return n


def _rows_for(n, bytes_per_row, budget=2 * 1024 * 1024):
    return _row_tile(n, max(8, budget // bytes_per_row))


def _dot(a, b):
    return lax.dot_general(a, b, (((1,), (0,)), ((), ())), preferred_element_type=F32)


def _dot_nt(a, b):
    return lax.dot_general(a, b, (((1,), (1,)), ((), ())), preferred_element_type=F32)


def _dot_tn(a, b):
    return lax.dot_general(a, b, (((0,), (0,)), ((), ())), preferred_element_type=F32)


def _split_bf16(x):
    hi = x.astype(BF16)
    lo = (x - hi.astype(F32)).astype(BF16)
    return hi, lo


def _softplus(x):
    return jnp.maximum(x, 0.0) + jnp.log1p(jnp.exp(-jnp.abs(x)))


def _gate_sigmoid(x):
    return 0.5 * jnp.tanh(0.5 * x) + 0.5


def _silu_and_grad(g):
    sg = _gate_sigmoid(g)
    return g * sg, sg * (1.0 + g * (1.0 - sg))


def _one_minus_square(a, log_a):
    x = 2.0 * log_a
    series = -x * (1.0 + x * (0.5 + x * (1.0 / 6.0)))
    return jnp.where(x > -0.03, series, 1.0 - a * a)


def _rms_fwd(x, g, name):
    T, D = x.shape
    tt = _row_tile(T, 512)

    def body(x_ref, g_ref, o_ref):
        xv = x_ref[...]
        r = lax.rsqrt(jnp.mean(xv * xv, axis=-1, keepdims=True) + EPS)
        o_ref[...] = ((xv * r) * g_ref[...]).astype(o_ref.dtype)

    return pl.pallas_call(
        body, name=name, grid=(T // tt,),
        in_specs=[pl.BlockSpec((tt, D), lambda i: (i, 0)), pl.BlockSpec((1, D), lambda i: (0, 0))],
        out_specs=pl.BlockSpec((tt, D), lambda i: (i, 0)),
        out_shape=jax.ShapeDtypeStruct((T, D), BF16),
        compiler_params=_params("parallel"),
    )(x, g.reshape(1, D))


def _rms_bwd(du, h, g, res, name):
    T, D = h.shape
    tt = _row_tile(T, 256)

    def body(du_ref, h_ref, g_ref, res_ref, dh_ref, dg_ref):
        @pl.when(pl.program_id(0) == 0)
        def _():
            dg_ref[...] = jnp.zeros_like(dg_ref)

        hv = h_ref[...]
        duv = du_ref[...]
        r = lax.rsqrt(jnp.mean(hv * hv, axis=-1, keepdims=True) + EPS)
        xhat = hv * r
        dxhat = duv * g_ref[...]
        dh_ref[...] = res_ref[...] + r * (dxhat - xhat * jnp.mean(dxhat * xhat, axis=-1, keepdims=True))
        dg_ref[...] += jnp.sum(duv * xhat, axis=0, keepdims=True)

    row = pl.BlockSpec((tt, D), lambda i: (i, 0))
    vec = pl.BlockSpec((1, D), lambda i: (0, 0))
    return pl.pallas_call(
        body, name=name, grid=(T // tt,),
        in_specs=[row, row, vec, row],
        out_specs=[row, vec],
        out_shape=[jax.ShapeDtypeStruct((T, D), F32), jax.ShapeDtypeStruct((1, D), F32)],
        compiler_params=_params("arbitrary"),
    )(du, h, g.reshape(1, D), res)


def _mm_nn(a, b3, res=None, out_dtype=F32, name=None, rider=None, norms=()):
    M, K = a.shape
    S, _, ns = b3.shape
    whole_b = S == 1 and K * ns * 2 <= RESIDENT_WEIGHT_BYTES
    tm = _row_tile(M, 512 if whole_b or a.dtype != BF16 else 1024)
    tn = ns if whole_b else _tile(ns, 512)
    nj = ns // tn

    assert not norms or (whole_b and rider is None)
    n_in = 2 + (res is not None) + len(norms)

    def body(*refs):
        a_ref, b_ref = refs[:2]
        o_ref = refs[n_in]
        acc = _dot(a_ref[...].astype(BF16), b_ref[...])
        if res is not None:
            acc = acc + refs[2][...]
        o_ref[...] = acc.astype(o_ref.dtype)
        if norms:
            r = lax.rsqrt(jnp.mean(acc * acc, axis=-1, keepdims=True) + EPS)
            for g_ref, u_ref in zip(refs[n_in - len(norms):n_in], refs[n_in + 1:]):
                u_ref[...] = ((acc * r) * g_ref[...]).astype(u_ref.dtype)

    tile = pl.BlockSpec((tm, tn), lambda i, j: (i, j))
    in_specs = [pl.BlockSpec((tm, K), lambda i, j: (i, 0)),
                pl.BlockSpec((None, K, tn), lambda i, j: (j // nj, 0, j % nj))]
    args = [a, b3]
    if res is not None:
        in_specs.append(tile)
        args.append(res)
    in_specs += [pl.BlockSpec((1, tn), lambda i, j: (0, j))] * len(norms)
    args += [g.reshape(1, -1) for g in norms]
    outs, landed = _call_with_rider(
        body, rider, *_grid_ends((M // tm, S * nj)), name=name, grid=(M // tm, S * nj),
        in_specs=in_specs,
        out_specs=[tile] * (1 + len(norms)),
        out_shape=[jax.ShapeDtypeStruct((M, S * ns), out_dtype)] + [jax.ShapeDtypeStruct((M, S * ns), BF16)] * len(norms),
        scratch_shapes=[], semantics=("parallel", "parallel"), args=args)
    if norms:
        return tuple(outs)
    return outs[0] if rider is None else (outs[0], landed)


def _grid_ends(grid):
    def first():
        return functools.reduce(jnp.logical_and, [pl.program_id(d) == 0 for d in range(len(grid))])

    def last():
        return functools.reduce(jnp.logical_and, [pl.program_id(d) == n - 1 for d, n in enumerate(grid)])

    return first, last


def _mm_nt(a, b3, out_dtype=F32, name=None, rider=None):
    M = a.shape[0]
    S, N, ns = b3.shape
    whole_b = S * N * ns * 2 <= RESIDENT_WEIGHT_BYTES
    tm = _row_tile(M, 512 if whole_b or a.dtype != BF16 else 1024)
    tn = N if whole_b else _tile(N, 512)

    def body(a_ref, b_ref, o_ref):
        acc = _dot_nt(a_ref[:, 0:ns].astype(BF16), b_ref[0])
        for s in range(1, S):
            acc = acc + _dot_nt(a_ref[:, s * ns:(s + 1) * ns].astype(BF16), b_ref[s])
        o_ref[...] = acc.astype(o_ref.dtype)

    (out,), landed = _call_with_rider(
        body, rider, *_grid_ends((M // tm, N // tn)), name=name, grid=(M // tm, N // tn),
        in_specs=[pl.BlockSpec((tm, S * ns), lambda i, j: (i, 0)),
                  pl.BlockSpec((S, tn, ns), lambda i, j: (0, j, 0))],
        out_specs=[pl.BlockSpec((tm, tn), lambda i, j: (i, j))],
        out_shape=[jax.ShapeDtypeStruct((M, N), out_dtype)],
        scratch_shapes=[], semantics=("parallel", "parallel"), args=(a, b3))
    return out if rider is None else (out, landed)


def _mm_tn(a, b, S, name=None, rider=None):
    T, K = a.shape
    ns = b.shape[1] // S
    tk = _tile(K, 1024)
    tn = _tile(ns, 512)
    nj = ns // tn
    tt = _row_tile(T, 2048)
    nt = T // tt

    def body(a_ref, b_ref, o_ref, acc_ref):
        t = pl.program_id(2)
        p = _dot_tn(a_ref[...].astype(BF16), b_ref[...].astype(BF16))

        @pl.when(t == 0)
        def _():
            acc_ref[...] = p

        @pl.when(t > 0)
        def _():
            acc_ref[...] += p

        @pl.when(t == nt - 1)
        def _():
            o_ref[...] = acc_ref[...].astype(o_ref.dtype)

    grid = (K // tk, S * nj, nt)
    (out,), landed = _call_with_rider(
        body, rider, *_grid_ends(grid), name=name, grid=grid,
        in_specs=[pl.BlockSpec((tt, tk), lambda i, j, t: (t, i)),
                  pl.BlockSpec((tt, tn), lambda i, j, t: (t, j))],
        out_specs=[pl.BlockSpec((None, tk, tn), lambda i, j, t: (j // nj, i, j % nj))],
        out_shape=[jax.ShapeDtypeStruct((S, K, ns), BF16)],
        scratch_shapes=[pltpu.VMEM((tk, tn), F32)],
        semantics=("parallel", "parallel", "arbitrary"), args=(a, b))
    return out if rider is None else (out, landed)


def _lru_coeffs(lam_row, r):
    cl = -LRU_C * _softplus(-lam_row)
    log_a = cl * r
    a = jnp.exp(log_a)
    em = _one_minus_square(a, log_a)
    return cl, a, em, jnp.sqrt(em)


def _rglru_fwd(proj, conv_w, conv_b, w_r, b_r, w_i, b_i, lam, name, rider=None, y_spare=0):
    T = proj.shape[0]
    DR = conv_w.shape[1]
    NB = DR // HEAD
    tt = _row_tile(T, 256)
    PAD = 8

    def body(x_ref, g_ref, cw_ref, cb_ref, wr_ref, br_ref, wi_ref, bi_ref, lam_ref,
             y_ref, xc_ref, r_ref, i_ref, h_ref, xbuf, hcar):
        @pl.when(pl.program_id(0) == 0)
        def _():
            xbuf[0:PAD, :] = jnp.zeros((PAD, DR), F32)
            hcar[...] = jnp.zeros_like(hcar)

        xbuf[PAD:PAD + tt, :] = x_ref[...]
        xc = cb_ref[...] + cw_ref[0:1, :] * xbuf[pl.ds(PAD - 3, tt), :]
        for k in range(1, CONV_WIDTH):
            xc = xc + cw_ref[k:k + 1, :] * xbuf[pl.ds(PAD - 3 + k, tt), :]
        xbuf[0:PAD, :] = xbuf[tt:tt + PAD, :]
        xc_ref[...] = xc
        xcb = xc.astype(BF16)
        for n in range(NB):
            sl = slice(n * HEAD, (n + 1) * HEAD)
            r_ref[:, sl] = jax.nn.sigmoid(_dot(xcb[:, sl], wr_ref[n]) + br_ref[:, sl])
            i_ref[:, sl] = _gate_sigmoid(_dot(xcb[:, sl], wi_ref[n]) + bi_ref[:, sl])
        _, a, _, mult = _lru_coeffs(lam_ref[...], r_ref[...])
        hs = mult * (i_ref[...] * xc)
        groups = tt // SUB
        a = a.reshape(groups, SUB, DR)
        hs = hs.reshape(groups, SUB, DR)
        sub = lax.broadcasted_iota(jnp.int32, (groups, SUB, DR), 1)
        d = 1
        while d < SUB:
            keep = sub >= d
            a_sh = jnp.where(keep, pltpu.roll(a, d, 1), 1.0)
            h_sh = jnp.where(keep, pltpu.roll(hs, d, 1), 0.0)
            hs = a * h_sh + hs
            a = a * a_sh
            d *= 2
        carry = hcar[...]
        for n in range(groups):
            h_ref[n * SUB:(n + 1) * SUB, :] = hs[n] + a[n] * carry
            carry = h_ref[(n + 1) * SUB - 1:(n + 1) * SUB, :]
        hcar[...] = carry
        h = h_ref[...]
        g = g_ref[...]
        y_ref[...] = (h * (g * _gate_sigmoid(g))).astype(y_ref.dtype)

    col = lambda j: pl.BlockSpec((tt, DR), lambda c: (c, j))
    vec = pl.BlockSpec((1, DR), lambda c: (0, 0))
    gate = pl.BlockSpec((NB, HEAD, HEAD), lambda c: (0, 0, 0))
    f32_out = jax.ShapeDtypeStruct((T, DR), F32)
    nc = T // tt
    return _call_with_rider(
        body, rider, lambda: pl.program_id(0) == 0, lambda: pl.program_id(0) == nc - 1,
        name=name, grid=(nc,),
        in_specs=[col(0), col(1), pl.BlockSpec((CONV_WIDTH, DR), lambda c: (0, 0)), vec, gate, vec, gate, vec, vec],
        out_specs=[col(0)] * 5,
        out_shape=[jax.ShapeDtypeStruct((T, DR + y_spare), BF16), f32_out, f32_out, f32_out, f32_out],
        scratch_shapes=[pltpu.VMEM((tt + PAD, DR), F32), pltpu.VMEM((1, DR), F32)],
        semantics=("arbitrary",),
        args=(proj, proj, conv_w, conv_b.reshape(1, DR), w_r, b_r.reshape(1, DR), w_i, b_i.reshape(1, DR),
              lam.reshape(1, DR)))


def _rglru_bwd(dy, proj, xc, r, i, h, conv_w, w_r, w_i, lam, name, rider=None, dx_spare=0):
    T = proj.shape[0]
    DR = conv_w.shape[1]
    NB = DR // HEAD
    tt = _row_tile(T, 128)
    nc = T // tt
    PAD = 8
    per = tt // PAD

    def body(dy_ref, x_ref, g_ref, xc_ref, r_ref, i_ref, h_ref, xprev_ref, hprev_ref,
             cw_ref, wr_ref, wi_ref, lam_ref,
             dxg_ref, dcw_ref, dcb_ref, dwr_ref, dbr_ref, dwi_ref, dbi_ref, dlam_ref,
             xbuf, dxcbuf, gcar, acar):
        step = pl.program_id(0)
        chunk = nc - 1 - step

        @pl.when(step == 0)
        def _():
            for ref in (dcw_ref, dcb_ref, dwr_ref, dbr_ref, dwi_ref, dbi_ref, dlam_ref, gcar, acar):
                ref[...] = jnp.zeros_like(ref)
            dxcbuf[tt:tt + PAD, :] = jnp.zeros((PAD, DR), F32)

        not_first = (chunk > 0).astype(F32)
        row = lax.broadcasted_iota(jnp.int32, (tt, DR), 0)
        silu, dsilu = _silu_and_grad(g_ref[...])
        dyv = dy_ref[...]
        hv = h_ref[...]
        dxg_ref[:, DR:2 * DR] = (dyv * hv * dsilu).astype(dxg_ref.dtype)
        dh = dyv * silu
        rv = r_ref[...]
        iv = i_ref[...]
        xcv = xc_ref[...]
        lam_row = lam_ref[...]
        cl, a, em, mult = _lru_coeffs(lam_row, rv)
        b = jnp.where(row == tt - 1, acar[...], pltpu.roll(a, tt - 1, 0))
        gs = dh
        d = 1
        while d < tt:
            keep = row < tt - d
            b_sh = jnp.where(keep, pltpu.roll(b, tt - d, 0), 1.0)
            g_sh = jnp.where(keep, pltpu.roll(gs, tt - d, 0), 0.0)
            gs = gs + b * g_sh
            b = b * b_sh
            d *= 2
        gt = gs + b * gcar[...]
        xbuf[0:tt, :] = gt
        gcar[...] = xbuf[0:1, :]
        acar[...] = _lru_coeffs(lam_row, r_ref[0:1, :])[1]
        h_before = hprev_ref[PAD - 1:PAD, :] * not_first
        hprev = jnp.where(row == 0, h_before, pltpu.roll(hv, 1, 0))
        da = gt * hprev
        dmult = gt * (iv * xcv)
        di = gt * mult * xcv
        dxc = gt * mult * iv
        dlog_a = da * a - dmult * (1.0 - em) / mult
        dr = dlog_a * cl
        dlam_ref[...] += jnp.sum(dlog_a * rv, axis=0, keepdims=True) * (LRU_C * jax.nn.sigmoid(-lam_row))
        drp = dr * rv * (1.0 - rv)
        dip = di * iv * (1.0 - iv)
        dbr_ref[...] += jnp.sum(drp, axis=0, keepdims=True)
        dbi_ref[...] += jnp.sum(dip, axis=0, keepdims=True)
        drpb = drp.astype(BF16)
        dipb = dip.astype(BF16)
        xcb = xcv.astype(BF16)
        for n in range(NB):
            sl = slice(n * HEAD, (n + 1) * HEAD)
            dxcbuf[0:tt, sl] = dxc[:, sl] + _dot_nt(drpb[:, sl], wr_ref[n]) + _dot_nt(dipb[:, sl], wi_ref[n])
            dwr_ref[n] += _dot_tn(xcb[:, sl], drpb[:, sl])
            dwi_ref[n] += _dot_tn(xcb[:, sl], dipb[:, sl])
        dxc_all = dxcbuf[0:tt, :]
        dcb_ref[...] += jnp.sum(dxc_all, axis=0, keepdims=True)
        xbuf[0:PAD, :] = xprev_ref[...] * not_first
        xbuf[PAD:PAD + tt, :] = x_ref[...]
        dx = cw_ref[0:1, :] * dxcbuf[pl.ds(3, tt), :]
        for k in range(1, CONV_WIDTH):
            dx = dx + cw_ref[k:k + 1, :] * dxcbuf[pl.ds(3 - k, tt), :]
        dxg_ref[:, 0:DR] = dx.astype(dxg_ref.dtype)
        for k in range(CONV_WIDTH):
            dcw_ref[k:k + 1, :] += jnp.sum(xbuf[pl.ds(PAD - 3 + k, tt), :] * dxc_all, axis=0, keepdims=True)
        dxcbuf[tt:tt + PAD, :] = dxcbuf[0:PAD, :]

    rev = lambda j: pl.BlockSpec((tt, DR), lambda s: (nc - 1 - s, j))
    prev = pl.BlockSpec((PAD, DR), lambda s: (jnp.maximum((nc - 1 - s) * per - 1, 0), 0))
    vec = pl.BlockSpec((1, DR), lambda s: (0, 0))
    gate = pl.BlockSpec((NB, HEAD, HEAD), lambda s: (0, 0, 0))
    taps = pl.BlockSpec((CONV_WIDTH, DR), lambda s: (0, 0))
    vec_out = jax.ShapeDtypeStruct((1, DR), F32)
    gate_out = jax.ShapeDtypeStruct((NB, HEAD, HEAD), F32)
    return _call_with_rider(
        body, rider, lambda: pl.program_id(0) == 0, lambda: pl.program_id(0) == nc - 1,
        name=name, grid=(nc,),
        in_specs=[rev(0), rev(0), rev(1), rev(0), rev(0), rev(0), rev(0), prev, prev, taps, gate, gate, vec],
        out_specs=[pl.BlockSpec((tt, 2 * DR), lambda s: (nc - 1 - s, 0)), taps, vec, gate, vec, gate, vec, vec],
        out_shape=[jax.ShapeDtypeStruct((T, 2 * DR + dx_spare), BF16), jax.ShapeDtypeStruct((CONV_WIDTH, DR), F32),
                   vec_out, gate_out, vec_out, gate_out, vec_out, vec_out],
        scratch_shapes=[pltpu.VMEM((tt + PAD, DR), F32), pltpu.VMEM((tt + PAD, DR), F32),
                        pltpu.VMEM((1, DR), F32), pltpu.VMEM((1, DR), F32)],
        semantics=("arbitrary",),
        args=(dy, proj, proj, xc, r, i, h, proj, h, conv_w, w_r, w_i, lam.reshape(1, DR)))


def _mem_probs(q, k, scale):
    s = _dot_nt(q, k) * scale
    p = jnp.exp(s - jnp.max(s, axis=-1, keepdims=True))
    return p * (1.0 / jnp.sum(p, axis=-1, keepdims=True))


def _memattn_fwd(proj, mkv, y_mix, DR, DM, name):
    T = proj.shape[0]
    M = mkv.shape[0]
    NH = DM // HEAD
    tt = _row_tile(T, 512)
    qcol = 2 * DR // DM
    scale = HEAD ** -0.5

    def body(q_ref, g_ref, k_ref, v_ref, _, y_ref):
        for n in range(NH):
            sl = slice(n * HEAD, (n + 1) * HEAD)
            p = _mem_probs(q_ref[:, sl].astype(BF16), k_ref[:, sl], scale)
            o = _dot(p.astype(BF16), v_ref[:, sl])
            g = g_ref[:, sl]
            y_ref[:, sl] = (o * (g * _gate_sigmoid(g))).astype(y_ref.dtype)

    return pl.pallas_call(
        body, name=name, grid=(T // tt,),
        in_specs=[pl.BlockSpec((tt, DM), lambda t: (t, qcol)), pl.BlockSpec((tt, DM), lambda t: (t, qcol + 1)),
                  pl.BlockSpec((M, DM), lambda t: (0, 0)), pl.BlockSpec((M, DM), lambda t: (0, 1)), HBM_SPEC],
        out_specs=pl.BlockSpec((tt, DM), lambda t: (t, DR // DM)),
        out_shape=jax.ShapeDtypeStruct(y_mix.shape, y_mix.dtype),
        input_output_aliases={4: 0},
        compiler_params=_params("parallel"),
    )(proj, proj, mkv, mkv, y_mix)


def _memattn_bwd(dy, proj, mkv, DR, DM, name, dproj=None):
    T = proj.shape[0]
    M = mkv.shape[0]
    NH = DM // HEAD
    tt = _row_tile(T, 512)
    qcol = 2 * DR // DM
    scale = HEAD ** -0.5

    def body(dy_ref, q_ref, g_ref, k_ref, v_ref, *rest):
        dqg_ref, dkv_ref = rest[-2:]

        @pl.when(pl.program_id(0) == 0)
        def _():
            dkv_ref[...] = jnp.zeros_like(dkv_ref)

        for n in range(NH):
            sl = slice(n * HEAD, (n + 1) * HEAD)
            qb = q_ref[:, sl].astype(BF16)
            kb = k_ref[:, sl]
            vb = v_ref[:, sl]
            p = _mem_probs(qb, kb, scale)
            pb = p.astype(BF16)
            o = _dot(pb, vb)
            silu, dsilu = _silu_and_grad(g_ref[:, sl])
            dyv = dy_ref[:, sl]
            dqg_ref[:, DM + n * HEAD:DM + (n + 1) * HEAD] = (dyv * o * dsilu).astype(dqg_ref.dtype)
            dob = (dyv * silu).astype(BF16)
            dp = _dot_nt(dob, vb)
            ds = (p * (dp - jnp.sum(dp * p, axis=-1, keepdims=True)) * scale).astype(BF16)
            dqg_ref[:, sl] = _dot(ds, kb).astype(dqg_ref.dtype)
            dkv_ref[:, sl] += _dot_tn(ds, qb)
            dkv_ref[:, DM + n * HEAD:DM + (n + 1) * HEAD] += _dot_tn(pb, dob)

    in_place = dproj is not None
    return pl.pallas_call(
        body, name=name, grid=(T // tt,),
        in_specs=[pl.BlockSpec((tt, DM), lambda t: (t, DR // DM)),
                  pl.BlockSpec((tt, DM), lambda t: (t, qcol)), pl.BlockSpec((tt, DM), lambda t: (t, qcol + 1)),
                  pl.BlockSpec((M, DM), lambda t: (0, 0)), pl.BlockSpec((M, DM), lambda t: (0, 1))]
                 + [HBM_SPEC] * in_place,
        out_specs=[pl.BlockSpec((tt, 2 * DM), lambda t: (t, DR // DM if in_place else 0)),
                   pl.BlockSpec((M, 2 * DM), lambda t: (0, 0))],
        out_shape=[jax.ShapeDtypeStruct(dproj.shape if in_place else (T, 2 * DM), BF16),
                   jax.ShapeDtypeStruct((M, 2 * DM), F32)],
        input_output_aliases={5: 0} if in_place else {},
        compiler_params=_params("arbitrary"),
    )(dy, proj, proj, mkv, mkv, *([dproj] if in_place else []))


def _sb_blocks(T):
    tk = _row_tile(T // 2, 256)
    tq = 2 * tk
    assert T % tq == 0
    return tq, tk


def _sb_upper(tk):
    row = lax.broadcasted_iota(jnp.int32, (tk, tk), 0)
    col = lax.broadcasted_iota(jnp.int32, (tk, tk), 1)
    return (row > col).astype(BF16)


def _sb_causal(tq, tk, d):
    row = lax.broadcasted_iota(jnp.int32, (tq, tk), 0)
    col = lax.broadcasted_iota(jnp.int32, (tq, tk), 1)
    return col + d * tk < row


def _sb_own_span(tile, carry, i, tq, tk):
    span = tq // tk
    for d in reversed(range(span)):
        lo = d * tk
        part = tile(span * i + d, tuple(c[lo:] for c in carry), _sb_causal(tq - lo, tk, 0), slice(lo, tq))
        carry = tuple(p if lo == 0 else jnp.concatenate([c[:lo], p], axis=0) for c, p in zip(carry, part))
    return carry


def _sb_earlier(tile, carry, jb_first, tq):
    half = tq // 2

    def walk(carry, jb, rows, watched):
        def more(c):
            return jnp.logical_and(c[0] >= 0, c[1] > 0)

        def step(c):
            jb, _, *rest = c
            rest = tile(jb, tuple(rest), None, rows)
            return jb - 1, _sb_alive(watched(rest[0])), *rest

        jb, _, *carry = lax.while_loop(more, step, (jb, _sb_alive(watched(carry[0])), *carry))
        return jb, tuple(carry)

    jb, carry = walk(carry, jb_first, slice(0, tq), lambda gone: gone[half:])
    _, low = walk(tuple(c[:half] for c in carry), jb, slice(0, half), lambda gone: gone)
    return tuple(jnp.concatenate([lo, c[half:]], axis=0) for lo, c in zip(low, carry))


def _later_sum(x, upper):
    n = x.shape[0]
    hi, lo = _split_bf16(x)
    both = _dot(jnp.concatenate([hi, lo], axis=0), upper)
    return both[0:n] + both[n:2 * n]


def _sb_alive(gone_c):
    return (jnp.min(gone_c) < SB_EXHAUSTED).astype(jnp.int32)


def _sb_weights(q, kb, gone_c, causal, upper, scale):
    return _sb_weights_of(_dot_nt(q, kb) * scale, gone_c, causal, upper)


def _sb_weights_of(z, gone_c, causal, upper):
    sp = jnp.where(z > 20.0, z, jnp.log(1.0 + jnp.exp(z)))
    spm = sp if causal is None else jnp.where(causal, sp, 0.0)
    gone = _later_sum(spm, upper) + gone_c
    w = jnp.exp(z - sp - gone)
    if causal is not None:
        w = jnp.where(causal, w, 0.0)
    return z, sp, spm, w


def _sb_fwd(proj, kv, DR, name, rider=None, y_spare=0):
    T = proj.shape[0]
    NH = DR // HEAD
    tq, tk = _sb_blocks(T)
    span = tq // tk
    scale = HEAD ** -0.5

    def body(q_ref, g_ref, k_ref, v_ref, y_ref, o_ref):
        i = pl.program_id(1)
        q = q_ref[...].astype(BF16)
        upper = _sb_upper(tk)

        def tile(jb, carry, causal, rows=slice(0, tq)):
            gone_c, acc = carry
            n = rows.stop - rows.start
            start = pl.multiple_of(jb * tk, tk)
            kb = k_ref[pl.ds(start, tk), :]
            vb = v_ref[pl.ds(start, tk), :]
            _, _, spm, w = _sb_weights(q[rows], kb, gone_c, causal, upper, scale)
            hi, lo = _split_bf16(w)
            pv = _dot(jnp.concatenate([hi, lo], axis=0), vb)
            return gone_c + jnp.sum(spm, axis=-1, keepdims=True), acc + pv[0:n] + pv[n:2 * n]

        carry = _sb_own_span(tile, (jnp.zeros((tq, 1), F32), jnp.zeros((tq, HEAD), F32)), i, tq, tk)

        _, acc = _sb_earlier(tile, carry, span * i - 1, tq)
        o_ref[...] = acc
        g = g_ref[...]
        y_ref[...] = (acc * (g * _gate_sigmoid(g))).astype(y_ref.dtype)

    blk = lambda off: pl.BlockSpec((tq, HEAD), lambda h, i: (i, off + h))
    whole = lambda off: pl.BlockSpec((T, HEAD), lambda h, i: (0, off + h))
    nq = T // tq
    return _call_with_rider(
        body, rider, lambda: (pl.program_id(0) == 0) & (pl.program_id(1) == 0),
        lambda: (pl.program_id(0) == NH - 1) & (pl.program_id(1) == nq - 1),
        name=name, grid=(NH, nq),
        in_specs=[blk(0), blk(NH), whole(0), whole(NH)],
        out_specs=[blk(0), blk(0)],
        out_shape=[jax.ShapeDtypeStruct((T, DR + y_spare), BF16), jax.ShapeDtypeStruct((T, DR), F32)],
        scratch_shapes=[], semantics=("parallel", "arbitrary"),
        args=(proj, proj, kv, kv))


def _sb_bwd(dy, proj, kv, o, DR, name, rider=None):
    T = proj.shape[0]
    NH = DR // HEAD
    tq, tk = _sb_blocks(T)
    span = tq // tk
    scale = HEAD ** -0.5

    def body(dy_ref, q_ref, g_ref, k_ref, v_ref, o_ref, dq_ref, dg_ref, dk_ref, dv_ref):
        i = pl.program_id(1)

        @pl.when(i == 0)
        def _():
            dk_ref[...] = jnp.zeros_like(dk_ref)
            dv_ref[...] = jnp.zeros_like(dv_ref)

        qf = q_ref[...]
        q = qf.astype(BF16)
        q_t = qf.T.astype(BF16)
        upper = _sb_upper(tk)
        silu, dsilu = _silu_and_grad(g_ref[...])
        dyv = dy_ref[...]
        ov = o_ref[...]
        dg_ref[...] = (dyv * ov * dsilu).astype(dg_ref.dtype)
        do = dyv * silu
        dob = do.astype(BF16)
        do_t = do.T.astype(BF16)
        total = jnp.sum(dob.astype(F32) * ov, axis=-1, keepdims=True)

        def tile(jb, carry, causal, rows=slice(0, tq)):
            gone_c, e_after, dq = carry
            start = pl.multiple_of(jb * tk, tk)
            kb = k_ref[pl.ds(start, tk), :]
            vb = v_ref[pl.ds(start, tk), :]
            z, sp, spm, w = _sb_weights(q[rows], kb, gone_c, causal, upper, scale)
            e = _dot_nt(dob[rows], vb) * w
            dz = e - (total[rows] - (_later_sum(e, upper) + e_after)) * jnp.exp(z - sp)
            if causal is not None:
                dz = jnp.where(causal, dz, 0.0)
            dzb = dz.astype(BF16)
            dk_ref[jb] += _dot(q_t[:, rows], dzb) * scale
            dv_ref[jb] += _dot(do_t[:, rows], w.astype(BF16))
            return (gone_c + jnp.sum(spm, axis=-1, keepdims=True), e_after + jnp.sum(e, axis=-1, keepdims=True),
                    dq + _dot(dzb, kb))

        carry = _sb_own_span(
            tile, (jnp.zeros((tq, 1), F32), jnp.zeros((tq, 1), F32), jnp.zeros((tq, HEAD), F32)), i, tq, tk)

        *_, dq = _sb_earlier(tile, carry, span * i - 1, tq)
        dq_ref[...] = (dq * scale).astype(dq_ref.dtype)

    blk = lambda off: pl.BlockSpec((tq, HEAD), lambda h, i: (i, off + h))
    whole = lambda off: pl.BlockSpec((T, HEAD), lambda h, i: (0, off + h))
    keys_t = pl.BlockSpec((None, T // tk, HEAD, tk), lambda h, i: (h, 0, 0, 0))
    keys_t_shape = jax.ShapeDtypeStruct((NH, T // tk, HEAD, tk), F32)
    nq = T // tq
    return _call_with_rider(
        body, rider, lambda: (pl.program_id(0) == 0) & (pl.program_id(1) == 0),
        lambda: (pl.program_id(0) == NH - 1) & (pl.program_id(1) == nq - 1),
        name=name, grid=(NH, nq),
        in_specs=[blk(0), blk(0), blk(NH), whole(0), whole(NH), blk(0)],
        out_specs=[blk(0), blk(0), keys_t, keys_t],
        out_shape=[jax.ShapeDtypeStruct((T, DR), BF16), jax.ShapeDtypeStruct((T, DR), BF16),
                   keys_t_shape, keys_t_shape],
        scratch_shapes=[], semantics=("parallel", "arbitrary"),
        args=(dy, proj, proj, kv, kv, o))


def _merge_dkv(parts, name):
    NH, nblk, _, tk = parts[0][0].shape
    DR = NH * HEAD
    n = len(parts)

    def body(*refs):
        o_ref = refs[-1]
        for which in range(2):
            for h in range(NH):
                acc = refs[which][h]
                for p in range(1, n):
                    acc = acc + refs[2 * p + which][h]
                col = which * DR + h * HEAD
                o_ref[:, col:col + HEAD] = acc.T.astype(o_ref.dtype)

    blk = pl.BlockSpec((NH, None, HEAD, tk), lambda t: (0, t, 0, 0))
    return pl.pallas_call(
        body, name=name, grid=(nblk,),
        in_specs=[blk] * (2 * n),
        out_specs=pl.BlockSpec((tk, 2 * DR), lambda t: (t, 0)),
        out_shape=jax.ShapeDtypeStruct((nblk * tk, 2 * DR), BF16),
        compiler_params=_params("parallel"),
    )(*[a for pair in parts for a in pair])


def _final_loss(h, g, target, name):
    T, D = h.shape
    tt = _row_tile(T, 256)

    def body(h_ref, g_ref, t_ref, dh_ref, dg_ref, sq_ref):
        @pl.when(pl.program_id(0) == 0)
        def _():
            dg_ref[...] = jnp.zeros_like(dg_ref)
            sq_ref[...] = jnp.zeros_like(sq_ref)

        hv = h_ref[...]
        gv = g_ref[...]
        r = lax.rsqrt(jnp.mean(hv * hv, axis=-1, keepdims=True) + EPS)
        xhat = hv * r
        err = xhat * gv - t_ref[...]
        sq_ref[...] += jnp.sum(err * err, axis=0, keepdims=True)
        dy = err * (1.0 / D)
        dxhat = dy * gv
        dh_ref[...] = r * (dxhat - xhat * jnp.mean(dxhat * xhat, axis=-1, keepdims=True))
        dg_ref[...] += jnp.sum(dy * xhat, axis=0, keepdims=True)

    row = pl.BlockSpec((tt, D), lambda i: (i, 0))
    vec = pl.BlockSpec((1, D), lambda i: (0, 0))
    return pl.pallas_call(
        body, name=name, grid=(T // tt,),
        in_specs=[row, vec, row],
        out_specs=[row, vec, vec],
        out_shape=[jax.ShapeDtypeStruct((T, D), F32), jax.ShapeDtypeStruct((1, D), F32),
                   jax.ShapeDtypeStruct((1, D), F32)],
        compiler_params=_params("arbitrary"),
    )(h, g.reshape(1, D), target)


def _sum_parts(parts_ref):
    g = parts_ref[0].astype(F32)
    for s in range(1, parts_ref.shape[0]):
        g = g + parts_ref[s].astype(F32)
    return g


def _adamw(parts, w, m, v, name, layer=None, into=None):
    P, R, C = parts.shape
    tr = _rows_for(R, P * C * 4)

    def body(p_ref, w_ref, m_ref, v_ref, *rest):
        g_ref, d_ref, nm_ref, nv_ref = rest[-4:]
        g = _sum_parts(p_ref)
        nm = ADAM_B1 * m_ref[...] + (1.0 - ADAM_B1) * g
        nv = ADAM_B2 * v_ref[...] + (1.0 - ADAM_B2) * jnp.square(g)
        m_hat = nm / (1.0 - ADAM_B1 ** ADAM_STEP)
        v_hat = nv / (1.0 - ADAM_B2 ** ADAM_STEP)
        g_ref[...] = g
        d_ref[...] = -ADAM_LR * (m_hat / (jnp.sqrt(v_hat) + ADAM_EPS) + ADAM_WD * w_ref[...])
        nm_ref[...] = nm
        nv_ref[...] = nv

    if layer is None:
        row = pl.BlockSpec((tr, C), lambda i: (i, 0))
        out = jax.ShapeDtypeStruct((R, C), F32)
    else:
        row = pl.BlockSpec((None, tr, C), lambda i: (layer, i, 0))
        out = jax.ShapeDtypeStruct(w.shape, F32)
    earlier = list(into) if into is not None else []
    return pl.pallas_call(
        body, name=name, grid=(R // tr,),
        in_specs=[pl.BlockSpec((P, tr, C), lambda i: (0, i, 0)), row, row, row] + [HBM_SPEC] * len(earlier),
        out_specs=[row] * 4,
        out_shape=[out] * 4,
        input_output_aliases={4 + k: k for k in range(len(earlier))},
        compiler_params=_params("parallel"),
    )(parts, w, m, v, *earlier)


def _sum_devices(parts, name):
    P, R, C = parts.shape
    tr = _rows_for(R, P * C * 4)

    def body(p_ref, o_ref):
        o_ref[...] = _sum_parts(p_ref)

    return pl.pallas_call(
        body, name=name, grid=(R // tr,),
        in_specs=[pl.BlockSpec((P, tr, C), lambda i: (0, i, 0))],
        out_specs=pl.BlockSpec((tr, C), lambda i: (i, 0)),
        out_shape=jax.ShapeDtypeStruct((R, C), F32),
        compiler_params=_params("parallel"),
    )(parts)


def _mesh_position():
    return lax.axis_index("x"), lax.axis_index("y"), lax.axis_index("c")


def _device_index(p):
    return 4 * p[0] + 2 * p[1] + p[2]


class _Gather:
    def __init__(self, arrs):
        self.arrs = list(arrs)
        self.n = len(self.arrs)

    def out_shape(self):
        return [jax.ShapeDtypeStruct((N_DEV,) + a.shape, a.dtype) for a in self.arrs]

    def scratch(self):
        return [pltpu.SemaphoreType.DMA((self.n, 7)), pltpu.SemaphoreType.DMA((self.n, 7)),
                pltpu.SemaphoreType.DMA((self.n,))]

    def _plan(self, ins, outs, sems):
        send_sems, recv_sems, local_sems = sems
        x, y, c = _mesh_position()
        me, sibling = (x, y, c), (x, y, 1 - c)
        chips = [(1 - x, y), (x, 1 - y), (1 - x, 1 - y)]

        def slot(a, p):
            return outs[a].at[_device_index(p)]

        def copy(a, k, block, to, src=None):
            return pltpu.make_async_remote_copy(
                src_ref=slot(a, block) if src is None else src, dst_ref=slot(a, block),
                send_sem=send_sems.at[a, k], recv_sem=recv_sems.at[a, k],
                device_id=to, device_id_type=pl.DeviceIdType.MESH)

        mine = [pltpu.make_async_copy(ins[a], slot(a, me), local_sems.at[a]) for a in range(self.n)]
        first = []
        for a in range(self.n):
            first.append(copy(a, 0, me, sibling, src=ins[a]))
            first += [copy(a, 1 + j, me, (*chip, c), src=ins[a]) for j, chip in enumerate(chips)]
        return me, sibling, c, chips, copy, mine, first

    def start(self, ins, outs, sems):
        *_, mine, first = self._plan(ins, outs, sems)
        for cp in mine + first:
            cp.start()

    def wait(self, ins, outs, sems):
        me, sibling, c, chips, copy, mine, first = self._plan(ins, outs, sems)
        passed = []
        for a in range(self.n):
            for j, chip in enumerate(chips):
                copy(a, 1 + j, (*chip, c), me).wait_recv()
                fwd = copy(a, 4 + j, (*chip, c), sibling)
                fwd.start()
                passed.append(fwd)
        for a in range(self.n):
            copy(a, 0, sibling, me).wait_recv()
            for j, chip in enumerate(chips):
                copy(a, 4 + j, (*chip, 1 - c), me).wait_recv()
        for cp in first + passed:
            cp.wait_send()
        for cp in mine:
            cp.wait()


def _exchange_now(ex, name):
    n = ex.n

    def body(*refs):
        ins, outs, sems = refs[:n], refs[n:2 * n], refs[2 * n:]
        ex.start(ins, outs, sems)
        ex.wait(ins, outs, sems)

    return pl.pallas_call(
        body, name=name,
        in_specs=[HBM_SPEC] * n, out_specs=[HBM_SPEC] * n,
        out_shape=ex.out_shape(), scratch_shapes=ex.scratch(),
    )(*ex.arrs)


def _all_gather(arrs, name):
    return _exchange_now(_Gather(arrs), name)


class _Scatter:
    def __init__(self, arrs):
        self.arrs = list(arrs)
        self.n = len(self.arrs)

    def out_shape(self):
        return [jax.ShapeDtypeStruct(a.shape, a.dtype) for a in self.arrs]

    def scratch(self):
        return [pltpu.SemaphoreType.DMA((self.n, 7)), pltpu.SemaphoreType.DMA((self.n, 7)),
                pltpu.SemaphoreType.DMA((self.n,))]

    def _copies(self, ins, outs, sems, arrivals):
        send_sems, recv_sems, local_sems = sems
        x, y, c = _mesh_position()
        me = _device_index((x, y, c))
        peers = [(1 - x if k & 4 else x, 1 - y if k & 2 else y, 1 - c if k & 1 else c) for k in range(1, N_DEV)]
        local, sends, recvs = [], [], []
        for a in range(self.n):
            local.append(pltpu.make_async_copy(ins[a].at[me], outs[a].at[me], local_sems.at[a]))
            for k, peer in enumerate(peers):
                there = _device_index(peer)
                src = ins[a].at[there]
                sem = dict(send_sem=send_sems.at[a, k], recv_sem=recv_sems.at[a, k],
                           device_id=peer, device_id_type=pl.DeviceIdType.MESH)
                sends.append(pltpu.make_async_remote_copy(src_ref=src, dst_ref=outs[a].at[me], **sem))
                if arrivals:
                    recvs.append(pltpu.make_async_remote_copy(src_ref=src, dst_ref=outs[a].at[there], **sem))
        return local, sends, recvs

    def start(self, ins, outs, sems):
        local, sends, _ = self._copies(ins, outs, sems, arrivals=False)
        for cp in local + sends:
            cp.start()

    def wait(self, ins, outs, sems):
        local, sends, recvs = self._copies(ins, outs, sems, arrivals=True)
        for cp in recvs:
            cp.wait_recv()
        for cp in sends:
            cp.wait_send()
        for cp in local:
            cp.wait()


def _all_to_all(arrs, name):
    return _exchange_now(_Scatter(arrs), name)


def _call_with_rider(body, rider, first, last, *, name, grid, in_specs, out_specs, out_shape, scratch_shapes,
                     semantics, args):
    if rider is None:
        out = pl.pallas_call(body, name=name, grid=grid, in_specs=in_specs, out_specs=out_specs,
                             out_shape=out_shape, scratch_shapes=scratch_shapes,
                             compiler_params=_params(*semantics))(*args)
        return out, None
    n, n_in, n_out = rider.n, len(in_specs), len(out_specs)

    def riding(*refs):
        ins, r_in = refs[:n_in], refs[n_in:n_in + n]
        outs, r_out = refs[n_in + n:n_in + n + n_out], refs[n_in + n + n_out:n_in + 2 * n + n_out]
        scratch, sems = refs[n_in + 2 * n + n_out:-3], refs[-3:]

        @pl.when(first())
        def _():
            rider.start(r_in, r_out, sems)

        body(*ins, *outs, *scratch)

        @pl.when(last())
        def _():
            rider.wait(r_in, r_out, sems)

    out = pl.pallas_call(
        riding, name=name, grid=grid,
        in_specs=list(in_specs) + [HBM_SPEC] * n, out_specs=list(out_specs) + [HBM_SPEC] * n,
        out_shape=list(out_shape) + rider.out_shape(),
        scratch_shapes=list(scratch_shapes) + rider.scratch(),
        compiler_params=_params(*["arbitrary"] * len(grid)),
    )(*args, *rider.arrs)
    return out[:n_out], out[n_out:]


def _pack(arrs, row_multiple):
    parts = []
    rows = 0
    for a in arrs:
        flat = a.reshape(-1).astype(F32)
        r = -(-flat.shape[0] // (8 * LANES)) * 8
        parts.append(jnp.pad(flat, (0, r * LANES - flat.shape[0])).reshape(r, LANES))
        rows += r
    pad = -rows % row_multiple
    if pad:
        parts.append(jnp.zeros((pad, LANES), F32))
    return jnp.concatenate(parts, axis=0)


def _unpack(buf, shapes, lead=()):
    out = []
    r0 = 0
    for shape in shapes:
        size = 1
        for s in shape:
            size *= s
        r = -(-size // (8 * LANES)) * 8
        part = buf[..., r0:r0 + r, :].reshape(lead + (r * LANES,))[..., :size]
        out.append(part.reshape(lead + tuple(shape)))
        r0 += r
    return out


def _gathered_cols(g):
    g = jnp.moveaxis(g, 0, -2)
    return g.reshape(g.shape[:-2] + (g.shape[-2] * g.shape[-1],))


def kernel(x, mem, mem_norm, w_mem_kv, norm_a, w_in_a, conv_w, conv_b, w_rec_gate, b_rec_gate, w_in_gate, b_in_gate, lru_lambda, w_out_a, kv_norm, w_kv, norm_b, w_in_b, w_out_b, final_norm, loss_target, m_mem_norm, m_w_mem_kv, m_norm_a, m_w_in_a, m_conv_w, m_conv_b, m_w_rec_gate, m_b_rec_gate, m_w_in_gate, m_b_in_gate, m_lru_lambda, m_w_out_a, m_kv_norm, m_w_kv, m_norm_b, m_w_in_b, m_w_out_b, m_final_norm, v_mem_norm, v_w_mem_kv, v_norm_a, v_w_in_a, v_conv_w, v_conv_b, v_w_rec_gate, v_b_rec_gate, v_w_in_gate, v_b_in_gate, v_lru_lambda, v_w_out_a, v_kv_norm, v_w_kv, v_norm_b, v_w_in_b, v_w_out_b, v_final_norm):
    xs = x[0]
    T, D = xs.shape
    L = w_mem_kv.shape[0]
    NA = w_in_a.shape[0]
    NB = w_in_b.shape[0]
    DM2 = w_mem_kv.shape[2]
    DM = DM2 // 2
    DR = w_rec_gate.shape[1] * w_rec_gate.shape[2]
    me = _device_index(_mesh_position())

    small_sharded = [norm_a, conv_w, conv_b, b_rec_gate, b_in_gate, lru_lambda]
    shard = {f"mem_kv{l}": w_mem_kv[l] for l in range(L)}
    shard.update({f"in_a{l}": w_in_a[l] for l in range(NA)}, **{f"out_a{l}": w_out_a[l] for l in range(NA)})
    shard.update({f"in_b{j}": w_in_b[j] for j in range(NB)}, **{f"out_b{j}": w_out_b[j] for j in range(NB)})
    shard["kv"] = w_kv
    shard = {k: w.astype(BF16) for k, w in shard.items()}
    now = ["in_a0", "mem_kv0"]
    gathered = _all_gather([shard[k] for k in now] + [_pack(small_sharded, 8)], "gather_params")
    full = dict(zip(now, gathered[:-1]))

    def gather_rider(keys):
        return _Gather([shard[k] for k in keys]), keys

    def out_proj(key):
        return full[key].reshape(1, -1, D)

    def mem_proj(l):
        return full[f"mem_kv{l}"].reshape(1, D, DM2)

    norm_a_f, conv_w_f, conv_b_f, b_r_f, b_i_f, lam_f = [
        _gathered_cols(s) for s in _unpack(gathered[-1], [s.shape for s in small_sharded], lead=(N_DEV,))]
    w_r_bf = w_rec_gate.astype(BF16)
    w_i_bf = w_in_gate.astype(BF16)

    zeros_mem = jnp.zeros_like(mem[0])
    mem_n = _rms_fwd(mem[0], mem_norm, "rms_mem")
    mkv = [_mm_nn(mem_n, mem_proj(0), out_dtype=BF16, name="mm_mem_kv0")]

    h = xs
    u = _rms_fwd(h, norm_a_f[0], "rms_a0")
    saved_a = []
    for l in range(NA):
        rider, keys = gather_rider([f"out_a{l}"])
        proj, landed = _mm_nn(u, full[f"in_a{l}"], name=f"mm_in_a{l}", rider=rider)
        full.update(zip(keys, landed))
        later_mem = [f"mem_kv{m}" for m in range(1, L)] if l == 0 else []
        rider, keys = gather_rider([f"in_a{l + 1}" if l + 1 < NA else "kv"] + later_mem)
        (y_rnn, xc, r, i, hr), landed = _rglru_fwd(proj, conv_w_f[l], conv_b_f[l], w_r_bf[l], b_r_f[l], w_i_bf[l],
                                                   b_i_f[l], lam_f[l], f"rglru_fwd{l}", rider, y_spare=DM)
        full.update(zip(keys, landed))
        if l == 0:
            mkv += [_mm_nn(mem_n, mem_proj(m), out_dtype=BF16, name=f"mm_mem_kv{m}") for m in range(1, L)]
        ycat = _memattn_fwd(proj, mkv[l], y_rnn, DR, DM, f"memattn_fwd_a{l}")
        gains = [norm_a_f[l + 1]] if l + 1 < NA else [kv_norm, norm_b[0]]
        h_next, *normed = _mm_nn(ycat, out_proj(f"out_a{l}"), res=h, name=f"mm_out_a{l}", norms=gains)
        saved_a.append((h, u, proj, xc, r, i, hr, ycat))
        h, u = h_next, normed[-1]
    h_kv, u_kv = h, normed[0]
    rider, keys = gather_rider(["in_b0"])
    kv, landed = _mm_nn(u_kv, full["kv"], out_dtype=BF16, name="mm_kv", rider=rider)
    full.update(zip(keys, landed))
    saved_b = []
    for j in range(NB):
        proj = _mm_nn(u, full[f"in_b{j}"], name=f"mm_in_b{j}")
        rider, keys = gather_rider([f"out_b{j}"] + ([f"in_b{j + 1}"] if j + 1 < NB else []))
        (y_sb, o_sb), landed = _sb_fwd(proj, kv, DR, f"sb_fwd{j}", rider, y_spare=DM)
        full.update(zip(keys, landed))
        ycat = _memattn_fwd(proj, mkv[NA + j], y_sb, DR, DM, f"memattn_fwd_b{j}")
        if j + 1 < NB:
            h_next, u_next = _mm_nn(ycat, out_proj(f"out_b{j}"), res=h, name=f"mm_out_b{j}", norms=[norm_b[j + 1]])
        else:
            h_next, u_next = _mm_nn(ycat, out_proj(f"out_b{j}"), res=h, name=f"mm_out_b{j}"), None
        saved_b.append((h, u, proj, o_sb, ycat))
        h, u = h_next, u_next

    dh, d_final_norm, sq = _final_loss(h, final_norm, loss_target[0], "final_loss")
    loss = lax.psum(0.5 * jnp.sum(sq) / D, ("x", "y", "c"))

    big_grads = {}
    received = {}

    def scatter_rider(keys):
        return _Scatter([big_grads[k] for k in keys]), keys

    dmkv = [None] * L
    d_norm_b = [None] * NB
    dkv_parts = []
    for j in reversed(range(NB)):
        h_in, u, proj, o_sb, ycat = saved_b[j]
        dy = _mm_nt(dh, out_proj(f"out_b{j}"), name=f"mm_dy_b{j}")
        big_grads[f"out_b{j}"] = _mm_tn(ycat, dh, 1, name=f"mm_dw_out_b{j}").reshape(N_DEV, -1, D)
        rider, keys = scatter_rider(([f"in_b{j + 1}"] if j + 1 < NB else []) + [f"out_b{j}"])
        (dq, dg, dk, dv), landed = _sb_bwd(dy, proj, kv, o_sb, DR, f"sb_bwd{j}", rider)
        received.update(zip(keys, landed))
        dkv_parts.append((dk, dv))
        dqg_mem, dmkv[NA + j] = _memattn_bwd(dy, proj, mkv[NA + j], DR, DM, f"memattn_bwd_b{j}")
        dproj = jnp.concatenate([dq, dg, dqg_mem], axis=-1)
        du = _mm_nt(dproj, full[f"in_b{j}"], name=f"mm_du_b{j}")
        big_grads[f"in_b{j}"] = _mm_tn(u, dproj, N_DEV, name=f"mm_dw_in_b{j}")
        dh, d_norm_b[j] = _rms_bwd(du, h_in, norm_b[j], dh, f"rms_bwd_b{j}")

    dkv = _merge_dkv(dkv_parts, "merge_dkv")
    rider, keys = scatter_rider(["in_b0"])
    du_kv, landed = _mm_nt(dkv, full["kv"], name="mm_du_kv", rider=rider)
    received.update(zip(keys, landed))
    big_grads["kv"] = _mm_tn(u_kv, dkv, N_DEV, name="mm_dw_kv")
    dh, d_kv_norm = _rms_bwd(du_kv, h_kv, kv_norm, dh, "rms_bwd_kv")

    d_norm_a, d_conv_w, d_conv_b, d_w_r, d_b_r, d_w_i, d_b_i, d_lam = ([None] * NA for _ in range(8))
    for l in reversed(range(NA)):
        h_in, u, proj, xc, r, i, hr, ycat = saved_a[l]
        dy = _mm_nt(dh, out_proj(f"out_a{l}"), name=f"mm_dy_a{l}")
        big_grads[f"out_a{l}"] = _mm_tn(ycat, dh, 1, name=f"mm_dw_out_a{l}").reshape(N_DEV, -1, D)
        rider, keys = scatter_rider([f"in_a{l + 1}" if l + 1 < NA else "kv", f"out_a{l}"])
        (dxg, d_conv_w[l], d_conv_b[l], d_w_r[l], d_b_r[l], d_w_i[l], d_b_i[l], d_lam[l]), landed = _rglru_bwd(
            dy, proj, xc, r, i, hr, conv_w_f[l], w_r_bf[l], w_i_bf[l], lam_f[l], f"rglru_bwd{l}", rider,
            dx_spare=DM2)
        received.update(zip(keys, landed))
        dproj, dmkv[l] = _memattn_bwd(dy, proj, mkv[l], DR, DM, f"memattn_bwd_a{l}", dproj=dxg)
        if l > 0:
            du = _mm_nt(dproj, full[f"in_a{l}"], name=f"mm_du_a{l}")
            big_grads[f"in_a{l}"] = _mm_tn(u, dproj, N_DEV, name=f"mm_dw_in_a{l}")
        else:
            dmkv_all = jnp.concatenate(dmkv, axis=-1)
            dmem_n = _mm_nt(dmkv_all, jnp.concatenate([mem_proj(m) for m in range(L)], axis=0), name="mm_dmem")
            for m in range(L):
                big_grads[f"mem_kv{m}"] = _mm_tn(mem_n, dmkv[m], 1, name=f"mm_dw_mem_kv{m}").reshape(N_DEV, -1, DM2)
            _, d_mem_norm = _rms_bwd(dmem_n, mem[0], mem_norm, zeros_mem, "rms_bwd_mem")
            rider, keys = scatter_rider([f"mem_kv{m}" for m in range(L)])
            du, landed = _mm_nt(dproj, full["in_a0"], name="mm_du_a0", rider=rider)
            received.update(zip(keys, landed))
            small_early = {
                "mem_norm": d_mem_norm.reshape(-1),
                "conv_w": jnp.stack(d_conv_w),
                "conv_b": jnp.concatenate(d_conv_b, axis=0),
                "w_rec_gate": jnp.stack(d_w_r),
                "b_rec_gate": jnp.concatenate(d_b_r, axis=0),
                "w_in_gate": jnp.stack(d_w_i),
                "b_in_gate": jnp.concatenate(d_b_i, axis=0),
                "lru_lambda": jnp.concatenate(d_lam, axis=0),
                "kv_norm": d_kv_norm.reshape(-1),
                "norm_b": jnp.concatenate(d_norm_b, axis=0),
                "final_norm": d_final_norm.reshape(-1),
            }
            rider = _Gather([_pack(list(small_early.values()), 256)])
            big_grads["in_a0"], (early_gathered,) = _mm_tn(u, dproj, N_DEV, name="mm_dw_in_a0", rider=rider)
        dh, d_norm_a[l] = _rms_bwd(du, h_in, norm_a_f[l], dh, f"rms_bwd_a{l}")
    grad_x = dh.reshape(x.shape)

    received.update(zip(["in_a0"], _all_to_all([big_grads["in_a0"]], "scatter_grads")))
    small_late = {"norm_a": jnp.concatenate(d_norm_a, axis=0)}
    (late_gathered,) = _all_gather([_pack(list(small_late.values()), 256)], "gather_small_grads")

    def update(key, w, m, v):
        shape = w.shape
        two_d = (-1, shape[-1])
        return [o.reshape(shape) for o in _adamw(received[key], w.reshape(two_d), m.reshape(two_d),
                                                 v.reshape(two_d), f"adamw_{key}")]

    def update_layers(prefix, w, m, v):
        outs = None
        for l in range(w.shape[0]):
            outs = _adamw(received[f"{prefix}{l}"], w, m, v, f"adamw_{prefix}{l}", layer=l, into=outs)
        return outs

    upd = {
        "w_mem_kv": update_layers("mem_kv", w_mem_kv, m_w_mem_kv, v_w_mem_kv),
        "w_in_a": update_layers("in_a", w_in_a, m_w_in_a, v_w_in_a),
        "w_out_a": update_layers("out_a", w_out_a, m_w_out_a, v_w_out_a),
        "w_kv": update("kv", w_kv, m_w_kv, v_w_kv),
        "w_in_b": update_layers("in_b", w_in_b, m_w_in_b, v_w_in_b),
        "w_out_b": update_layers("out_b", w_out_b, m_w_out_b, v_w_out_b),
    }

    small_grad = {}
    for part, got, name in ((small_early, early_gathered, "sum_small_early"), (small_late, late_gathered, "sum_small_late")):
        summed = _unpack(_sum_devices(got, name), [g.shape for g in part.values()])
        small_grad.update(zip(part, summed))
    small_names = list(small_grad)
    small_w = {"mem_norm": (mem_norm, m_mem_norm, v_mem_norm), "norm_a": (norm_a, m_norm_a, v_norm_a),
               "conv_w": (conv_w, m_conv_w, v_conv_w), "conv_b": (conv_b, m_conv_b, v_conv_b),
               "w_rec_gate": (w_rec_gate, m_w_rec_gate, v_w_rec_gate),
               "b_rec_gate": (b_rec_gate, m_b_rec_gate, v_b_rec_gate),
               "w_in_gate": (w_in_gate, m_w_in_gate, v_w_in_gate), "b_in_gate": (b_in_gate, m_b_in_gate, v_b_in_gate),
               "lru_lambda": (lru_lambda, m_lru_lambda, v_lru_lambda), "kv_norm": (kv_norm, m_kv_norm, v_kv_norm),
               "norm_b": (norm_b, m_norm_b, v_norm_b), "final_norm": (final_norm, m_final_norm, v_final_norm)}
    for k in small_names:
        w = small_w[k][0]
        if small_grad[k].shape != w.shape:
            n = w.shape[-1]
            small_grad[k] = lax.dynamic_slice_in_dim(small_grad[k], me * n, n, axis=-1)
    small_shapes = [small_w[k][0].shape for k in small_names]
    packed = [_pack([small_grad[k] for k in small_names], 256)[None]]
    packed += [_pack([small_w[k][t] for k in small_names], 256) for t in range(3)]
    small_out = [_unpack(o, small_shapes) for o in _adamw(*packed, "adamw_small")]
    for idx, k in enumerate(small_names):
        upd[k] = [small_out[t][idx] for t in range(4)]

    order = ["mem_norm", "w_mem_kv", "norm_a", "w_in_a", "conv_w", "conv_b", "w_rec_gate", "b_rec_gate", "w_in_gate",
             "b_in_gate", "lru_lambda", "w_out_a", "kv_norm", "w_kv", "norm_b", "w_in_b", "w_out_b", "final_norm"]
    return (loss, grad_x, *[upd[k][0] for k in order], *[upd[k][1] for k in order],
            *[upd[k][2] for k in order], *[upd[k][3] for k in order])
```

```python
import functools

import jax
import jax.numpy as jnp
from jax import lax
from jax.experimental import pallas as pl
from jax.experimental.pallas import tpu as pltpu

F32 = jnp.float32
BF16 = jnp.bfloat16

N_DEV = 8
EPS = 1e-6
LRU_C = 8.0
HEAD = 128
CONV_WIDTH = 4
LANES = 128
SUB = 8
SB_EXHAUSTED = 110.0
VMEM_LIMIT = 56 * 1024 * 1024
RESIDENT_WEIGHT_BYTES = 8 * 1024 * 1024

ADAM_LR = 0.001
ADAM_B1 = 0.9
ADAM_B2 = 0.999
ADAM_EPS = 1e-08
ADAM_WD = 0.01
ADAM_STEP = 10

HBM_SPEC = pl.BlockSpec(memory_space=pltpu.HBM)


def _params(*semantics):
    return pltpu.CompilerParams(dimension_semantics=semantics, vmem_limit_bytes=VMEM_LIMIT)


def _tile(n, cap):
    if n <= cap:
        return n
    t = cap - cap % LANES
    while n % t:
        t -= LANES
    return t


def _row_tile(n, cap):
    if n <= cap:
        return n
    t = cap - cap % 8
    while t >= 8:
        if n % t == 0:
            return t
        t -= 8
    return n


def _rows_for(n, bytes_per_row, budget=2 * 1024 * 1024):
    return _row_tile(n, max(8, budget // bytes_per_row))


def _dot(a, b):
    return lax.dot_general(a, b, (((1,), (0,)), ((), ())), preferred_element_type=F32)


def _dot_nt(a, b):
    return lax.dot_general(a, b, (((1,), (1,)), ((), ())), preferred_element_type=F32)


def _dot_tn(a, b):
    return lax.dot_general(a, b, (((0,), (0,)), ((), ())), preferred_element_type=F32)


def _split_bf16(x):
    hi = x.astype(BF16)
    lo = (x - hi.astype(F32)).astype(BF16)
    return hi, lo


def _softplus(x):
    return jnp.maximum(x, 0.0) + jnp.log1p(jnp.exp(-jnp.abs(x)))


def _gate_sigmoid(x):
    return 0.5 * jnp.tanh(0.5 * x) + 0.5


def _silu_and_grad(g):
    sg = _gate_sigmoid(g)
    return g * sg, sg * (1.0 + g * (1.0 - sg))


def _one_minus_square(a, log_a):
    x = 2.0 * log_a
    series = -x * (1.0 + x * (0.5 + x * (1.0 / 6.0)))
    return jnp.where(x > -0.03, series, 1.0 - a * a)


def _rms_fwd(x, g, name, rider=None):
    T, D = x.shape
    tt = _row_tile(T, 512)

    def body(x_ref, g_ref, o_ref):
        xv = x_ref[...]
        r = lax.rsqrt(jnp.mean(xv * xv, axis=-1, keepdims=True) + EPS)
        o_ref[...] = ((xv * r) * g_ref[...]).astype(o_ref.dtype)

    (out,), landed = _call_with_rider(
        body, rider, *_grid_ends((T // tt,)), name=name, grid=(T // tt,),
        in_specs=[pl.BlockSpec((tt, D), lambda i: (i, 0)), pl.BlockSpec((1, D), lambda i: (0, 0))],
        out_specs=[pl.BlockSpec((tt, D), lambda i: (i, 0))],
        out_shape=[jax.ShapeDtypeStruct((T, D), BF16)],
        scratch_shapes=[], semantics=("parallel",), args=(x, g.reshape(1, D)))
    return out if rider is None else (out, landed)


def _rms_bwd(du, h, g, res, name):
    T, D = h.shape
    tt = _row_tile(T, 256)

    def body(du_ref, h_ref, g_ref, res_ref, dh_ref, dg_ref):
        @pl.when(pl.program_id(0) == 0)
        def _():
            dg_ref[...] = jnp.zeros_like(dg_ref)

        hv = h_ref[...]
        duv = du_ref[...]
        r = lax.rsqrt(jnp.mean(hv * hv, axis=-1, keepdims=True) + EPS)
        xhat = hv * r
        dxhat = duv * g_ref[...]
        dh_ref[...] = res_ref[...] + r * (dxhat - xhat * jnp.mean(dxhat * xhat, axis=-1, keepdims=True))
        dg_ref[...] += jnp.sum(duv * xhat, axis=0, keepdims=True)

    row = pl.BlockSpec((tt, D), lambda i: (i, 0))
    vec = pl.BlockSpec((1, D), lambda i: (0, 0))
    return pl.pallas_call(
        body, name=name, grid=(T // tt,),
        in_specs=[row, row, vec, row],
        out_specs=[row, vec],
        out_shape=[jax.ShapeDtypeStruct((T, D), F32), jax.ShapeDtypeStruct((1, D), F32)],
        compiler_params=_params("arbitrary"),
    )(du, h, g.reshape(1, D), res)


def _mm_nn(a, b3, res=None, out_dtype=F32, name=None, rider=None, norms=()):
    M, K = a.shape
    S, _, ns = b3.shape
    whole_b = S == 1 and K * ns * 2 <= RESIDENT_WEIGHT_BYTES
    tm = _row_tile(M, 512 if whole_b or a.dtype != BF16 else 1024)
    tn = ns if whole_b else _tile(ns, 512)
    nj = ns // tn

    assert not norms or (whole_b and rider is None)
    n_in = 2 + (res is not None) + len(norms)

    def body(*refs):
        a_ref, b_ref = refs[:2]
        o_ref = refs[n_in]
        acc = _dot(a_ref[...].astype(BF16), b_ref[...])
        if res is not None:
            acc = acc + refs[2][...]
        o_ref[...] = acc.astype(o_ref.dtype)
        if norms:
            r = lax.rsqrt(jnp.mean(acc * acc, axis=-1, keepdims=True) + EPS)
            for g_ref, u_ref in zip(refs[n_in - len(norms):n_in], refs[n_in + 1:]):
                u_ref[...] = ((acc * r) * g_ref[...]).astype(u_ref.dtype)

    tile = pl.BlockSpec((tm, tn), lambda i, j: (i, j))
    in_specs = [pl.BlockSpec((tm, K), lambda i, j: (i, 0)),
                pl.BlockSpec((None, K, tn), lambda i, j: (j // nj, 0, j % nj))]
    args = [a, b3]
    if res is not None:
        in_specs.append(tile)
        args.append(res)
    in_specs += [pl.BlockSpec((1, tn), lambda i, j: (0, j))] * len(norms)
    args += [g.reshape(1, -1) for g in norms]
    outs, landed = _call_with_rider(
        body, rider, *_grid_ends((M // tm, S * nj)), name=name, grid=(M // tm, S * nj),
        in_specs=in_specs,
        out_specs=[tile] * (1 + len(norms)),
        out_shape=[jax.ShapeDtypeStruct((M, S * ns), out_dtype)] + [jax.ShapeDtypeStruct((M, S * ns), BF16)] * len(norms),
        scratch_shapes=[], semantics=("parallel", "parallel"), args=args)
    if norms:
        return tuple(outs)
    return outs[0] if rider is None else (outs[0], landed)


def _grid_ends(grid):
    def first():
        return functools.reduce(jnp.logical_and, [pl.program_id(d) == 0 for d in range(len(grid))])

    def last():
        return functools.reduce(jnp.logical_and, [pl.program_id(d) == n - 1 for d, n in enumerate(grid)])

    return first, last


def _mm_nt(a, b3, out_dtype=F32, name=None, rider=None):
    M = a.shape[0]
    S, N, ns = b3.shape
    whole_b = S * N * ns * 2 <= RESIDENT_WEIGHT_BYTES
    tm = _row_tile(M, 512 if whole_b or a.dtype != BF16 else 1024)
    tn = N if whole_b else _tile(N, 512)

    def body(a_ref, b_ref, o_ref):
        acc = _dot_nt(a_ref[:, 0:ns].astype(BF16), b_ref[0])
        for s in range(1, S):
            acc = acc + _dot_nt(a_ref[:, s * ns:(s + 1) * ns].astype(BF16), b_ref[s])
        o_ref[...] = acc.astype(o_ref.dtype)

    (out,), landed = _call_with_rider(
        body, rider, *_grid_ends((M // tm, N // tn)), name=name, grid=(M // tm, N // tn),
        in_specs=[pl.BlockSpec((tm, S * ns), lambda i, j: (i, 0)),
                  pl.BlockSpec((S, tn, ns), lambda i, j: (0, j, 0))],
        out_specs=[pl.BlockSpec((tm, tn), lambda i, j: (i, j))],
        out_shape=[jax.ShapeDtypeStruct((M, N), out_dtype)],
        scratch_shapes=[], semantics=("parallel", "parallel"), args=(a, b3))
    return out if rider is None else (out, landed)


def _mm_tn(a, b, S, name=None, rider=None):
    T, K = a.shape
    ns = b.shape[1] // S
    tk = _tile(K, 1024)
    tn = _tile(ns, 512)
    nj = ns // tn
    tt = _row_tile(T, 2048)
    nt = T // tt

    def body(a_ref, b_ref, o_ref, acc_ref):
        t = pl.program_id(2)
        p = _dot_tn(a_ref[...].astype(BF16), b_ref[...].astype(BF16))

        @pl.when(t == 0)
        def _():
            acc_ref[...] = p

        @pl.when(t > 0)
        def _():
            acc_ref[...] += p

        @pl.when(t == nt - 1)
        def _():
            o_ref[...] = acc_ref[...].astype(o_ref.dtype)

    grid = (K // tk, S * nj, nt)
    (out,), landed = _call_with_rider(
        body, rider, *_grid_ends(grid), name=name, grid=grid,
        in_specs=[pl.BlockSpec((tt, tk), lambda i, j, t: (t, i)),
                  pl.BlockSpec((tt, tn), lambda i, j, t: (t, j))],
        out_specs=[pl.BlockSpec((None, tk, tn), lambda i, j, t: (j // nj, i, j % nj))],
        out_shape=[jax.ShapeDtypeStruct((S, K, ns), BF16)],
        scratch_shapes=[pltpu.VMEM((tk, tn), F32)],
        semantics=("parallel", "parallel", "arbitrary"), args=(a, b))
    return out if rider is None else (out, landed)


def _lru_coeffs(lam_row, r):
    cl = -LRU_C * _softplus(-lam_row)
    log_a = cl * r
    a = jnp.exp(log_a)
    em = _one_minus_square(a, log_a)
    return cl, a, em, jnp.sqrt(em)


def _rglru_fwd(proj, conv_w, conv_b, w_r, b_r, w_i, b_i, lam, name, rider=None, y_spare=0):
    T = proj.shape[0]
    DR = conv_w.shape[1]
    NB = DR // HEAD
    tt = _row_tile(T, 256)
    PAD = 8

    def body(x_ref, g_ref, cw_ref, cb_ref, wr_ref, br_ref, wi_ref, bi_ref, lam_ref,
             y_ref, xc_ref, r_ref, i_ref, h_ref, xbuf, hcar):
        @pl.when(pl.program_id(0) == 0)
        def _():
            xbuf[0:PAD, :] = jnp.zeros((PAD, DR), F32)
            hcar[...] = jnp.zeros_like(hcar)

        xbuf[PAD:PAD + tt, :] = x_ref[...]
        xc = cb_ref[...] + cw_ref[0:1, :] * xbuf[pl.ds(PAD - 3, tt), :]
        for k in range(1, CONV_WIDTH):
            xc = xc + cw_ref[k:k + 1, :] * xbuf[pl.ds(PAD - 3 + k, tt), :]
        xbuf[0:PAD, :] = xbuf[tt:tt + PAD, :]
        xc_ref[...] = xc
        xcb = xc.astype(BF16)
        for n in range(NB):
            sl = slice(n * HEAD, (n + 1) * HEAD)
            r_ref[:, sl] = jax.nn.sigmoid(_dot(xcb[:, sl], wr_ref[n]) + br_ref[:, sl])
            i_ref[:, sl] = _gate_sigmoid(_dot(xcb[:, sl], wi_ref[n]) + bi_ref[:, sl])
        _, a, _, mult = _lru_coeffs(lam_ref[...], r_ref[...])
        hs = mult * (i_ref[...] * xc)
        groups = tt // SUB
        a = a.reshape(groups, SUB, DR)
        hs = hs.reshape(groups, SUB, DR)
        sub = lax.broadcasted_iota(jnp.int32, (groups, SUB, DR), 1)
        d = 1
        while d < SUB:
            keep = sub >= d
            a_sh = jnp.where(keep, pltpu.roll(a, d, 1), 1.0)
            h_sh = jnp.where(keep, pltpu.roll(hs, d, 1), 0.0)
            hs = a * h_sh + hs
            a = a * a_sh
            d *= 2
        carry = hcar[...]
        for n in range(groups):
            h_ref[n * SUB:(n + 1) * SUB, :] = hs[n] + a[n] * carry
            carry = h_ref[(n + 1) * SUB - 1:(n + 1) * SUB, :]
        hcar[...] = carry
        h = h_ref[...]
        g = g_ref[...]
        y_ref[...] = (h * (g * _gate_sigmoid(g))).astype(y_ref.dtype)

    col = lambda j: pl.BlockSpec((tt, DR), lambda c: (c, j))
    vec = pl.BlockSpec((1, DR), lambda c: (0, 0))
    gate = pl.BlockSpec((NB, HEAD, HEAD), lambda c: (0, 0, 0))
    f32_out = jax.ShapeDtypeStruct((T, DR), F32)
    nc = T // tt
    return _call_with_rider(
        body, rider, lambda: pl.program_id(0) == 0, lambda: pl.program_id(0) == nc - 1,
        name=name, grid=(nc,),
        in_specs=[col(0), col(1), pl.BlockSpec((CONV_WIDTH, DR), lambda c: (0, 0)), vec, gate, vec, gate, vec, vec],
        out_specs=[col(0)] * 5,
        out_shape=[jax.ShapeDtypeStruct((T, DR + y_spare), BF16), f32_out, f32_out, f32_out, f32_out],
        scratch_shapes=[pltpu.VMEM((tt + PAD, DR), F32), pltpu.VMEM((1, DR), F32)],
        semantics=("arbitrary",),
        args=(proj, proj, conv_w, conv_b.reshape(1, DR), w_r, b_r.reshape(1, DR), w_i, b_i.reshape(1, DR),
              lam.reshape(1, DR)))


def _rglru_bwd(dy, proj, xc, r, i, h, conv_w, w_r, w_i, lam, name, rider=None, dx_spare=0):
    T = proj.shape[0]
    DR = conv_w.shape[1]
    NB = DR // HEAD
    tt = _row_tile(T, 128)
    nc = T // tt
    PAD = 8
    per = tt // PAD

    def body(dy_ref, x_ref, g_ref, xc_ref, r_ref, i_ref, h_ref, xprev_ref, hprev_ref,
             cw_ref, wr_ref, wi_ref, lam_ref,
             dxg_ref, dcw_ref, dcb_ref, dwr_ref, dbr_ref, dwi_ref, dbi_ref, dlam_ref,
             xbuf, dxcbuf, gcar, acar):
        step = pl.program_id(0)
        chunk = nc - 1 - step

        @pl.when(step == 0)
        def _():
            for ref in (dcw_ref, dcb_ref, dwr_ref, dbr_ref, dwi_ref, dbi_ref, dlam_ref, gcar, acar):
                ref[...] = jnp.zeros_like(ref)
            dxcbuf[tt:tt + PAD, :] = jnp.zeros((PAD, DR), F32)

        not_first = (chunk > 0).astype(F32)
        row = lax.broadcasted_iota(jnp.int32, (tt, DR), 0)
        silu, dsilu = _silu_and_grad(g_ref[...])
        dyv = dy_ref[...]
        hv = h_ref[...]
        dxg_ref[:, DR:2 * DR] = (dyv * hv * dsilu).astype(dxg_ref.dtype)
        dh = dyv * silu
        rv = r_ref[...]
        iv = i_ref[...]
        xcv = xc_ref[...]
        lam_row = lam_ref[...]
        cl, a, em, mult = _lru_coeffs(lam_row, rv)
        b = jnp.where(row == tt - 1, acar[...], pltpu.roll(a, tt - 1, 0))
        gs = dh
        d = 1
        while d < tt:
            keep = row < tt - d
            b_sh = jnp.where(keep, pltpu.roll(b, tt - d, 0), 1.0)
            g_sh = jnp.where(keep, pltpu.roll(gs, tt - d, 0), 0.0)
            gs = gs + b * g_sh
            b = b * b_sh
            d *= 2
        gt = gs + b * gcar[...]
        xbuf[0:tt, :] = gt
        gcar[...] = xbuf[0:1, :]
        acar[...] = _lru_coeffs(lam_row, r_ref[0:1, :])[1]
        h_before = hprev_ref[PAD - 1:PAD, :] * not_first
        hprev = jnp.where(row == 0, h_before, pltpu.roll(hv, 1, 0))
        da = gt * hprev
        dmult = gt * (iv * xcv)
        di = gt * mult * xcv
        dxc = gt * mult * iv
        dlog_a = da * a - dmult * (1.0 - em) / mult
        dr = dlog_a * cl
        dlam_ref[...] += jnp.sum(dlog_a * rv, axis=0, keepdims=True) * (LRU_C * jax.nn.sigmoid(-lam_row))
        drp = dr * rv * (1.0 - rv)
        dip = di * iv * (1.0 - iv)
        dbr_ref[...] += jnp.sum(drp, axis=0, keepdims=True)
        dbi_ref[...] += jnp.sum(dip, axis=0, keepdims=True)
        drpb = drp.astype(BF16)
        dipb = dip.astype(BF16)
        xcb = xcv.astype(BF16)
        for n in range(NB):
            sl = slice(n * HEAD, (n + 1) * HEAD)
            dxcbuf[0:tt, sl] = dxc[:, sl] + _dot_nt(drpb[:, sl], wr_ref[n]) + _dot_nt(dipb[:, sl], wi_ref[n])
            dwr_ref[n] += _dot_tn(xcb[:, sl], drpb[:, sl])
            dwi_ref[n] += _dot_tn(xcb[:, sl], dipb[:, sl])
        dxc_all = dxcbuf[0:tt, :]
        dcb_ref[...] += jnp.sum(dxc_all, axis=0, keepdims=True)
        xbuf[0:PAD, :] = xprev_ref[...] * not_first
        xbuf[PAD:PAD + tt, :] = x_ref[...]
        dx = cw_ref[0:1, :] * dxcbuf[pl.ds(3, tt), :]
        for k in range(1, CONV_WIDTH):
            dx = dx + cw_ref[k:k + 1, :] * dxcbuf[pl.ds(3 - k, tt), :]
        dxg_ref[:, 0:DR] = dx.astype(dxg_ref.dtype)
        for k in range(CONV_WIDTH):
            dcw_ref[k:k + 1, :] += jnp.sum(xbuf[pl.ds(PAD - 3 + k, tt), :] * dxc_all, axis=0, keepdims=True)
        dxcbuf[tt:tt + PAD, :] = dxcbuf[0:PAD, :]

    rev = lambda j: pl.BlockSpec((tt, DR), lambda s: (nc - 1 - s, j))
    prev = pl.BlockSpec((PAD, DR), lambda s: (jnp.maximum((nc - 1 - s) * per - 1, 0), 0))
    vec = pl.BlockSpec((1, DR), lambda s: (0, 0))
    gate = pl.BlockSpec((NB, HEAD, HEAD), lambda s: (0, 0, 0))
    taps = pl.BlockSpec((CONV_WIDTH, DR), lambda s: (0, 0))
    vec_out = jax.ShapeDtypeStruct((1, DR), F32)
    gate_out = jax.ShapeDtypeStruct((NB, HEAD, HEAD), F32)
    return _call_with_rider(
        body, rider, lambda: pl.program_id(0) == 0, lambda: pl.program_id(0) == nc - 1,
        name=name, grid=(nc,),
        in_specs=[rev(0), rev(0), rev(1), rev(0), rev(0), rev(0), rev(0), prev, prev, taps, gate, gate, vec],
        out_specs=[pl.BlockSpec((tt, 2 * DR), lambda s: (nc - 1 - s, 0)), taps, vec, gate, vec, gate, vec, vec],
        out_shape=[jax.ShapeDtypeStruct((T, 2 * DR + dx_spare), BF16), jax.ShapeDtypeStruct((CONV_WIDTH, DR), F32),
                   vec_out, gate_out, vec_out, gate_out, vec_out, vec_out],
        scratch_shapes=[pltpu.VMEM((tt + PAD, DR), F32), pltpu.VMEM((tt + PAD, DR), F32),
                        pltpu.VMEM((1, DR), F32), pltpu.VMEM((1, DR), F32)],
        semantics=("arbitrary",),
        args=(dy, proj, proj, xc, r, i, h, proj, h, conv_w, w_r, w_i, lam.reshape(1, DR)))


def _mem_probs(q, k, scale):
    s = _dot_nt(q, k) * scale
    p = jnp.exp(s - jnp.max(s, axis=-1, keepdims=True))
    return p * (1.0 / jnp.sum(p, axis=-1, keepdims=True))


def _memattn_fwd(proj, mkv, y_mix, DR, DM, name):
    T = proj.shape[0]
    M = mkv.shape[0]
    NH = DM // HEAD
    tt = _row_tile(T, 512)
    qcol = 2 * DR // DM
    scale = HEAD ** -0.5

    def body(q_ref, g_ref, k_ref, v_ref, _, y_ref):
        for n in range(NH):
            sl = slice(n * HEAD, (n + 1) * HEAD)
            p = _mem_probs(q_ref[:, sl].astype(BF16), k_ref[:, sl], scale)
            o = _dot(p.astype(BF16), v_ref[:, sl])
            g = g_ref[:, sl]
            y_ref[:, sl] = (o * (g * _gate_sigmoid(g))).astype(y_ref.dtype)

    return pl.pallas_call(
        body, name=name, grid=(T // tt,),
        in_specs=[pl.BlockSpec((tt, DM), lambda t: (t, qcol)), pl.BlockSpec((tt, DM), lambda t: (t, qcol + 1)),
                  pl.BlockSpec((M, DM), lambda t: (0, 0)), pl.BlockSpec((M, DM), lambda t: (0, 1)), HBM_SPEC],
        out_specs=pl.BlockSpec((tt, DM), lambda t: (t, DR // DM)),
        out_shape=jax.ShapeDtypeStruct(y_mix.shape, y_mix.dtype),
        input_output_aliases={4: 0},
        compiler_params=_params("parallel"),
    )(proj, proj, mkv, mkv, y_mix)


def _memattn_bwd(dy, proj, mkv, DR, DM, name, dproj=None):
    T = proj.shape[0]
    M = mkv.shape[0]
    NH = DM // HEAD
    tt = _row_tile(T, 512)
    qcol = 2 * DR // DM
    scale = HEAD ** -0.5

    def body(dy_ref, q_ref, g_ref, k_ref, v_ref, *rest):
        dqg_ref, dkv_ref = rest[-2:]

        @pl.when(pl.program_id(0) == 0)
        def _():
            dkv_ref[...] = jnp.zeros_like(dkv_ref)

        for n in range(NH):
            sl = slice(n * HEAD, (n + 1) * HEAD)
            qb = q_ref[:, sl].astype(BF16)
            kb = k_ref[:, sl]
            vb = v_ref[:, sl]
            p = _mem_probs(qb, kb, scale)
            pb = p.astype(BF16)
            o = _dot(pb, vb)
            silu, dsilu = _silu_and_grad(g_ref[:, sl])
            dyv = dy_ref[:, sl]
            dqg_ref[:, DM + n * HEAD:DM + (n + 1) * HEAD] = (dyv * o * dsilu).astype(dqg_ref.dtype)
            dob = (dyv * silu).astype(BF16)
            dp = _dot_nt(dob, vb)
            ds = (p * (dp - jnp.sum(dp * p, axis=-1, keepdims=True)) * scale).astype(BF16)
            dqg_ref[:, sl] = _dot(ds, kb).astype(dqg_ref.dtype)
            dkv_ref[:, sl] += _dot_tn(ds, qb)
            dkv_ref[:, DM + n * HEAD:DM + (n + 1) * HEAD] += _dot_tn(pb, dob)

    in_place = dproj is not None
    return pl.pallas_call(
        body, name=name, grid=(T // tt,),
        in_specs=[pl.BlockSpec((tt, DM), lambda t: (t, DR // DM)),
                  pl.BlockSpec((tt, DM), lambda t: (t, qcol)), pl.BlockSpec((tt, DM), lambda t: (t, qcol + 1)),
                  pl.BlockSpec((M, DM), lambda t: (0, 0)), pl.BlockSpec((M, DM), lambda t: (0, 1))]
                 + [HBM_SPEC] * in_place,
        out_specs=[pl.BlockSpec((tt, 2 * DM), lambda t: (t, DR // DM if in_place else 0)),
                   pl.BlockSpec((M, 2 * DM), lambda t: (0, 0))],
        out_shape=[jax.ShapeDtypeStruct(dproj.shape if in_place else (T, 2 * DM), BF16),
                   jax.ShapeDtypeStruct((M, 2 * DM), F32)],
        input_output_aliases={5: 0} if in_place else {},
        compiler_params=_params("arbitrary"),
    )(dy, proj, proj, mkv, mkv, *([dproj] if in_place else []))


def _sb_blocks(T):
    tk = _row_tile(T // 2, 256)
    tq = 2 * tk
    assert T % tq == 0
    return tq, tk


def _sb_upper(tk):
    row = lax.broadcasted_iota(jnp.int32, (tk, tk), 0)
    col = lax.broadcasted_iota(jnp.int32, (tk, tk), 1)
    return (row > col).astype(BF16)


def _sb_causal(tq, tk, d):
    row = lax.broadcasted_iota(jnp.int32, (tq, tk), 0)
    col = lax.broadcasted_iota(jnp.int32, (tq, tk), 1)
    return col + d * tk < row


def _sb_own_span(tile, carry, i, tq, tk):
    span = tq // tk
    for d in reversed(range(span)):
        lo = d * tk
        part = tile(span * i + d, tuple(c[lo:] for c in carry), _sb_causal(tq - lo, tk, 0), slice(lo, tq))
        carry = tuple(p if lo == 0 else jnp.concatenate([c[:lo], p], axis=0) for c, p in zip(carry, part))
    return carry


def _sb_earlier(tile, carry, jb_first, tq):
    half = tq // 2

    def walk(carry, jb, rows, watched):
        def more(c):
            return jnp.logical_and(c[0] >= 0, c[1] > 0)

        def step(c):
            jb, _, *rest = c
            rest = tile(jb, tuple(rest), None, rows)
            return jb - 1, _sb_alive(watched(rest[0])), *rest

        jb, _, *carry = lax.while_loop(more, step, (jb, _sb_alive(watched(carry[0])), *carry))
        return jb, tuple(carry)

    jb, carry = walk(carry, jb_first, slice(0, tq), lambda gone: gone[half:])
    _, low = walk(tuple(c[:half] for c in carry), jb, slice(0, half), lambda gone: gone)
    return tuple(jnp.concatenate([lo, c[half:]], axis=0) for lo, c in zip(low, carry))


def _later_sum(x, upper):
    n = x.shape[0]
    hi, lo = _split_bf16(x)
    both = _dot(jnp.concatenate([hi, lo], axis=0), upper)
    return both[0:n] + both[n:2 * n]


def _sb_alive(gone_c):
    return (jnp.min(gone_c) < SB_EXHAUSTED).astype(jnp.int32)


def _sb_weights(q, kb, gone_c, causal, upper, scale):
    return _sb_weights_of(_dot_nt(q, kb) * scale, gone_c, causal, upper)


def _sb_weights_of(z, gone_c, causal, upper):
    sp = jnp.where(z > 20.0, z, jnp.log(1.0 + jnp.exp(z)))
    spm = sp if causal is None else jnp.where(causal, sp, 0.0)
    gone = _later_sum(spm, upper) + gone_c
    w = jnp.exp(z - sp - gone)
    if causal is not None:
        w = jnp.where(causal, w, 0.0)
    return z, sp, spm, w


def _sb_fwd(proj, kv, DR, name, rider=None, y_spare=0):
    T = proj.shape[0]
    NH = DR // HEAD
    tq, tk = _sb_blocks(T)
    span = tq // tk
    scale = HEAD ** -0.5

    def body(q_ref, g_ref, k_ref, v_ref, y_ref, o_ref):
        i = pl.program_id(1)
        q = q_ref[...].astype(BF16)
        upper = _sb_upper(tk)

        def tile(jb, carry, causal, rows=slice(0, tq)):
            gone_c, acc = carry
            n = rows.stop - rows.start
            start = pl.multiple_of(jb * tk, tk)
            kb = k_ref[pl.ds(start, tk), :]
            vb = v_ref[pl.ds(start, tk), :]
            _, _, spm, w = _sb_weights(q[rows], kb, gone_c, causal, upper, scale)
            hi, lo = _split_bf16(w)
            pv = _dot(jnp.concatenate([hi, lo], axis=0), vb)
            return gone_c + jnp.sum(spm, axis=-1, keepdims=True), acc + pv[0:n] + pv[n:2 * n]

        carry = _sb_own_span(tile, (jnp.zeros((tq, 1), F32), jnp.zeros((tq, HEAD), F32)), i, tq, tk)

        _, acc = _sb_earlier(tile, carry, span * i - 1, tq)
        o_ref[...] = acc
        g = g_ref[...]
        y_ref[...] = (acc * (g * _gate_sigmoid(g))).astype(y_ref.dtype)

    blk = lambda off: pl.BlockSpec((tq, HEAD), lambda h, i: (i, off + h))
    whole = lambda off: pl.BlockSpec((T, HEAD), lambda h, i: (0, off + h))
    nq = T // tq
    return _call_with_rider(
        body, rider, lambda: (pl.program_id(0) == 0) & (pl.program_id(1) == 0),
        lambda: (pl.program_id(0) == NH - 1) & (pl.program_id(1) == nq - 1),
        name=name, grid=(NH, nq),
        in_specs=[blk(0), blk(NH), whole(0), whole(NH)],
        out_specs=[blk(0), blk(0)],
        out_shape=[jax.ShapeDtypeStruct((T, DR + y_spare), BF16), jax.ShapeDtypeStruct((T, DR), F32)],
        scratch_shapes=[], semantics=("parallel", "arbitrary"),
        args=(proj, proj, kv, kv))


def _sb_bwd(dy, proj, kv, o, DR, name, rider=None):
    T = proj.shape[0]
    NH = DR // HEAD
    tq, tk = _sb_blocks(T)
    span = tq // tk
    scale = HEAD ** -0.5

    def body(dy_ref, q_ref, g_ref, k_ref, v_ref, o_ref, dq_ref, dg_ref, dk_ref, dv_ref):
        i = pl.program_id(1)

        @pl.when(i == 0)
        def _():
            dk_ref[...] = jnp.zeros_like(dk_ref)
            dv_ref[...] = jnp.zeros_like(dv_ref)

        qf = q_ref[...]
        q = qf.astype(BF16)
        q_t = qf.T.astype(BF16)
        upper = _sb_upper(tk)
        silu, dsilu = _silu_and_grad(g_ref[...])
        dyv = dy_ref[...]
        ov = o_ref[...]
        dg_ref[...] = (dyv * ov * dsilu).astype(dg_ref.dtype)
        do = dyv * silu
        dob = do.astype(BF16)
        do_t = do.T.astype(BF16)
        total = jnp.sum(dob.astype(F32) * ov, axis=-1, keepdims=True)

        def tile(jb, carry, causal, rows=slice(0, tq)):
            gone_c, e_after, dq = carry
            start = pl.multiple_of(jb * tk, tk)
            kb = k_ref[pl.ds(start, tk), :]
            vb = v_ref[pl.ds(start, tk), :]
            z, sp, spm, w = _sb_weights(q[rows], kb, gone_c, causal, upper, scale)
            e = _dot_nt(dob[rows], vb) * w
            dz = e - (total[rows] - (_later_sum(e, upper) + e_after)) * jnp.exp(z - sp)
            if causal is not None:
                dz = jnp.where(causal, dz, 0.0)
            dzb = dz.astype(BF16)
            dk_ref[jb] += _dot(q_t[:, rows], dzb) * scale
            dv_ref[jb] += _dot(do_t[:, rows], w.astype(BF16))
            return (gone_c + jnp.sum(spm, axis=-1, keepdims=True), e_after + jnp.sum(e, axis=-1, keepdims=True),
                    dq + _dot(dzb, kb))

        carry = _sb_own_span(
            tile, (jnp.zeros((tq, 1), F32), jnp.zeros((tq, 1), F32), jnp.zeros((tq, HEAD), F32)), i, tq, tk)

        *_, dq = _sb_earlier(tile, carry, span * i - 1, tq)
        dq_ref[...] = (dq * scale).astype(dq_ref.dtype)

    blk = lambda off: pl.BlockSpec((tq, HEAD), lambda h, i: (i, off + h))
    whole = lambda off: pl.BlockSpec((T, HEAD), lambda h, i: (0, off + h))
    keys_t = pl.BlockSpec((None, T // tk, HEAD, tk), lambda h, i: (h, 0, 0, 0))
    keys_t_shape = jax.ShapeDtypeStruct((NH, T // tk, HEAD, tk), F32)
    nq = T // tq
    return _call_with_rider(
        body, rider, lambda: (pl.program_id(0) == 0) & (pl.program_id(1) == 0),
        lambda: (pl.program_id(0) == NH - 1) & (pl.program_id(1) == nq - 1),
        name=name, grid=(NH, nq),
        in_specs=[blk(0), blk(0), blk(NH), whole(0), whole(NH), blk(0)],
        out_specs=[blk(0), blk(0), keys_t, keys_t],
        out_shape=[jax.ShapeDtypeStruct((T, DR), BF16), jax.ShapeDtypeStruct((T, DR), BF16),
                   keys_t_shape, keys_t_shape],
        scratch_shapes=[], semantics=("parallel", "arbitrary"),
        args=(dy, proj, proj, kv, kv, o))


def _merge_dkv(parts, name):
    NH, nblk, _, tk = parts[0][0].shape
    DR = NH * HEAD
    n = len(parts)

    def body(*refs):
        o_ref = refs[-1]
        for which in range(2):
            for h in range(NH):
                acc = refs[which][h]
                for p in range(1, n):
                    acc = acc + refs[2 * p + which][h]
                col = which * DR + h * HEAD
                o_ref[:, col:col + HEAD] = acc.T.astype(o_ref.dtype)

    blk = pl.BlockSpec((NH, None, HEAD, tk), lambda t: (0, t, 0, 0))
    return pl.pallas_call(
        body, name=name, grid=(nblk,),
        in_specs=[blk] * (2 * n),
        out_specs=pl.BlockSpec((tk, 2 * DR), lambda t: (t, 0)),
        out_shape=jax.ShapeDtypeStruct((nblk * tk, 2 * DR), BF16),
        compiler_params=_params("parallel"),
    )(*[a for pair in parts for a in pair])


def _final_loss(h, g, target, name):
    T, D = h.shape
    tt = _row_tile(T, 256)

    def body(h_ref, g_ref, t_ref, dh_ref, dg_ref, sq_ref):
        @pl.when(pl.program_id(0) == 0)
        def _():
            dg_ref[...] = jnp.zeros_like(dg_ref)
            sq_ref[...] = jnp.zeros_like(sq_ref)

        hv = h_ref[...]
        gv = g_ref[...]
        r = lax.rsqrt(jnp.mean(hv * hv, axis=-1, keepdims=True) + EPS)
        xhat = hv * r
        err = xhat * gv - t_ref[...]
        sq_ref[...] += jnp.sum(err * err, axis=0, keepdims=True)
        dy = err * (1.0 / D)
        dxhat = dy * gv
        dh_ref[...] = r * (dxhat - xhat * jnp.mean(dxhat * xhat, axis=-1, keepdims=True))
        dg_ref[...] += jnp.sum(dy * xhat, axis=0, keepdims=True)

    row = pl.BlockSpec((tt, D), lambda i: (i, 0))
    vec = pl.BlockSpec((1, D), lambda i: (0, 0))
    return pl.pallas_call(
        body, name=name, grid=(T // tt,),
        in_specs=[row, vec, row],
        out_specs=[row, vec, vec],
        out_shape=[jax.ShapeDtypeStruct((T, D), F32), jax.ShapeDtypeStruct((1, D), F32),
                   jax.ShapeDtypeStruct((1, D), F32)],
        compiler_params=_params("arbitrary"),
    )(h, g.reshape(1, D), target)


def _sum_parts(parts_ref):
    g = parts_ref[0].astype(F32)
    for s in range(1, parts_ref.shape[0]):
        g = g + parts_ref[s].astype(F32)
    return g


def _adamw(parts, w, m, v, name, layer=None, into=None):
    P, R, C = parts.shape
    tr = _rows_for(R, P * C * 4)

    def body(p_ref, w_ref, m_ref, v_ref, *rest):
        g_ref, d_ref, nm_ref, nv_ref = rest[-4:]
        g = _sum_parts(p_ref)
        nm = ADAM_B1 * m_ref[...] + (1.0 - ADAM_B1) * g
        nv = ADAM_B2 * v_ref[...] + (1.0 - ADAM_B2) * jnp.square(g)
        m_hat = nm / (1.0 - ADAM_B1 ** ADAM_STEP)
        v_hat = nv / (1.0 - ADAM_B2 ** ADAM_STEP)
        g_ref[...] = g
        d_ref[...] = -ADAM_LR * (m_hat / (jnp.sqrt(v_hat) + ADAM_EPS) + ADAM_WD * w_ref[...])
        nm_ref[...] = nm
        nv_ref[...] = nv

    if layer is None:
        row = pl.BlockSpec((tr, C), lambda i: (i, 0))
        out = jax.ShapeDtypeStruct((R, C), F32)
    else:
        row = pl.BlockSpec((None, tr, C), lambda i: (layer, i, 0))
        out = jax.ShapeDtypeStruct(w.shape, F32)
    earlier = list(into) if into is not None else []
    return pl.pallas_call(
        body, name=name, grid=(R // tr,),
        in_specs=[pl.BlockSpec((P, tr, C), lambda i: (0, i, 0)), row, row, row] + [HBM_SPEC] * len(earlier),
        out_specs=[row] * 4,
        out_shape=[out] * 4,
        input_output_aliases={4 + k: k for k in range(len(earlier))},
        compiler_params=_params("parallel"),
    )(parts, w, m, v, *earlier)


def _sum_devices(parts, name):
    P, R, C = parts.shape
    tr = _rows_for(R, P * C * 4)

    def body(p_ref, o_ref):
        o_ref[...] = _sum_parts(p_ref)

    return pl.pallas_call(
        body, name=name, grid=(R // tr,),
        in_specs=[pl.BlockSpec((P, tr, C), lambda i: (0, i, 0))],
        out_specs=pl.BlockSpec((tr, C), lambda i: (i, 0)),
        out_shape=jax.ShapeDtypeStruct((R, C), F32),
        compiler_params=_params("parallel"),
    )(parts)


def _mesh_position():
    return lax.axis_index("x"), lax.axis_index("y"), lax.axis_index("c")


def _device_index(p):
    return 4 * p[0] + 2 * p[1] + p[2]


class _Gather:
    def __init__(self, arrs):
        self.arrs = list(arrs)
        self.n = len(self.arrs)

    def out_shape(self):
        return [jax.ShapeDtypeStruct((N_DEV,) + a.shape, a.dtype) for a in self.arrs]

    def scratch(self):
        return [pltpu.SemaphoreType.DMA((self.n, 7)), pltpu.SemaphoreType.DMA((self.n, 7)),
                pltpu.SemaphoreType.DMA((self.n,))]

    def _plan(self, ins, outs, sems):
        send_sems, recv_sems, local_sems = sems
        x, y, c = _mesh_position()
        me, sibling = (x, y, c), (x, y, 1 - c)
        chips = [(1 - x, y), (x, 1 - y), (1 - x, 1 - y)]

        def slot(a, p):
            return outs[a].at[_device_index(p)]

        def copy(a, k, block, to, src=None):
            return pltpu.make_async_remote_copy(
                src_ref=slot(a, block) if src is None else src, dst_ref=slot(a, block),
                send_sem=send_sems.at[a, k], recv_sem=recv_sems.at[a, k],
                device_id=to, device_id_type=pl.DeviceIdType.MESH)

        mine = [pltpu.make_async_copy(ins[a], slot(a, me), local_sems.at[a]) for a in range(self.n)]
        first = []
        for a in range(self.n):
            first.append(copy(a, 0, me, sibling, src=ins[a]))
            first += [copy(a, 1 + j, me, (*chip, c), src=ins[a]) for j, chip in enumerate(chips)]
        return me, sibling, c, chips, copy, mine, first

    def start(self, ins, outs, sems):
        *_, mine, first = self._plan(ins, outs, sems)
        for cp in mine + first:
            cp.start()

    def wait(self, ins, outs, sems):
        me, sibling, c, chips, copy, mine, first = self._plan(ins, outs, sems)
        passed = []
        for a in range(self.n):
            for j, chip in enumerate(chips):
                copy(a, 1 + j, (*chip, c), me).wait_recv()
                fwd = copy(a, 4 + j, (*chip, c), sibling)
                fwd.start()
                passed.append(fwd)
        for a in range(self.n):
            copy(a, 0, sibling, me).wait_recv()
            for j, chip in enumerate(chips):
                copy(a, 4 + j, (*chip, 1 - c), me).wait_recv()
        for cp in first + passed:
            cp.wait_send()
        for cp in mine:
            cp.wait()


def _exchange_now(ex, name):
    n = ex.n

    def body(*refs):
        ins, outs, sems = refs[:n], refs[n:2 * n], refs[2 * n:]
        ex.start(ins, outs, sems)
        ex.wait(ins, outs, sems)

    return pl.pallas_call(
        body, name=name,
        in_specs=[HBM_SPEC] * n, out_specs=[HBM_SPEC] * n,
        out_shape=ex.out_shape(), scratch_shapes=ex.scratch(),
    )(*ex.arrs)


def _all_gather(arrs, name):
    return _exchange_now(_Gather(arrs), name)


class _Scatter:
    def __init__(self, arrs):
        self.arrs = list(arrs)
        self.n = len(self.arrs)

    def out_shape(self):
        return [jax.ShapeDtypeStruct(a.shape, a.dtype) for a in self.arrs]

    def scratch(self):
        return [pltpu.SemaphoreType.DMA((self.n, 7)), pltpu.SemaphoreType.DMA((self.n, 7)),
                pltpu.SemaphoreType.DMA((self.n,))]

    def _copies(self, ins, outs, sems, arrivals):
        send_sems, recv_sems, local_sems = sems
        x, y, c = _mesh_position()
        me = _device_index((x, y, c))
        peers = [(1 - x if k & 4 else x, 1 - y if k & 2 else y, 1 - c if k & 1 else c) for k in range(1, N_DEV)]
        local, sends, recvs = [], [], []
        for a in range(self.n):
            local.append(pltpu.make_async_copy(ins[a].at[me], outs[a].at[me], local_sems.at[a]))
            for k, peer in enumerate(peers):
                there = _device_index(peer)
                src = ins[a].at[there]
                sem = dict(send_sem=send_sems.at[a, k], recv_sem=recv_sems.at[a, k],
                           device_id=peer, device_id_type=pl.DeviceIdType.MESH)
                sends.append(pltpu.make_async_remote_copy(src_ref=src, dst_ref=outs[a].at[me], **sem))
                if arrivals:
                    recvs.append(pltpu.make_async_remote_copy(src_ref=src, dst_ref=outs[a].at[there], **sem))
        return local, sends, recvs

    def start(self, ins, outs, sems):
        local, sends, _ = self._copies(ins, outs, sems, arrivals=False)
        for cp in local + sends:
            cp.start()

    def wait(self, ins, outs, sems):
        local, sends, recvs = self._copies(ins, outs, sems, arrivals=True)
        for cp in recvs:
            cp.wait_recv()
        for cp in sends:
            cp.wait_send()
        for cp in local:
            cp.wait()


def _all_to_all(arrs, name):
    return _exchange_now(_Scatter(arrs), name)


def _call_with_rider(body, rider, first, last, *, name, grid, in_specs, out_specs, out_shape, scratch_shapes,
                     semantics, args):
    if rider is None:
        out = pl.pallas_call(body, name=name, grid=grid, in_specs=in_specs, out_specs=out_specs,
                             out_shape=out_shape, scratch_shapes=scratch_shapes,
                             compiler_params=_params(*semantics))(*args)
        return out, None
    n, n_in, n_out = rider.n, len(in_specs), len(out_specs)

    def riding(*refs):
        ins, r_in = refs[:n_in], refs[n_in:n_in + n]
        outs, r_out = refs[n_in + n:n_in + n + n_out], refs[n_in + n + n_out:n_in + 2 * n + n_out]
        scratch, sems = refs[n_in + 2 * n + n_out:-3], refs[-3:]

        @pl.when(first())
        def _():
            rider.start(r_in, r_out, sems)

        body(*ins, *outs, *scratch)

        @pl.when(last())
        def _():
            rider.wait(r_in, r_out, sems)

    out = pl.pallas_call(
        riding, name=name, grid=grid,
        in_specs=list(in_specs) + [HBM_SPEC] * n, out_specs=list(out_specs) + [HBM_SPEC] * n,
        out_shape=list(out_shape) + rider.out_shape(),
        scratch_shapes=list(scratch_shapes) + rider.scratch(),
        compiler_params=_params(*["arbitrary"] * len(grid)),
    )(*args, *rider.arrs)
    return out[:n_out], out[n_out:]


def _pack(arrs, row_multiple):
    parts = []
    rows = 0
    for a in arrs:
        flat = a.reshape(-1).astype(F32)
        r = -(-flat.shape[0] // (8 * LANES)) * 8
        parts.append(jnp.pad(flat, (0, r * LANES - flat.shape[0])).reshape(r, LANES))
        rows += r
    pad = -rows % row_multiple
    if pad:
        parts.append(jnp.zeros((pad, LANES), F32))
    return jnp.concatenate(parts, axis=0)


def _unpack(buf, shapes, lead=()):
    out = []
    r0 = 0
    for shape in shapes:
        size = 1
        for s in shape:
            size *= s
        r = -(-size // (8 * LANES)) * 8
        part = buf[..., r0:r0 + r, :].reshape(lead + (r * LANES,))[..., :size]
        out.append(part.reshape(lead + tuple(shape)))
        r0 += r
    return out


def _gathered_cols(g):
    g = jnp.moveaxis(g, 0, -2)
    return g.reshape(g.shape[:-2] + (g.shape[-2] * g.shape[-1],))


def kernel(x, mem, mem_norm, w_mem_kv, norm_a, w_in_a, conv_w, conv_b, w_rec_gate, b_rec_gate, w_in_gate, b_in_gate, lru_lambda, w_out_a, kv_norm, w_kv, norm_b, w_in_b, w_out_b, final_norm, loss_target, m_mem_norm, m_w_mem_kv, m_norm_a, m_w_in_a, m_conv_w, m_conv_b, m_w_rec_gate, m_b_rec_gate, m_w_in_gate, m_b_in_gate, m_lru_lambda, m_w_out_a, m_kv_norm, m_w_kv, m_norm_b, m_w_in_b, m_w_out_b, m_final_norm, v_mem_norm, v_w_mem_kv, v_norm_a, v_w_in_a, v_conv_w, v_conv_b, v_w_rec_gate, v_b_rec_gate, v_w_in_gate, v_b_in_gate, v_lru_lambda, v_w_out_a, v_kv_norm, v_w_kv, v_norm_b, v_w_in_b, v_w_out_b, v_final_norm):
    xs = x[0]
    T, D = xs.shape
    L = w_mem_kv.shape[0]
    NA = w_in_a.shape[0]
    NB = w_in_b.shape[0]
    DM2 = w_mem_kv.shape[2]
    DM = DM2 // 2
    DR = w_rec_gate.shape[1] * w_rec_gate.shape[2]
    me = _device_index(_mesh_position())

    small_sharded = [norm_a, conv_w, conv_b, b_rec_gate, b_in_gate, lru_lambda]
    shard = {f"mem_kv{l}": w_mem_kv[l] for l in range(L)}
    shard.update({f"in_a{l}": w_in_a[l] for l in range(NA)}, **{f"out_a{l}": w_out_a[l] for l in range(NA)})
    shard.update({f"in_b{j}": w_in_b[j] for j in range(NB)}, **{f"out_b{j}": w_out_b[j] for j in range(NB)})
    shard["kv"] = w_kv
    shard = {k: w.astype(BF16) for k, w in shard.items()}
    now = ["mem_kv0"]
    gathered = _all_gather([shard[k] for k in now] + [_pack(small_sharded, 8)], "gather_params")
    full = dict(zip(now, gathered[:-1]))

    def gather_rider(keys):
        return _Gather([shard[k] for k in keys]), keys

    def out_proj(key):
        return full[key].reshape(1, -1, D)

    def mem_proj(l):
        return full[f"mem_kv{l}"].reshape(1, D, DM2)

    norm_a_f, conv_w_f, conv_b_f, b_r_f, b_i_f, lam_f = [
        _gathered_cols(s) for s in _unpack(gathered[-1], [s.shape for s in small_sharded], lead=(N_DEV,))]
    w_r_bf = w_rec_gate.astype(BF16)
    w_i_bf = w_in_gate.astype(BF16)

    zeros_mem = jnp.zeros_like(mem[0])
    mem_n = _rms_fwd(mem[0], mem_norm, "rms_mem")
    mkv = [_mm_nn(mem_n, mem_proj(0), out_dtype=BF16, name="mm_mem_kv0")]

    h = xs
    rider, keys = gather_rider(["in_a0"])
    u, landed = _rms_fwd(h, norm_a_f[0], "rms_a0", rider)
    full.update(zip(keys, landed))
    saved_a = []
    for l in range(NA):
        rider, keys = gather_rider([f"out_a{l}"])
        proj, landed = _mm_nn(u, full[f"in_a{l}"], name=f"mm_in_a{l}", rider=rider)
        full.update(zip(keys, landed))
        later_mem = [f"mem_kv{m}" for m in range(1, L)] if l == 0 else []
        rider, keys = gather_rider([f"in_a{l + 1}" if l + 1 < NA else "kv"] + later_mem)
        (y_rnn, xc, r, i, hr), landed = _rglru_fwd(proj, conv_w_f[l], conv_b_f[l], w_r_bf[l], b_r_f[l], w_i_bf[l],
                                                   b_i_f[l], lam_f[l], f"rglru_fwd{l}", rider, y_spare=DM)
        full.update(zip(keys, landed))
        if l == 0:
            mkv += [_mm_nn(mem_n, mem_proj(m), out_dtype=BF16, name=f"mm_mem_kv{m}") for m in range(1, L)]
        ycat = _memattn_fwd(proj, mkv[l], y_rnn, DR, DM, f"memattn_fwd_a{l}")
        gains = [norm_a_f[l + 1]] if l + 1 < NA else [kv_norm, norm_b[0]]
        h_next, *normed = _mm_nn(ycat, out_proj(f"out_a{l}"), res=h, name=f"mm_out_a{l}", norms=gains)
        saved_a.append((h, u, proj, xc, r, i, hr, ycat))
        h, u = h_next, normed[-1]
    h_kv, u_kv = h, normed[0]
    rider, keys = gather_rider(["in_b0"])
    kv, landed = _mm_nn(u_kv, full["kv"], out_dtype=BF16, name="mm_kv", rider=rider)
    full.update(zip(keys, landed))
    saved_b = []
    for j in range(NB):
        proj = _mm_nn(u, full[f"in_b{j}"], name=f"mm_in_b{j}")
        rider, keys = gather_rider([f"out_b{j}"] + ([f"in_b{j + 1}"] if j + 1 < NB else []))
        (y_sb, o_sb), landed = _sb_fwd(proj, kv, DR, f"sb_fwd{j}", rider, y_spare=DM)
        full.update(zip(keys, landed))
        ycat = _memattn_fwd(proj, mkv[NA + j], y_sb, DR, DM, f"memattn_fwd_b{j}")
        if j + 1 < NB:
            h_next, u_next = _mm_nn(ycat, out_proj(f"out_b{j}"), res=h, name=f"mm_out_b{j}", norms=[norm_b[j + 1]])
        else:
            h_next, u_next = _mm_nn(ycat, out_proj(f"out_b{j}"), res=h, name=f"mm_out_b{j}"), None
        saved_b.append((h, u, proj, o_sb, ycat))
        h, u = h_next, u_next

    dh, d_final_norm, sq = _final_loss(h, final_norm, loss_target[0], "final_loss")
    loss = lax.psum(0.5 * jnp.sum(sq) / D, ("x", "y", "c"))

    big_grads = {}
    received = {}

    def scatter_rider(keys):
        return _Scatter([big_grads[k] for k in keys]), keys

    dmkv = [None] * L
    d_norm_b = [None] * NB
    dkv_parts = []
    for j in reversed(range(NB)):
        h_in, u, proj, o_sb, ycat = saved_b[j]
        dy = _mm_nt(dh, out_proj(f"out_b{j}"), name=f"mm_dy_b{j}")
        big_grads[f"out_b{j}"] = _mm_tn(ycat, dh, 1, name=f"mm_dw_out_b{j}").reshape(N_DEV, -1, D)
        rider, keys = scatter_rider(([f"in_b{j + 1}"] if j + 1 < NB else []) + [f"out_b{j}"])
        (dq, dg, dk, dv), landed = _sb_bwd(dy, proj, kv, o_sb, DR, f"sb_bwd{j}", rider)
        received.update(zip(keys, landed))
        dkv_parts.append((dk, dv))
        dqg_mem, dmkv[NA + j] = _memattn_bwd(dy, proj, mkv[NA + j], DR, DM, f"memattn_bwd_b{j}")
        dproj = jnp.concatenate([dq, dg, dqg_mem], axis=-1)
        du = _mm_nt(dproj, full[f"in_b{j}"], name=f"mm_du_b{j}")
        big_grads[f"in_b{j}"] = _mm_tn(u, dproj, N_DEV, name=f"mm_dw_in_b{j}")
        dh, d_norm_b[j] = _rms_bwd(du, h_in, norm_b[j], dh, f"rms_bwd_b{j}")

    dkv = _merge_dkv(dkv_parts, "merge_dkv")
    rider, keys = scatter_rider(["in_b0"])
    du_kv, landed = _mm_nt(dkv, full["kv"], name="mm_du_kv", rider=rider)
    received.update(zip(keys, landed))
    big_grads["kv"] = _mm_tn(u_kv, dkv, N_DEV, name="mm_dw_kv")
    dh, d_kv_norm = _rms_bwd(du_kv, h_kv, kv_norm, dh, "rms_bwd_kv")

    d_norm_a, d_conv_w, d_conv_b, d_w_r, d_b_r, d_w_i, d_b_i, d_lam = ([None] * NA for _ in range(8))
    for l in reversed(range(NA)):
        h_in, u, proj, xc, r, i, hr, ycat = saved_a[l]
        dy = _mm_nt(dh, out_proj(f"out_a{l}"), name=f"mm_dy_a{l}")
        big_grads[f"out_a{l}"] = _mm_tn(ycat, dh, 1, name=f"mm_dw_out_a{l}").reshape(N_DEV, -1, D)
        rider, keys = scatter_rider([f"in_a{l + 1}" if l + 1 < NA else "kv", f"out_a{l}"])
        (dxg, d_conv_w[l], d_conv_b[l], d_w_r[l], d_b_r[l], d_w_i[l], d_b_i[l], d_lam[l]), landed = _rglru_bwd(
            dy, proj, xc, r, i, hr, conv_w_f[l], w_r_bf[l], w_i_bf[l], lam_f[l], f"rglru_bwd{l}", rider,
            dx_spare=DM2)
        received.update(zip(keys, landed))
        dproj, dmkv[l] = _memattn_bwd(dy, proj, mkv[l], DR, DM, f"memattn_bwd_a{l}", dproj=dxg)
        if l > 0:
            du = _mm_nt(dproj, full[f"in_a{l}"], name=f"mm_du_a{l}")
            big_grads[f"in_a{l}"] = _mm_tn(u, dproj, N_DEV, name=f"mm_dw_in_a{l}")
        else:
            dmkv_all = jnp.concatenate(dmkv, axis=-1)
            dmem_n = _mm_nt(dmkv_all, jnp.concatenate([mem_proj(m) for m in range(L)], axis=0), name="mm_dmem")
            for m in range(L):
                big_grads[f"mem_kv{m}"] = _mm_tn(mem_n, dmkv[m], 1, name=f"mm_dw_mem_kv{m}").reshape(N_DEV, -1, DM2)
            _, d_mem_norm = _rms_bwd(dmem_n, mem[0], mem_norm, zeros_mem, "rms_bwd_mem")
            rider, keys = scatter_rider([f"mem_kv{m}" for m in range(L)])
            du, landed = _mm_nt(dproj, full["in_a0"], name="mm_du_a0", rider=rider)
            received.update(zip(keys, landed))
            small_early = {
                "mem_norm": d_mem_norm.reshape(-1),
                "conv_w": jnp.stack(d_conv_w),
                "conv_b": jnp.concatenate(d_conv_b, axis=0),
                "w_rec_gate": jnp.stack(d_w_r),
                "b_rec_gate": jnp.concatenate(d_b_r, axis=0),
                "w_in_gate": jnp.stack(d_w_i),
                "b_in_gate": jnp.concatenate(d_b_i, axis=0),
                "lru_lambda": jnp.concatenate(d_lam, axis=0),
                "kv_norm": d_kv_norm.reshape(-1),
                "norm_b": jnp.concatenate(d_norm_b, axis=0),
                "final_norm": d_final_norm.reshape(-1),
            }
            rider = _Gather([_pack(list(small_early.values()), 256)])
            big_grads["in_a0"], (early_gathered,) = _mm_tn(u, dproj, N_DEV, name="mm_dw_in_a0", rider=rider)
        dh, d_norm_a[l] = _rms_bwd(du, h_in, norm_a_f[l], dh, f"rms_bwd_a{l}")
    grad_x = dh.reshape(x.shape)

    received.update(zip(["in_a0"], _all_to_all([big_grads["in_a0"]], "scatter_grads")))
    small_late = {"norm_a": jnp.concatenate(d_norm_a, axis=0)}
    (late_gathered,) = _all_gather([_pack(list(small_late.values()), 256)], "gather_small_grads")

    def update(key, w, m, v):
        shape = w.shape
        two_d = (-1, shape[-1])
        return [o.reshape(shape) for o in _adamw(received[key], w.reshape(two_d), m.reshape(two_d),
                                                 v.reshape(two_d), f"adamw_{key}")]

    def update_layers(prefix, w, m, v):
        outs = None
        for l in range(w.shape[0]):
            outs = _adamw(received[f"{prefix}{l}"], w, m, v, f"adamw_{prefix}{l}", layer=l, into=outs)
        return outs

    upd = {
        "w_mem_kv": update_layers("mem_kv", w_mem_kv, m_w_mem_kv, v_w_mem_kv),
        "w_in_a": update_layers("in_a", w_in_a, m_w_in_a, v_w_in_a),
        "w_out_a": update_layers("out_a", w_out_a, m_w_out_a, v_w_out_a),
        "w_kv": update("kv", w_kv, m_w_kv, v_w_kv),
        "w_in_b": update_layers("in_b", w_in_b, m_w_in_b, v_w_in_b),
        "w_out_b": update_layers("out_b", w_out_b, m_w_out_b, v_w_out_b),
    }

    small_grad = {}
    for part, got, name in ((small_early, early_gathered, "sum_small_early"), (small_late, late_gathered, "sum_small_late")):
        summed = _unpack(_sum_devices(got, name), [g.shape for g in part.values()])
        small_grad.update(zip(part, summed))
    small_names = list(small_grad)
    small_w = {"mem_norm": (mem_norm, m_mem_norm, v_mem_norm), "norm_a": (norm_a, m_norm_a, v_norm_a),
               "conv_w": (conv_w, m_conv_w, v_conv_w), "conv_b": (conv_b, m_conv_b, v_conv_b),
               "w_rec_gate": (w_rec_gate, m_w_rec_gate, v_w_rec_gate),
               "b_rec_gate": (b_rec_gate, m_b_rec_gate, v_b_rec_gate),
               "w_in_gate": (w_in_gate, m_w_in_gate, v_w_in_gate), "b_in_gate": (b_in_gate, m_b_in_gate, v_b_in_gate),
               "lru_lambda": (lru_lambda, m_lru_lambda, v_lru_lambda), "kv_norm": (kv_norm, m_kv_norm, v_kv_norm),
               "norm_b": (norm_b, m_norm_b, v_norm_b), "final_norm": (final_norm, m_final_norm, v_final_norm)}
    for k in small_names:
        w = small_w[k][0]
        if small_grad[k].shape != w.shape:
            n = w.shape[-1]
            small_grad[k] = lax.dynamic_slice_in_dim(small_grad[k], me * n, n, axis=-1)
    small_shapes = [small_w[k][0].shape for k in small_names]
    packed = [_pack([small_grad[k] for k in small_names], 256)[None]]
    packed += [_pack([small_w[k][t] for k in small_names], 256) for t in range(3)]
    small_out = [_unpack(o, small_shapes) for o in _adamw(*packed, "adamw_small")]
    for idx, k in enumerate(small_names):
        upd[k] = [small_out[t][idx] for t in range(4)]

    order = ["mem_norm", "w_mem_kv", "norm_a", "w_in_a", "conv_w", "conv_b", "w_rec_gate", "b_rec_gate", "w_in_gate",
             "b_in_gate", "lru_lambda", "w_out_a", "kv_norm", "w_kv", "norm_b", "w_in_b", "w_out_b", "final_norm"]
    return (loss, grad_x, *[upd[k][0] for k in order], *[upd[k][1] for k in order],
            *[upd[k][2] for k in order], *[upd[k][3] for k in order])
```

```python
import functools

import jax
import jax.numpy as jnp
from jax import lax
from jax.experimental import pallas as pl
from jax.experimental.pallas import tpu as pltpu

F32 = jnp.float32
BF16 = jnp.bfloat16

N_DEV = 8
EPS = 1e-6
LRU_C = 8.0
HEAD = 128
CONV_WIDTH = 4
LANES = 128
SUB = 8
SB_EXHAUSTED = 110.0
VMEM_LIMIT = 56 * 1024 * 1024
RESIDENT_WEIGHT_BYTES = 8 * 1024 * 1024

ADAM_LR = 0.001
ADAM_B1 = 0.9
ADAM_B2 = 0.999
ADAM_EPS = 1e-08
ADAM_WD = 0.01
ADAM_STEP = 10

HBM_SPEC = pl.BlockSpec(memory_space=pltpu.HBM)


def _params(*semantics):
    return pltpu.CompilerParams(dimension_semantics=semantics, vmem_limit_bytes=VMEM_LIMIT)


def _tile(n, cap):
    if n <= cap:
        return n
    t = cap - cap % LANES
    while n % t:
        t -= LANES
    return t


def _row_tile(n, cap):
    if n <= cap:
        return n
    t = cap - cap % 8
    while t >= 8:
        if n % t == 0:
            return t
        t -= 8
    return n


def _rows_for(n, bytes_per_row, budget=2 * 1024 * 1024):
    return _row_tile(n, max(8, budget // bytes_per_row))


def _dot(a, b):
    return lax.dot_general(a, b, (((1,), (0,)), ((), ())), preferred_element_type=F32)


def _dot_nt(a, b):
    return lax.dot_general(a, b, (((1,), (1,)), ((), ())), preferred_element_type=F32)


def _dot_tn(a, b):
    return lax.dot_general(a, b, (((0,), (0,)), ((), ())), preferred_element_type=F32)


def _split_bf16(x):
    hi = x.astype(BF16)
    lo = (x - hi.astype(F32)).astype(BF16)
    return hi, lo


def _softplus(x):
    return jnp.maximum(x, 0.0) + jnp.log1p(jnp.exp(-jnp.abs(x)))


def _gate_sigmoid(x):
    return 0.5 * jnp.tanh(0.5 * x) + 0.5


def _silu_and_grad(g):
    sg = _gate_sigmoid(g)
    return g * sg, sg * (1.0 + g * (1.0 - sg))


def _one_minus_square(a, log_a):
    x = 2.0 * log_a
    series = -x * (1.0 + x * (0.5 + x * (1.0 / 6.0)))
    return jnp.where(x > -0.03, series, 1.0 - a * a)


def _rms_fwd(x, g, name, rider=None):
    T, D = x.shape
    tt = _row_tile(T, 512)

    def body(x_ref, g_ref, o_ref):
        xv = x_ref[...]
        r = lax.rsqrt(jnp.mean(xv * xv, axis=-1, keepdims=True) + EPS)
        o_ref[...] = ((xv * r) * g_ref[...]).astype(o_ref.dtype)

    (out,), landed = _call_with_rider(
        body, rider, *_grid_ends((T // tt,)), name=name, grid=(T // tt,),
        in_specs=[pl.BlockSpec((tt, D), lambda i: (i, 0)), pl.BlockSpec((1, D), lambda i: (0, 0))],
        out_specs=[pl.BlockSpec((tt, D), lambda i: (i, 0))],
        out_shape=[jax.ShapeDtypeStruct((T, D), BF16)],
        scratch_shapes=[], semantics=("parallel",), args=(x, g.reshape(1, D)))
    return out if rider is None else (out, landed)


def _rms_bwd(du, h, g, res, name):
    T, D = h.shape
    tt = _row_tile(T, 256)

    def body(du_ref, h_ref, g_ref, res_ref, dh_ref, dg_ref):
        @pl.when(pl.program_id(0) == 0)
        def _():
            dg_ref[...] = jnp.zeros_like(dg_ref)

        hv = h_ref[...]
        duv = du_ref[...]
        r = lax.rsqrt(jnp.mean(hv * hv, axis=-1, keepdims=True) + EPS)
        xhat = hv * r
        dxhat = duv * g_ref[...]
        dh_ref[...] = res_ref[...] + r * (dxhat - xhat * jnp.mean(dxhat * xhat, axis=-1, keepdims=True))
        dg_ref[...] += jnp.sum(duv * xhat, axis=0, keepdims=True)

    row = pl.BlockSpec((tt, D), lambda i: (i, 0))
    vec = pl.BlockSpec((1, D), lambda i: (0, 0))
    return pl.pallas_call(
        body, name=name, grid=(T // tt,),
        in_specs=[row, row, vec, row],
        out_specs=[row, vec],
        out_shape=[jax.ShapeDtypeStruct((T, D), F32), jax.ShapeDtypeStruct((1, D), F32)],
        compiler_params=_params("arbitrary"),
    )(du, h, g.reshape(1, D), res)


def _mm_nn(a, b3, res=None, out_dtype=F32, name=None, rider=None, norms=()):
    M, K = a.shape
    S, _, ns = b3.shape
    whole_b = S == 1 and K * ns * 2 <= RESIDENT_WEIGHT_BYTES
    tm = _row_tile(M, 512 if whole_b or a.dtype != BF16 else 1024)
    tn = ns if whole_b else _tile(ns, 512)
    nj = ns // tn

    assert not norms or (whole_b and rider is None)
    n_in = 2 + (res is not None) + len(norms)

    def body(*refs):
        a_ref, b_ref = refs[:2]
        o_ref = refs[n_in]
        acc = _dot(a_ref[...].astype(BF16), b_ref[...])
        if res is not None:
            acc = acc + refs[2][...]
        o_ref[...] = acc.astype(o_ref.dtype)
        if norms:
            r = lax.rsqrt(jnp.mean(acc * acc, axis=-1, keepdims=True) + EPS)
            for g_ref, u_ref in zip(refs[n_in - len(norms):n_in], refs[n_in + 1:]):
                u_ref[...] = ((acc * r) * g_ref[...]).astype(u_ref.dtype)

    tile = pl.BlockSpec((tm, tn), lambda i, j: (i, j))
    in_specs = [pl.BlockSpec((tm, K), lambda i, j: (i, 0)),
                pl.BlockSpec((None, K, tn), lambda i, j: (j // nj, 0, j % nj))]
    args = [a, b3]
    if res is not None:
        in_specs.append(tile)
        args.append(res)
    in_specs += [pl.BlockSpec((1, tn), lambda i, j: (0, j))] * len(norms)
    args += [g.reshape(1, -1) for g in norms]
    outs, landed = _call_with_rider(
        body, rider, *_grid_ends((M // tm, S * nj)), name=name, grid=(M // tm, S * nj),
        in_specs=in_specs,
        out_specs=[tile] * (1 + len(norms)),
        out_shape=[jax.ShapeDtypeStruct((M, S * ns), out_dtype)] + [jax.ShapeDtypeStruct((M, S * ns), BF16)] * len(norms),
        scratch_shapes=[], semantics=("parallel", "parallel"), args=args)
    if norms:
        return tuple(outs)
    return outs[0] if rider is None else (outs[0], landed)


def _grid_ends(grid):
    def first():
        return functools.reduce(jnp.logical_and, [pl.program_id(d) == 0 for d in range(len(grid))])

    def last():
        return functools.reduce(jnp.logical_and, [pl.program_id(d) == n - 1 for d, n in enumerate(grid)])

    return first, last


def _mm_nt(a, b3, out_dtype=F32, name=None, rider=None):
    M = a.shape[0]
    S, N, ns = b3.shape
    whole_b = S * N * ns * 2 <= RESIDENT_WEIGHT_BYTES
    tm = _row_tile(M, 512 if whole_b or a.dtype != BF16 else 1024)
    tn = N if whole_b else _tile(N, 512)

    def body(a_ref, b_ref, o_ref):
        acc = _dot_nt(a_ref[:, 0:ns].astype(BF16), b_ref[0])
        for s in range(1, S):
            acc = acc + _dot_nt(a_ref[:, s * ns:(s + 1) * ns].astype(BF16), b_ref[s])
        o_ref[...] = acc.astype(o_ref.dtype)

    (out,), landed = _call_with_rider(
        body, rider, *_grid_ends((M // tm, N // tn)), name=name, grid=(M // tm, N // tn),
        in_specs=[pl.BlockSpec((tm, S * ns), lambda i, j: (i, 0)),
                  pl.BlockSpec((S, tn, ns), lambda i, j: (0, j, 0))],
        out_specs=[pl.BlockSpec((tm, tn), lambda i, j: (i, j))],
        out_shape=[jax.ShapeDtypeStruct((M, N), out_dtype)],
        scratch_shapes=[], semantics=("parallel", "parallel"), args=(a, b3))
    return out if rider is None else (out, landed)


def _mm_tn(a, b, S, name=None, rider=None):
    T, K = a.shape
    ns = b.shape[1] // S
    tk = _tile(K, 1024)
    tn = _tile(ns, 512)
    nj = ns // tn
    tt = _row_tile(T, 2048)
    nt = T // tt

    def body(a_ref, b_ref, o_ref, acc_ref):
        t = pl.program_id(2)
        p = _dot_tn(a_ref[...].astype(BF16), b_ref[...].astype(BF16))

        @pl.when(t == 0)
        def _():
            acc_ref[...] = p

        @pl.when(t > 0)
        def _():
            acc_ref[...] += p

        @pl.when(t == nt - 1)
        def _():
            o_ref[...] = acc_ref[...].astype(o_ref.dtype)

    grid = (K // tk, S * nj, nt)
    (out,), landed = _call_with_rider(
        body, rider, *_grid_ends(grid), name=name, grid=grid,
        in_specs=[pl.BlockSpec((tt, tk), lambda i, j, t: (t, i)),
                  pl.BlockSpec((tt, tn), lambda i, j, t: (t, j))],
        out_specs=[pl.BlockSpec((None, tk, tn), lambda i, j, t: (j // nj, i, j % nj))],
        out_shape=[jax.ShapeDtypeStruct((S, K, ns), BF16)],
        scratch_shapes=[pltpu.VMEM((tk, tn), F32)],
        semantics=("parallel", "parallel", "arbitrary"), args=(a, b))
    return out if rider is None else (out, landed)


def _lru_coeffs(lam_row, r):
    cl = -LRU_C * _softplus(-lam_row)
    log_a = cl * r
    a = jnp.exp(log_a)
    em = _one_minus_square(a, log_a)
    return cl, a, em, jnp.sqrt(em)


def _rglru_fwd(proj, conv_w, conv_b, w_r, b_r, w_i, b_i, lam, name, rider=None, y_spare=0):
    T = proj.shape[0]
    DR = conv_w.shape[1]
    NB = DR // HEAD
    tt = _row_tile(T, 256)
    PAD = 8

    def body(x_ref, g_ref, cw_ref, cb_ref, wr_ref, br_ref, wi_ref, bi_ref, lam_ref,
             y_ref, xc_ref, r_ref, i_ref, h_ref, xbuf, hcar):
        @pl.when(pl.program_id(0) == 0)
        def _():
            xbuf[0:PAD, :] = jnp.zeros((PAD, DR), F32)
            hcar[...] = jnp.zeros_like(hcar)

        xbuf[PAD:PAD + tt, :] = x_ref[...]
        xc = cb_ref[...] + cw_ref[0:1, :] * xbuf[pl.ds(PAD - 3, tt), :]
        for k in range(1, CONV_WIDTH):
            xc = xc + cw_ref[k:k + 1, :] * xbuf[pl.ds(PAD - 3 + k, tt), :]
        xbuf[0:PAD, :] = xbuf[tt:tt + PAD, :]
        xc_ref[...] = xc
        xcb = xc.astype(BF16)
        for n in range(NB):
            sl = slice(n * HEAD, (n + 1) * HEAD)
            r_ref[:, sl] = jax.nn.sigmoid(_dot(xcb[:, sl], wr_ref[n]) + br_ref[:, sl])
            i_ref[:, sl] = _gate_sigmoid(_dot(xcb[:, sl], wi_ref[n]) + bi_ref[:, sl])
        _, a, _, mult = _lru_coeffs(lam_ref[...], r_ref[...])
        hs = mult * (i_ref[...] * xc)
        groups = tt // SUB
        a = a.reshape(groups, SUB, DR)
        hs = hs.reshape(groups, SUB, DR)
        sub = lax.broadcasted_iota(jnp.int32, (groups, SUB, DR), 1)
        d = 1
        while d < SUB:
            keep = sub >= d
            a_sh = jnp.where(keep, pltpu.roll(a, d, 1), 1.0)
            h_sh = jnp.where(keep, pltpu.roll(hs, d, 1), 0.0)
            hs = a * h_sh + hs
            a = a * a_sh
            d *= 2
        carry = hcar[...]
        for n in range(groups):
            h_ref[n * SUB:(n + 1) * SUB, :] = hs[n] + a[n] * carry
            carry = h_ref[(n + 1) * SUB - 1:(n + 1) * SUB, :]
        hcar[...] = carry
        h = h_ref[...]
        g = g_ref[...]
        y_ref[...] = (h * (g * _gate_sigmoid(g))).astype(y_ref.dtype)

    col = lambda j: pl.BlockSpec((tt, DR), lambda c: (c, j))
    vec = pl.BlockSpec((1, DR), lambda c: (0, 0))
    gate = pl.BlockSpec((NB, HEAD, HEAD), lambda c: (0, 0, 0))
    f32_out = jax.ShapeDtypeStruct((T, DR), F32)
    nc = T // tt
    return _call_with_rider(
        body, rider, lambda: pl.program_id(0) == 0, lambda: pl.program_id(0) == nc - 1,
        name=name, grid=(nc,),
        in_specs=[col(0), col(1), pl.BlockSpec((CONV_WIDTH, DR), lambda c: (0, 0)), vec, gate, vec, gate, vec, vec],
        out_specs=[col(0)] * 5,
        out_shape=[jax.ShapeDtypeStruct((T, DR + y_spare), BF16), f32_out, f32_out, f32_out, f32_out],
        scratch_shapes=[pltpu.VMEM((tt + PAD, DR), F32), pltpu.VMEM((1, DR), F32)],
        semantics=("arbitrary",),
        args=(proj, proj, conv_w, conv_b.reshape(1, DR), w_r, b_r.reshape(1, DR), w_i, b_i.reshape(1, DR),
              lam.reshape(1, DR)))


def _rglru_bwd(dy, proj, xc, r, i, h, conv_w, w_r, w_i, lam, name, rider=None, dx_spare=0):
    T = proj.shape[0]
    DR = conv_w.shape[1]
    NB = DR // HEAD
    tt = _row_tile(T, 128)
    nc = T // tt
    PAD = 8
    per = tt // PAD

    def body(dy_ref, x_ref, g_ref, xc_ref, r_ref, i_ref, h_ref, xprev_ref, hprev_ref,
             cw_ref, wr_ref, wi_ref, lam_ref,
             dxg_ref, dcw_ref, dcb_ref, dwr_ref, dbr_ref, dwi_ref, dbi_ref, dlam_ref,
             xbuf, dxcbuf, gcar, acar):
        step = pl.program_id(0)
        chunk = nc - 1 - step

        @pl.when(step == 0)
        def _():
            for ref in (dcw_ref, dcb_ref, dwr_ref, dbr_ref, dwi_ref, dbi_ref, dlam_ref, gcar, acar):
                ref[...] = jnp.zeros_like(ref)
            dxcbuf[tt:tt + PAD, :] = jnp.zeros((PAD, DR), F32)

        not_first = (chunk > 0).astype(F32)
        row = lax.broadcasted_iota(jnp.int32, (tt, DR), 0)
        silu, dsilu = _silu_and_grad(g_ref[...])
        dyv = dy_ref[...]
        hv = h_ref[...]
        dxg_ref[:, DR:2 * DR] = (dyv * hv * dsilu).astype(dxg_ref.dtype)
        dh = dyv * silu
        rv = r_ref[...]
        iv = i_ref[...]
        xcv = xc_ref[...]
        lam_row = lam_ref[...]
        cl, a, em, mult = _lru_coeffs(lam_row, rv)
        b = jnp.where(row == tt - 1, acar[...], pltpu.roll(a, tt - 1, 0))
        gs = dh
        d = 1
        while d < tt:
            keep = row < tt - d
            b_sh = jnp.where(keep, pltpu.roll(b, tt - d, 0), 1.0)
            g_sh = jnp.where(keep, pltpu.roll(gs, tt - d, 0), 0.0)
            gs = gs + b * g_sh
            b = b * b_sh
            d *= 2
        gt = gs + b * gcar[...]
        xbuf[0:tt, :] = gt
        gcar[...] = xbuf[0:1, :]
        acar[...] = _lru_coeffs(lam_row, r_ref[0:1, :])[1]
        h_before = hprev_ref[PAD - 1:PAD, :] * not_first
        hprev = jnp.where(row == 0, h_before, pltpu.roll(hv, 1, 0))
        da = gt * hprev
        dmult = gt * (iv * xcv)
        di = gt * mult * xcv
        dxc = gt * mult * iv
        dlog_a = da * a - dmult * (1.0 - em) / mult
        dr = dlog_a * cl
        dlam_ref[...] += jnp.sum(dlog_a * rv, axis=0, keepdims=True) * (LRU_C * jax.nn.sigmoid(-lam_row))
        drp = dr * rv * (1.0 - rv)
        dip = di * iv * (1.0 - iv)
        dbr_ref[...] += jnp.sum(drp, axis=0, keepdims=True)
        dbi_ref[...] += jnp.sum(dip, axis=0, keepdims=True)
        drpb = drp.astype(BF16)
        dipb = dip.astype(BF16)
        xcb = xcv.astype(BF16)
        for n in range(NB):
            sl = slice(n * HEAD, (n + 1) * HEAD)
            dxcbuf[0:tt, sl] = dxc[:, sl] + _dot_nt(drpb[:, sl], wr_ref[n]) + _dot_nt(dipb[:, sl], wi_ref[n])
            dwr_ref[n] += _dot_tn(xcb[:, sl], drpb[:, sl])
            dwi_ref[n] += _dot_tn(xcb[:, sl], dipb[:, sl])
        dxc_all = dxcbuf[0:tt, :]
        dcb_ref[...] += jnp.sum(dxc_all, axis=0, keepdims=True)
        xbuf[0:PAD, :] = xprev_ref[...] * not_first
        xbuf[PAD:PAD + tt, :] = x_ref[...]
        dx = cw_ref[0:1, :] * dxcbuf[pl.ds(3, tt), :]
        for k in range(1, CONV_WIDTH):
            dx = dx + cw_ref[k:k + 1, :] * dxcbuf[pl.ds(3 - k, tt), :]
        dxg_ref[:, 0:DR] = dx.astype(dxg_ref.dtype)
        for k in range(CONV_WIDTH):
            dcw_ref[k:k + 1, :] += jnp.sum(xbuf[pl.ds(PAD - 3 + k, tt), :] * dxc_all, axis=0, keepdims=True)
        dxcbuf[tt:tt + PAD, :] = dxcbuf[0:PAD, :]

    rev = lambda j: pl.BlockSpec((tt, DR), lambda s: (nc - 1 - s, j))
    prev = pl.BlockSpec((PAD, DR), lambda s: (jnp.maximum((nc - 1 - s) * per - 1, 0), 0))
    vec = pl.BlockSpec((1, DR), lambda s: (0, 0))
    gate = pl.BlockSpec((NB, HEAD, HEAD), lambda s: (0, 0, 0))
    taps = pl.BlockSpec((CONV_WIDTH, DR), lambda s: (0, 0))
    vec_out = jax.ShapeDtypeStruct((1, DR), F32)
    gate_out = jax.ShapeDtypeStruct((NB, HEAD, HEAD), F32)
    return _call_with_rider(
        body, rider, lambda: pl.program_id(0) == 0, lambda: pl.program_id(0) == nc - 1,
        name=name, grid=(nc,),
        in_specs=[rev(0), rev(0), rev(1), rev(0), rev(0), rev(0), rev(0), prev, prev, taps, gate, gate, vec],
        out_specs=[pl.BlockSpec((tt, 2 * DR), lambda s: (nc - 1 - s, 0)), taps, vec, gate, vec, gate, vec, vec],
        out_shape=[jax.ShapeDtypeStruct((T, 2 * DR + dx_spare), BF16), jax.ShapeDtypeStruct((CONV_WIDTH, DR), F32),
                   vec_out, gate_out, vec_out, gate_out, vec_out, vec_out],
        scratch_shapes=[pltpu.VMEM((tt + PAD, DR), F32), pltpu.VMEM((tt + PAD, DR), F32),
                        pltpu.VMEM((1, DR), F32), pltpu.VMEM((1, DR), F32)],
        semantics=("arbitrary",),
        args=(dy, proj, proj, xc, r, i, h, proj, h, conv_w, w_r, w_i, lam.reshape(1, DR)))


def _mem_probs(q, k, scale):
    s = _dot_nt(q, k) * scale
    p = jnp.exp(s - jnp.max(s, axis=-1, keepdims=True))
    return p * (1.0 / jnp.sum(p, axis=-1, keepdims=True))


def _memattn_fwd(proj, mkv, y_mix, DR, DM, name):
    T = proj.shape[0]
    M = mkv.shape[0]
    NH = DM // HEAD
    tt = _row_tile(T, 512)
    qcol = 2 * DR // DM
    scale = HEAD ** -0.5

    def body(q_ref, g_ref, k_ref, v_ref, _, y_ref):
        for n in range(NH):
            sl = slice(n * HEAD, (n + 1) * HEAD)
            p = _mem_probs(q_ref[:, sl].astype(BF16), k_ref[:, sl], scale)
            o = _dot(p.astype(BF16), v_ref[:, sl])
            g = g_ref[:, sl]
            y_ref[:, sl] = (o * (g * _gate_sigmoid(g))).astype(y_ref.dtype)

    return pl.pallas_call(
        body, name=name, grid=(T // tt,),
        in_specs=[pl.BlockSpec((tt, DM), lambda t: (t, qcol)), pl.BlockSpec((tt, DM), lambda t: (t, qcol + 1)),
                  pl.BlockSpec((M, DM), lambda t: (0, 0)), pl.BlockSpec((M, DM), lambda t: (0, 1)), HBM_SPEC],
        out_specs=pl.BlockSpec((tt, DM), lambda t: (t, DR // DM)),
        out_shape=jax.ShapeDtypeStruct(y_mix.shape, y_mix.dtype),
        input_output_aliases={4: 0},
        compiler_params=_params("parallel"),
    )(proj, proj, mkv, mkv, y_mix)


def _memattn_bwd(dy, proj, mkv, DR, DM, name, dproj=None):
    T = proj.shape[0]
    M = mkv.shape[0]
    NH = DM // HEAD
    tt = _row_tile(T, 512)
    qcol = 2 * DR // DM
    scale = HEAD ** -0.5

    def body(dy_ref, q_ref, g_ref, k_ref, v_ref, *rest):
        dqg_ref, dkv_ref = rest[-2:]

        @pl.when(pl.program_id(0) == 0)
        def _():
            dkv_ref[...] = jnp.zeros_like(dkv_ref)

        for n in range(NH):
            sl = slice(n * HEAD, (n + 1) * HEAD)
            qb = q_ref[:, sl].astype(BF16)
            kb = k_ref[:, sl]
            vb = v_ref[:, sl]
            p = _mem_probs(qb, kb, scale)
            pb = p.astype(BF16)
            o = _dot(pb, vb)
            silu, dsilu = _silu_and_grad(g_ref[:, sl])
            dyv = dy_ref[:, sl]
            dqg_ref[:, DM + n * HEAD:DM + (n + 1) * HEAD] = (dyv * o * dsilu).astype(dqg_ref.dtype)
            dob = (dyv * silu).astype(BF16)
            dp = _dot_nt(dob, vb)
            ds = (p * (dp - jnp.sum(dp * p, axis=-1, keepdims=True)) * scale).astype(BF16)
            dqg_ref[:, sl] = _dot(ds, kb).astype(dqg_ref.dtype)
            dkv_ref[:, sl] += _dot_tn(ds, qb)
            dkv_ref[:, DM + n * HEAD:DM + (n + 1) * HEAD] += _dot_tn(pb, dob)

    in_place = dproj is not None
    return pl.pallas_call(
        body, name=name, grid=(T // tt,),
        in_specs=[pl.BlockSpec((tt, DM), lambda t: (t, DR // DM)),
                  pl.BlockSpec((tt, DM), lambda t: (t, qcol)), pl.BlockSpec((tt, DM), lambda t: (t, qcol + 1)),
                  pl.BlockSpec((M, DM), lambda t: (0, 0)), pl.BlockSpec((M, DM), lambda t: (0, 1))]
                 + [HBM_SPEC] * in_place,
        out_specs=[pl.BlockSpec((tt, 2 * DM), lambda t: (t, DR // DM if in_place else 0)),
                   pl.BlockSpec((M, 2 * DM), lambda t: (0, 0))],
        out_shape=[jax.ShapeDtypeStruct(dproj.shape if in_place else (T, 2 * DM), BF16),
                   jax.ShapeDtypeStruct((M, 2 * DM), F32)],
        input_output_aliases={5: 0} if in_place else {},
        compiler_params=_params("arbitrary"),
    )(dy, proj, proj, mkv, mkv, *([dproj] if in_place else []))


def _sb_blocks(T):
    tk = _row_tile(T // 2, 256)
    tq = 2 * tk
    assert T % tq == 0
    return tq, tk


def _sb_upper(tk):
    row = lax.broadcasted_iota(jnp.int32, (tk, tk), 0)
    col = lax.broadcasted_iota(jnp.int32, (tk, tk), 1)
    return (row > col).astype(BF16)


def _sb_causal(tq, tk, d):
    row = lax.broadcasted_iota(jnp.int32, (tq, tk), 0)
    col = lax.broadcasted_iota(jnp.int32, (tq, tk), 1)
    return col + d * tk < row


def _sb_own_span(tile, carry, i, tq, tk):
    span = tq // tk
    for d in reversed(range(span)):
        lo = d * tk
        part = tile(span * i + d, tuple(c[lo:] for c in carry), _sb_causal(tq - lo, tk, 0), slice(lo, tq))
        carry = tuple(p if lo == 0 else jnp.concatenate([c[:lo], p], axis=0) for c, p in zip(carry, part))
    return carry


def _sb_earlier(tile, carry, jb_first, tq):
    half = tq // 2

    def walk(carry, jb, rows, watched):
        def more(c):
            return jnp.logical_and(c[0] >= 0, c[1] > 0)

        def step(c):
            jb, _, *rest = c
            rest = tile(jb, tuple(rest), None, rows)
            return jb - 1, _sb_alive(watched(rest[0])), *rest

        jb, _, *carry = lax.while_loop(more, step, (jb, _sb_alive(watched(carry[0])), *carry))
        return jb, tuple(carry)

    jb, carry = walk(carry, jb_first, slice(0, tq), lambda gone: gone[half:])
    _, low = walk(tuple(c[:half] for c in carry), jb, slice(0, half), lambda gone: gone)
    return tuple(jnp.concatenate([lo, c[half:]], axis=0) for lo, c in zip(low, carry))


def _later_sum(x, upper):
    n = x.shape[0]
    hi, lo = _split_bf16(x)
    both = _dot(jnp.concatenate([hi, lo], axis=0), upper)
    return both[0:n] + both[n:2 * n]


def _sb_alive(gone_c):
    return (jnp.min(gone_c) < SB_EXHAUSTED).astype(jnp.int32)


def _sb_weights(q, kb, gone_c, causal, upper, scale):
    return _sb_weights_of(_dot_nt(q, kb) * scale, gone_c, causal, upper)


def _sb_weights_of(z, gone_c, causal, upper):
    sp = jnp.where(z > 20.0, z, jnp.log(1.0 + jnp.exp(z)))
    spm = sp if causal is None else jnp.where(causal, sp, 0.0)
    gone = _later_sum(spm, upper) + gone_c
    w = jnp.exp(z - sp - gone)
    if causal is not None:
        w = jnp.where(causal, w, 0.0)
    return z, sp, spm, w


def _sb_fwd(proj, kv, DR, name, rider=None, y_spare=0):
    T = proj.shape[0]
    NH = DR // HEAD
    tq, tk = _sb_blocks(T)
    span = tq // tk
    scale = HEAD ** -0.5

    def body(q_ref, g_ref, k_ref, v_ref, y_ref, o_ref):
        i = pl.program_id(1)
        q = q_ref[...].astype(BF16)
        upper = _sb_upper(tk)

        def tile(jb, carry, causal, rows=slice(0, tq)):
            gone_c, acc = carry
            n = rows.stop - rows.start
            start = pl.multiple_of(jb * tk, tk)
            kb = k_ref[pl.ds(start, tk), :]
            vb = v_ref[pl.ds(start, tk), :]
            _, _, spm, w = _sb_weights(q[rows], kb, gone_c, causal, upper, scale)
            hi, lo = _split_bf16(w)
            pv = _dot(jnp.concatenate([hi, lo], axis=0), vb)
            return gone_c + jnp.sum(spm, axis=-1, keepdims=True), acc + pv[0:n] + pv[n:2 * n]

        carry = _sb_own_span(tile, (jnp.zeros((tq, 1), F32), jnp.zeros((tq, HEAD), F32)), i, tq, tk)

        _, acc = _sb_earlier(tile, carry, span * i - 1, tq)
        o_ref[...] = acc
        g = g_ref[...]
        y_ref[...] = (acc * (g * _gate_sigmoid(g))).astype(y_ref.dtype)

    blk = lambda off: pl.BlockSpec((tq, HEAD), lambda h, i: (i, off + h))
    whole = lambda off: pl.BlockSpec((T, HEAD), lambda h, i: (0, off + h))
    nq = T // tq
    return _call_with_rider(
        body, rider, lambda: (pl.program_id(0) == 0) & (pl.program_id(1) == 0),
        lambda: (pl.program_id(0) == NH - 1) & (pl.program_id(1) == nq - 1),
        name=name, grid=(NH, nq),
        in_specs=[blk(0), blk(NH), whole(0), whole(NH)],
        out_specs=[blk(0), blk(0)],
        out_shape=[jax.ShapeDtypeStruct((T, DR + y_spare), BF16), jax.ShapeDtypeStruct((T, DR), F32)],
        scratch_shapes=[], semantics=("parallel", "arbitrary"),
        args=(proj, proj, kv, kv))


def _sb_bwd(dy, proj, kv, o, DR, name, rider=None):
    T = proj.shape[0]
    NH = DR // HEAD
    tq, tk = _sb_blocks(T)
    span = tq // tk
    scale = HEAD ** -0.5

    def body(dy_ref, q_ref, g_ref, k_ref, v_ref, o_ref, dq_ref, dg_ref, dk_ref, dv_ref):
        i = pl.program_id(1)

        @pl.when(i == 0)
        def _():
            dk_ref[...] = jnp.zeros_like(dk_ref)
            dv_ref[...] = jnp.zeros_like(dv_ref)

        qf = q_ref[...]
        q = qf.astype(BF16)
        q_t = qf.T.astype(BF16)
        upper = _sb_upper(tk)
        silu, dsilu = _silu_and_grad(g_ref[...])
        dyv = dy_ref[...]
        ov = o_ref[...]
        dg_ref[...] = (dyv * ov * dsilu).astype(dg_ref.dtype)
        do = dyv * silu
        dob = do.astype(BF16)
        do_t = do.T.astype(BF16)
        total = jnp.sum(dob.astype(F32) * ov, axis=-1, keepdims=True)

        def tile(jb, carry, causal, rows=slice(0, tq)):
            gone_c, e_after, dq = carry
            start = pl.multiple_of(jb * tk, tk)
            kb = k_ref[pl.ds(start, tk), :]
            vb = v_ref[pl.ds(start, tk), :]
            z, sp, spm, w = _sb_weights(q[rows], kb, gone_c, causal, upper, scale)
            e = _dot_nt(dob[rows], vb) * w
            dz = e - (total[rows] - (_later_sum(e, upper) + e_after)) * jnp.exp(z - sp)
            if causal is not None:
                dz = jnp.where(causal, dz, 0.0)
            dzb = dz.astype(BF16)
            dk_ref[jb] += _dot(q_t[:, rows], dzb) * scale
            dv_ref[jb] += _dot(do_t[:, rows], w.astype(BF16))
            return (gone_c + jnp.sum(spm, axis=-1, keepdims=True), e_after + jnp.sum(e, axis=-1, keepdims=True),
                    dq + _dot(dzb, kb))

        carry = _sb_own_span(
            tile, (jnp.zeros((tq, 1), F32), jnp.zeros((tq, 1), F32), jnp.zeros((tq, HEAD), F32)), i, tq, tk)

        *_, dq = _sb_earlier(tile, carry, span * i - 1, tq)
        dq_ref[...] = (dq * scale).astype(dq_ref.dtype)

    blk = lambda off: pl.BlockSpec((tq, HEAD), lambda h, i: (i, off + h))
    whole = lambda off: pl.BlockSpec((T, HEAD), lambda h, i: (0, off + h))
    keys_t = pl.BlockSpec((None, T // tk, HEAD, tk), lambda h, i: (h, 0, 0, 0))
    keys_t_shape = jax.ShapeDtypeStruct((NH, T // tk, HEAD, tk), F32)
    nq = T // tq
    return _call_with_rider(
        body, rider, lambda: (pl.program_id(0) == 0) & (pl.program_id(1) == 0),
        lambda: (pl.program_id(0) == NH - 1) & (pl.program_id(1) == nq - 1),
        name=name, grid=(NH, nq),
        in_specs=[blk(0), blk(0), blk(NH), whole(0), whole(NH), blk(0)],
        out_specs=[blk(0), blk(0), keys_t, keys_t],
        out_shape=[jax.ShapeDtypeStruct((T, DR), BF16), jax.ShapeDtypeStruct((T, DR), BF16),
                   keys_t_shape, keys_t_shape],
        scratch_shapes=[], semantics=("parallel", "arbitrary"),
        args=(dy, proj, proj, kv, kv, o))


def _merge_dkv(parts, name):
    NH, nblk, _, tk = parts[0][0].shape
    DR = NH * HEAD
    n = len(parts)

    def body(*refs):
        o_ref = refs[-1]
        for which in range(2):
            for h in range(NH):
                acc = refs[which][h]
                for p in range(1, n):
                    acc = acc + refs[2 * p + which][h]
                col = which * DR + h * HEAD
                o_ref[:, col:col + HEAD] = acc.T.astype(o_ref.dtype)

    blk = pl.BlockSpec((NH, None, HEAD, tk), lambda t: (0, t, 0, 0))
    return pl.pallas_call(
        body, name=name, grid=(nblk,),
        in_specs=[blk] * (2 * n),
        out_specs=pl.BlockSpec((tk, 2 * DR), lambda t: (t, 0)),
        out_shape=jax.ShapeDtypeStruct((nblk * tk, 2 * DR), BF16),
        compiler_params=_params("parallel"),
    )(*[a for pair in parts for a in pair])


def _final_loss(h, g, target, name):
    T, D = h.shape
    tt = _row_tile(T, 256)

    def body(h_ref, g_ref, t_ref, dh_ref, dg_ref, sq_ref):
        @pl.when(pl.program_id(0) == 0)
        def _():
            dg_ref[...] = jnp.zeros_like(dg_ref)
            sq_ref[...] = jnp.zeros_like(sq_ref)

        hv = h_ref[...]
        gv = g_ref[...]
        r = lax.rsqrt(jnp.mean(hv * hv, axis=-1, keepdims=True) + EPS)
        xhat = hv * r
        err = xhat * gv - t_ref[...]
        sq_ref[...] += jnp.sum(err * err, axis=0, keepdims=True)
        dy = err * (1.0 / D)
        dxhat = dy * gv
        dh_ref[...] = r * (dxhat - xhat * jnp.mean(dxhat * xhat, axis=-1, keepdims=True))
        dg_ref[...] += jnp.sum(dy * xhat, axis=0, keepdims=True)

    row = pl.BlockSpec((tt, D), lambda i: (i, 0))
    vec = pl.BlockSpec((1, D), lambda i: (0, 0))
    return pl.pallas_call(
        body, name=name, grid=(T // tt,),
        in_specs=[row, vec, row],
        out_specs=[row, vec, vec],
        out_shape=[jax.ShapeDtypeStruct((T, D), F32), jax.ShapeDtypeStruct((1, D), F32),
                   jax.ShapeDtypeStruct((1, D), F32)],
        compiler_params=_params("arbitrary"),
    )(h, g.reshape(1, D), target)


def _sum_parts(parts_ref):
    g = parts_ref[0].astype(F32)
    for s in range(1, parts_ref.shape[0]):
        g = g + parts_ref[s].astype(F32)
    return g


def _adamw(parts, w, m, v, name, layer=None, into=None):
    P, R, C = parts.shape
    tr = _rows_for(R, P * C * 4)

    def body(p_ref, w_ref, m_ref, v_ref, *rest):
        g_ref, d_ref, nm_ref, nv_ref = rest[-4:]
        g = _sum_parts(p_ref)
        nm = ADAM_B1 * m_ref[...] + (1.0 - ADAM_B1) * g
        nv = ADAM_B2 * v_ref[...] + (1.0 - ADAM_B2) * jnp.square(g)
        m_hat = nm / (1.0 - ADAM_B1 ** ADAM_STEP)
        v_hat = nv / (1.0 - ADAM_B2 ** ADAM_STEP)
        g_ref[...] = g
        d_ref[...] = -ADAM_LR * (m_hat / (jnp.sqrt(v_hat) + ADAM_EPS) + ADAM_WD * w_ref[...])
        nm_ref[...] = nm
        nv_ref[...] = nv

    if layer is None:
        row = pl.BlockSpec((tr, C), lambda i: (i, 0))
        out = jax.ShapeDtypeStruct((R, C), F32)
    else:
        row = pl.BlockSpec((None, tr, C), lambda i: (layer, i, 0))
        out = jax.ShapeDtypeStruct(w.shape, F32)
    earlier = list(into) if into is not None else []
    return pl.pallas_call(
        body, name=name, grid=(R // tr,),
        in_specs=[pl.BlockSpec((P, tr, C), lambda i: (0, i, 0)), row, row, row] + [HBM_SPEC] * len(earlier),
        out_specs=[row] * 4,
        out_shape=[out] * 4,
        input_output_aliases={4 + k: k for k in range(len(earlier))},
        compiler_params=_params("parallel"),
    )(parts, w, m, v, *earlier)


def _sum_devices(parts, name):
    P, R, C = parts.shape
    tr = _rows_for(R, P * C * 4)

    def body(p_ref, o_ref):
        o_ref[...] = _sum_parts(p_ref)

    return pl.pallas_call(
        body, name=name, grid=(R // tr,),
        in_specs=[pl.BlockSpec((P, tr, C), lambda i: (0, i, 0))],
        out_specs=pl.BlockSpec((tr, C), lambda i: (i, 0)),
        out_shape=jax.ShapeDtypeStruct((R, C), F32),
        compiler_params=_params("parallel"),
    )(parts)


def _mesh_position():
    return lax.axis_index("x"), lax.axis_index("y"), lax.axis_index("c")


def _device_index(p):
    return 4 * p[0] + 2 * p[1] + p[2]


class _Gather:
    def __init__(self, arrs):
        self.arrs = list(arrs)
        self.n = len(self.arrs)

    def out_shape(self):
        return [jax.ShapeDtypeStruct((N_DEV,) + a.shape, a.dtype) for a in self.arrs]

    def scratch(self):
        return [pltpu.SemaphoreType.DMA((self.n, 7)), pltpu.SemaphoreType.DMA((self.n, 7)),
                pltpu.SemaphoreType.DMA((self.n,))]

    def _plan(self, ins, outs, sems):
        send_sems, recv_sems, local_sems = sems
        x, y, c = _mesh_position()
        me, sibling = (x, y, c), (x, y, 1 - c)
        chips = [(1 - x, y), (x, 1 - y), (1 - x, 1 - y)]

        def slot(a, p):
            return outs[a].at[_device_index(p)]

        def copy(a, k, block, to, src=None):
            return pltpu.make_async_remote_copy(
                src_ref=slot(a, block) if src is None else src, dst_ref=slot(a, block),
                send_sem=send_sems.at[a, k], recv_sem=recv_sems.at[a, k],
                device_id=to, device_id_type=pl.DeviceIdType.MESH)

        mine = [pltpu.make_async_copy(ins[a], slot(a, me), local_sems.at[a]) for a in range(self.n)]
        first = []
        for a in range(self.n):
            first.append(copy(a, 0, me, sibling, src=ins[a]))
            first += [copy(a, 1 + j, me, (*chip, c), src=ins[a]) for j, chip in enumerate(chips)]
        return me, sibling, c, chips, copy, mine, first

    def start(self, ins, outs, sems):
        *_, mine, first = self._plan(ins, outs, sems)
        for cp in mine + first:
            cp.start()

    def wait(self, ins, outs, sems):
        me, sibling, c, chips, copy, mine, first = self._plan(ins, outs, sems)
        passed = []
        for a in range(self.n):
            for j, chip in enumerate(chips):
                copy(a, 1 + j, (*chip, c), me).wait_recv()
                fwd = copy(a, 4 + j, (*chip, c), sibling)
                fwd.start()
                passed.append(fwd)
        for a in range(self.n):
            copy(a, 0, sibling, me).wait_recv()
            for j, chip in enumerate(chips):
                copy(a, 4 + j, (*chip, 1 - c), me).wait_recv()
        for cp in first + passed:
            cp.wait_send()
        for cp in mine:
            cp.wait()


def _exchange_now(ex, name, also=None):
    exs = [ex] if also is None else [ex, also]
    n = sum(e.n for e in exs)

    def body(*refs):
        ins, outs, sems = refs[:n], refs[n:2 * n], refs[2 * n:]
        parts, at = [], 0
        for k, e in enumerate(exs):
            parts.append((e, ins[at:at + e.n], outs[at:at + e.n], sems[3 * k:3 * k + 3]))
            at += e.n
        for e, e_in, e_out, e_sems in parts:
            e.start(e_in, e_out, e_sems)
        for e, e_in, e_out, e_sems in parts:
            e.wait(e_in, e_out, e_sems)

    return pl.pallas_call(
        body, name=name,
        in_specs=[HBM_SPEC] * n, out_specs=[HBM_SPEC] * n,
        out_shape=[s for e in exs for s in e.out_shape()],
        scratch_shapes=[s for e in exs for s in e.scratch()],
    )(*[a for e in exs for a in e.arrs])


def _all_gather(arrs, name):
    return _exchange_now(_Gather(arrs), name)


class _Scatter:
    def __init__(self, arrs):
        self.arrs = list(arrs)
        self.n = len(self.arrs)

    def out_shape(self):
        return [jax.ShapeDtypeStruct(a.shape, a.dtype) for a in self.arrs]

    def scratch(self):
        return [pltpu.SemaphoreType.DMA((self.n, 7)), pltpu.SemaphoreType.DMA((self.n, 7)),
                pltpu.SemaphoreType.DMA((self.n,))]

    def _copies(self, ins, outs, sems, arrivals):
        send_sems, recv_sems, local_sems = sems
        x, y, c = _mesh_position()
        me = _device_index((x, y, c))
        peers = [(1 - x if k & 4 else x, 1 - y if k & 2 else y, 1 - c if k & 1 else c) for k in range(1, N_DEV)]
        local, sends, recvs = [], [], []
        for a in range(self.n):
            local.append(pltpu.make_async_copy(ins[a].at[me], outs[a].at[me], local_sems.at[a]))
            for k, peer in enumerate(peers):
                there = _device_index(peer)
                src = ins[a].at[there]
                sem = dict(send_sem=send_sems.at[a, k], recv_sem=recv_sems.at[a, k],
                           device_id=peer, device_id_type=pl.DeviceIdType.MESH)
                sends.append(pltpu.make_async_remote_copy(src_ref=src, dst_ref=outs[a].at[me], **sem))
                if arrivals:
                    recvs.append(pltpu.make_async_remote_copy(src_ref=src, dst_ref=outs[a].at[there], **sem))
        return local, sends, recvs

    def start(self, ins, outs, sems):
        local, sends, _ = self._copies(ins, outs, sems, arrivals=False)
        for cp in local + sends:
            cp.start()

    def wait(self, ins, outs, sems):
        local, sends, recvs = self._copies(ins, outs, sems, arrivals=True)
        for cp in recvs:
            cp.wait_recv()
        for cp in sends:
            cp.wait_send()
        for cp in local:
            cp.wait()


def _all_to_all(arrs, name):
    return _exchange_now(_Scatter(arrs), name)


def _call_with_rider(body, rider, first, last, *, name, grid, in_specs, out_specs, out_shape, scratch_shapes,
                     semantics, args):
    if rider is None:
        out = pl.pallas_call(body, name=name, grid=grid, in_specs=in_specs, out_specs=out_specs,
                             out_shape=out_shape, scratch_shapes=scratch_shapes,
                             compiler_params=_params(*semantics))(*args)
        return out, None
    n, n_in, n_out = rider.n, len(in_specs), len(out_specs)

    def riding(*refs):
        ins, r_in = refs[:n_in], refs[n_in:n_in + n]
        outs, r_out = refs[n_in + n:n_in + n + n_out], refs[n_in + n + n_out:n_in + 2 * n + n_out]
        scratch, sems = refs[n_in + 2 * n + n_out:-3], refs[-3:]

        @pl.when(first())
        def _():
            rider.start(r_in, r_out, sems)

        body(*ins, *outs, *scratch)

        @pl.when(last())
        def _():
            rider.wait(r_in, r_out, sems)

    out = pl.pallas_call(
        riding, name=name, grid=grid,
        in_specs=list(in_specs) + [HBM_SPEC] * n, out_specs=list(out_specs) + [HBM_SPEC] * n,
        out_shape=list(out_shape) + rider.out_shape(),
        scratch_shapes=list(scratch_shapes) + rider.scratch(),
        compiler_params=_params(*["arbitrary"] * len(grid)),
    )(*args, *rider.arrs)
    return out[:n_out], out[n_out:]


def _pack(arrs, row_multiple):
    parts = []
    rows = 0
    for a in arrs:
        flat = a.reshape(-1).astype(F32)
        r = -(-flat.shape[0] // (8 * LANES)) * 8
        parts.append(jnp.pad(flat, (0, r * LANES - flat.shape[0])).reshape(r, LANES))
        rows += r
    pad = -rows % row_multiple
    if pad:
        parts.append(jnp.zeros((pad, LANES), F32))
    return jnp.concatenate(parts, axis=0)


def _unpack(buf, shapes, lead=()):
    out = []
    r0 = 0
    for shape in shapes:
        size = 1
        for s in shape:
            size *= s
        r = -(-size // (8 * LANES)) * 8
        part = buf[..., r0:r0 + r, :].reshape(lead + (r * LANES,))[..., :size]
        out.append(part.reshape(lead + tuple(shape)))
        r0 += r
    return out


def _gathered_cols(g):
    g = jnp.moveaxis(g, 0, -2)
    return g.reshape(g.shape[:-2] + (g.shape[-2] * g.shape[-1],))


def kernel(x, mem, mem_norm, w_mem_kv, norm_a, w_in_a, conv_w, conv_b, w_rec_gate, b_rec_gate, w_in_gate, b_in_gate, lru_lambda, w_out_a, kv_norm, w_kv, norm_b, w_in_b, w_out_b, final_norm, loss_target, m_mem_norm, m_w_mem_kv, m_norm_a, m_w_in_a, m_conv_w, m_conv_b, m_w_rec_gate, m_b_rec_gate, m_w_in_gate, m_b_in_gate, m_lru_lambda, m_w_out_a, m_kv_norm, m_w_kv, m_norm_b, m_w_in_b, m_w_out_b, m_final_norm, v_mem_norm, v_w_mem_kv, v_norm_a, v_w_in_a, v_conv_w, v_conv_b, v_w_rec_gate, v_b_rec_gate, v_w_in_gate, v_b_in_gate, v_lru_lambda, v_w_out_a, v_kv_norm, v_w_kv, v_norm_b, v_w_in_b, v_w_out_b, v_final_norm):
    xs = x[0]
    T, D = xs.shape
    L = w_mem_kv.shape[0]
    NA = w_in_a.shape[0]
    NB = w_in_b.shape[0]
    DM2 = w_mem_kv.shape[2]
    DM = DM2 // 2
    DR = w_rec_gate.shape[1] * w_rec_gate.shape[2]
    me = _device_index(_mesh_position())

    small_sharded = [norm_a, conv_w, conv_b, b_rec_gate, b_in_gate, lru_lambda]
    shard = {f"mem_kv{l}": w_mem_kv[l] for l in range(L)}
    shard.update({f"in_a{l}": w_in_a[l] for l in range(NA)}, **{f"out_a{l}": w_out_a[l] for l in range(NA)})
    shard.update({f"in_b{j}": w_in_b[j] for j in range(NB)}, **{f"out_b{j}": w_out_b[j] for j in range(NB)})
    shard["kv"] = w_kv
    shard = {k: w.astype(BF16) for k, w in shard.items()}
    now = ["mem_kv0"]
    gathered = _all_gather([shard[k] for k in now] + [_pack(small_sharded, 8)], "gather_params")
    full = dict(zip(now, gathered[:-1]))

    def gather_rider(keys):
        return _Gather([shard[k] for k in keys]), keys

    def out_proj(key):
        return full[key].reshape(1, -1, D)

    def mem_proj(l):
        return full[f"mem_kv{l}"].reshape(1, D, DM2)

    norm_a_f, conv_w_f, conv_b_f, b_r_f, b_i_f, lam_f = [
        _gathered_cols(s) for s in _unpack(gathered[-1], [s.shape for s in small_sharded], lead=(N_DEV,))]
    w_r_bf = w_rec_gate.astype(BF16)
    w_i_bf = w_in_gate.astype(BF16)

    zeros_mem = jnp.zeros_like(mem[0])
    mem_n = _rms_fwd(mem[0], mem_norm, "rms_mem")
    mkv = [_mm_nn(mem_n, mem_proj(0), out_dtype=BF16, name="mm_mem_kv0")]

    h = xs
    rider, keys = gather_rider(["in_a0"])
    u, landed = _rms_fwd(h, norm_a_f[0], "rms_a0", rider)
    full.update(zip(keys, landed))
    saved_a = []
    for l in range(NA):
        rider, keys = gather_rider([f"out_a{l}"])
        proj, landed = _mm_nn(u, full[f"in_a{l}"], name=f"mm_in_a{l}", rider=rider)
        full.update(zip(keys, landed))
        later_mem = [f"mem_kv{m}" for m in range(1, L)] if l == 0 else []
        rider, keys = gather_rider([f"in_a{l + 1}" if l + 1 < NA else "kv"] + later_mem)
        (y_rnn, xc, r, i, hr), landed = _rglru_fwd(proj, conv_w_f[l], conv_b_f[l], w_r_bf[l], b_r_f[l], w_i_bf[l],
                                                   b_i_f[l], lam_f[l], f"rglru_fwd{l}", rider, y_spare=DM)
        full.update(zip(keys, landed))
        if l == 0:
            mkv += [_mm_nn(mem_n, mem_proj(m), out_dtype=BF16, name=f"mm_mem_kv{m}") for m in range(1, L)]
        ycat = _memattn_fwd(proj, mkv[l], y_rnn, DR, DM, f"memattn_fwd_a{l}")
        gains = [norm_a_f[l + 1]] if l + 1 < NA else [kv_norm, norm_b[0]]
        h_next, *normed = _mm_nn(ycat, out_proj(f"out_a{l}"), res=h, name=f"mm_out_a{l}", norms=gains)
        saved_a.append((h, u, proj, xc, r, i, hr, ycat))
        h, u = h_next, normed[-1]
    h_kv, u_kv = h, normed[0]
    rider, keys = gather_rider(["in_b0"])
    kv, landed = _mm_nn(u_kv, full["kv"], out_dtype=BF16, name="mm_kv", rider=rider)
    full.update(zip(keys, landed))
    saved_b = []
    for j in range(NB):
        proj = _mm_nn(u, full[f"in_b{j}"], name=f"mm_in_b{j}")
        rider, keys = gather_rider([f"out_b{j}"] + ([f"in_b{j + 1}"] if j + 1 < NB else []))
        (y_sb, o_sb), landed = _sb_fwd(proj, kv, DR, f"sb_fwd{j}", rider, y_spare=DM)
        full.update(zip(keys, landed))
        ycat = _memattn_fwd(proj, mkv[NA + j], y_sb, DR, DM, f"memattn_fwd_b{j}")
        if j + 1 < NB:
            h_next, u_next = _mm_nn(ycat, out_proj(f"out_b{j}"), res=h, name=f"mm_out_b{j}", norms=[norm_b[j + 1]])
        else:
            h_next, u_next = _mm_nn(ycat, out_proj(f"out_b{j}"), res=h, name=f"mm_out_b{j}"), None
        saved_b.append((h, u, proj, o_sb, ycat))
        h, u = h_next, u_next

    dh, d_final_norm, sq = _final_loss(h, final_norm, loss_target[0], "final_loss")
    loss = lax.psum(0.5 * jnp.sum(sq) / D, ("x", "y", "c"))

    big_grads = {}
    received = {}

    def scatter_rider(keys):
        return _Scatter([big_grads[k] for k in keys]), keys

    dmkv = [None] * L
    d_norm_b = [None] * NB
    dkv_parts = []
    for j in reversed(range(NB)):
        h_in, u, proj, o_sb, ycat = saved_b[j]
        dy = _mm_nt(dh, out_proj(f"out_b{j}"), name=f"mm_dy_b{j}")
        big_grads[f"out_b{j}"] = _mm_tn(ycat, dh, 1, name=f"mm_dw_out_b{j}").reshape(N_DEV, -1, D)
        rider, keys = scatter_rider(([f"in_b{j + 1}"] if j + 1 < NB else []) + [f"out_b{j}"])
        (dq, dg, dk, dv), landed = _sb_bwd(dy, proj, kv, o_sb, DR, f"sb_bwd{j}", rider)
        received.update(zip(keys, landed))
        dkv_parts.append((dk, dv))
        dqg_mem, dmkv[NA + j] = _memattn_bwd(dy, proj, mkv[NA + j], DR, DM, f"memattn_bwd_b{j}")
        dproj = jnp.concatenate([dq, dg, dqg_mem], axis=-1)
        du = _mm_nt(dproj, full[f"in_b{j}"], name=f"mm_du_b{j}")
        big_grads[f"in_b{j}"] = _mm_tn(u, dproj, N_DEV, name=f"mm_dw_in_b{j}")
        dh, d_norm_b[j] = _rms_bwd(du, h_in, norm_b[j], dh, f"rms_bwd_b{j}")

    dkv = _merge_dkv(dkv_parts, "merge_dkv")
    rider, keys = scatter_rider(["in_b0"])
    du_kv, landed = _mm_nt(dkv, full["kv"], name="mm_du_kv", rider=rider)
    received.update(zip(keys, landed))
    big_grads["kv"] = _mm_tn(u_kv, dkv, N_DEV, name="mm_dw_kv")
    dh, d_kv_norm = _rms_bwd(du_kv, h_kv, kv_norm, dh, "rms_bwd_kv")

    d_norm_a, d_conv_w, d_conv_b, d_w_r, d_b_r, d_w_i, d_b_i, d_lam = ([None] * NA for _ in range(8))
    for l in reversed(range(NA)):
        h_in, u, proj, xc, r, i, hr, ycat = saved_a[l]
        dy = _mm_nt(dh, out_proj(f"out_a{l}"), name=f"mm_dy_a{l}")
        big_grads[f"out_a{l}"] = _mm_tn(ycat, dh, 1, name=f"mm_dw_out_a{l}").reshape(N_DEV, -1, D)
        rider, keys = scatter_rider([f"in_a{l + 1}" if l + 1 < NA else "kv", f"out_a{l}"])
        (dxg, d_conv_w[l], d_conv_b[l], d_w_r[l], d_b_r[l], d_w_i[l], d_b_i[l], d_lam[l]), landed = _rglru_bwd(
            dy, proj, xc, r, i, hr, conv_w_f[l], w_r_bf[l], w_i_bf[l], lam_f[l], f"rglru_bwd{l}", rider,
            dx_spare=DM2)
        received.update(zip(keys, landed))
        dproj, dmkv[l] = _memattn_bwd(dy, proj, mkv[l], DR, DM, f"memattn_bwd_a{l}", dproj=dxg)
        if l > 0:
            du = _mm_nt(dproj, full[f"in_a{l}"], name=f"mm_du_a{l}")
            big_grads[f"in_a{l}"] = _mm_tn(u, dproj, N_DEV, name=f"mm_dw_in_a{l}")
        else:
            dmkv_all = jnp.concatenate(dmkv, axis=-1)
            dmem_n = _mm_nt(dmkv_all, jnp.concatenate([mem_proj(m) for m in range(L)], axis=0), name="mm_dmem")
            for m in range(L):
                big_grads[f"mem_kv{m}"] = _mm_tn(mem_n, dmkv[m], 1, name=f"mm_dw_mem_kv{m}").reshape(N_DEV, -1, DM2)
            _, d_mem_norm = _rms_bwd(dmem_n, mem[0], mem_norm, zeros_mem, "rms_bwd_mem")
            rider, keys = scatter_rider([f"mem_kv{m}" for m in range(L)])
            du, landed = _mm_nt(dproj, full["in_a0"], name="mm_du_a0", rider=rider)
            received.update(zip(keys, landed))
            small_early = {
                "mem_norm": d_mem_norm.reshape(-1),
                "conv_w": jnp.stack(d_conv_w),
                "conv_b": jnp.concatenate(d_conv_b, axis=0),
                "w_rec_gate": jnp.stack(d_w_r),
                "b_rec_gate": jnp.concatenate(d_b_r, axis=0),
                "w_in_gate": jnp.stack(d_w_i),
                "b_in_gate": jnp.concatenate(d_b_i, axis=0),
                "lru_lambda": jnp.concatenate(d_lam, axis=0),
                "kv_norm": d_kv_norm.reshape(-1),
                "norm_b": jnp.concatenate(d_norm_b, axis=0),
                "final_norm": d_final_norm.reshape(-1),
            }
            rider = _Gather([_pack(list(small_early.values()), 256)])
            big_grads["in_a0"], (early_gathered,) = _mm_tn(u, dproj, N_DEV, name="mm_dw_in_a0", rider=rider)
        dh, d_norm_a[l] = _rms_bwd(du, h_in, norm_a_f[l], dh, f"rms_bwd_a{l}")
    grad_x = dh.reshape(x.shape)

    small_late = {"norm_a": jnp.concatenate(d_norm_a, axis=0)}
    received["in_a0"], late_gathered = _exchange_now(
        _Scatter([big_grads["in_a0"]]), "last_exchange", also=_Gather([_pack(list(small_late.values()), 256)]))

    def update(key, w, m, v):
        shape = w.shape
        two_d = (-1, shape[-1])
        return [o.reshape(shape) for o in _adamw(received[key], w.reshape(two_d), m.reshape(two_d),
                                                 v.reshape(two_d), f"adamw_{key}")]

    def update_layers(prefix, w, m, v):
        outs = None
        for l in range(w.shape[0]):
            outs = _adamw(received[f"{prefix}{l}"], w, m, v, f"adamw_{prefix}{l}", layer=l, into=outs)
        return outs

    upd = {
        "w_mem_kv": update_layers("mem_kv", w_mem_kv, m_w_mem_kv, v_w_mem_kv),
        "w_in_a": update_layers("in_a", w_in_a, m_w_in_a, v_w_in_a),
        "w_out_a": update_layers("out_a", w_out_a, m_w_out_a, v_w_out_a),
        "w_kv": update("kv", w_kv, m_w_kv, v_w_kv),
        "w_in_b": update_layers("in_b", w_in_b, m_w_in_b, v_w_in_b),
        "w_out_b": update_layers("out_b", w_out_b, m_w_out_b, v_w_out_b),
    }

    small_grad = {}
    for part, got, name in ((small_early, early_gathered, "sum_small_early"), (small_late, late_gathered, "sum_small_late")):
        summed = _unpack(_sum_devices(got, name), [g.shape for g in part.values()])
        small_grad.update(zip(part, summed))
    small_names = list(small_grad)
    small_w = {"mem_norm": (mem_norm, m_mem_norm, v_mem_norm), "norm_a": (norm_a, m_norm_a, v_norm_a),
               "conv_w": (conv_w, m_conv_w, v_conv_w), "conv_b": (conv_b, m_conv_b, v_conv_b),
               "w_rec_gate": (w_rec_gate, m_w_rec_gate, v_w_rec_gate),
               "b_rec_gate": (b_rec_gate, m_b_rec_gate, v_b_rec_gate),
               "w_in_gate": (w_in_gate, m_w_in_gate, v_w_in_gate), "b_in_gate": (b_in_gate, m_b_in_gate, v_b_in_gate),
               "lru_lambda": (lru_lambda, m_lru_lambda, v_lru_lambda), "kv_norm": (kv_norm, m_kv_norm, v_kv_norm),
               "norm_b": (norm_b, m_norm_b, v_norm_b), "final_norm": (final_norm, m_final_norm, v_final_norm)}
    for k in small_names:
        w = small_w[k][0]
        if small_grad[k].shape != w.shape:
            n = w.shape[-1]
            small_grad[k] = lax.dynamic_slice_in_dim(small_grad[k], me * n, n, axis=-1)
    small_shapes = [small_w[k][0].shape for k in small_names]
    packed = [_pack([small_grad[k] for k in small_names], 256)[None]]
    packed += [_pack([small_w[k][t] for k in small_names], 256) for t in range(3)]
    small_out = [_unpack(o, small_shapes) for o in _adamw(*packed, "adamw_small")]
    for idx, k in enumerate(small_names):
        upd[k] = [small_out[t][idx] for t in range(4)]

    order = ["mem_norm", "w_mem_kv", "norm_a", "w_in_a", "conv_w", "conv_b", "w_rec_gate", "b_rec_gate", "w_in_gate",
             "b_in_gate", "lru_lambda", "w_out_a", "kv_norm", "w_kv", "norm_b", "w_in_b", "w_out_b", "final_norm"]
    return (loss, grad_x, *[upd[k][0] for k in order], *[upd[k][1] for k in order],
            *[upd[k][2] for k in order], *[upd[k][3] for k in order])
```
